```python
import math
import jax, jax.numpy as jnp
from jax import lax
import numpy as np

D_MODEL = 1024
BATCH = 8
SEQ = 8192
DEPTH = 4

GRID_W = 64
HEAD_DIM = 64
N_HEADS_NA = 8
N_HEADS_DIL = 8
NA_KH_MAX = 8
NA_KW = 16
DIL_PATTERNS = ((128, 1), (512, 4), (2048, 16))
DIL_BLOCK = 64
N_BUCKETS = 32
T5_MAX_DIST = 1024
CONF_CH = 512
CONF_K = 31
SC_CH = 512
SC_K = 3
D_FF = 2816
FFN_K = 3
LN_EPS = 1e-5
NEG = -1e30
ALPHA = (2 * DEPTH) ** 0.25
BETA = (8 * DEPTH) ** -0.25
N_ATTN_LAYERS = (DEPTH + 1) // 2
N_CONV_LAYERS = DEPTH // 2
ATTN_W = (N_HEADS_NA + N_HEADS_DIL) * HEAD_DIM
CONV_W = CONF_CH + SC_CH

kernel_name = "hybrid_natten_dilated_conformer_shortconv_encoder"


def layer_norm(x, g, b):
    xf = x.astype(jnp.float32)
    mu = jnp.mean(xf, axis=-1, keepdims=True)
    var = jnp.mean(jnp.square(xf - mu), axis=-1, keepdims=True)
    y = (xf - mu) * lax.rsqrt(var + LN_EPS) * g.astype(jnp.float32) + b.astype(jnp.float32)
    return y.astype(x.dtype)


def dwconv(x, w):
    k, c = w.shape
    return lax.conv_general_dilated(
        x, w[:, None, :].astype(x.dtype), window_strides=(1,),
        padding=[(k // 2, k - 1 - k // 2)],
        dimension_numbers=('NWC', 'WIO', 'NWC'), feature_group_count=c)


def t5_bucket(rel):
    nb = N_BUCKETS // 2
    max_exact = nb // 2
    ret = jnp.where(rel > 0, nb, 0)
    n = jnp.abs(rel)
    large = max_exact + (jnp.log(jnp.maximum(n, 1).astype(jnp.float32) / max_exact)
                         / math.log(T5_MAX_DIST / max_exact) * (nb - max_exact)).astype(jnp.int32)
    large = jnp.minimum(large, nb - 1)
    return ret + jnp.where(n < max_exact, n, large)


def neighbourhood_attention(q, k, v, rpb):
    B, S, H, Dh = q.shape
    rows = S // GRID_W
    kh = min(NA_KH_MAX, rows)
    shp = (B, rows, GRID_W, H, Dh)
    qg, kg, vg = q.reshape(shp), k.reshape(shp), v.reshape(shp)
    cols = jnp.arange(GRID_W)
    c0 = jnp.clip(cols - NA_KW // 2, 0, GRID_W - NA_KW)
    col_idx = c0[:, None] + jnp.arange(NA_KW)[None, :]
    col_rel = col_idx - cols[:, None] + NA_KW - 1

    def one_row(r):
        r0 = jnp.clip(r - kh // 2, 0, rows - kh)
        qr = lax.dynamic_index_in_dim(qg, r, axis=1, keepdims=False)
        kr = lax.dynamic_slice_in_dim(kg, r0, kh, axis=1)[:, :, col_idx]
        vr = lax.dynamic_slice_in_dim(vg, r0, kh, axis=1)[:, :, col_idx]
        s = jnp.einsum('bchd,bicjhd->bhcij', qr, kr).astype(jnp.float32)
        row_rel = r0 + jnp.arange(kh) - r + NA_KH_MAX - 1
        bias = rpb[:, row_rel][:, :, col_rel].astype(jnp.float32)
        s = s + jnp.transpose(bias, (0, 2, 1, 3))[None]
        a = jax.nn.softmax(s.reshape(B, H, GRID_W, kh * NA_KW), axis=-1).reshape(s.shape)
        return jnp.einsum('bhcij,bicjhd->bchd', a.astype(vr.dtype), vr)

    out = lax.map(one_row, jnp.arange(rows))
    return jnp.transpose(out, (1, 0, 2, 3, 4)).reshape(B, S, H * Dh)


def dilated_branch(q, k, v, t5_table, window, dil):
    B, S, H, Dh = q.shape
    L = S // dil
    half = window // (2 * dil)
    nblk = -(-L // DIL_BLOCK)
    Lp = nblk * DIL_BLOCK

    def to_res(t):
        t = t.reshape(B, L, dil, H, Dh)
        return jnp.pad(t, ((0, 0), (0, Lp - L), (0, 0), (0, 0), (0, 0)))

    def band(t):
        t = jnp.pad(to_res(t), ((0, 0), (DIL_BLOCK, DIL_BLOCK), (0, 0), (0, 0), (0, 0)))
        t = t.reshape(B, nblk + 2, DIL_BLOCK, dil, H, Dh)
        return jnp.concatenate([t[:, :-2], t[:, 1:-1], t[:, 2:]], axis=2)

    qr = to_res(q).reshape(B, nblk, DIL_BLOCK, dil, H, Dh)
    kb, vb = band(k), band(v)
    s = jnp.einsum('bnqrhd,bnkrhd->bnrhqk', qr, kb).astype(jnp.float32)
    qi = jnp.arange(DIL_BLOCK)
    ki = jnp.arange(3 * DIL_BLOCK)
    rel = ki[None, :] - DIL_BLOCK - qi[:, None]
    bias = jnp.transpose(t5_table[t5_bucket(rel * dil)], (2, 0, 1)).astype(jnp.float32)
    kpos = jnp.arange(nblk)[:, None] * DIL_BLOCK + ki[None, :] - DIL_BLOCK
    valid = ((jnp.abs(rel) <= half)[None]
             & ((kpos >= 0) & (kpos < L))[:, None, :])
    s = jnp.where(valid[None, :, None, None], s + bias, NEG)
    m = jnp.max(s, axis=-1, keepdims=True)
    p = jnp.exp(s - m)
    den = jnp.sum(p, axis=-1, keepdims=True)
    o = jnp.einsum('bnrhqk,bnkrhd->bnqrhd', (p / den).astype(vb.dtype), vb)
    lse = (m + jnp.log(den))[..., 0]
    o = o.reshape(B, Lp, dil, H, Dh)[:, :L].reshape(B, S, H, Dh)
    lse = jnp.transpose(lse, (0, 1, 4, 2, 3)).reshape(B, Lp, dil, H)[:, :L].reshape(B, S, H)
    return o, lse


def dilated_attention(q, k, v, t5_table):
    B, S, H, Dh = q.shape
    res = [dilated_branch(q, k, v, t5_table, w, d) for (w, d) in DIL_PATTERNS]
    outs = jnp.stack([r[0] for r in res]).astype(jnp.float32)
    lses = jnp.stack([r[1] for r in res])
    wts = jax.nn.softmax(lses, axis=0)
    y = jnp.einsum('pbsh,pbshd->bshd', wts, outs)
    return y.astype(q.dtype).reshape(B, S, H * Dh)


def attn_mixer(x, w_in, w_out, rpb, t5_table):
    B, S, _ = x.shape
    h = x @ w_in
    qa, ka, va, qb, kb, vb = jnp.split(h, 6, axis=-1)
    scale = HEAD_DIM ** -0.5
    hs = lambda t, nh: t.reshape(B, S, nh, HEAD_DIM)
    ya = neighbourhood_attention(hs(qa * scale, N_HEADS_NA), hs(ka, N_HEADS_NA), hs(va, N_HEADS_NA), rpb)
    yb = dilated_attention(hs(qb * scale, N_HEADS_DIL), hs(kb, N_HEADS_DIL), hs(vb, N_HEADS_DIL), t5_table)
    return jnp.concatenate([ya, yb], axis=-1) @ w_out


def conv_mixer(x, w_in, conf_dw_w, conf_dw_b, conf_ln_g, conf_ln_b, sconv_w, w_out):
    h = x @ w_in
    ca, cg, gb, gc, hx = jnp.split(
        h, [CONF_CH, 2 * CONF_CH, 2 * CONF_CH + SC_CH, 2 * CONF_CH + 2 * SC_CH], axis=-1)
    u = ca * jax.nn.sigmoid(cg)
    u = dwconv(u, conf_dw_w) + conf_dw_b
    u = jax.nn.silu(layer_norm(u, conf_ln_g, conf_ln_b))
    z = gb * dwconv(gc * hx, sconv_w)
    return jnp.concatenate([u, z], axis=-1) @ w_out


def conv_ffn(x, w_up, dw_w, w_down):
    h = dwconv(x @ w_up, dw_w)
    g, u = jnp.split(h, 2, axis=-1)
    return (jax.nn.silu(g) * u) @ w_down


def _fwd_setup_inputs(seed: int = 0) -> dict:
    key = jax.random.key(seed)
    ks = jax.random.split(key, 20)
    n = lambda k, s, sc: jax.random.normal(k, s, jnp.float32) * sc
    return {
        "x": n(ks[0], (BATCH, SEQ, D_MODEL), 1.0),
        "t5_bias": n(ks[1], (N_BUCKETS, N_HEADS_DIL), 0.1),
        "attn_w_in": n(ks[2], (N_ATTN_LAYERS, D_MODEL, 3 * ATTN_W), D_MODEL ** -0.5),
        "attn_w_out": n(ks[3], (N_ATTN_LAYERS, ATTN_W, D_MODEL), ATTN_W ** -0.5 * BETA),
        "na_rpb": n(ks[4], (N_ATTN_LAYERS, N_HEADS_NA, 2 * NA_KH_MAX - 1, 2 * NA_KW - 1), 0.1),
        "conv_w_in": n(ks[5], (N_CONV_LAYERS, D_MODEL, 2 * CONF_CH + 3 * SC_CH), D_MODEL ** -0.5),
        "conf_dw_w": n(ks[6], (N_CONV_LAYERS, CONF_K, CONF_CH), CONF_K ** -0.5),
        "conf_dw_b": n(ks[7], (N_CONV_LAYERS, CONF_CH), 0.01),
        "conf_ln_g": 1.0 + n(ks[8], (N_CONV_LAYERS, CONF_CH), 0.01),
        "conf_ln_b": n(ks[9], (N_CONV_LAYERS, CONF_CH), 0.01),
        "sconv_w": n(ks[10], (N_CONV_LAYERS, SC_K, SC_CH), SC_K ** -0.5),
        "conv_w_out": n(ks[11], (N_CONV_LAYERS, CONV_W, D_MODEL), CONV_W ** -0.5 * BETA),
        "ffn_w_up": n(ks[12], (DEPTH, D_MODEL, 2 * D_FF), D_MODEL ** -0.5),
        "ffn_dw_w": n(ks[13], (DEPTH, FFN_K, 2 * D_FF), FFN_K ** -0.5),
        "ffn_w_down": n(ks[14], (DEPTH, D_FF, D_MODEL), D_FF ** -0.5 * BETA),
        "mix_ln_g": 1.0 + n(ks[15], (DEPTH, D_MODEL), 0.01),
        "mix_ln_b": n(ks[16], (DEPTH, D_MODEL), 0.01),
        "ffn_ln_g": 1.0 + n(ks[17], (DEPTH, D_MODEL), 0.01),
        "ffn_ln_b": n(ks[18], (DEPTH, D_MODEL), 0.01),
    }


def _fwd_reference(x, t5_bias, attn_w_in, attn_w_out, na_rpb, conv_w_in, conf_dw_w, conf_dw_b,
              conf_ln_g, conf_ln_b, sconv_w, conv_w_out, ffn_w_up, ffn_dw_w, ffn_w_down,
              mix_ln_g, mix_ln_b, ffn_ln_g, ffn_ln_b):
    for i in range(DEPTH):
        j = i // 2
        if i % 2 == 0:
            y = attn_mixer(x, attn_w_in[j], attn_w_out[j], na_rpb[j], t5_bias)
        else:
            y = conv_mixer(x, conv_w_in[j], conf_dw_w[j], conf_dw_b[j], conf_ln_g[j],
                           conf_ln_b[j], sconv_w[j], conv_w_out[j])
        x = layer_norm(ALPHA * x + y, mix_ln_g[i], mix_ln_b[i])
        x = layer_norm(ALPHA * x + conv_ffn(x, ffn_w_up[i], ffn_dw_w[i], ffn_w_down[i]),
                       ffn_ln_g[i], ffn_ln_b[i])
    return x


import jax as _jax
import jax.numpy as _jnp

TWIN_FORMAT = 'train_step'
FWD_PARAMS = ['x', 't5_bias', 'attn_w_in', 'attn_w_out', 'na_rpb', 'conv_w_in', 'conf_dw_w', 'conf_dw_b', 'conf_ln_g', 'conf_ln_b', 'sconv_w', 'conv_w_out', 'ffn_w_up', 'ffn_dw_w', 'ffn_w_down', 'mix_ln_g', 'mix_ln_b', 'ffn_ln_g', 'ffn_ln_b']
TWIN_WEIGHTS = ['t5_bias', 'attn_w_in', 'attn_w_out', 'na_rpb', 'conv_w_in', 'conf_dw_w', 'conf_dw_b', 'conf_ln_g', 'conf_ln_b', 'sconv_w', 'conv_w_out', 'ffn_w_up', 'ffn_dw_w', 'ffn_w_down', 'mix_ln_g', 'mix_ln_b', 'ffn_ln_g', 'ffn_ln_b']
TWIN_DIFF_INPUT = 'x'
TWIN_INPUTS = ['x', 't5_bias', 'attn_w_in', 'attn_w_out', 'na_rpb', 'conv_w_in', 'conf_dw_w', 'conf_dw_b', 'conf_ln_g', 'conf_ln_b', 'sconv_w', 'conv_w_out', 'ffn_w_up', 'ffn_dw_w', 'ffn_w_down', 'mix_ln_g', 'mix_ln_b', 'ffn_ln_g', 'ffn_ln_b', 'loss_target', 'm_t5_bias', 'm_attn_w_in', 'm_attn_w_out', 'm_na_rpb', 'm_conv_w_in', 'm_conf_dw_w', 'm_conf_dw_b', 'm_conf_ln_g', 'm_conf_ln_b', 'm_sconv_w', 'm_conv_w_out', 'm_ffn_w_up', 'm_ffn_dw_w', 'm_ffn_w_down', 'm_mix_ln_g', 'm_mix_ln_b', 'm_ffn_ln_g', 'm_ffn_ln_b', 'v_t5_bias', 'v_attn_w_in', 'v_attn_w_out', 'v_na_rpb', 'v_conv_w_in', 'v_conf_dw_w', 'v_conf_dw_b', 'v_conf_ln_g', 'v_conf_ln_b', 'v_sconv_w', 'v_conv_w_out', 'v_ffn_w_up', 'v_ffn_dw_w', 'v_ffn_w_down', 'v_mix_ln_g', 'v_mix_ln_b', 'v_ffn_ln_g', 'v_ffn_ln_b']
TWIN_OUTPUTS = ['loss', 'grad_x', 'grad_t5_bias', 'grad_attn_w_in', 'grad_attn_w_out', 'grad_na_rpb', 'grad_conv_w_in', 'grad_conf_dw_w', 'grad_conf_dw_b', 'grad_conf_ln_g', 'grad_conf_ln_b', 'grad_sconv_w', 'grad_conv_w_out', 'grad_ffn_w_up', 'grad_ffn_dw_w', 'grad_ffn_w_down', 'grad_mix_ln_g', 'grad_mix_ln_b', 'grad_ffn_ln_g', 'grad_ffn_ln_b', 'delta_t5_bias', 'delta_attn_w_in', 'delta_attn_w_out', 'delta_na_rpb', 'delta_conv_w_in', 'delta_conf_dw_w', 'delta_conf_dw_b', 'delta_conf_ln_g', 'delta_conf_ln_b', 'delta_sconv_w', 'delta_conv_w_out', 'delta_ffn_w_up', 'delta_ffn_dw_w', 'delta_ffn_w_down', 'delta_mix_ln_g', 'delta_mix_ln_b', 'delta_ffn_ln_g', 'delta_ffn_ln_b', 'new_m_t5_bias', 'new_m_attn_w_in', 'new_m_attn_w_out', 'new_m_na_rpb', 'new_m_conv_w_in', 'new_m_conf_dw_w', 'new_m_conf_dw_b', 'new_m_conf_ln_g', 'new_m_conf_ln_b', 'new_m_sconv_w', 'new_m_conv_w_out', 'new_m_ffn_w_up', 'new_m_ffn_dw_w', 'new_m_ffn_w_down', 'new_m_mix_ln_g', 'new_m_mix_ln_b', 'new_m_ffn_ln_g', 'new_m_ffn_ln_b', 'new_v_t5_bias', 'new_v_attn_w_in', 'new_v_attn_w_out', 'new_v_na_rpb', 'new_v_conv_w_in', 'new_v_conf_dw_w', 'new_v_conf_dw_b', 'new_v_conf_ln_g', 'new_v_conf_ln_b', 'new_v_sconv_w', 'new_v_conv_w_out', 'new_v_ffn_w_up', 'new_v_ffn_dw_w', 'new_v_ffn_w_down', 'new_v_mix_ln_g', 'new_v_mix_ln_b', 'new_v_ffn_ln_g', 'new_v_ffn_ln_b']
TWIN_LEAF_KINDS = {'loss': 'loss', 'grad_x': 'grad_x', 'grad_t5_bias': 'grad_w', 'grad_attn_w_in': 'grad_w', 'grad_attn_w_out': 'grad_w', 'grad_na_rpb': 'grad_w', 'grad_conv_w_in': 'grad_w', 'grad_conf_dw_w': 'grad_w', 'grad_conf_dw_b': 'grad_w', 'grad_conf_ln_g': 'grad_w', 'grad_conf_ln_b': 'grad_w', 'grad_sconv_w': 'grad_w', 'grad_conv_w_out': 'grad_w', 'grad_ffn_w_up': 'grad_w', 'grad_ffn_dw_w': 'grad_w', 'grad_ffn_w_down': 'grad_w', 'grad_mix_ln_g': 'grad_w', 'grad_mix_ln_b': 'grad_w', 'grad_ffn_ln_g': 'grad_w', 'grad_ffn_ln_b': 'grad_w', 'delta_t5_bias': 'delta_w', 'delta_attn_w_in': 'delta_w', 'delta_attn_w_out': 'delta_w', 'delta_na_rpb': 'delta_w', 'delta_conv_w_in': 'delta_w', 'delta_conf_dw_w': 'delta_w', 'delta_conf_dw_b': 'delta_w', 'delta_conf_ln_g': 'delta_w', 'delta_conf_ln_b': 'delta_w', 'delta_sconv_w': 'delta_w', 'delta_conv_w_out': 'delta_w', 'delta_ffn_w_up': 'delta_w', 'delta_ffn_dw_w': 'delta_w', 'delta_ffn_w_down': 'delta_w', 'delta_mix_ln_g': 'delta_w', 'delta_mix_ln_b': 'delta_w', 'delta_ffn_ln_g': 'delta_w', 'delta_ffn_ln_b': 'delta_w', 'new_m_t5_bias': 'new_m', 'new_m_attn_w_in': 'new_m', 'new_m_attn_w_out': 'new_m', 'new_m_na_rpb': 'new_m', 'new_m_conv_w_in': 'new_m', 'new_m_conf_dw_w': 'new_m', 'new_m_conf_dw_b': 'new_m', 'new_m_conf_ln_g': 'new_m', 'new_m_conf_ln_b': 'new_m', 'new_m_sconv_w': 'new_m', 'new_m_conv_w_out': 'new_m', 'new_m_ffn_w_up': 'new_m', 'new_m_ffn_dw_w': 'new_m', 'new_m_ffn_w_down': 'new_m', 'new_m_mix_ln_g': 'new_m', 'new_m_mix_ln_b': 'new_m', 'new_m_ffn_ln_g': 'new_m', 'new_m_ffn_ln_b': 'new_m', 'new_v_t5_bias': 'new_v', 'new_v_attn_w_in': 'new_v', 'new_v_attn_w_out': 'new_v', 'new_v_na_rpb': 'new_v', 'new_v_conv_w_in': 'new_v', 'new_v_conf_dw_w': 'new_v', 'new_v_conf_dw_b': 'new_v', 'new_v_conf_ln_g': 'new_v', 'new_v_conf_ln_b': 'new_v', 'new_v_sconv_w': 'new_v', 'new_v_conv_w_out': 'new_v', 'new_v_ffn_w_up': 'new_v', 'new_v_ffn_dw_w': 'new_v', 'new_v_ffn_w_down': 'new_v', 'new_v_mix_ln_g': 'new_v', 'new_v_mix_ln_b': 'new_v', 'new_v_ffn_ln_g': 'new_v', 'new_v_ffn_ln_b': 'new_v'}


def _forward(args):
    return _fwd_reference(*[args[k] for k in FWD_PARAMS])


def _output_shape():
    def fwd():
        inp = _fwd_setup_inputs(0)
        return _fwd_reference(*[inp[k] for k in FWD_PARAMS])
    out = _jax.eval_shape(fwd)
    return out.shape, out.dtype

N_MICROBATCH = 1
ADAM_LR = 0.001
ADAM_B1 = 0.9
ADAM_B2 = 0.999
ADAM_EPS = 1e-08
ADAM_WD = 0.01
ADAM_STEP = 10
PER_EXAMPLE_BATCH_AXIS = {'x': 0, 'loss_target': 0}
SHARED_INPUTS = []
_WEIGHT_DTYPES = {'t5_bias': _jnp.float32, 'attn_w_in': _jnp.float32, 'attn_w_out': _jnp.float32, 'na_rpb': _jnp.float32, 'conv_w_in': _jnp.float32, 'conf_dw_w': _jnp.float32, 'conf_dw_b': _jnp.float32, 'conf_ln_g': _jnp.float32, 'conf_ln_b': _jnp.float32, 'sconv_w': _jnp.float32, 'conv_w_out': _jnp.float32, 'ffn_w_up': _jnp.float32, 'ffn_dw_w': _jnp.float32, 'ffn_w_down': _jnp.float32, 'mix_ln_g': _jnp.float32, 'mix_ln_b': _jnp.float32, 'ffn_ln_g': _jnp.float32, 'ffn_ln_b': _jnp.float32}
MOMENT_SCALE = {'t5_bias': 1.666119e-02, 'attn_w_in': 1.163663e-02, 'attn_w_out': 2.900246e-02, 'na_rpb': 4.701499e-03, 'conv_w_in': 5.257941e-02, 'conf_dw_w': 3.964002e-02, 'conf_dw_b': 1.118252e-01, 'conf_ln_g': 5.040500e-02, 'conf_ln_b': 6.066012e-02, 'sconv_w': 6.211712e-02, 'conv_w_out': 1.281195e-01, 'ffn_w_up': 2.474281e-02, 'ffn_dw_w': 2.465319e-02, 'ffn_w_down': 9.601732e-02, 'mix_ln_g': 8.857335e-01, 'mix_ln_b': 4.608281e-01, 'ffn_ln_g': 3.199806e+01, 'ffn_ln_b': 1.584415e+00}


def _to_microbatches(a, axis):
    t = _jnp.moveaxis(a, axis, 0)
    t = t.reshape((N_MICROBATCH, t.shape[0] // N_MICROBATCH) + t.shape[1:])
    return _jnp.moveaxis(t, 1, axis + 1)


def setup_inputs(seed: int = 0) -> dict:
    inp = _fwd_setup_inputs(seed)
    key = _jax.random.fold_in(_jax.random.key(seed), 7919)
    shape, _ = _output_shape()
    out = dict(inp)
    out["loss_target"] = _jax.random.normal(_jax.random.fold_in(key, 0), shape, _jnp.float32)
    for i, name in enumerate(TWIN_WEIGHTS):
        w = inp[name].astype(_jnp.float32)
        if MOMENT_SCALE is None:
            s = _jnp.sqrt(_jnp.mean(_jnp.square(w)) + 1e-30)
        else:
            s = MOMENT_SCALE[name]
        km, kv = _jax.random.split(_jax.random.fold_in(key, i + 1))
        out[name] = w
        out["m_" + name] = s * _jax.random.normal(km, w.shape, _jnp.float32)
        out["v_" + name] = (s * s) * _jax.random.uniform(kv, w.shape, _jnp.float32, 0.5, 1.5)
    if N_MICROBATCH > 1:
        for name, axis in PER_EXAMPLE_BATCH_AXIS.items():
            out[name] = _to_microbatches(out[name], axis)
    return {'x': out['x'], 't5_bias': out['t5_bias'], 'attn_w_in': out['attn_w_in'], 'attn_w_out': out['attn_w_out'], 'na_rpb': out['na_rpb'], 'conv_w_in': out['conv_w_in'], 'conf_dw_w': out['conf_dw_w'], 'conf_dw_b': out['conf_dw_b'], 'conf_ln_g': out['conf_ln_g'], 'conf_ln_b': out['conf_ln_b'], 'sconv_w': out['sconv_w'], 'conv_w_out': out['conv_w_out'], 'ffn_w_up': out['ffn_w_up'], 'ffn_dw_w': out['ffn_dw_w'], 'ffn_w_down': out['ffn_w_down'], 'mix_ln_g': out['mix_ln_g'], 'mix_ln_b': out['mix_ln_b'], 'ffn_ln_g': out['ffn_ln_g'], 'ffn_ln_b': out['ffn_ln_b'], 'loss_target': out['loss_target'], 'm_t5_bias': out['m_t5_bias'], 'm_attn_w_in': out['m_attn_w_in'], 'm_attn_w_out': out['m_attn_w_out'], 'm_na_rpb': out['m_na_rpb'], 'm_conv_w_in': out['m_conv_w_in'], 'm_conf_dw_w': out['m_conf_dw_w'], 'm_conf_dw_b': out['m_conf_dw_b'], 'm_conf_ln_g': out['m_conf_ln_g'], 'm_conf_ln_b': out['m_conf_ln_b'], 'm_sconv_w': out['m_sconv_w'], 'm_conv_w_out': out['m_conv_w_out'], 'm_ffn_w_up': out['m_ffn_w_up'], 'm_ffn_dw_w': out['m_ffn_dw_w'], 'm_ffn_w_down': out['m_ffn_w_down'], 'm_mix_ln_g': out['m_mix_ln_g'], 'm_mix_ln_b': out['m_mix_ln_b'], 'm_ffn_ln_g': out['m_ffn_ln_g'], 'm_ffn_ln_b': out['m_ffn_ln_b'], 'v_t5_bias': out['v_t5_bias'], 'v_attn_w_in': out['v_attn_w_in'], 'v_attn_w_out': out['v_attn_w_out'], 'v_na_rpb': out['v_na_rpb'], 'v_conv_w_in': out['v_conv_w_in'], 'v_conf_dw_w': out['v_conf_dw_w'], 'v_conf_dw_b': out['v_conf_dw_b'], 'v_conf_ln_g': out['v_conf_ln_g'], 'v_conf_ln_b': out['v_conf_ln_b'], 'v_sconv_w': out['v_sconv_w'], 'v_conv_w_out': out['v_conv_w_out'], 'v_ffn_w_up': out['v_ffn_w_up'], 'v_ffn_dw_w': out['v_ffn_dw_w'], 'v_ffn_w_down': out['v_ffn_w_down'], 'v_mix_ln_g': out['v_mix_ln_g'], 'v_mix_ln_b': out['v_mix_ln_b'], 'v_ffn_ln_g': out['v_ffn_ln_g'], 'v_ffn_ln_b': out['v_ffn_ln_b']}


def _loss(weights, diff, rest, loss_target):
    with _jax.named_scope("forward"):
        args = {**rest, TWIN_DIFF_INPUT: diff, **{k: w.astype(_WEIGHT_DTYPES[k]) for k, w in weights.items()}}
        y = _forward(args)
    with _jax.named_scope("loss_head"):
        err = _jnp.square(y.astype(_jnp.float32) - loss_target)
        return 0.5 * _jnp.sum(_jnp.mean(err, axis=-1)) if err.ndim else 0.5 * err


def _adamw(w, g, m, v):
    m = ADAM_B1 * m + (1.0 - ADAM_B1) * g
    v = ADAM_B2 * v + (1.0 - ADAM_B2) * _jnp.square(g)
    m_hat = m / (1.0 - ADAM_B1 ** ADAM_STEP)
    v_hat = v / (1.0 - ADAM_B2 ** ADAM_STEP)
    delta = -ADAM_LR * (m_hat / (_jnp.sqrt(v_hat) + ADAM_EPS) + ADAM_WD * w)
    return delta, m, v


def reference(x, t5_bias, attn_w_in, attn_w_out, na_rpb, conv_w_in, conf_dw_w, conf_dw_b, conf_ln_g, conf_ln_b, sconv_w, conv_w_out, ffn_w_up, ffn_dw_w, ffn_w_down, mix_ln_g, mix_ln_b, ffn_ln_g, ffn_ln_b, loss_target, m_t5_bias, m_attn_w_in, m_attn_w_out, m_na_rpb, m_conv_w_in, m_conf_dw_w, m_conf_dw_b, m_conf_ln_g, m_conf_ln_b, m_sconv_w, m_conv_w_out, m_ffn_w_up, m_ffn_dw_w, m_ffn_w_down, m_mix_ln_g, m_mix_ln_b, m_ffn_ln_g, m_ffn_ln_b, v_t5_bias, v_attn_w_in, v_attn_w_out, v_na_rpb, v_conv_w_in, v_conf_dw_w, v_conf_dw_b, v_conf_ln_g, v_conf_ln_b, v_sconv_w, v_conv_w_out, v_ffn_w_up, v_ffn_dw_w, v_ffn_w_down, v_mix_ln_g, v_mix_ln_b, v_ffn_ln_g, v_ffn_ln_b):
    given = dict(x=x, t5_bias=t5_bias, attn_w_in=attn_w_in, attn_w_out=attn_w_out, na_rpb=na_rpb, conv_w_in=conv_w_in, conf_dw_w=conf_dw_w, conf_dw_b=conf_dw_b, conf_ln_g=conf_ln_g, conf_ln_b=conf_ln_b, sconv_w=sconv_w, conv_w_out=conv_w_out, ffn_w_up=ffn_w_up, ffn_dw_w=ffn_dw_w, ffn_w_down=ffn_w_down, mix_ln_g=mix_ln_g, mix_ln_b=mix_ln_b, ffn_ln_g=ffn_ln_g, ffn_ln_b=ffn_ln_b, loss_target=loss_target, m_t5_bias=m_t5_bias, m_attn_w_in=m_attn_w_in, m_attn_w_out=m_attn_w_out, m_na_rpb=m_na_rpb, m_conv_w_in=m_conv_w_in, m_conf_dw_w=m_conf_dw_w, m_conf_dw_b=m_conf_dw_b, m_conf_ln_g=m_conf_ln_g, m_conf_ln_b=m_conf_ln_b, m_sconv_w=m_sconv_w, m_conv_w_out=m_conv_w_out, m_ffn_w_up=m_ffn_w_up, m_ffn_dw_w=m_ffn_dw_w, m_ffn_w_down=m_ffn_w_down, m_mix_ln_g=m_mix_ln_g, m_mix_ln_b=m_mix_ln_b, m_ffn_ln_g=m_ffn_ln_g, m_ffn_ln_b=m_ffn_ln_b, v_t5_bias=v_t5_bias, v_attn_w_in=v_attn_w_in, v_attn_w_out=v_attn_w_out, v_na_rpb=v_na_rpb, v_conv_w_in=v_conv_w_in, v_conf_dw_w=v_conf_dw_w, v_conf_dw_b=v_conf_dw_b, v_conf_ln_g=v_conf_ln_g, v_conf_ln_b=v_conf_ln_b, v_sconv_w=v_sconv_w, v_conv_w_out=v_conv_w_out, v_ffn_w_up=v_ffn_w_up, v_ffn_dw_w=v_ffn_dw_w, v_ffn_w_down=v_ffn_w_down, v_mix_ln_g=v_mix_ln_g, v_mix_ln_b=v_mix_ln_b, v_ffn_ln_g=v_ffn_ln_g, v_ffn_ln_b=v_ffn_ln_b)
    weights = {n: given[n] for n in TWIN_WEIGHTS}
    shared = {n: given[n] for n in SHARED_INPUTS}
    per_example = {n: given[n] for n in ['x']}
    grad_fn = _jax.value_and_grad(_loss, argnums=(0, 1))

    def one_microbatch(ex, loss_target):
        ex = dict(ex)
        diff = ex.pop(TWIN_DIFF_INPUT)
        return grad_fn(weights, diff, {**shared, **ex}, loss_target)

    if N_MICROBATCH == 1:
        loss, (grad_w, grad_x) = one_microbatch(per_example, given["loss_target"])
    else:
        def body(carry, xs):
            loss_sum, grad_sum = carry
            l_k, (gw_k, gx_k) = one_microbatch(xs[0], xs[1])
            with _jax.named_scope("update"):
                return (loss_sum + l_k, _jax.tree.map(_jnp.add, grad_sum, gw_k)), gx_k

        init = (_jnp.zeros((), _jnp.float32), _jax.tree.map(_jnp.zeros_like, weights))
        (loss, grad_w), grad_x = _jax.lax.scan(body, init, (per_example, given["loss_target"]))
    with _jax.named_scope("update"):
        delta_w, new_m, new_v = {}, {}, {}
        for n in TWIN_WEIGHTS:
            delta_w[n], new_m[n], new_v[n] = _adamw(weights[n], grad_w[n], given["m_" + n], given["v_" + n])
    return (loss, grad_x, *[grad_w[n] for n in TWIN_WEIGHTS], *[delta_w[n] for n in TWIN_WEIGHTS],
            *[new_m[n] for n in TWIN_WEIGHTS], *[new_v[n] for n in TWIN_WEIGHTS])
```

```python
import functools
import math

import jax
import jax.numpy as jnp
import numpy as np
from jax import lax
from jax.experimental import pallas as pl
from jax.experimental.pallas import tpu as pltpu

F32 = jnp.float32
BF16 = jnp.bfloat16

N_DEV = 8
MESH_AXES = ("x", "y", "c")
DEPTH = 4
GRID_W = 64
GRID_SHIFT = 6
HEAD_DIM = 64
NA_KH = 8
NA_KW = 16
NA_QROWS = 4
DIL_PATTERNS = ((128, 1), (512, 4), (2048, 16))
DIL_HALF = 64
DIL_TQ = 128
N_BUCKETS = 32
T5_MAX_DIST = 1024
CONF_CH = 512
CONF_K = 31
SC_K = 3
FFN_K = 3
LN_EPS = 1e-5
NEG = -1e30
ALPHA = (2 * DEPTH) ** 0.25
ADAM_LR = 0.001
ADAM_B1 = 0.9
ADAM_B2 = 0.999
ADAM_EPS = 1e-08
ADAM_WD = 0.01
ADAM_STEP = 10

LANES = 128
SUBLANES = 8
VMEM_BIG = 48 * 1024 * 1024
FLAT_COLS = 1024
FLAT_ROW_TILE = 256

WEIGHTS = ['t5_bias', 'attn_w_in', 'attn_w_out', 'na_rpb', 'conv_w_in', 'conf_dw_w', 'conf_dw_b', 'conf_ln_g',
           'conf_ln_b', 'sconv_w', 'conv_w_out', 'ffn_w_up', 'ffn_dw_w', 'ffn_w_down', 'mix_ln_g', 'mix_ln_b',
           'ffn_ln_g', 'ffn_ln_b']
SHARD_AXIS = {'attn_w_in': 2, 'attn_w_out': 1, 'conv_w_in': 2, 'conf_dw_w': 2, 'conf_dw_b': 1, 'conf_ln_g': 1,
              'conf_ln_b': 1, 'sconv_w': 2, 'conv_w_out': 1, 'ffn_w_up': 2, 'ffn_dw_w': 2, 'ffn_w_down': 1}
MATMUL_WEIGHTS = ['attn_w_in', 'attn_w_out', 'conv_w_in', 'conv_w_out', 'ffn_w_up', 'ffn_w_down']
SMALL_SHARDED = ['conf_dw_w', 'conf_dw_b', 'conf_ln_g', 'conf_ln_b', 'sconv_w', 'ffn_dw_w']
SHARDED = MATMUL_WEIGHTS + SMALL_SHARDED
REPLICATED = ['t5_bias', 'na_rpb', 'mix_ln_g', 'mix_ln_b', 'ffn_ln_g', 'ffn_ln_b']


def _tile(n, cands):
    for c in cands:
        if n % c == 0:
            return c
    return n


def _params(sem, vmem=None):
    return pltpu.CompilerParams(dimension_semantics=sem, vmem_limit_bytes=vmem)


def _sigmoid(x):
    return 1.0 / (1.0 + jnp.exp(-x))


def _mm(a, b, *, ta=False, tb=False, name):
    m, k = (a.shape[1], a.shape[0]) if ta else a.shape
    n = b.shape[0] if tb else b.shape[1]
    tm, tn, tk = _tile(m, (512, 256, 128)), _tile(n, (512, 256, 128)), _tile(k, (512, 256, 128))
    nk = k // tk
    dims = (((0 if ta else 1,), (1 if tb else 0,)), ((), ()))

    def body(a_ref, b_ref, o_ref, acc_ref):
        kk = pl.program_id(2)

        @pl.when(kk == 0)
        def _():
            acc_ref[...] = jnp.zeros_like(acc_ref)

        acc_ref[...] += lax.dot_general(a_ref[...].astype(BF16), b_ref[...].astype(BF16), dims,
                                        preferred_element_type=F32)

        @pl.when(kk == nk - 1)
        def _():
            o_ref[...] = acc_ref[...]

    a_spec = pl.BlockSpec((tk, tm), lambda i, j, q: (q, i)) if ta else pl.BlockSpec((tm, tk), lambda i, j, q: (i, q))
    b_spec = pl.BlockSpec((tn, tk), lambda i, j, q: (j, q)) if tb else pl.BlockSpec((tk, tn), lambda i, j, q: (q, j))
    return pl.pallas_call(
        body, name=name, grid=(m // tm, n // tn, nk), in_specs=[a_spec, b_spec],
        out_specs=pl.BlockSpec((tm, tn), lambda i, j, q: (i, j)),
        out_shape=jax.ShapeDtypeStruct((m, n), F32), scratch_shapes=[pltpu.VMEM((tm, tn), F32)],
        compiler_params=_params(("parallel", "parallel", "arbitrary")))(a, b)


def _ln_stats(z):
    mu = jnp.mean(z, axis=-1, keepdims=True)
    zc = z - mu
    var = jnp.mean(zc * zc, axis=-1, keepdims=True)
    rstd = lax.rsqrt(var + LN_EPS)
    return zc * rstd, rstd


def _ln_fwd(x, y, g, b, *, name):
    s, d = x.shape
    t = _tile(s, (256, 128, 64, 8))

    def body(x_ref, y_ref, g_ref, b_ref, o_ref):
        xh, _ = _ln_stats(ALPHA * x_ref[...] + y_ref[...])
        o_ref[...] = xh * g_ref[...] + b_ref[...]

    row = pl.BlockSpec((t, d), lambda i: (i, 0))
    vec = pl.BlockSpec((1, d), lambda i: (0, 0))
    return pl.pallas_call(body, name=name, grid=(s // t,), in_specs=[row, row, vec, vec], out_specs=row,
                          out_shape=jax.ShapeDtypeStruct((s, d), F32),
                          compiler_params=_params(("parallel",)))(x, y, g.reshape(1, d), b.reshape(1, d))


def _ln_bwd(x, y, g, d1, d2, *, name):
    s, d = x.shape
    t = _tile(s, (256, 128, 64, 8))
    two = d2 is not None

    def body(*refs):
        if two:
            x_ref, y_ref, g_ref, d1_ref, d2_ref, dz_ref, dg_ref, db_ref = refs
            dout = ALPHA * d1_ref[...] + d2_ref[...]
        else:
            x_ref, y_ref, g_ref, d1_ref, dz_ref, dg_ref, db_ref = refs
            dout = d1_ref[...]

        @pl.when(pl.program_id(0) == 0)
        def _():
            dg_ref[...] = jnp.zeros_like(dg_ref)
            db_ref[...] = jnp.zeros_like(db_ref)

        xh, rstd = _ln_stats(ALPHA * x_ref[...] + y_ref[...])
        dxh = dout * g_ref[...]
        dz_ref[...] = rstd * (dxh - jnp.mean(dxh, axis=-1, keepdims=True)
                              - xh * jnp.mean(dxh * xh, axis=-1, keepdims=True))
        dg_ref[...] += jnp.sum(dout * xh, axis=0, keepdims=True)
        db_ref[...] += jnp.sum(dout, axis=0, keepdims=True)

    row = pl.BlockSpec((t, d), lambda i: (i, 0))
    vec = pl.BlockSpec((1, d), lambda i: (0, 0))
    ins = [x, y, g.reshape(1, d), d1] + ([d2] if two else [])
    specs = [row, row, vec, row] + ([row] if two else [])
    dz, dg, db = pl.pallas_call(
        body, name=name, grid=(s // t,), in_specs=specs, out_specs=[row, vec, vec],
        out_shape=[jax.ShapeDtypeStruct((s, d), F32), jax.ShapeDtypeStruct((1, d), F32),
                   jax.ShapeDtypeStruct((1, d), F32)],
        compiler_params=_params(("arbitrary",)))(*ins)
    return dz, dg.reshape(d), db.reshape(d)


def _axpy(d1, d2, *, name):
    s, d = d1.shape
    t = _tile(s, (256, 128, 64, 8))

    def body(a_ref, b_ref, o_ref):
        o_ref[...] = ALPHA * a_ref[...] + b_ref[...]

    row = pl.BlockSpec((t, d), lambda i: (i, 0))
    return pl.pallas_call(body, name=name, grid=(s // t,), in_specs=[row, row], out_specs=row,
                          out_shape=jax.ShapeDtypeStruct((s, d), F32), compiler_params=_params(("parallel",)))(d1, d2)


def _loss_head(y, tgt, *, name):
    s, d = y.shape
    t = _tile(s, (256, 128, 64, 8))

    def body(y_ref, t_ref, l_ref, dy_ref):
        @pl.when(pl.program_id(0) == 0)
        def _():
            l_ref[...] = jnp.zeros_like(l_ref)

        err = y_ref[...] - t_ref[...]
        dy_ref[...] = err * (1.0 / d)
        l_ref[...] += 0.5 * jnp.sum(jnp.sum(err * err, axis=1, keepdims=True), axis=0, keepdims=True) * (1.0 / d)

    row = pl.BlockSpec((t, d), lambda i: (i, 0))
    one = pl.BlockSpec((SUBLANES, LANES), lambda i: (0, 0))
    loss, dy = pl.pallas_call(
        body, name=name, grid=(s // t,), in_specs=[row, row], out_specs=[one, row],
        out_shape=[jax.ShapeDtypeStruct((SUBLANES, LANES), F32), jax.ShapeDtypeStruct((s, d), F32)],
        compiler_params=_params(("arbitrary",)))(y, tgt)
    return loss[0, 0], dy


def _halo_specs(s, t, halo, cb, col):
    per = t // halo
    last = s // halo - 1
    return [pl.BlockSpec((t, cb), lambda j, i: (i, col(j))),
            pl.BlockSpec((halo, cb), lambda j, i: (jnp.maximum(i * per - 1, 0), col(j))),
            pl.BlockSpec((halo, cb), lambda j, i: (jnp.minimum((i + 1) * per, last), col(j)))]


def _extended(main_ref, prev_ref, next_ref, i, n):
    prev = jnp.where(i > 0, prev_ref[...], 0.0)
    nxt = jnp.where(i < n - 1, next_ref[...], 0.0)
    return jnp.concatenate([prev, main_ref[...], nxt], axis=0)


def _shift(ext, o):
    if o == 0:
        return ext
    return pltpu.roll(ext, (-o) % ext.shape[0], 0)


def _conv(ext, w_ref, k, sign=1):
    acc = None
    for j in range(k):
        term = w_ref[j:j + 1, :] * _shift(ext, sign * (j - k // 2))
        acc = term if acc is None else acc + term
    return acc


def _conv_wgrad(dw_ref, d_main, x_ext, k, halo, t):
    for j in range(k):
        xs = _shift(x_ext, j - k // 2)[halo:halo + t]
        dw_ref[j:j + 1, :] += jnp.sum(d_main * xs, axis=0, keepdims=True)


def _ffn_mid_fwd(hu, w, *, name):
    s, f2 = hu.shape
    f = f2 // 2
    t, cb, halo = _tile(s, (512, 256, 128)), _tile(f, (256, 128)), SUBLANES
    nt, nc = s // t, f // cb

    def body(g_ref, gp_ref, gn_ref, u_ref, up_ref, un_ref, wg_ref, wu_ref, a_ref):
        i = pl.program_id(1)
        hg = _conv(_extended(g_ref, gp_ref, gn_ref, i, nt), wg_ref, FFN_K)[halo:halo + t]
        hu_ = _conv(_extended(u_ref, up_ref, un_ref, i, nt), wu_ref, FFN_K)[halo:halo + t]
        a_ref[...] = hg * _sigmoid(hg) * hu_

    specs = (_halo_specs(s, t, halo, cb, lambda j: j) + _halo_specs(s, t, halo, cb, lambda j: j + nc)
             + [pl.BlockSpec((FFN_K, cb), lambda j, i: (0, j)), pl.BlockSpec((FFN_K, cb), lambda j, i: (0, j + nc))])
    return pl.pallas_call(body, name=name, grid=(nc, nt), in_specs=specs,
                          out_specs=pl.BlockSpec((t, cb), lambda j, i: (i, j)),
                          out_shape=jax.ShapeDtypeStruct((s, f), F32),
                          compiler_params=_params(("parallel", "parallel")))(hu, hu, hu, hu, hu, hu, w, w)


def _ffn_mid_bwd(hu, w, da, *, name):
    s, f2 = hu.shape
    f = f2 // 2
    t, cb, halo = _tile(s, (512, 256, 128)), _tile(f, (256, 128)), SUBLANES
    nt, nc = s // t, f // cb

    def body(g_ref, gp_ref, gn_ref, u_ref, up_ref, un_ref, a_ref, ap_ref, an_ref, wg_ref, wu_ref,
             dg_ref, du_ref, dwg_ref, dwu_ref):
        i = pl.program_id(1)

        @pl.when(i == 0)
        def _():
            dwg_ref[...] = jnp.zeros_like(dwg_ref)
            dwu_ref[...] = jnp.zeros_like(dwu_ref)

        xg = _extended(g_ref, gp_ref, gn_ref, i, nt)
        xu = _extended(u_ref, up_ref, un_ref, i, nt)
        dae = _extended(a_ref, ap_ref, an_ref, i, nt)
        hg = _conv(xg, wg_ref, FFN_K)
        hu_ = _conv(xu, wu_ref, FFN_K)
        sg = _sigmoid(hg)
        d_hg = dae * hu_ * (sg * (1.0 + hg * (1.0 - sg)))
        d_hu = dae * (hg * sg)
        dg_ref[...] = _conv(d_hg, wg_ref, FFN_K, sign=-1)[halo:halo + t]
        du_ref[...] = _conv(d_hu, wu_ref, FFN_K, sign=-1)[halo:halo + t]
        _conv_wgrad(dwg_ref, d_hg[halo:halo + t], xg, FFN_K, halo, t)
        _conv_wgrad(dwu_ref, d_hu[halo:halo + t], xu, FFN_K, halo, t)

    wspec = lambda off: pl.BlockSpec((FFN_K, cb), lambda j, i: (0, j + off))
    specs = (_halo_specs(s, t, halo, cb, lambda j: j) + _halo_specs(s, t, halo, cb, lambda j: j + nc)
             + _halo_specs(s, t, halo, cb, lambda j: j) + [wspec(0), wspec(nc)])
    tile = pl.BlockSpec((t, cb), lambda j, i: (i, j))
    dg, du, dwg, dwu = pl.pallas_call(
        body, name=name, grid=(nc, nt), in_specs=specs, out_specs=[tile, tile, wspec(0), wspec(0)],
        out_shape=[jax.ShapeDtypeStruct((s, f), F32), jax.ShapeDtypeStruct((s, f), F32),
                   jax.ShapeDtypeStruct((FFN_K, f), F32), jax.ShapeDtypeStruct((FFN_K, f), F32)],
        compiler_params=_params(("parallel", "arbitrary")))(hu, hu, hu, hu, hu, hu, da, da, da, w, w)
    return jnp.concatenate([dg, du], axis=1), jnp.concatenate([dwg, dwu], axis=1)


CONV_HALO = 16


def _conv_mid_fwd(h, dw_w, dw_b, ln_g, ln_b, sc_w, *, name):
    s = h.shape[0]
    c = CONF_CH
    t, halo = _tile(s, (256, 128)), CONV_HALO
    nt = s // t

    def body(ca, cap, can, cg, cgp, cgn, gb, gc, gcp, gcn, hx, hxp, hxn, w31, b31, lg, lb, w3, o_ref, u2_ref):
        i = pl.program_id(1)
        u1 = _extended(ca, cap, can, i, nt) * _sigmoid(_extended(cg, cgp, cgn, i, nt))
        u2 = _conv(u1, w31, CONF_K)[halo:halo + t] + b31[...]
        u2_ref[...] = u2
        xh, _ = _ln_stats(u2)
        yl = xh * lg[...] + lb[...]
        o_ref[:, 0:c] = yl * _sigmoid(yl)
        p = _extended(gc, gcp, gcn, i, nt) * _extended(hx, hxp, hxn, i, nt)
        o_ref[:, c:2 * c] = gb[...] * _conv(p, w3, SC_K)[halo:halo + t]

    hs = lambda blk: _halo_specs(s, t, halo, c, lambda j: blk)
    vec = lambda r: pl.BlockSpec((r, c), lambda j, i: (0, 0))
    specs = hs(0) + hs(1) + hs(2)[:1] + hs(3) + hs(4) + [vec(CONF_K), vec(1), vec(1), vec(1), vec(SC_K)]
    return pl.pallas_call(
        body, name=name, grid=(1, nt), in_specs=specs,
        out_specs=[pl.BlockSpec((t, 2 * c), lambda j, i: (i, 0)), pl.BlockSpec((t, c), lambda j, i: (i, 0))],
        out_shape=[jax.ShapeDtypeStruct((s, 2 * c), F32), jax.ShapeDtypeStruct((s, c), F32)],
        compiler_params=_params(("parallel", "parallel")))(
            h, h, h, h, h, h, h, h, h, h, h, h, h, dw_w, dw_b.reshape(1, c), ln_g.reshape(1, c),
            ln_b.reshape(1, c), sc_w)


def _conv_mid_bwd(h, u2, dm, dw_w, ln_g, ln_b, sc_w, *, name):
    s = h.shape[0]
    c = CONF_CH
    t, halo = _tile(s, (256, 128)), CONV_HALO
    nt = s // t

    def body(ca, cap, can, cg, cgp, cgn, gb, gbp, gbn, gc, gcp, gcn, hx, hxp, hxn, u2r, u2p, u2n,
             du, dup, dun, dz, dzp, dzn, w31, lg, lb, w3,
             dh_ref, dw31_ref, db31_ref, dlg_ref, dlb_ref, dw3_ref):
        i = pl.program_id(1)

        @pl.when(i == 0)
        def _():
            for r in (dw31_ref, db31_ref, dlg_ref, dlb_ref, dw3_ref):
                r[...] = jnp.zeros_like(r)

        main = slice(halo, halo + t)
        xh, rstd = _ln_stats(_extended(u2r, u2p, u2n, i, nt))
        yl = xh * lg[...] + lb[...]
        sg = _sigmoid(yl)
        d_yl = _extended(du, dup, dun, i, nt) * (sg * (1.0 + yl * (1.0 - sg)))
        dlg_ref[...] += jnp.sum((d_yl * xh)[main], axis=0, keepdims=True)
        dlb_ref[...] += jnp.sum(d_yl[main], axis=0, keepdims=True)
        dxh = d_yl * lg[...]
        du2 = rstd * (dxh - jnp.mean(dxh, axis=-1, keepdims=True) - xh * jnp.mean(dxh * xh, axis=-1, keepdims=True))
        db31_ref[...] += jnp.sum(du2[main], axis=0, keepdims=True)
        cae = _extended(ca, cap, can, i, nt)
        sc = _sigmoid(_extended(cg, cgp, cgn, i, nt))
        u1 = cae * sc
        _conv_wgrad(dw31_ref, du2[main], u1, CONF_K, halo, t)
        du1 = _conv(du2, w31, CONF_K, sign=-1)[main]
        dh_ref[:, 0:c] = du1 * sc[main]
        dh_ref[:, c:2 * c] = du1 * (cae * sc * (1.0 - sc))[main]
        gce = _extended(gc, gcp, gcn, i, nt)
        hxe = _extended(hx, hxp, hxn, i, nt)
        p = gce * hxe
        dze = _extended(dz, dzp, dzn, i, nt)
        d_c3 = dze * _extended(gb, gbp, gbn, i, nt)
        dh_ref[:, 2 * c:3 * c] = dze[main] * _conv(p, w3, SC_K)[main]
        _conv_wgrad(dw3_ref, d_c3[main], p, SC_K, halo, t)
        dp = _conv(d_c3, w3, SC_K, sign=-1)[main]
        dh_ref[:, 3 * c:4 * c] = dp * hxe[main]
        dh_ref[:, 4 * c:5 * c] = dp * gce[main]

    hs = lambda blk: _halo_specs(s, t, halo, c, lambda j: blk)
    vec = lambda r: pl.BlockSpec((r, c), lambda j, i: (0, 0))
    specs = (hs(0) + hs(1) + hs(2) + hs(3) + hs(4) + hs(0) + hs(0) + hs(1)
             + [vec(CONF_K), vec(1), vec(1), vec(SC_K)])
    outs = pl.pallas_call(
        body, name=name, grid=(1, nt), in_specs=specs,
        out_specs=[pl.BlockSpec((t, 5 * c), lambda j, i: (i, 0)), vec(CONF_K), vec(1), vec(1), vec(1), vec(SC_K)],
        out_shape=[jax.ShapeDtypeStruct((s, 5 * c), F32), jax.ShapeDtypeStruct((CONF_K, c), F32),
                   jax.ShapeDtypeStruct((1, c), F32), jax.ShapeDtypeStruct((1, c), F32),
                   jax.ShapeDtypeStruct((1, c), F32), jax.ShapeDtypeStruct((SC_K, c), F32)],
        compiler_params=_params(("arbitrary", "arbitrary")))(
            h, h, h, h, h, h, h, h, h, h, h, h, h, h, h, u2, u2, u2, dm, dm, dm, dm, dm, dm,
            dw_w, ln_g.reshape(1, c), ln_b.reshape(1, c), sc_w)
    dh, dw31, db31, dlg, dlb, dw3 = outs
    return dh, dw31, db31.reshape(c), dlg.reshape(c), dlb.reshape(c), dw3


def _attn_mask(kind, n, tq, length):
    iq = lax.broadcasted_iota(jnp.int32, (tq, 1), 0)
    ik = lax.broadcasted_iota(jnp.int32, (1, 3 * tq), 1)
    if kind == "band":
        rel = ik - tq - iq
        kpos = (n - 1) * tq + ik
        return (jnp.abs(rel) <= DIL_HALF) & (kpos >= 0) & (kpos < length)
    rows = length // GRID_W
    rq = n * NA_QROWS + (iq >> GRID_SHIFT)
    cq = iq & (GRID_W - 1)
    rk = (n - 1) * NA_QROWS + (ik >> GRID_SHIFT)
    ck = ik & (GRID_W - 1)
    r0 = jnp.clip(rq - NA_KH // 2, 0, rows - NA_KH)
    c0 = jnp.clip(cq - NA_KW // 2, 0, GRID_W - NA_KW)
    return (rk >= r0) & (rk < r0 + NA_KH) & (ck >= c0) & (ck < c0 + NA_KW)


def _attn_specs(length, d, tq, w128, cols):
    nb = length // tq
    qc, kc, vc = cols
    blk = lambda col, off: pl.BlockSpec(
        (tq, LANES), lambda hp, r, n: (jnp.clip(n + off, 0, nb - 1), r * w128 + col + hp))
    return [blk(qc, 0)] + [blk(kc, o) for o in (-1, 0, 1)] + [blk(vc, o) for o in (-1, 0, 1)]


def _attn_fwd(hv, bias, *, kind, d, tq, w128, cols, name):
    length = hv.shape[0]
    nb = length // tq
    scale = HEAD_DIM ** -0.5

    def body(q_ref, k0, k1, k2, v0, v1, v2, b_ref, o_ref, l_ref):
        n = pl.program_id(2)
        q = q_ref[...] * scale
        kcat = jnp.concatenate([k0[...], k1[...], k2[...]], axis=0).astype(BF16)
        vcat = jnp.concatenate([v0[...], v1[...], v2[...]], axis=0).astype(BF16)
        mask = _attn_mask(kind, n, tq, length)
        low = lax.broadcasted_iota(jnp.int32, (1, LANES), 1) < HEAD_DIM
        outs, lses = [], []
        for hh in range(2):
            qh = jnp.where(low if hh == 0 else ~low, q, 0.0).astype(BF16)
            sc = lax.dot_general(qh, kcat, (((1,), (1,)), ((), ())), preferred_element_type=F32) + b_ref[hh]
            sc = jnp.where(mask, sc, NEG)
            m = jnp.max(sc, axis=1, keepdims=True)
            p = jnp.exp(sc - m)
            den = jnp.sum(p, axis=1, keepdims=True)
            outs.append(jnp.dot((p / den).astype(BF16), vcat, preferred_element_type=F32))
            lses.append(m + jnp.log(den))
        o_ref[...] = jnp.where(low, outs[0], outs[1])
        l_ref[...] = jnp.where(low, lses[0], lses[1])

    out = pl.BlockSpec((tq, LANES), lambda hp, r, n: (n, r * 4 + hp))
    specs = _attn_specs(length, d, tq, w128, cols) + [pl.BlockSpec((2, tq, 3 * tq), lambda hp, r, n: (hp, 0, 0))]
    shape = jax.ShapeDtypeStruct((length, d * 4 * LANES), F32)
    return pl.pallas_call(body, name=name, grid=(4, d, nb), in_specs=specs, out_specs=[out, out],
                          out_shape=[shape, shape],
                          compiler_params=_params(("parallel", "parallel", "parallel")))(
                              hv, hv, hv, hv, hv, hv, hv, bias)


def _attn_bwd(hv, bias, dyv, yv, lse, *, kind, d, tq, w128, cols, yw128, ycol, name):
    length = hv.shape[0]
    nb = length // tq
    scale = HEAD_DIM ** -0.5

    def body(q_ref, k0, k1, k2, v0, v1, v2, b_ref, dy_ref, y_ref, l_ref, dq_ref, dk_ref, dv_ref, db_ref):
        r, n = pl.program_id(1), pl.program_id(2)

        @pl.when(n == 0)
        def _():
            dk_ref[...] = jnp.zeros_like(dk_ref)
            dv_ref[...] = jnp.zeros_like(dv_ref)

        @pl.when((n == 0) & (r == 0))
        def _():
            db_ref[...] = jnp.zeros_like(db_ref)

        q = q_ref[...] * scale
        kcat = jnp.concatenate([k0[...], k1[...], k2[...]], axis=0).astype(BF16)
        vcat = jnp.concatenate([v0[...], v1[...], v2[...]], axis=0).astype(BF16)
        mask = _attn_mask(kind, n, tq, length)
        low = lax.broadcasted_iota(jnp.int32, (1, LANES), 1) < HEAD_DIM
        dy = dy_ref[...]
        dyy = dy * y_ref[...]
        lse_all = l_ref[...]
        dq = jnp.zeros((tq, LANES), F32)
        dk = jnp.zeros((3 * tq, LANES), F32)
        dv = jnp.zeros((3 * tq, LANES), F32)
        for hh in range(2):
            sel = low if hh == 0 else ~low
            qh = jnp.where(sel, q, 0.0).astype(BF16)
            dyh = jnp.where(sel, dy, 0.0).astype(BF16)
            dsum = jnp.sum(jnp.where(sel, dyy, 0.0), axis=1, keepdims=True)
            lse_h = lse_all[:, hh * HEAD_DIM:hh * HEAD_DIM + 1]
            sc = lax.dot_general(qh, kcat, (((1,), (1,)), ((), ())), preferred_element_type=F32) + b_ref[hh]
            p = jnp.where(mask, jnp.exp(jnp.where(mask, sc, NEG) - lse_h), 0.0)
            dp = lax.dot_general(dyh, vcat, (((1,), (1,)), ((), ())), preferred_element_type=F32)
            ds = p * (dp - dsum)
            db_ref[hh] += ds
            pb, dsb = p.astype(BF16), ds.astype(BF16)
            dv += lax.dot_general(pb, dyh, (((0,), (0,)), ((), ())), preferred_element_type=F32)
            dk += lax.dot_general(dsb, qh, (((0,), (0,)), ((), ())), preferred_element_type=F32)
            dq += jnp.where(sel, jnp.dot(dsb, kcat, preferred_element_type=F32), 0.0)
        dq_ref[...] = dq * scale
        for j in range(3):
            start = pl.multiple_of(jnp.clip(n - 1 + j, 0, nb - 1) * tq, tq)
            dk_ref[pl.ds(start, tq), :] += dk[j * tq:(j + 1) * tq]
            dv_ref[pl.ds(start, tq), :] += dv[j * tq:(j + 1) * tq]

    blk = pl.BlockSpec((tq, LANES), lambda hp, r, n: (n, r * 4 + hp))
    yblk = pl.BlockSpec((tq, LANES), lambda hp, r, n: (n, r * yw128 + ycol + hp))
    slab = pl.BlockSpec((length, LANES), lambda hp, r, n: (0, r * 4 + hp))
    bspec = pl.BlockSpec((2, tq, 3 * tq), lambda hp, r, n: (hp, 0, 0))
    specs = _attn_specs(length, d, tq, w128, cols) + [bspec, yblk, yblk, blk]
    shape = jax.ShapeDtypeStruct((length, d * 4 * LANES), F32)
    return pl.pallas_call(
        body, name=name, grid=(4, d, nb), in_specs=specs, out_specs=[blk, slab, slab, bspec],
        out_shape=[shape, shape, shape, jax.ShapeDtypeStruct(bias.shape, F32)],
        compiler_params=_params(("arbitrary", "arbitrary", "arbitrary"), VMEM_BIG))(
            hv, hv, hv, hv, hv, hv, hv, bias, dyv, yv, lse)


def _dil_combine(outs, lses, *, name):
    s, c = outs[0].shape
    t = _tile(s, (512, 256, 128, 64, 8))

    def body(o0, o1, o2, l0, l1, l2, y_ref, lt_ref):
        ls = [l0[...], l1[...], l2[...]]
        m = jnp.maximum(jnp.maximum(ls[0], ls[1]), ls[2])
        es = [jnp.exp(l - m) for l in ls]
        den = es[0] + es[1] + es[2]
        y_ref[...] = (es[0] / den) * o0[...] + (es[1] / den) * o1[...] + (es[2] / den) * o2[...]
        lt_ref[...] = m + jnp.log(den)

    row = pl.BlockSpec((t, c), lambda i: (i, 0))
    shape = jax.ShapeDtypeStruct((s, c), F32)
    return pl.pallas_call(body, name=name, grid=(s // t,), in_specs=[row] * 6, out_specs=[row, row],
                          out_shape=[shape, shape], compiler_params=_params(("parallel",)))(*outs, *lses)


def _attn_dh(na, dil, *, name):
    s, c = na[0].shape
    t = _tile(s, (256, 128, 64, 8))

    def body(*refs):
        ins, o_ref = refs[:-1], refs[-1]
        for a in range(3):
            o_ref[:, a * c:(a + 1) * c] = ins[a][...]
            o_ref[:, (3 + a) * c:(4 + a) * c] = ins[3 + a][...] + ins[6 + a][...] + ins[9 + a][...]

    row = pl.BlockSpec((t, c), lambda i: (i, 0))
    flat = list(na) + [g[a] for g in dil for a in range(3)]
    return pl.pallas_call(body, name=name, grid=(s // t,), in_specs=[row] * 12,
                          out_specs=pl.BlockSpec((t, 6 * c), lambda i: (i, 0)),
                          out_shape=jax.ShapeDtypeStruct((s, 6 * c), F32),
                          compiler_params=_params(("parallel",)))(*flat)


def _t5_bucket(rel):
    nb = N_BUCKETS // 2
    max_exact = nb // 2
    ret = jnp.where(rel > 0, nb, 0)
    n = jnp.abs(rel)
    large = max_exact + (jnp.log(jnp.maximum(n, 1).astype(F32) / max_exact)
                         / math.log(T5_MAX_DIST / max_exact) * (nb - max_exact)).astype(jnp.int32)
    large = jnp.minimum(large, nb - 1)
    return ret + jnp.where(n < max_exact, n, large)


def _band_bucket_index(dil):
    tq = DIL_TQ
    rel = jnp.arange(3 * tq)[None, :] - tq - jnp.arange(tq)[:, None]
    return _t5_bucket(rel * dil).astype(jnp.int32)


def _na_rel_index():
    tq = NA_QROWS * GRID_W
    iq, ik = np.arange(tq)[:, None], np.arange(3 * tq)[None, :]
    rr = np.clip(ik // GRID_W - iq // GRID_W + NA_KH - 1 - NA_QROWS, 0, 2 * NA_KH - 2)
    cc = np.clip(ik % GRID_W - iq % GRID_W + NA_KW - 1, 0, 2 * NA_KW - 2)
    return rr, cc


def _t5_grad(dbs, idxs, *, name):
    def body(d0, d1, d2, i0, i1, i2, o_ref):
        lane = lax.broadcasted_iota(jnp.int32, (1, LANES), 1)
        lines = [jnp.zeros((1, LANES), F32) for _ in range(8)]
        for dref, iref in ((d0, i0), (d1, i1), (d2, i2)):
            idx = iref[...]
            for hh in range(8):
                xh = dref[hh]
                for b in range(N_BUCKETS):
                    val = jnp.sum(jnp.sum(jnp.where(idx == b, xh, 0.0), axis=1, keepdims=True), axis=0, keepdims=True)
                    lines[hh] = lines[hh] + jnp.where(lane == b, val, 0.0)
        for hh in range(8):
            o_ref[hh:hh + 1, :] = lines[hh]

    out = pl.pallas_call(body, name=name, out_shape=jax.ShapeDtypeStruct((8, LANES), F32))(*dbs, *idxs)
    return out[:, :N_BUCKETS].T


def _rpb_grad(db, *, name):
    nr, nc = 2 * NA_KH - 1, 2 * NA_KW - 1
    tq = NA_QROWS * GRID_W
    w = GRID_W

    def body(d_ref, o_ref):
        x = d_ref[0]
        rows = []
        for dr in range(nr):
            acc = jnp.zeros((w, w), F32)
            for i in range(NA_QROWS):
                j = i + dr - (NA_KH - 1 - NA_QROWS)
                if 0 <= j < 3 * NA_QROWS:
                    acc = acc + x[i * w:(i + 1) * w, j * w:(j + 1) * w]
            rows.append(acc)
        diff = (lax.broadcasted_iota(jnp.int32, (w, w), 1) - lax.broadcasted_iota(jnp.int32, (w, w), 0)
                + NA_KW - 1)
        lane = lax.broadcasted_iota(jnp.int32, (1, LANES), 1)
        for dr in range(nr):
            line = jnp.zeros((1, LANES), F32)
            for dc in range(nc):
                val = jnp.sum(jnp.sum(jnp.where(diff == dc, rows[dr], 0.0), axis=1, keepdims=True),
                              axis=0, keepdims=True)
                line = jnp.where(lane == dc, val, line)
            o_ref[0, dr:dr + 1, :] = line

    out = pl.pallas_call(body, name=name, grid=(8,),
                         in_specs=[pl.BlockSpec((1, tq, 3 * tq), lambda h: (h, 0, 0))],
                         out_specs=pl.BlockSpec((1, nr, LANES), lambda h: (h, 0, 0)),
                         out_shape=jax.ShapeDtypeStruct((8, nr, LANES), F32),
                         compiler_params=_params(("parallel",)))(db)
    return out[:, :, :nc]


def _exchange(src, *, gather, name):
    shape = src.shape if not gather else (N_DEV,) + src.shape

    def body(src_ref, out_ref, send_sems, recv_sems, local_sem):
        x, y, c = lax.axis_index("x"), lax.axis_index("y"), lax.axis_index("c")
        me = 4 * x + 2 * y + c

        def outgoing(p):
            return src_ref if gather else src_ref.at[p]

        own = pltpu.make_async_copy(outgoing(me), out_ref.at[me], local_sem)
        own.start()
        sends, peers = [], []
        for k in range(1, N_DEV):
            px = 1 - x if k & 4 else x
            py = 1 - y if k & 2 else y
            pc = 1 - c if k & 1 else c
            p = 4 * px + 2 * py + pc
            cp = pltpu.make_async_remote_copy(
                src_ref=outgoing(p), dst_ref=out_ref.at[me], send_sem=send_sems.at[k - 1],
                recv_sem=recv_sems.at[k - 1], device_id=(px, py, pc), device_id_type=pl.DeviceIdType.MESH)
            cp.start()
            sends.append(cp)
            peers.append((p, (px, py, pc)))
        for k, (p, pid) in enumerate(peers):
            pltpu.make_async_remote_copy(
                src_ref=outgoing(p), dst_ref=out_ref.at[p], send_sem=send_sems.at[k], recv_sem=recv_sems.at[k],
                device_id=pid, device_id_type=pl.DeviceIdType.MESH).wait_recv()
        for cp in sends:
            cp.wait_send()
        own.wait()

    any_spec = pl.BlockSpec(memory_space=pl.ANY)
    return pl.pallas_call(
        body, name=name, in_specs=[any_spec], out_specs=any_spec, out_shape=jax.ShapeDtypeStruct(shape, src.dtype),
        scratch_shapes=[pltpu.SemaphoreType.DMA((N_DEV - 1,)), pltpu.SemaphoreType.DMA((N_DEV - 1,)),
                        pltpu.SemaphoreType.DMA])(src)


def _adamw(parts, w, m, v, *, name):
    rows, cols = w.shape
    t = _tile(rows, (FLAT_ROW_TILE, 128, 64, 32, 16, 8))

    def body(p_ref, w_ref, m_ref, v_ref, g_ref, d_ref, nm_ref, nv_ref):
        g = p_ref[0]
        for k in range(1, N_DEV):
            g = g + p_ref[k]
        nm = ADAM_B1 * m_ref[...] + (1.0 - ADAM_B1) * g
        nv = ADAM_B2 * v_ref[...] + (1.0 - ADAM_B2) * (g * g)
        m_hat = nm / (1.0 - ADAM_B1 ** ADAM_STEP)
        v_hat = nv / (1.0 - ADAM_B2 ** ADAM_STEP)
        g_ref[...] = g
        d_ref[...] = -ADAM_LR * (m_hat / (jnp.sqrt(v_hat) + ADAM_EPS) + ADAM_WD * w_ref[...])
        nm_ref[...] = nm
        nv_ref[...] = nv

    row = pl.BlockSpec((t, cols), lambda i: (i, 0))
    shape = jax.ShapeDtypeStruct((rows, cols), F32)
    return pl.pallas_call(body, name=name, grid=(rows // t,),
                          in_specs=[pl.BlockSpec((N_DEV, t, cols), lambda i: (0, i, 0)), row, row, row],
                          out_specs=[row] * 4, out_shape=[shape] * 4,
                          compiler_params=_params(("parallel",)))(parts, w, m, v)


def _flatten(arrays, dtype, row_mult):
    flat = jnp.concatenate([a.reshape(-1).astype(dtype) for a in arrays])
    chunk = FLAT_COLS * row_mult
    padded = -(-flat.shape[0] // chunk) * chunk
    return jnp.pad(flat, (0, padded - flat.shape[0])).reshape(padded // FLAT_COLS, FLAT_COLS)


def _unflatten(flat, shapes):
    flat = flat.reshape(-1)
    out, pos = [], 0
    for shp in shapes:
        size = int(np.prod(shp))
        out.append(flat[pos:pos + size].reshape(shp))
        pos += size
    return out


def _gather_full(names, local, dtype, row_mult, label):
    got = _exchange(_flatten([local[n] for n in names], dtype, row_mult), gather=True, name=label)
    per_dev = [_unflatten(got[p], [local[n].shape for n in names]) for p in range(N_DEV)]
    return {n: jnp.concatenate([per_dev[p][i] for p in range(N_DEV)], axis=SHARD_AXIS[n])
            for i, n in enumerate(names)}


def _local_step(x, tgt, w):
    s = x.shape[0]
    na_tq = NA_QROWS * GRID_W
    rr, cc = _na_rel_index()
    band_idx = [_band_bucket_index(d) for _, d in DIL_PATTERNS]
    band_bias = [jnp.transpose(w['t5_bias'][idx], (2, 0, 1)) for idx in band_idx]
    na_cols, dil_cols = (0, 4, 8), (12, 16, 20)
    grads = {n: [None] * w[n].shape[0] for n in WEIGHTS if n not in ('t5_bias',)}
    saved = []

    for i in range(DEPTH):
        j = i // 2
        st = {'x': x}
        if i % 2 == 0:
            h = _mm(x, w['attn_w_in'][j], name="attn_in")
            na_bias = w['na_rpb'][j][:, rr, cc]
            o_na, l_na = _attn_fwd(h, na_bias, kind="na", d=1, tq=na_tq, w128=24, cols=na_cols, name="na_fwd")
            outs, lses = [], []
            for (_, d), bias in zip(DIL_PATTERNS, band_bias):
                o, l = _attn_fwd(h.reshape(s // d, d * h.shape[1]), bias, kind="band", d=d, tq=DIL_TQ, w128=24,
                                 cols=dil_cols, name=f"dil_fwd_{d}")
                outs.append(o.reshape(s, -1))
                lses.append(l.reshape(s, -1))
            y_dil, l_dil = _dil_combine(outs, lses, name="dil_combine")
            mid = jnp.concatenate([o_na, y_dil], axis=1)
            ymix = _mm(mid, w['attn_w_out'][j], name="attn_out")
            st.update(h=h, mid=mid, l_na=l_na, l_dil=l_dil, na_bias=na_bias)
        else:
            h = _mm(x, w['conv_w_in'][j], name="conv_in")
            mid, u2 = _conv_mid_fwd(h, w['conf_dw_w'][j], w['conf_dw_b'][j], w['conf_ln_g'][j], w['conf_ln_b'][j],
                                    w['sconv_w'][j], name="conv_mid_fwd")
            ymix = _mm(mid, w['conv_w_out'][j], name="conv_out")
            st.update(h=h, mid=mid, u2=u2)
        xa = _ln_fwd(x, ymix, w['mix_ln_g'][i], w['mix_ln_b'][i], name="mix_ln")
        hu = _mm(xa, w['ffn_w_up'][i], name="ffn_up")
        act = _ffn_mid_fwd(hu, w['ffn_dw_w'][i], name="ffn_mid_fwd")
        yffn = _mm(act, w['ffn_w_down'][i], name="ffn_down")
        xb = _ln_fwd(xa, yffn, w['ffn_ln_g'][i], w['ffn_ln_b'][i], name="ffn_ln")
        st.update(ymix=ymix, xa=xa, hu=hu, act=act, yffn=yffn)
        saved.append(st)
        x = xb

    loss, d1 = _loss_head(x, tgt, name="loss_head")
    d2 = None
    g_t5 = None
    for i in reversed(range(DEPTH)):
        j = i // 2
        st = saved[i]
        dz, dg, db = _ln_bwd(st['xa'], st['yffn'], w['ffn_ln_g'][i], d1, d2, name="ffn_ln_bwd")
        grads['ffn_ln_g'][i], grads['ffn_ln_b'][i] = dg, db
        grads['ffn_w_down'][i] = _mm(st['act'], dz, ta=True, name="ffn_down_dw")
        dact = _mm(dz, w['ffn_w_down'][i], tb=True, name="ffn_down_dx")
        dhu, grads['ffn_dw_w'][i] = _ffn_mid_bwd(st['hu'], w['ffn_dw_w'][i], dact, name="ffn_mid_bwd")
        grads['ffn_w_up'][i] = _mm(st['xa'], dhu, ta=True, name="ffn_up_dw")
        dxa = _mm(dhu, w['ffn_w_up'][i], tb=True, name="ffn_up_dx")
        dz1, dg, db = _ln_bwd(st['x'], st['ymix'], w['mix_ln_g'][i], dz, dxa, name="mix_ln_bwd")
        grads['mix_ln_g'][i], grads['mix_ln_b'][i] = dg, db
        if i % 2 == 0:
            grads['attn_w_out'][j] = _mm(st['mid'], dz1, ta=True, name="attn_out_dw")
            dmid = _mm(dz1, w['attn_w_out'][j], tb=True, name="attn_out_dx")
            h = st['h']
            dq, dk, dv, dbias = _attn_bwd(h, st['na_bias'], dmid, st['mid'], st['l_na'], kind="na", d=1, tq=na_tq,
                                          w128=24, cols=na_cols, yw128=8, ycol=0, name="na_bwd")
            grads['na_rpb'][j] = _rpb_grad(dbias, name="rpb_grad")
            dil, dbs = [], []
            for (_, d), bias in zip(DIL_PATTERNS, band_bias):
                view = lambda a: a.reshape(s // d, d * a.shape[1])
                g = _attn_bwd(view(h), bias, view(dmid), view(st['mid']), view(st['l_dil']), kind="band", d=d,
                              tq=DIL_TQ, w128=24, cols=dil_cols, yw128=8, ycol=4, name=f"dil_bwd_{d}")
                dil.append([a.reshape(s, -1) for a in g[:3]])
                dbs.append(g[3])
            t5 = _t5_grad(dbs, band_idx, name="t5_grad")
            g_t5 = t5 if g_t5 is None else g_t5 + t5
            dh = _attn_dh((dq, dk, dv), dil, name="attn_dh")
            w_in, key = w['attn_w_in'][j], 'attn_w_in'
        else:
            grads['conv_w_out'][j] = _mm(st['mid'], dz1, ta=True, name="conv_out_dw")
            dmid = _mm(dz1, w['conv_w_out'][j], tb=True, name="conv_out_dx")
            dh, dw31, db31, dlg, dlb, dw3 = _conv_mid_bwd(st['h'], st['u2'], dmid, w['conf_dw_w'][j],
                                                          w['conf_ln_g'][j], w['conf_ln_b'][j], w['sconv_w'][j],
                                                          name="conv_mid_bwd")
            grads['conf_dw_w'][j], grads['conf_dw_b'][j] = dw31, db31
            grads['conf_ln_g'][j], grads['conf_ln_b'][j], grads['sconv_w'][j] = dlg, dlb, dw3
            w_in, key = w['conv_w_in'][j], 'conv_w_in'
        grads[key][j] = _mm(st['x'], dh, ta=True, name=key + "_dw")
        d1, d2 = dz1, _mm(dh, w_in, tb=True, name=key + "_dx")
    dx = _axpy(d1, d2, name="grad_x")
    full = {n: jnp.stack(g) for n, g in grads.items()}
    full['t5_bias'] = g_t5
    return loss, dx, full


def kernel(x, t5_bias, attn_w_in, attn_w_out, na_rpb, conv_w_in, conf_dw_w, conf_dw_b, conf_ln_g, conf_ln_b, sconv_w, conv_w_out, ffn_w_up, ffn_dw_w, ffn_w_down, mix_ln_g, mix_ln_b, ffn_ln_g, ffn_ln_b, loss_target, m_t5_bias, m_attn_w_in, m_attn_w_out, m_na_rpb, m_conv_w_in, m_conf_dw_w, m_conf_dw_b, m_conf_ln_g, m_conf_ln_b, m_sconv_w, m_conv_w_out, m_ffn_w_up, m_ffn_dw_w, m_ffn_w_down, m_mix_ln_g, m_mix_ln_b, m_ffn_ln_g, m_ffn_ln_b, v_t5_bias, v_attn_w_in, v_attn_w_out, v_na_rpb, v_conv_w_in, v_conf_dw_w, v_conf_dw_b, v_conf_ln_g, v_conf_ln_b, v_sconv_w, v_conv_w_out, v_ffn_w_up, v_ffn_dw_w, v_ffn_w_down, v_mix_ln_g, v_mix_ln_b, v_ffn_ln_g, v_ffn_ln_b):
    args = dict(locals())
    local = {n: args[n] for n in WEIGHTS}
    mom1 = {n: args['m_' + n] for n in WEIGHTS}
    mom2 = {n: args['v_' + n] for n in WEIGHTS}

    full = {n: local[n] for n in REPLICATED}
    full.update(_gather_full(MATMUL_WEIGHTS, local, BF16, 16, "gather_matmul_weights"))
    full.update(_gather_full(SMALL_SHARDED, local, F32, 8, "gather_small_weights"))

    loss, dx, grads = _local_step(x[0], loss_target[0], full)
    loss = lax.psum(loss, MESH_AXES)

    out = {}
    for names, sharded, label in ((SHARDED, True, "sharded"), (REPLICATED, False, "replicated")):
        shapes = [local[n].shape for n in names]
        if sharded:
            per_dev = [_flatten([lax.slice_in_dim(grads[n], p * local[n].shape[SHARD_AXIS[n]],
                                                  (p + 1) * local[n].shape[SHARD_AXIS[n]], axis=SHARD_AXIS[n])
                                 for n in names], F32, FLAT_ROW_TILE) for p in range(N_DEV)]
            parts = _exchange(jnp.stack(per_dev), gather=False, name="scatter_grads")
        else:
            parts = _exchange(_flatten([grads[n] for n in names], F32, 8), gather=True, name="gather_replicated_grads")
        mult = FLAT_ROW_TILE if sharded else 8
        res = _adamw(parts, _flatten([local[n] for n in names], F32, mult), _flatten([mom1[n] for n in names], F32, mult),
                     _flatten([mom2[n] for n in names], F32, mult), name="adamw_" + label)
        for kind, flat in zip(('grad', 'delta', 'new_m', 'new_v'), res):
            for n, a in zip(names, _unflatten(flat, shapes)):
                out[kind + '_' + n] = a

    return (loss, dx[None], *[out[k + '_' + n] for k in ('grad', 'delta', 'new_m', 'new_v') for n in WEIGHTS])
```

```python
import functools
import math

import jax
import jax.numpy as jnp
import numpy as np
from jax import lax
from jax.experimental import pallas as pl
from jax.experimental.pallas import tpu as pltpu

F32 = jnp.float32
BF16 = jnp.bfloat16

N_DEV = 8
MESH_AXES = ("x", "y", "c")
DEPTH = 4
GRID_W = 64
GRID_SHIFT = 6
HEAD_DIM = 64
NA_KH = 8
NA_KW = 16
NA_QROWS = 4
DIL_PATTERNS = ((128, 1), (512, 4), (2048, 16))
DIL_HALF = 64
DIL_TQ = 128
N_BUCKETS = 32
T5_MAX_DIST = 1024
CONF_CH = 512
CONF_K = 31
SC_K = 3
FFN_K = 3
LN_EPS = 1e-5
NEG = -1e30
ALPHA = (2 * DEPTH) ** 0.25
ADAM_LR = 0.001
ADAM_B1 = 0.9
ADAM_B2 = 0.999
ADAM_EPS = 1e-08
ADAM_WD = 0.01
ADAM_STEP = 10

LANES = 128
SUBLANES = 8
VMEM_BIG = 48 * 1024 * 1024
FLAT_COLS = 1024
FLAT_ROW_TILE = 256

WEIGHTS = ['t5_bias', 'attn_w_in', 'attn_w_out', 'na_rpb', 'conv_w_in', 'conf_dw_w', 'conf_dw_b', 'conf_ln_g',
           'conf_ln_b', 'sconv_w', 'conv_w_out', 'ffn_w_up', 'ffn_dw_w', 'ffn_w_down', 'mix_ln_g', 'mix_ln_b',
           'ffn_ln_g', 'ffn_ln_b']
SHARD_AXIS = {'attn_w_in': 2, 'attn_w_out': 1, 'conv_w_in': 2, 'conf_dw_w': 2, 'conf_dw_b': 1, 'conf_ln_g': 1,
              'conf_ln_b': 1, 'sconv_w': 2, 'conv_w_out': 1, 'ffn_w_up': 2, 'ffn_dw_w': 2, 'ffn_w_down': 1}
MATMUL_WEIGHTS = ['attn_w_in', 'attn_w_out', 'conv_w_in', 'conv_w_out', 'ffn_w_up', 'ffn_w_down']
SMALL_SHARDED = ['conf_dw_w', 'conf_dw_b', 'conf_ln_g', 'conf_ln_b', 'sconv_w', 'ffn_dw_w']
SHARDED = MATMUL_WEIGHTS + SMALL_SHARDED
REPLICATED = ['t5_bias', 'na_rpb', 'mix_ln_g', 'mix_ln_b', 'ffn_ln_g', 'ffn_ln_b']


def _tile(n, cands):
    for c in cands:
        if n % c == 0:
            return c
    return n


def _params(sem, vmem=None):
    return pltpu.CompilerParams(dimension_semantics=sem, vmem_limit_bytes=vmem)


def _sigmoid(x):
    return 1.0 / (1.0 + jnp.exp(-x))


MM_TILES = (1024, 512, 256, 128)
MM_FULL_K = 3072


def _mm(a, b, *, ta=False, tb=False, out_dtype=F32, name):
    assert a.dtype == BF16 and b.dtype == BF16, (name, a.dtype, b.dtype)
    m, k = (a.shape[1], a.shape[0]) if ta else a.shape
    n = b.shape[0] if tb else b.shape[1]
    tm, tn = _tile(m, MM_TILES), _tile(n, MM_TILES)
    tk = k if k <= MM_FULL_K else _tile(k, MM_TILES)
    nk = k // tk
    dims = (((0 if ta else 1,), (1 if tb else 0,)), ((), ()))
    use_acc = nk > 1 and out_dtype != F32

    def body(a_ref, b_ref, o_ref, *scratch):
        part = lax.dot_general(a_ref[...], b_ref[...], dims, preferred_element_type=F32)
        if nk == 1:
            o_ref[...] = part.astype(out_dtype)
            return
        acc_ref = scratch[0] if use_acc else o_ref
        kk = pl.program_id(2)

        @pl.when(kk == 0)
        def _():
            acc_ref[...] = part

        @pl.when(kk > 0)
        def _():
            acc_ref[...] += part

        if use_acc:
            @pl.when(kk == nk - 1)
            def _():
                o_ref[...] = acc_ref[...].astype(out_dtype)

    a_spec = pl.BlockSpec((tk, tm), lambda i, j, q: (q, i)) if ta else pl.BlockSpec((tm, tk), lambda i, j, q: (i, q))
    b_spec = pl.BlockSpec((tn, tk), lambda i, j, q: (j, q)) if tb else pl.BlockSpec((tk, tn), lambda i, j, q: (q, j))
    return pl.pallas_call(
        body, name=name, grid=(m // tm, n // tn, nk), in_specs=[a_spec, b_spec],
        out_specs=pl.BlockSpec((tm, tn), lambda i, j, q: (i, j)),
        out_shape=jax.ShapeDtypeStruct((m, n), out_dtype),
        scratch_shapes=[pltpu.VMEM((tm, tn), F32)] if use_acc else [],
        compiler_params=_params(("parallel", "parallel", "arbitrary"), VMEM_BIG))(a, b)


def _ln_stats(z):
    mu = jnp.mean(z, axis=-1, keepdims=True)
    zc = z - mu
    var = jnp.mean(zc * zc, axis=-1, keepdims=True)
    rstd = lax.rsqrt(var + LN_EPS)
    return zc * rstd, rstd


def _ln_fwd(x, y, g, b, *, name):
    s, d = x.shape
    t = _tile(s, (256, 128, 64, 8))

    def body(x_ref, y_ref, g_ref, b_ref, o_ref, o16_ref):
        xh, _ = _ln_stats(ALPHA * x_ref[...] + y_ref[...])
        out = xh * g_ref[...] + b_ref[...]
        o_ref[...] = out
        o16_ref[...] = out.astype(BF16)

    row = pl.BlockSpec((t, d), lambda i: (i, 0))
    vec = pl.BlockSpec((1, d), lambda i: (0, 0))
    return pl.pallas_call(body, name=name, grid=(s // t,), in_specs=[row, row, vec, vec], out_specs=[row, row],
                          out_shape=[jax.ShapeDtypeStruct((s, d), F32), jax.ShapeDtypeStruct((s, d), BF16)],
                          compiler_params=_params(("parallel",)))(x, y, g.reshape(1, d), b.reshape(1, d))


def _ln_bwd(x, y, g, d1, d2, *, name):
    s, d = x.shape
    t = _tile(s, (256, 128, 64, 8))
    two = d2 is not None

    def body(*refs):
        if two:
            x_ref, y_ref, g_ref, d1_ref, d2_ref, dz_ref, dz16_ref, dg_ref, db_ref = refs
            dout = ALPHA * d1_ref[...] + d2_ref[...]
        else:
            x_ref, y_ref, g_ref, d1_ref, dz_ref, dz16_ref, dg_ref, db_ref = refs
            dout = d1_ref[...]

        @pl.when(pl.program_id(0) == 0)
        def _():
            dg_ref[...] = jnp.zeros_like(dg_ref)
            db_ref[...] = jnp.zeros_like(db_ref)

        xh, rstd = _ln_stats(ALPHA * x_ref[...] + y_ref[...])
        dxh = dout * g_ref[...]
        dz = rstd * (dxh - jnp.mean(dxh, axis=-1, keepdims=True) - xh * jnp.mean(dxh * xh, axis=-1, keepdims=True))
        dz_ref[...] = dz
        dz16_ref[...] = dz.astype(BF16)
        dg_ref[...] += jnp.sum(dout * xh, axis=0, keepdims=True)
        db_ref[...] += jnp.sum(dout, axis=0, keepdims=True)

    row = pl.BlockSpec((t, d), lambda i: (i, 0))
    vec = pl.BlockSpec((1, d), lambda i: (0, 0))
    ins = [x, y, g.reshape(1, d), d1] + ([d2] if two else [])
    specs = [row, row, vec, row] + ([row] if two else [])
    dz, dz16, dg, db = pl.pallas_call(
        body, name=name, grid=(s // t,), in_specs=specs, out_specs=[row, row, vec, vec],
        out_shape=[jax.ShapeDtypeStruct((s, d), F32), jax.ShapeDtypeStruct((s, d), BF16),
                   jax.ShapeDtypeStruct((1, d), F32), jax.ShapeDtypeStruct((1, d), F32)],
        compiler_params=_params(("arbitrary",)))(*ins)
    return dz, dz16, dg.reshape(d), db.reshape(d)


def _axpy(d1, d2, *, name):
    s, d = d1.shape
    t = _tile(s, (256, 128, 64, 8))

    def body(a_ref, b_ref, o_ref):
        o_ref[...] = ALPHA * a_ref[...] + b_ref[...]

    row = pl.BlockSpec((t, d), lambda i: (i, 0))
    return pl.pallas_call(body, name=name, grid=(s // t,), in_specs=[row, row], out_specs=row,
                          out_shape=jax.ShapeDtypeStruct((s, d), F32), compiler_params=_params(("parallel",)))(d1, d2)


def _loss_head(y, tgt, *, name):
    s, d = y.shape
    t = _tile(s, (256, 128, 64, 8))

    def body(y_ref, t_ref, l_ref, dy_ref):
        @pl.when(pl.program_id(0) == 0)
        def _():
            l_ref[...] = jnp.zeros_like(l_ref)

        err = y_ref[...] - t_ref[...]
        dy_ref[...] = err * (1.0 / d)
        l_ref[...] += 0.5 * jnp.sum(jnp.sum(err * err, axis=1, keepdims=True), axis=0, keepdims=True) * (1.0 / d)

    row = pl.BlockSpec((t, d), lambda i: (i, 0))
    one = pl.BlockSpec((SUBLANES, LANES), lambda i: (0, 0))
    loss, dy = pl.pallas_call(
        body, name=name, grid=(s // t,), in_specs=[row, row], out_specs=[one, row],
        out_shape=[jax.ShapeDtypeStruct((SUBLANES, LANES), F32), jax.ShapeDtypeStruct((s, d), F32)],
        compiler_params=_params(("arbitrary",)))(y, tgt)
    return loss[0, 0], dy


def _halo_specs(s, t, halo, cb, col):
    per = t // halo
    last = s // halo - 1
    return [pl.BlockSpec((t, cb), lambda j, i: (i, col(j))),
            pl.BlockSpec((halo, cb), lambda j, i: (jnp.maximum(i * per - 1, 0), col(j))),
            pl.BlockSpec((halo, cb), lambda j, i: (jnp.minimum((i + 1) * per, last), col(j)))]


def _extended(main_ref, prev_ref, next_ref, i, n):
    prev = jnp.where(i > 0, prev_ref[...], 0.0)
    nxt = jnp.where(i < n - 1, next_ref[...], 0.0)
    return jnp.concatenate([prev, main_ref[...], nxt], axis=0)


def _shift(ext, o):
    if o == 0:
        return ext
    return pltpu.roll(ext, (-o) % ext.shape[0], 0)


def _conv(ext, w_ref, k, sign=1):
    acc = None
    for j in range(k):
        term = w_ref[j:j + 1, :] * _shift(ext, sign * (j - k // 2))
        acc = term if acc is None else acc + term
    return acc


def _conv_wgrad(dw_ref, d_main, x_ext, k, halo, t):
    for j in range(k):
        xs = _shift(x_ext, j - k // 2)[halo:halo + t]
        dw_ref[j:j + 1, :] += jnp.sum(d_main * xs, axis=0, keepdims=True)


def _ffn_mid_fwd(hu, w, *, name):
    s, f2 = hu.shape
    f = f2 // 2
    t, cb, halo = _tile(s, (512, 256, 128)), _tile(f, (256, 128)), SUBLANES
    nt, nc = s // t, f // cb

    def body(g_ref, gp_ref, gn_ref, u_ref, up_ref, un_ref, wg_ref, wu_ref, a_ref):
        i = pl.program_id(1)
        hg = _conv(_extended(g_ref, gp_ref, gn_ref, i, nt), wg_ref, FFN_K)[halo:halo + t]
        hu_ = _conv(_extended(u_ref, up_ref, un_ref, i, nt), wu_ref, FFN_K)[halo:halo + t]
        a_ref[...] = (hg * _sigmoid(hg) * hu_).astype(BF16)

    specs = (_halo_specs(s, t, halo, cb, lambda j: j) + _halo_specs(s, t, halo, cb, lambda j: j + nc)
             + [pl.BlockSpec((FFN_K, cb), lambda j, i: (0, j)), pl.BlockSpec((FFN_K, cb), lambda j, i: (0, j + nc))])
    return pl.pallas_call(body, name=name, grid=(nc, nt), in_specs=specs,
                          out_specs=pl.BlockSpec((t, cb), lambda j, i: (i, j)),
                          out_shape=jax.ShapeDtypeStruct((s, f), BF16),
                          compiler_params=_params(("parallel", "parallel")))(hu, hu, hu, hu, hu, hu, w, w)


def _ffn_mid_bwd(hu, w, da, *, name):
    s, f2 = hu.shape
    f = f2 // 2
    t, cb, halo = _tile(s, (512, 256, 128)), _tile(f, (256, 128)), SUBLANES
    nt, nc = s // t, f // cb

    def body(g_ref, gp_ref, gn_ref, u_ref, up_ref, un_ref, a_ref, ap_ref, an_ref, wg_ref, wu_ref,
             dg_ref, du_ref, dwg_ref, dwu_ref):
        i = pl.program_id(1)

        @pl.when(i == 0)
        def _():
            dwg_ref[...] = jnp.zeros_like(dwg_ref)
            dwu_ref[...] = jnp.zeros_like(dwu_ref)

        xg = _extended(g_ref, gp_ref, gn_ref, i, nt)
        xu = _extended(u_ref, up_ref, un_ref, i, nt)
        dae = _extended(a_ref, ap_ref, an_ref, i, nt)
        hg = _conv(xg, wg_ref, FFN_K)
        hu_ = _conv(xu, wu_ref, FFN_K)
        sg = _sigmoid(hg)
        d_hg = dae * hu_ * (sg * (1.0 + hg * (1.0 - sg)))
        d_hu = dae * (hg * sg)
        dg_ref[...] = _conv(d_hg, wg_ref, FFN_K, sign=-1)[halo:halo + t].astype(BF16)
        du_ref[...] = _conv(d_hu, wu_ref, FFN_K, sign=-1)[halo:halo + t].astype(BF16)
        _conv_wgrad(dwg_ref, d_hg[halo:halo + t], xg, FFN_K, halo, t)
        _conv_wgrad(dwu_ref, d_hu[halo:halo + t], xu, FFN_K, halo, t)

    wspec = lambda off: pl.BlockSpec((FFN_K, cb), lambda j, i: (0, j + off))
    specs = (_halo_specs(s, t, halo, cb, lambda j: j) + _halo_specs(s, t, halo, cb, lambda j: j + nc)
             + _halo_specs(s, t, halo, cb, lambda j: j) + [wspec(0), wspec(nc)])
    tile = pl.BlockSpec((t, cb), lambda j, i: (i, j))
    dg, du, dwg, dwu = pl.pallas_call(
        body, name=name, grid=(nc, nt), in_specs=specs, out_specs=[tile, tile, wspec(0), wspec(0)],
        out_shape=[jax.ShapeDtypeStruct((s, f), BF16), jax.ShapeDtypeStruct((s, f), BF16),
                   jax.ShapeDtypeStruct((FFN_K, f), F32), jax.ShapeDtypeStruct((FFN_K, f), F32)],
        compiler_params=_params(("parallel", "arbitrary")))(hu, hu, hu, hu, hu, hu, da, da, da, w, w)
    return jnp.concatenate([dg, du], axis=1), jnp.concatenate([dwg, dwu], axis=1)


CONV_HALO = 16


def _conv_mid_fwd(h, dw_w, dw_b, ln_g, ln_b, sc_w, *, name):
    s = h.shape[0]
    c = CONF_CH
    t, halo = _tile(s, (256, 128)), CONV_HALO
    nt = s // t

    def body(ca, cap, can, cg, cgp, cgn, gb, gc, gcp, gcn, hx, hxp, hxn, w31, b31, lg, lb, w3, o_ref, u2_ref):
        i = pl.program_id(1)
        u1 = _extended(ca, cap, can, i, nt) * _sigmoid(_extended(cg, cgp, cgn, i, nt))
        u2 = _conv(u1, w31, CONF_K)[halo:halo + t] + b31[...]
        u2_ref[...] = u2
        xh, _ = _ln_stats(u2)
        yl = xh * lg[...] + lb[...]
        o_ref[:, 0:c] = (yl * _sigmoid(yl)).astype(BF16)
        p = _extended(gc, gcp, gcn, i, nt) * _extended(hx, hxp, hxn, i, nt)
        o_ref[:, c:2 * c] = (gb[...] * _conv(p, w3, SC_K)[halo:halo + t]).astype(BF16)

    hs = lambda blk: _halo_specs(s, t, halo, c, lambda j: blk)
    vec = lambda r: pl.BlockSpec((r, c), lambda j, i: (0, 0))
    specs = hs(0) + hs(1) + hs(2)[:1] + hs(3) + hs(4) + [vec(CONF_K), vec(1), vec(1), vec(1), vec(SC_K)]
    return pl.pallas_call(
        body, name=name, grid=(1, nt), in_specs=specs,
        out_specs=[pl.BlockSpec((t, 2 * c), lambda j, i: (i, 0)), pl.BlockSpec((t, c), lambda j, i: (i, 0))],
        out_shape=[jax.ShapeDtypeStruct((s, 2 * c), BF16), jax.ShapeDtypeStruct((s, c), F32)],
        compiler_params=_params(("parallel", "parallel")))(
            h, h, h, h, h, h, h, h, h, h, h, h, h, dw_w, dw_b.reshape(1, c), ln_g.reshape(1, c),
            ln_b.reshape(1, c), sc_w)


def _conv_mid_bwd(h, u2, dm, dw_w, ln_g, ln_b, sc_w, *, name):
    s = h.shape[0]
    c = CONF_CH
    t, halo = _tile(s, (256, 128)), CONV_HALO
    nt = s // t

    def body(ca, cap, can, cg, cgp, cgn, gb, gbp, gbn, gc, gcp, gcn, hx, hxp, hxn, u2r, u2p, u2n,
             du, dup, dun, dz, dzp, dzn, w31, lg, lb, w3,
             dh_ref, dw31_ref, db31_ref, dlg_ref, dlb_ref, dw3_ref):
        i = pl.program_id(1)

        @pl.when(i == 0)
        def _():
            for r in (dw31_ref, db31_ref, dlg_ref, dlb_ref, dw3_ref):
                r[...] = jnp.zeros_like(r)

        main = slice(halo, halo + t)
        xh, rstd = _ln_stats(_extended(u2r, u2p, u2n, i, nt))
        yl = xh * lg[...] + lb[...]
        sg = _sigmoid(yl)
        d_yl = _extended(du, dup, dun, i, nt) * (sg * (1.0 + yl * (1.0 - sg)))
        dlg_ref[...] += jnp.sum((d_yl * xh)[main], axis=0, keepdims=True)
        dlb_ref[...] += jnp.sum(d_yl[main], axis=0, keepdims=True)
        dxh = d_yl * lg[...]
        du2 = rstd * (dxh - jnp.mean(dxh, axis=-1, keepdims=True) - xh * jnp.mean(dxh * xh, axis=-1, keepdims=True))
        db31_ref[...] += jnp.sum(du2[main], axis=0, keepdims=True)
        cae = _extended(ca, cap, can, i, nt)
        sc = _sigmoid(_extended(cg, cgp, cgn, i, nt))
        u1 = cae * sc
        _conv_wgrad(dw31_ref, du2[main], u1, CONF_K, halo, t)
        du1 = _conv(du2, w31, CONF_K, sign=-1)[main]
        dh_ref[:, 0:c] = (du1 * sc[main]).astype(BF16)
        dh_ref[:, c:2 * c] = (du1 * (cae * sc * (1.0 - sc))[main]).astype(BF16)
        gce = _extended(gc, gcp, gcn, i, nt)
        hxe = _extended(hx, hxp, hxn, i, nt)
        p = gce * hxe
        dze = _extended(dz, dzp, dzn, i, nt)
        d_c3 = dze * _extended(gb, gbp, gbn, i, nt)
        dh_ref[:, 2 * c:3 * c] = (dze[main] * _conv(p, w3, SC_K)[main]).astype(BF16)
        _conv_wgrad(dw3_ref, d_c3[main], p, SC_K, halo, t)
        dp = _conv(d_c3, w3, SC_K, sign=-1)[main]
        dh_ref[:, 3 * c:4 * c] = (dp * hxe[main]).astype(BF16)
        dh_ref[:, 4 * c:5 * c] = (dp * gce[main]).astype(BF16)

    hs = lambda blk: _halo_specs(s, t, halo, c, lambda j: blk)
    vec = lambda r: pl.BlockSpec((r, c), lambda j, i: (0, 0))
    specs = (hs(0) + hs(1) + hs(2) + hs(3) + hs(4) + hs(0) + hs(0) + hs(1)
             + [vec(CONF_K), vec(1), vec(1), vec(SC_K)])
    outs = pl.pallas_call(
        body, name=name, grid=(1, nt), in_specs=specs,
        out_specs=[pl.BlockSpec((t, 5 * c), lambda j, i: (i, 0)), vec(CONF_K), vec(1), vec(1), vec(1), vec(SC_K)],
        out_shape=[jax.ShapeDtypeStruct((s, 5 * c), BF16), jax.ShapeDtypeStruct((CONF_K, c), F32),
                   jax.ShapeDtypeStruct((1, c), F32), jax.ShapeDtypeStruct((1, c), F32),
                   jax.ShapeDtypeStruct((1, c), F32), jax.ShapeDtypeStruct((SC_K, c), F32)],
        compiler_params=_params(("arbitrary", "arbitrary")))(
            h, h, h, h, h, h, h, h, h, h, h, h, h, h, h, u2, u2, u2, dm, dm, dm, dm, dm, dm,
            dw_w, ln_g.reshape(1, c), ln_b.reshape(1, c), sc_w)
    dh, dw31, db31, dlg, dlb, dw3 = outs
    return dh, dw31, db31.reshape(c), dlg.reshape(c), dlb.reshape(c), dw3


def _attn_mask(kind, n, tq, length):
    iq = lax.broadcasted_iota(jnp.int32, (tq, 1), 0)
    ik = lax.broadcasted_iota(jnp.int32, (1, 3 * tq), 1)
    if kind == "band":
        rel = ik - tq - iq
        kpos = (n - 1) * tq + ik
        return (jnp.abs(rel) <= DIL_HALF) & (kpos >= 0) & (kpos < length)
    rows = length // GRID_W
    rq = n * NA_QROWS + (iq >> GRID_SHIFT)
    cq = iq & (GRID_W - 1)
    rk = (n - 1) * NA_QROWS + (ik >> GRID_SHIFT)
    ck = ik & (GRID_W - 1)
    r0 = jnp.clip(rq - NA_KH // 2, 0, rows - NA_KH)
    c0 = jnp.clip(cq - NA_KW // 2, 0, GRID_W - NA_KW)
    return (rk >= r0) & (rk < r0 + NA_KH) & (ck >= c0) & (ck < c0 + NA_KW)


def _attn_specs(length, d, tq, w128, cols):
    nb = length // tq
    qc, kc, vc = cols
    blk = lambda col, off: pl.BlockSpec(
        (tq, LANES), lambda hp, r, n: (jnp.clip(n + off, 0, nb - 1), r * w128 + col + hp))
    return [blk(qc, 0)] + [blk(kc, o) for o in (-1, 0, 1)] + [blk(vc, o) for o in (-1, 0, 1)]


def _attn_fwd(hv, bias, *, kind, d, tq, w128, cols, name):
    length = hv.shape[0]
    nb = length // tq
    scale = HEAD_DIM ** -0.5

    def body(q_ref, k0, k1, k2, v0, v1, v2, b_ref, o_ref, l_ref):
        n = pl.program_id(2)
        q = q_ref[...].astype(F32) * scale
        kcat = jnp.concatenate([k0[...], k1[...], k2[...]], axis=0).astype(BF16)
        vcat = jnp.concatenate([v0[...], v1[...], v2[...]], axis=0).astype(BF16)
        mask = _attn_mask(kind, n, tq, length)
        low = lax.broadcasted_iota(jnp.int32, (1, LANES), 1) < HEAD_DIM
        outs, lses = [], []
        for hh in range(2):
            qh = jnp.where(low if hh == 0 else ~low, q, 0.0).astype(BF16)
            sc = lax.dot_general(qh, kcat, (((1,), (1,)), ((), ())), preferred_element_type=F32) + b_ref[hh]
            sc = jnp.where(mask, sc, NEG)
            m = jnp.max(sc, axis=1, keepdims=True)
            p = jnp.exp(sc - m)
            den = jnp.sum(p, axis=1, keepdims=True)
            outs.append(jnp.dot((p / den).astype(BF16), vcat, preferred_element_type=F32))
            lses.append(m + jnp.log(den))
        o_ref[...] = jnp.where(low, outs[0], outs[1])
        l_ref[...] = jnp.where(low, lses[0], lses[1])

    out = pl.BlockSpec((tq, LANES), lambda hp, r, n: (n, r * 4 + hp))
    specs = _attn_specs(length, d, tq, w128, cols) + [pl.BlockSpec((2, tq, 3 * tq), lambda hp, r, n: (hp, 0, 0))]
    shape = jax.ShapeDtypeStruct((length, d * 4 * LANES), F32)
    return pl.pallas_call(body, name=name, grid=(4, d, nb), in_specs=specs, out_specs=[out, out],
                          out_shape=[shape, shape],
                          compiler_params=_params(("parallel", "parallel", "parallel")))(
                              hv, hv, hv, hv, hv, hv, hv, bias)


def _attn_bwd(hv, bias, dyv, yv, lse, *, kind, d, tq, w128, cols, yw128, ycol, name):
    length = hv.shape[0]
    nb = length // tq
    scale = HEAD_DIM ** -0.5

    def body(q_ref, k0, k1, k2, v0, v1, v2, b_ref, dy_ref, y_ref, l_ref, dq_ref, dk_ref, dv_ref, db_ref):
        r, n = pl.program_id(1), pl.program_id(2)

        @pl.when(n == 0)
        def _():
            dk_ref[...] = jnp.zeros_like(dk_ref)
            dv_ref[...] = jnp.zeros_like(dv_ref)

        @pl.when((n == 0) & (r == 0))
        def _():
            db_ref[...] = jnp.zeros_like(db_ref)

        q = q_ref[...].astype(F32) * scale
        kcat = jnp.concatenate([k0[...], k1[...], k2[...]], axis=0).astype(BF16)
        vcat = jnp.concatenate([v0[...], v1[...], v2[...]], axis=0).astype(BF16)
        mask = _attn_mask(kind, n, tq, length)
        low = lax.broadcasted_iota(jnp.int32, (1, LANES), 1) < HEAD_DIM
        dy = dy_ref[...]
        dyy = dy * y_ref[...]
        lse_all = l_ref[...]
        dq = jnp.zeros((tq, LANES), F32)
        dk = jnp.zeros((3 * tq, LANES), F32)
        dv = jnp.zeros((3 * tq, LANES), F32)
        for hh in range(2):
            sel = low if hh == 0 else ~low
            qh = jnp.where(sel, q, 0.0).astype(BF16)
            dyh = jnp.where(sel, dy, 0.0).astype(BF16)
            dsum = jnp.sum(jnp.where(sel, dyy, 0.0), axis=1, keepdims=True)
            lse_h = lse_all[:, hh * HEAD_DIM:hh * HEAD_DIM + 1]
            sc = lax.dot_general(qh, kcat, (((1,), (1,)), ((), ())), preferred_element_type=F32) + b_ref[hh]
            p = jnp.where(mask, jnp.exp(jnp.where(mask, sc, NEG) - lse_h), 0.0)
            dp = lax.dot_general(dyh, vcat, (((1,), (1,)), ((), ())), preferred_element_type=F32)
            ds = p * (dp - dsum)
            db_ref[hh] += ds
            pb, dsb = p.astype(BF16), ds.astype(BF16)
            dv += lax.dot_general(pb, dyh, (((0,), (0,)), ((), ())), preferred_element_type=F32)
            dk += lax.dot_general(dsb, qh, (((0,), (0,)), ((), ())), preferred_element_type=F32)
            dq += jnp.where(sel, jnp.dot(dsb, kcat, preferred_element_type=F32), 0.0)
        dq_ref[...] = dq * scale
        for j in range(3):
            start = pl.multiple_of(jnp.clip(n - 1 + j, 0, nb - 1) * tq, tq)
            dk_ref[pl.ds(start, tq), :] += dk[j * tq:(j + 1) * tq]
            dv_ref[pl.ds(start, tq), :] += dv[j * tq:(j + 1) * tq]

    blk = pl.BlockSpec((tq, LANES), lambda hp, r, n: (n, r * 4 + hp))
    yblk = pl.BlockSpec((tq, LANES), lambda hp, r, n: (n, r * yw128 + ycol + hp))
    slab = pl.BlockSpec((length, LANES), lambda hp, r, n: (0, r * 4 + hp))
    bspec = pl.BlockSpec((2, tq, 3 * tq), lambda hp, r, n: (hp, 0, 0))
    specs = _attn_specs(length, d, tq, w128, cols) + [bspec, yblk, yblk, blk]
    shape = jax.ShapeDtypeStruct((length, d * 4 * LANES), F32)
    return pl.pallas_call(
        body, name=name, grid=(4, d, nb), in_specs=specs, out_specs=[blk, slab, slab, bspec],
        out_shape=[shape, shape, shape, jax.ShapeDtypeStruct(bias.shape, F32)],
        compiler_params=_params(("arbitrary", "arbitrary", "arbitrary"), VMEM_BIG))(
            hv, hv, hv, hv, hv, hv, hv, bias, dyv, yv, lse)


def _dil_combine(outs, lses, *, name):
    s, c = outs[0].shape
    t = _tile(s, (512, 256, 128, 64, 8))

    def body(o0, o1, o2, l0, l1, l2, y_ref, lt_ref):
        ls = [l0[...], l1[...], l2[...]]
        m = jnp.maximum(jnp.maximum(ls[0], ls[1]), ls[2])
        es = [jnp.exp(l - m) for l in ls]
        den = es[0] + es[1] + es[2]
        y_ref[...] = (es[0] / den) * o0[...] + (es[1] / den) * o1[...] + (es[2] / den) * o2[...]
        lt_ref[...] = m + jnp.log(den)

    row = pl.BlockSpec((t, c), lambda i: (i, 0))
    shape = jax.ShapeDtypeStruct((s, c), F32)
    return pl.pallas_call(body, name=name, grid=(s // t,), in_specs=[row] * 6, out_specs=[row, row],
                          out_shape=[shape, shape], compiler_params=_params(("parallel",)))(*outs, *lses)


def _attn_dh(na, dil, *, name):
    s, c = na[0].shape
    t = _tile(s, (256, 128, 64, 8))

    def body(*refs):
        ins, o_ref = refs[:-1], refs[-1]
        for a in range(3):
            o_ref[:, a * c:(a + 1) * c] = ins[a][...].astype(BF16)
            o_ref[:, (3 + a) * c:(4 + a) * c] = (ins[3 + a][...] + ins[6 + a][...] + ins[9 + a][...]).astype(BF16)

    row = pl.BlockSpec((t, c), lambda i: (i, 0))
    flat = list(na) + [g[a] for g in dil for a in range(3)]
    return pl.pallas_call(body, name=name, grid=(s // t,), in_specs=[row] * 12,
                          out_specs=pl.BlockSpec((t, 6 * c), lambda i: (i, 0)),
                          out_shape=jax.ShapeDtypeStruct((s, 6 * c), BF16),
                          compiler_params=_params(("parallel",)))(*flat)


def _t5_bucket(rel):
    nb = N_BUCKETS // 2
    max_exact = nb // 2
    ret = np.where(rel > 0, nb, 0)
    n = np.abs(rel)
    large = max_exact + (np.log(np.maximum(n, 1).astype(np.float32) / np.float32(max_exact))
                         / np.float32(math.log(T5_MAX_DIST / max_exact)) * np.float32(nb - max_exact)).astype(np.int32)
    large = np.minimum(large, nb - 1)
    return (ret + np.where(n < max_exact, n, large)).astype(np.int32)


def _band_bucket_index(dil):
    tq = DIL_TQ
    rel = np.arange(3 * tq)[None, :] - tq - np.arange(tq)[:, None]
    return _t5_bucket(rel * dil)


def _band_bias(t5, dil, *, name):
    tq = DIL_TQ
    buckets = [int(b) for b in _t5_bucket(np.arange(-DIL_HALF, DIL_HALF + 1) * dil)]

    def body(t_ref, o_ref):
        hh = pl.program_id(0)
        rel = (lax.broadcasted_iota(jnp.int32, (tq, 3 * tq), 1) - tq
               - lax.broadcasted_iota(jnp.int32, (tq, 3 * tq), 0))
        acc = jnp.zeros((tq, 3 * tq), F32)
        for r, b in zip(range(-DIL_HALF, DIL_HALF + 1), buckets):
            acc = jnp.where(rel == r, t_ref[b * 8 + hh], acc)
        o_ref[0] = acc

    return pl.pallas_call(body, name=name, grid=(8,),
                          in_specs=[pl.BlockSpec(memory_space=pltpu.SMEM)],
                          out_specs=pl.BlockSpec((1, tq, 3 * tq), lambda h: (h, 0, 0)),
                          out_shape=jax.ShapeDtypeStruct((8, tq, 3 * tq), F32),
                          compiler_params=_params(("parallel",)))(t5.reshape(-1))


def _na_bias(rpb, *, name):
    nr, nc = 2 * NA_KH - 1, 2 * NA_KW - 1
    tq = NA_QROWS * GRID_W
    w = GRID_W

    def body(r_ref, o_ref):
        base = pl.program_id(0) * (nr * nc)
        lane = lax.broadcasted_iota(jnp.int32, (w, LANES), 1)
        upper = lane >= w
        diff = (lane & (w - 1)) - lax.broadcasted_iota(jnp.int32, (w, LANES), 0) + NA_KW - 1
        tiles = {}
        for i in range(NA_QROWS):
            for m in range(3 * NA_QROWS // 2):
                lo = 2 * m - i + NA_KH - 1 - NA_QROWS
                if lo not in tiles:
                    acc = jnp.zeros((w, LANES), F32)
                    for dc in range(nc):
                        v_lo = r_ref[base + lo * nc + dc] if 0 <= lo < nr else 0.0
                        v_hi = r_ref[base + (lo + 1) * nc + dc] if 0 <= lo + 1 < nr else 0.0
                        acc = jnp.where(diff == dc, jnp.where(upper, v_hi, v_lo), acc)
                    tiles[lo] = acc
                o_ref[0, i * w:(i + 1) * w, m * LANES:(m + 1) * LANES] = tiles[lo]

    return pl.pallas_call(body, name=name, grid=(8,),
                          in_specs=[pl.BlockSpec(memory_space=pltpu.SMEM)],
                          out_specs=pl.BlockSpec((1, tq, 3 * tq), lambda h: (h, 0, 0)),
                          out_shape=jax.ShapeDtypeStruct((8, tq, 3 * tq), F32),
                          compiler_params=_params(("parallel",)))(rpb.reshape(-1))


def _t5_grad(dbs, idxs, *, name):
    def body(d0, d1, d2, i0, i1, i2, o_ref):
        lane = lax.broadcasted_iota(jnp.int32, (1, LANES), 1)
        lines = [jnp.zeros((1, LANES), F32) for _ in range(8)]
        for dref, iref in ((d0, i0), (d1, i1), (d2, i2)):
            idx = iref[...]
            for hh in range(8):
                xh = dref[hh]
                for b in range(N_BUCKETS):
                    val = jnp.sum(jnp.sum(jnp.where(idx == b, xh, 0.0), axis=1, keepdims=True), axis=0, keepdims=True)
                    lines[hh] = lines[hh] + jnp.where(lane == b, val, 0.0)
        for hh in range(8):
            o_ref[hh:hh + 1, :] = lines[hh]

    out = pl.pallas_call(body, name=name, out_shape=jax.ShapeDtypeStruct((8, LANES), F32))(*dbs, *idxs)
    return out[:, :N_BUCKETS].T


def _rpb_grad(db, *, name):
    nr, nc = 2 * NA_KH - 1, 2 * NA_KW - 1
    tq = NA_QROWS * GRID_W
    w = GRID_W

    def body(d_ref, o_ref):
        x = d_ref[0]
        rows = []
        for dr in range(nr):
            acc = jnp.zeros((w, w), F32)
            for i in range(NA_QROWS):
                j = i + dr - (NA_KH - 1 - NA_QROWS)
                if 0 <= j < 3 * NA_QROWS:
                    acc = acc + x[i * w:(i + 1) * w, j * w:(j + 1) * w]
            rows.append(acc)
        diff = (lax.broadcasted_iota(jnp.int32, (w, w), 1) - lax.broadcasted_iota(jnp.int32, (w, w), 0)
                + NA_KW - 1)
        lane = lax.broadcasted_iota(jnp.int32, (1, LANES), 1)
        for dr in range(nr):
            line = jnp.zeros((1, LANES), F32)
            for dc in range(nc):
                val = jnp.sum(jnp.sum(jnp.where(diff == dc, rows[dr], 0.0), axis=1, keepdims=True),
                              axis=0, keepdims=True)
                line = jnp.where(lane == dc, val, line)
            o_ref[0, dr:dr + 1, :] = line

    out = pl.pallas_call(body, name=name, grid=(8,),
                         in_specs=[pl.BlockSpec((1, tq, 3 * tq), lambda h: (h, 0, 0))],
                         out_specs=pl.BlockSpec((1, nr, LANES), lambda h: (h, 0, 0)),
                         out_shape=jax.ShapeDtypeStruct((8, nr, LANES), F32),
                         compiler_params=_params(("parallel",)))(db)
    return out[:, :, :nc]


def _exchange(src, *, gather, name):
    shape = src.shape if not gather else (N_DEV,) + src.shape

    def body(src_ref, out_ref, send_sems, recv_sems, local_sem):
        x, y, c = lax.axis_index("x"), lax.axis_index("y"), lax.axis_index("c")
        me = 4 * x + 2 * y + c

        def outgoing(p):
            return src_ref if gather else src_ref.at[p]

        own = pltpu.make_async_copy(outgoing(me), out_ref.at[me], local_sem)
        own.start()
        sends, peers = [], []
        for k in range(1, N_DEV):
            px = 1 - x if k & 4 else x
            py = 1 - y if k & 2 else y
            pc = 1 - c if k & 1 else c
            p = 4 * px + 2 * py + pc
            cp = pltpu.make_async_remote_copy(
                src_ref=outgoing(p), dst_ref=out_ref.at[me], send_sem=send_sems.at[k - 1],
                recv_sem=recv_sems.at[k - 1], device_id=(px, py, pc), device_id_type=pl.DeviceIdType.MESH)
            cp.start()
            sends.append(cp)
            peers.append((p, (px, py, pc)))
        for k, (p, pid) in enumerate(peers):
            pltpu.make_async_remote_copy(
                src_ref=outgoing(p), dst_ref=out_ref.at[p], send_sem=send_sems.at[k], recv_sem=recv_sems.at[k],
                device_id=pid, device_id_type=pl.DeviceIdType.MESH).wait_recv()
        for cp in sends:
            cp.wait_send()
        own.wait()

    any_spec = pl.BlockSpec(memory_space=pl.ANY)
    return pl.pallas_call(
        body, name=name, in_specs=[any_spec], out_specs=any_spec, out_shape=jax.ShapeDtypeStruct(shape, src.dtype),
        scratch_shapes=[pltpu.SemaphoreType.DMA((N_DEV - 1,)), pltpu.SemaphoreType.DMA((N_DEV - 1,)),
                        pltpu.SemaphoreType.DMA])(src)


def _adamw(parts, w, m, v, *, name):
    rows, cols = w.shape
    t = _tile(rows, (FLAT_ROW_TILE, 128, 64, 32, 16, 8))

    def body(p_ref, w_ref, m_ref, v_ref, g_ref, d_ref, nm_ref, nv_ref):
        g = p_ref[0]
        for k in range(1, N_DEV):
            g = g + p_ref[k]
        nm = ADAM_B1 * m_ref[...] + (1.0 - ADAM_B1) * g
        nv = ADAM_B2 * v_ref[...] + (1.0 - ADAM_B2) * (g * g)
        m_hat = nm / (1.0 - ADAM_B1 ** ADAM_STEP)
        v_hat = nv / (1.0 - ADAM_B2 ** ADAM_STEP)
        g_ref[...] = g
        d_ref[...] = -ADAM_LR * (m_hat / (jnp.sqrt(v_hat) + ADAM_EPS) + ADAM_WD * w_ref[...])
        nm_ref[...] = nm
        nv_ref[...] = nv

    row = pl.BlockSpec((t, cols), lambda i: (i, 0))
    shape = jax.ShapeDtypeStruct((rows, cols), F32)
    return pl.pallas_call(body, name=name, grid=(rows // t,),
                          in_specs=[pl.BlockSpec((N_DEV, t, cols), lambda i: (0, i, 0)), row, row, row],
                          out_specs=[row] * 4, out_shape=[shape] * 4,
                          compiler_params=_params(("parallel",), VMEM_BIG))(parts, w, m, v)


def _flatten(arrays, dtype, row_mult):
    flat = jnp.concatenate([a.reshape(-1).astype(dtype) for a in arrays])
    chunk = FLAT_COLS * row_mult
    padded = -(-flat.shape[0] // chunk) * chunk
    return jnp.pad(flat, (0, padded - flat.shape[0])).reshape(padded // FLAT_COLS, FLAT_COLS)


def _unflatten(flat, shapes):
    flat = flat.reshape(-1)
    out, pos = [], 0
    for shp in shapes:
        size = int(np.prod(shp))
        out.append(flat[pos:pos + size].reshape(shp))
        pos += size
    return out


def _gather_full(names, local, dtype, row_mult, label):
    got = _exchange(_flatten([local[n] for n in names], dtype, row_mult), gather=True, name=label)
    per_dev = [_unflatten(got[p], [local[n].shape for n in names]) for p in range(N_DEV)]
    return {n: jnp.concatenate([per_dev[p][i] for p in range(N_DEV)], axis=SHARD_AXIS[n])
            for i, n in enumerate(names)}


def _from_shards(stacked, axis):
    _, l, a, b = stacked.shape
    if axis == 2:
        return jnp.transpose(stacked, (1, 2, 0, 3)).reshape(l, a, N_DEV * b)
    return jnp.transpose(stacked, (1, 0, 2, 3)).reshape(l, N_DEV * a, b)


def _to_shards(full, axis):
    l, ra, rb = full.shape
    if axis == 2:
        b = rb // N_DEV
        return jnp.transpose(full.reshape(l, ra, N_DEV, b), (2, 0, 1, 3)).reshape(N_DEV, l * ra, b)
    a = ra // N_DEV
    return jnp.transpose(full.reshape(l, N_DEV, a, rb), (1, 0, 2, 3)).reshape(N_DEV, l * a, rb)


def _local_step(x, tgt, w):
    s = x.shape[0]
    na_tq = NA_QROWS * GRID_W
    band_idx = [_band_bucket_index(d) for _, d in DIL_PATTERNS]
    band_bias = [_band_bias(w['t5_bias'], d, name=f"band_bias_{d}") for _, d in DIL_PATTERNS]
    na_cols, dil_cols = (0, 4, 8), (12, 16, 20)
    grads = {n: [None] * w[n].shape[0] for n in WEIGHTS if n not in ('t5_bias',)}
    saved = []
    x16 = x.astype(BF16)

    for i in range(DEPTH):
        j = i // 2
        st = {'x': x, 'x16': x16}
        if i % 2 == 0:
            h = _mm(x16, w['attn_w_in'][j], out_dtype=BF16, name="attn_in")
            na_bias = _na_bias(w['na_rpb'][j], name="na_bias")
            o_na, l_na = _attn_fwd(h, na_bias, kind="na", d=1, tq=na_tq, w128=24, cols=na_cols, name="na_fwd")
            outs, lses = [], []
            for (_, d), bias in zip(DIL_PATTERNS, band_bias):
                o, l = _attn_fwd(h.reshape(s // d, d * h.shape[1]), bias, kind="band", d=d, tq=DIL_TQ, w128=24,
                                 cols=dil_cols, name=f"dil_fwd_{d}")
                outs.append(o.reshape(s, -1))
                lses.append(l.reshape(s, -1))
            y_dil, l_dil = _dil_combine(outs, lses, name="dil_combine")
            mid = jnp.concatenate([o_na, y_dil], axis=1)
            mid16 = mid.astype(BF16)
            ymix = _mm(mid16, w['attn_w_out'][j], name="attn_out")
            st.update(h=h, mid=mid, mid16=mid16, l_na=l_na, l_dil=l_dil, na_bias=na_bias)
        else:
            h = _mm(x16, w['conv_w_in'][j], name="conv_in")
            mid16, u2 = _conv_mid_fwd(h, w['conf_dw_w'][j], w['conf_dw_b'][j], w['conf_ln_g'][j], w['conf_ln_b'][j],
                                      w['sconv_w'][j], name="conv_mid_fwd")
            ymix = _mm(mid16, w['conv_w_out'][j], name="conv_out")
            st.update(h=h, mid16=mid16, u2=u2)
        xa, xa16 = _ln_fwd(x, ymix, w['mix_ln_g'][i], w['mix_ln_b'][i], name="mix_ln")
        hu = _mm(xa16, w['ffn_w_up'][i], name="ffn_up")
        act16 = _ffn_mid_fwd(hu, w['ffn_dw_w'][i], name="ffn_mid_fwd")
        yffn = _mm(act16, w['ffn_w_down'][i], name="ffn_down")
        xb, xb16 = _ln_fwd(xa, yffn, w['ffn_ln_g'][i], w['ffn_ln_b'][i], name="ffn_ln")
        st.update(ymix=ymix, xa=xa, xa16=xa16, hu=hu, act16=act16, yffn=yffn)
        saved.append(st)
        x, x16 = xb, xb16

    loss, d1 = _loss_head(x, tgt, name="loss_head")
    d2 = None
    g_t5 = None
    for i in reversed(range(DEPTH)):
        j = i // 2
        st = saved[i]
        dz, dz16, dg, db = _ln_bwd(st['xa'], st['yffn'], w['ffn_ln_g'][i], d1, d2, name="ffn_ln_bwd")
        grads['ffn_ln_g'][i], grads['ffn_ln_b'][i] = dg, db
        grads['ffn_w_down'][i] = _mm(st['act16'], dz16, ta=True, name="ffn_down_dw")
        dact = _mm(dz16, w['ffn_w_down'][i], tb=True, name="ffn_down_dx")
        dhu, grads['ffn_dw_w'][i] = _ffn_mid_bwd(st['hu'], w['ffn_dw_w'][i], dact, name="ffn_mid_bwd")
        grads['ffn_w_up'][i] = _mm(st['xa16'], dhu, ta=True, name="ffn_up_dw")
        dxa = _mm(dhu, w['ffn_w_up'][i], tb=True, name="ffn_up_dx")
        dz, dz1, dg, db = _ln_bwd(st['x'], st['ymix'], w['mix_ln_g'][i], dz, dxa, name="mix_ln_bwd")
        grads['mix_ln_g'][i], grads['mix_ln_b'][i] = dg, db
        if i % 2 == 0:
            grads['attn_w_out'][j] = _mm(st['mid16'], dz1, ta=True, name="attn_out_dw")
            dmid = _mm(dz1, w['attn_w_out'][j], tb=True, name="attn_out_dx")
            h = st['h']
            dq, dk, dv, dbias = _attn_bwd(h, st['na_bias'], dmid, st['mid'], st['l_na'], kind="na", d=1, tq=na_tq,
                                          w128=24, cols=na_cols, yw128=8, ycol=0, name="na_bwd")
            grads['na_rpb'][j] = _rpb_grad(dbias, name="rpb_grad")
            dil, dbs = [], []
            for (_, d), bias in zip(DIL_PATTERNS, band_bias):
                view = lambda a: a.reshape(s // d, d * a.shape[1])
                g = _attn_bwd(view(h), bias, view(dmid), view(st['mid']), view(st['l_dil']), kind="band", d=d,
                              tq=DIL_TQ, w128=24, cols=dil_cols, yw128=8, ycol=4, name=f"dil_bwd_{d}")
                dil.append([a.reshape(s, -1) for a in g[:3]])
                dbs.append(g[3])
            t5 = _t5_grad(dbs, band_idx, name="t5_grad")
            g_t5 = t5 if g_t5 is None else g_t5 + t5
            dh = _attn_dh((dq, dk, dv), dil, name="attn_dh")
            w_in, key = w['attn_w_in'][j], 'attn_w_in'
        else:
            grads['conv_w_out'][j] = _mm(st['mid16'], dz1, ta=True, name="conv_out_dw")
            dmid = _mm(dz1, w['conv_w_out'][j], tb=True, name="conv_out_dx")
            dh, dw31, db31, dlg, dlb, dw3 = _conv_mid_bwd(st['h'], st['u2'], dmid, w['conf_dw_w'][j],
                                                          w['conf_ln_g'][j], w['conf_ln_b'][j], w['sconv_w'][j],
                                                          name="conv_mid_bwd")
            grads['conf_dw_w'][j], grads['conf_dw_b'][j] = dw31, db31
            grads['conf_ln_g'][j], grads['conf_ln_b'][j], grads['sconv_w'][j] = dlg, dlb, dw3
            w_in, key = w['conv_w_in'][j], 'conv_w_in'
        grads[key][j] = _mm(st['x16'], dh, ta=True, name=key + "_dw")
        d1, d2 = dz, _mm(dh, w_in, tb=True, name=key + "_dx")
    dx = _axpy(d1, d2, name="grad_x")
    full = {n: jnp.stack(g) for n, g in grads.items()}
    full['t5_bias'] = g_t5
    return loss, dx, full


def kernel(x, t5_bias, attn_w_in, attn_w_out, na_rpb, conv_w_in, conf_dw_w, conf_dw_b, conf_ln_g, conf_ln_b, sconv_w, conv_w_out, ffn_w_up, ffn_dw_w, ffn_w_down, mix_ln_g, mix_ln_b, ffn_ln_g, ffn_ln_b, loss_target, m_t5_bias, m_attn_w_in, m_attn_w_out, m_na_rpb, m_conv_w_in, m_conf_dw_w, m_conf_dw_b, m_conf_ln_g, m_conf_ln_b, m_sconv_w, m_conv_w_out, m_ffn_w_up, m_ffn_dw_w, m_ffn_w_down, m_mix_ln_g, m_mix_ln_b, m_ffn_ln_g, m_ffn_ln_b, v_t5_bias, v_attn_w_in, v_attn_w_out, v_na_rpb, v_conv_w_in, v_conf_dw_w, v_conf_dw_b, v_conf_ln_g, v_conf_ln_b, v_sconv_w, v_conv_w_out, v_ffn_w_up, v_ffn_dw_w, v_ffn_w_down, v_mix_ln_g, v_mix_ln_b, v_ffn_ln_g, v_ffn_ln_b):
    args = dict(locals())
    local = {n: args[n] for n in WEIGHTS}
    mom1 = {n: args['m_' + n] for n in WEIGHTS}
    mom2 = {n: args['v_' + n] for n in WEIGHTS}

    kinds = ('grad', 'delta', 'new_m', 'new_v')
    full = {n: local[n] for n in REPLICATED}
    for n in MATMUL_WEIGHTS:
        l, a, b = local[n].shape
        got = _exchange(local[n].astype(BF16).reshape(l * a, b), gather=True, name="gather_" + n)
        full[n] = _from_shards(got.reshape(N_DEV, l, a, b), SHARD_AXIS[n])
    full.update(_gather_full(SMALL_SHARDED, local, F32, 8, "gather_small_weights"))

    loss, dx, grads = _local_step(x[0], loss_target[0], full)
    loss = lax.psum(loss, MESH_AXES)

    out = {}
    for n in MATMUL_WEIGHTS:
        l, a, b = local[n].shape
        parts = _exchange(_to_shards(grads[n], SHARD_AXIS[n]), gather=False, name="scatter_" + n)
        res = _adamw(parts, *[t.reshape(l * a, b) for t in (local[n], mom1[n], mom2[n])], name="adamw_" + n)
        for kind, r in zip(kinds, res):
            out[kind + '_' + n] = r.reshape(l, a, b)
    for names, sharded, label in ((SMALL_SHARDED, True, "small"), (REPLICATED, False, "replicated")):
        shapes = [local[n].shape for n in names]
        if sharded:
            per_dev = [_flatten([lax.slice_in_dim(grads[n], p * local[n].shape[SHARD_AXIS[n]],
                                                  (p + 1) * local[n].shape[SHARD_AXIS[n]], axis=SHARD_AXIS[n])
                                 for n in names], F32, 8) for p in range(N_DEV)]
            parts = _exchange(jnp.stack(per_dev), gather=False, name="scatter_small_grads")
        else:
            parts = _exchange(_flatten([grads[n] for n in names], F32, 8), gather=True, name="gather_replicated_grads")
        res = _adamw(parts, _flatten([local[n] for n in names], F32, 8), _flatten([mom1[n] for n in names], F32, 8),
                     _flatten([mom2[n] for n in names], F32, 8), name="adamw_" + label)
        for kind, flat in zip(kinds, res):
            for n, a in zip(names, _unflatten(flat, shapes)):
                out[kind + '_' + n] = a

    return (loss, dx[None], *[out[k + '_' + n] for k in ('grad', 'delta', 'new_m', 'new_v') for n in WEIGHTS])
```

```python
import functools
import math

import jax
import jax.numpy as jnp
import numpy as np
from jax import lax
from jax.experimental import pallas as pl
from jax.experimental.pallas import tpu as pltpu

F32 = jnp.float32
BF16 = jnp.bfloat16

N_DEV = 8
MESH_AXES = ("x", "y", "c")
DEPTH = 4
GRID_W = 64
GRID_SHIFT = 6
HEAD_DIM = 64
NA_KH = 8
NA_KW = 16
NA_QROWS = 4
DIL_PATTERNS = ((128, 1), (512, 4), (2048, 16))
DIL_HALF = 64
DIL_TQ = 128
N_BUCKETS = 32
T5_MAX_DIST = 1024
CONF_CH = 512
CONF_K = 31
SC_K = 3
FFN_K = 3
LN_EPS = 1e-5
NEG = -1e30
ALPHA = (2 * DEPTH) ** 0.25
ADAM_LR = 0.001
ADAM_B1 = 0.9
ADAM_B2 = 0.999
ADAM_EPS = 1e-08
ADAM_WD = 0.01
ADAM_STEP = 10

LANES = 128
SUBLANES = 8
VMEM_BIG = 48 * 1024 * 1024
FLAT_COLS = 1024
FLAT_ROW_TILE = 256

WEIGHTS = ['t5_bias', 'attn_w_in', 'attn_w_out', 'na_rpb', 'conv_w_in', 'conf_dw_w', 'conf_dw_b', 'conf_ln_g',
           'conf_ln_b', 'sconv_w', 'conv_w_out', 'ffn_w_up', 'ffn_dw_w', 'ffn_w_down', 'mix_ln_g', 'mix_ln_b',
           'ffn_ln_g', 'ffn_ln_b']
SHARD_AXIS = {'attn_w_in': 2, 'attn_w_out': 1, 'conv_w_in': 2, 'conf_dw_w': 2, 'conf_dw_b': 1, 'conf_ln_g': 1,
              'conf_ln_b': 1, 'sconv_w': 2, 'conv_w_out': 1, 'ffn_w_up': 2, 'ffn_dw_w': 2, 'ffn_w_down': 1}
MATMUL_WEIGHTS = ['attn_w_in', 'attn_w_out', 'conv_w_in', 'conv_w_out', 'ffn_w_up', 'ffn_w_down']
SMALL_SHARDED = ['conf_dw_w', 'conf_dw_b', 'conf_ln_g', 'conf_ln_b', 'sconv_w', 'ffn_dw_w']
SHARDED = MATMUL_WEIGHTS + SMALL_SHARDED
REPLICATED = ['t5_bias', 'na_rpb', 'mix_ln_g', 'mix_ln_b', 'ffn_ln_g', 'ffn_ln_b']


def _tile(n, cands):
    for c in cands:
        if n % c == 0:
            return c
    return n


def _params(sem, vmem=None):
    return pltpu.CompilerParams(dimension_semantics=sem, vmem_limit_bytes=vmem)


def _sigmoid(x):
    return 1.0 / (1.0 + jnp.exp(-x))


MM_TILES = (1024, 512, 256, 128)
MM_FULL_K = 3072


def _mm(a, b, *, ta=False, tb=False, out_dtype=F32, carry=None, name):
    assert a.dtype == BF16 and b.dtype == BF16, (name, a.dtype, b.dtype)
    m, k = (a.shape[1], a.shape[0]) if ta else a.shape
    n = b.shape[0] if tb else b.shape[1]
    tm, tn = _tile(m, MM_TILES), _tile(n, MM_TILES)
    tk = k if k <= MM_FULL_K else _tile(k, MM_TILES)
    grid = (m // tm, n // tn, k // tk)
    nk = grid[2]
    dims = (((0 if ta else 1,), (1 if tb else 0,)), ((), ()))
    use_acc = nk > 1 and out_dtype != F32
    src, gather = carry if carry is not None else (None, False)
    carried = src is not None

    def body(*refs):
        if carried:
            a_ref, b_ref, src_ref, o_ref, x_ref = refs[:5]
            send_sems, recv_sems, local_sem = refs[-3:]
            scratch = refs[5:-3]
            ids = [pl.program_id(ax) for ax in range(3)]
            first = (ids[0] == 0) & (ids[1] == 0) & (ids[2] == 0)
            last = (ids[0] == grid[0] - 1) & (ids[1] == grid[1] - 1) & (ids[2] == grid[2] - 1)

            @pl.when(first)
            def _():
                _exchange_start(src_ref, x_ref, send_sems, recv_sems, local_sem, gather)
        else:
            a_ref, b_ref, o_ref = refs[:3]
            scratch = refs[3:]

        part = lax.dot_general(a_ref[...], b_ref[...], dims, preferred_element_type=F32)
        if nk == 1:
            o_ref[...] = part.astype(out_dtype)
        else:
            acc_ref = scratch[0] if use_acc else o_ref
            kk = pl.program_id(2)

            @pl.when(kk == 0)
            def _():
                acc_ref[...] = part

            @pl.when(kk > 0)
            def _():
                acc_ref[...] += part

            if use_acc:
                @pl.when(kk == nk - 1)
                def _():
                    o_ref[...] = acc_ref[...].astype(out_dtype)

        if carried:
            @pl.when(last)
            def _():
                _exchange_wait(src_ref, x_ref, send_sems, recv_sems, local_sem, gather)

    a_spec = pl.BlockSpec((tk, tm), lambda i, j, q: (q, i)) if ta else pl.BlockSpec((tm, tk), lambda i, j, q: (i, q))
    b_spec = pl.BlockSpec((tn, tk), lambda i, j, q: (j, q)) if tb else pl.BlockSpec((tk, tn), lambda i, j, q: (q, j))
    o_spec = pl.BlockSpec((tm, tn), lambda i, j, q: (i, j))
    o_shape = jax.ShapeDtypeStruct((m, n), out_dtype)
    scratch = [pltpu.VMEM((tm, tn), F32)] if use_acc else []
    if not carried:
        out = pl.pallas_call(
            body, name=name, grid=grid, in_specs=[a_spec, b_spec], out_specs=o_spec, out_shape=o_shape,
            scratch_shapes=scratch,
            compiler_params=_params(("parallel", "parallel", "arbitrary"), VMEM_BIG))(a, b)
        return out, None
    any_spec = pl.BlockSpec(memory_space=pl.ANY)
    x_shape = jax.ShapeDtypeStruct(((N_DEV,) + src.shape) if gather else src.shape, src.dtype)
    return pl.pallas_call(
        body, name=name, grid=grid, in_specs=[a_spec, b_spec, any_spec], out_specs=[o_spec, any_spec],
        out_shape=[o_shape, x_shape], scratch_shapes=scratch + _exchange_sems(),
        compiler_params=_params(("arbitrary", "arbitrary", "arbitrary"), VMEM_BIG))(a, b, src)


def _ln_stats(z):
    mu = jnp.mean(z, axis=-1, keepdims=True)
    zc = z - mu
    var = jnp.mean(zc * zc, axis=-1, keepdims=True)
    rstd = lax.rsqrt(var + LN_EPS)
    return zc * rstd, rstd


def _ln_fwd(x, y, g, b, *, name):
    s, d = x.shape
    t = _tile(s, (256, 128, 64, 8))

    def body(x_ref, y_ref, g_ref, b_ref, o_ref, o16_ref):
        xh, _ = _ln_stats(ALPHA * x_ref[...] + y_ref[...])
        out = xh * g_ref[...] + b_ref[...]
        o_ref[...] = out
        o16_ref[...] = out.astype(BF16)

    row = pl.BlockSpec((t, d), lambda i: (i, 0))
    vec = pl.BlockSpec((1, d), lambda i: (0, 0))
    return pl.pallas_call(body, name=name, grid=(s // t,), in_specs=[row, row, vec, vec], out_specs=[row, row],
                          out_shape=[jax.ShapeDtypeStruct((s, d), F32), jax.ShapeDtypeStruct((s, d), BF16)],
                          compiler_params=_params(("parallel",)))(x, y, g.reshape(1, d), b.reshape(1, d))


def _ln_bwd(x, y, g, d1, d2, *, name):
    s, d = x.shape
    t = _tile(s, (256, 128, 64, 8))
    two = d2 is not None

    def body(*refs):
        if two:
            x_ref, y_ref, g_ref, d1_ref, d2_ref, dz_ref, dz16_ref, dg_ref, db_ref = refs
            dout = ALPHA * d1_ref[...] + d2_ref[...]
        else:
            x_ref, y_ref, g_ref, d1_ref, dz_ref, dz16_ref, dg_ref, db_ref = refs
            dout = d1_ref[...]

        @pl.when(pl.program_id(0) == 0)
        def _():
            dg_ref[...] = jnp.zeros_like(dg_ref)
            db_ref[...] = jnp.zeros_like(db_ref)

        xh, rstd = _ln_stats(ALPHA * x_ref[...] + y_ref[...])
        dxh = dout * g_ref[...]
        dz = rstd * (dxh - jnp.mean(dxh, axis=-1, keepdims=True) - xh * jnp.mean(dxh * xh, axis=-1, keepdims=True))
        dz_ref[...] = dz
        dz16_ref[...] = dz.astype(BF16)
        dg_ref[...] += jnp.sum(dout * xh, axis=0, keepdims=True)
        db_ref[...] += jnp.sum(dout, axis=0, keepdims=True)

    row = pl.BlockSpec((t, d), lambda i: (i, 0))
    vec = pl.BlockSpec((1, d), lambda i: (0, 0))
    ins = [x, y, g.reshape(1, d), d1] + ([d2] if two else [])
    specs = [row, row, vec, row] + ([row] if two else [])
    dz, dz16, dg, db = pl.pallas_call(
        body, name=name, grid=(s // t,), in_specs=specs, out_specs=[row, row, vec, vec],
        out_shape=[jax.ShapeDtypeStruct((s, d), F32), jax.ShapeDtypeStruct((s, d), BF16),
                   jax.ShapeDtypeStruct((1, d), F32), jax.ShapeDtypeStruct((1, d), F32)],
        compiler_params=_params(("arbitrary",)))(*ins)
    return dz, dz16, dg.reshape(d), db.reshape(d)


def _axpy(d1, d2, *, name):
    s, d = d1.shape
    t = _tile(s, (256, 128, 64, 8))

    def body(a_ref, b_ref, o_ref):
        o_ref[...] = ALPHA * a_ref[...] + b_ref[...]

    row = pl.BlockSpec((t, d), lambda i: (i, 0))
    return pl.pallas_call(body, name=name, grid=(s // t,), in_specs=[row, row], out_specs=row,
                          out_shape=jax.ShapeDtypeStruct((s, d), F32), compiler_params=_params(("parallel",)))(d1, d2)


def _loss_head(y, tgt, *, name):
    s, d = y.shape
    t = _tile(s, (256, 128, 64, 8))

    def body(y_ref, t_ref, l_ref, dy_ref):
        @pl.when(pl.program_id(0) == 0)
        def _():
            l_ref[...] = jnp.zeros_like(l_ref)

        err = y_ref[...] - t_ref[...]
        dy_ref[...] = err * (1.0 / d)
        l_ref[...] += 0.5 * jnp.sum(jnp.sum(err * err, axis=1, keepdims=True), axis=0, keepdims=True) * (1.0 / d)

    row = pl.BlockSpec((t, d), lambda i: (i, 0))
    one = pl.BlockSpec((SUBLANES, LANES), lambda i: (0, 0))
    loss, dy = pl.pallas_call(
        body, name=name, grid=(s // t,), in_specs=[row, row], out_specs=[one, row],
        out_shape=[jax.ShapeDtypeStruct((SUBLANES, LANES), F32), jax.ShapeDtypeStruct((s, d), F32)],
        compiler_params=_params(("arbitrary",)))(y, tgt)
    return loss[0, 0], dy


def _halo_specs(s, t, halo, cb, col):
    per = t // halo
    last = s // halo - 1
    return [pl.BlockSpec((t, cb), lambda j, i: (i, col(j))),
            pl.BlockSpec((halo, cb), lambda j, i: (jnp.maximum(i * per - 1, 0), col(j))),
            pl.BlockSpec((halo, cb), lambda j, i: (jnp.minimum((i + 1) * per, last), col(j)))]


def _extended(main_ref, prev_ref, next_ref, i, n):
    prev = jnp.where(i > 0, prev_ref[...], 0.0)
    nxt = jnp.where(i < n - 1, next_ref[...], 0.0)
    return jnp.concatenate([prev, main_ref[...], nxt], axis=0)


def _shift(ext, o):
    if o == 0:
        return ext
    return pltpu.roll(ext, (-o) % ext.shape[0], 0)


def _conv(ext, w_ref, k, sign=1):
    acc = None
    for j in range(k):
        term = w_ref[j:j + 1, :] * _shift(ext, sign * (j - k // 2))
        acc = term if acc is None else acc + term
    return acc


def _conv_wgrad(dw_ref, d_main, x_ext, k, halo, t):
    for j in range(k):
        xs = _shift(x_ext, j - k // 2)[halo:halo + t]
        dw_ref[j:j + 1, :] += jnp.sum(d_main * xs, axis=0, keepdims=True)


def _ffn_mid_fwd(hu, w, *, name):
    s, f2 = hu.shape
    f = f2 // 2
    t, cb, halo = _tile(s, (512, 256, 128)), _tile(f, (256, 128)), SUBLANES
    nt, nc = s // t, f // cb

    def body(g_ref, gp_ref, gn_ref, u_ref, up_ref, un_ref, wg_ref, wu_ref, a_ref):
        i = pl.program_id(1)
        hg = _conv(_extended(g_ref, gp_ref, gn_ref, i, nt), wg_ref, FFN_K)[halo:halo + t]
        hu_ = _conv(_extended(u_ref, up_ref, un_ref, i, nt), wu_ref, FFN_K)[halo:halo + t]
        a_ref[...] = (hg * _sigmoid(hg) * hu_).astype(BF16)

    specs = (_halo_specs(s, t, halo, cb, lambda j: j) + _halo_specs(s, t, halo, cb, lambda j: j + nc)
             + [pl.BlockSpec((FFN_K, cb), lambda j, i: (0, j)), pl.BlockSpec((FFN_K, cb), lambda j, i: (0, j + nc))])
    return pl.pallas_call(body, name=name, grid=(nc, nt), in_specs=specs,
                          out_specs=pl.BlockSpec((t, cb), lambda j, i: (i, j)),
                          out_shape=jax.ShapeDtypeStruct((s, f), BF16),
                          compiler_params=_params(("parallel", "parallel")))(hu, hu, hu, hu, hu, hu, w, w)


def _ffn_mid_bwd(hu, w, da, *, name):
    s, f2 = hu.shape
    f = f2 // 2
    t, cb, halo = _tile(s, (512, 256, 128)), _tile(f, (256, 128)), SUBLANES
    nt, nc = s // t, f // cb

    def body(g_ref, gp_ref, gn_ref, u_ref, up_ref, un_ref, a_ref, ap_ref, an_ref, wg_ref, wu_ref,
             dg_ref, du_ref, dwg_ref, dwu_ref):
        i = pl.program_id(1)

        @pl.when(i == 0)
        def _():
            dwg_ref[...] = jnp.zeros_like(dwg_ref)
            dwu_ref[...] = jnp.zeros_like(dwu_ref)

        xg = _extended(g_ref, gp_ref, gn_ref, i, nt)
        xu = _extended(u_ref, up_ref, un_ref, i, nt)
        dae = _extended(a_ref, ap_ref, an_ref, i, nt)
        hg = _conv(xg, wg_ref, FFN_K)
        hu_ = _conv(xu, wu_ref, FFN_K)
        sg = _sigmoid(hg)
        d_hg = dae * hu_ * (sg * (1.0 + hg * (1.0 - sg)))
        d_hu = dae * (hg * sg)
        dg_ref[...] = _conv(d_hg, wg_ref, FFN_K, sign=-1)[halo:halo + t].astype(BF16)
        du_ref[...] = _conv(d_hu, wu_ref, FFN_K, sign=-1)[halo:halo + t].astype(BF16)
        _conv_wgrad(dwg_ref, d_hg[halo:halo + t], xg, FFN_K, halo, t)
        _conv_wgrad(dwu_ref, d_hu[halo:halo + t], xu, FFN_K, halo, t)

    wspec = lambda off: pl.BlockSpec((FFN_K, cb), lambda j, i: (0, j + off))
    specs = (_halo_specs(s, t, halo, cb, lambda j: j) + _halo_specs(s, t, halo, cb, lambda j: j + nc)
             + _halo_specs(s, t, halo, cb, lambda j: j) + [wspec(0), wspec(nc)])
    tile = pl.BlockSpec((t, cb), lambda j, i: (i, j))
    dg, du, dwg, dwu = pl.pallas_call(
        body, name=name, grid=(nc, nt), in_specs=specs, out_specs=[tile, tile, wspec(0), wspec(0)],
        out_shape=[jax.ShapeDtypeStruct((s, f), BF16), jax.ShapeDtypeStruct((s, f), BF16),
                   jax.ShapeDtypeStruct((FFN_K, f), F32), jax.ShapeDtypeStruct((FFN_K, f), F32)],
        compiler_params=_params(("parallel", "arbitrary")))(hu, hu, hu, hu, hu, hu, da, da, da, w, w)
    return jnp.concatenate([dg, du], axis=1), jnp.concatenate([dwg, dwu], axis=1)


CONV_HALO = 16


def _conv_mid_fwd(h, dw_w, dw_b, ln_g, ln_b, sc_w, *, name):
    s = h.shape[0]
    c = CONF_CH
    t, halo = _tile(s, (256, 128)), CONV_HALO
    nt = s // t

    def body(ca, cap, can, cg, cgp, cgn, gb, gc, gcp, gcn, hx, hxp, hxn, w31, b31, lg, lb, w3, o_ref, u2_ref):
        i = pl.program_id(1)
        u1 = _extended(ca, cap, can, i, nt) * _sigmoid(_extended(cg, cgp, cgn, i, nt))
        u2 = _conv(u1, w31, CONF_K)[halo:halo + t] + b31[...]
        u2_ref[...] = u2
        xh, _ = _ln_stats(u2)
        yl = xh * lg[...] + lb[...]
        o_ref[:, 0:c] = (yl * _sigmoid(yl)).astype(BF16)
        p = _extended(gc, gcp, gcn, i, nt) * _extended(hx, hxp, hxn, i, nt)
        o_ref[:, c:2 * c] = (gb[...] * _conv(p, w3, SC_K)[halo:halo + t]).astype(BF16)

    hs = lambda blk: _halo_specs(s, t, halo, c, lambda j: blk)
    vec = lambda r: pl.BlockSpec((r, c), lambda j, i: (0, 0))
    specs = hs(0) + hs(1) + hs(2)[:1] + hs(3) + hs(4) + [vec(CONF_K), vec(1), vec(1), vec(1), vec(SC_K)]
    return pl.pallas_call(
        body, name=name, grid=(1, nt), in_specs=specs,
        out_specs=[pl.BlockSpec((t, 2 * c), lambda j, i: (i, 0)), pl.BlockSpec((t, c), lambda j, i: (i, 0))],
        out_shape=[jax.ShapeDtypeStruct((s, 2 * c), BF16), jax.ShapeDtypeStruct((s, c), F32)],
        compiler_params=_params(("parallel", "parallel")))(
            h, h, h, h, h, h, h, h, h, h, h, h, h, dw_w, dw_b.reshape(1, c), ln_g.reshape(1, c),
            ln_b.reshape(1, c), sc_w)


def _conv_mid_bwd(h, u2, dm, dw_w, ln_g, ln_b, sc_w, *, name):
    s = h.shape[0]
    c = CONF_CH
    t, halo = _tile(s, (256, 128)), CONV_HALO
    nt = s // t

    def body(ca, cap, can, cg, cgp, cgn, gb, gbp, gbn, gc, gcp, gcn, hx, hxp, hxn, u2r, u2p, u2n,
             du, dup, dun, dz, dzp, dzn, w31, lg, lb, w3,
             dh_ref, dw31_ref, db31_ref, dlg_ref, dlb_ref, dw3_ref):
        i = pl.program_id(1)

        @pl.when(i == 0)
        def _():
            for r in (dw31_ref, db31_ref, dlg_ref, dlb_ref, dw3_ref):
                r[...] = jnp.zeros_like(r)

        main = slice(halo, halo + t)
        xh, rstd = _ln_stats(_extended(u2r, u2p, u2n, i, nt))
        yl = xh * lg[...] + lb[...]
        sg = _sigmoid(yl)
        d_yl = _extended(du, dup, dun, i, nt) * (sg * (1.0 + yl * (1.0 - sg)))
        dlg_ref[...] += jnp.sum((d_yl * xh)[main], axis=0, keepdims=True)
        dlb_ref[...] += jnp.sum(d_yl[main], axis=0, keepdims=True)
        dxh = d_yl * lg[...]
        du2 = rstd * (dxh - jnp.mean(dxh, axis=-1, keepdims=True) - xh * jnp.mean(dxh * xh, axis=-1, keepdims=True))
        db31_ref[...] += jnp.sum(du2[main], axis=0, keepdims=True)
        cae = _extended(ca, cap, can, i, nt)
        sc = _sigmoid(_extended(cg, cgp, cgn, i, nt))
        u1 = cae * sc
        _conv_wgrad(dw31_ref, du2[main], u1, CONF_K, halo, t)
        du1 = _conv(du2, w31, CONF_K, sign=-1)[main]
        dh_ref[:, 0:c] = (du1 * sc[main]).astype(BF16)
        dh_ref[:, c:2 * c] = (du1 * (cae * sc * (1.0 - sc))[main]).astype(BF16)
        gce = _extended(gc, gcp, gcn, i, nt)
        hxe = _extended(hx, hxp, hxn, i, nt)
        p = gce * hxe
        dze = _extended(dz, dzp, dzn, i, nt)
        d_c3 = dze * _extended(gb, gbp, gbn, i, nt)
        dh_ref[:, 2 * c:3 * c] = (dze[main] * _conv(p, w3, SC_K)[main]).astype(BF16)
        _conv_wgrad(dw3_ref, d_c3[main], p, SC_K, halo, t)
        dp = _conv(d_c3, w3, SC_K, sign=-1)[main]
        dh_ref[:, 3 * c:4 * c] = (dp * hxe[main]).astype(BF16)
        dh_ref[:, 4 * c:5 * c] = (dp * gce[main]).astype(BF16)

    hs = lambda blk: _halo_specs(s, t, halo, c, lambda j: blk)
    vec = lambda r: pl.BlockSpec((r, c), lambda j, i: (0, 0))
    specs = (hs(0) + hs(1) + hs(2) + hs(3) + hs(4) + hs(0) + hs(0) + hs(1)
             + [vec(CONF_K), vec(1), vec(1), vec(SC_K)])
    outs = pl.pallas_call(
        body, name=name, grid=(1, nt), in_specs=specs,
        out_specs=[pl.BlockSpec((t, 5 * c), lambda j, i: (i, 0)), vec(CONF_K), vec(1), vec(1), vec(1), vec(SC_K)],
        out_shape=[jax.ShapeDtypeStruct((s, 5 * c), BF16), jax.ShapeDtypeStruct((CONF_K, c), F32),
                   jax.ShapeDtypeStruct((1, c), F32), jax.ShapeDtypeStruct((1, c), F32),
                   jax.ShapeDtypeStruct((1, c), F32), jax.ShapeDtypeStruct((SC_K, c), F32)],
        compiler_params=_params(("arbitrary", "arbitrary")))(
            h, h, h, h, h, h, h, h, h, h, h, h, h, h, h, u2, u2, u2, dm, dm, dm, dm, dm, dm,
            dw_w, ln_g.reshape(1, c), ln_b.reshape(1, c), sc_w)
    dh, dw31, db31, dlg, dlb, dw3 = outs
    return dh, dw31, db31.reshape(c), dlg.reshape(c), dlb.reshape(c), dw3


def _attn_mask(kind, n, tq, length):
    iq = lax.broadcasted_iota(jnp.int32, (tq, 1), 0)
    ik = lax.broadcasted_iota(jnp.int32, (1, 3 * tq), 1)
    if kind == "band":
        rel = ik - tq - iq
        kpos = (n - 1) * tq + ik
        return (jnp.abs(rel) <= DIL_HALF) & (kpos >= 0) & (kpos < length)
    rows = length // GRID_W
    rq = n * NA_QROWS + (iq >> GRID_SHIFT)
    cq = iq & (GRID_W - 1)
    rk = (n - 1) * NA_QROWS + (ik >> GRID_SHIFT)
    ck = ik & (GRID_W - 1)
    r0 = jnp.clip(rq - NA_KH // 2, 0, rows - NA_KH)
    c0 = jnp.clip(cq - NA_KW // 2, 0, GRID_W - NA_KW)
    return (rk >= r0) & (rk < r0 + NA_KH) & (ck >= c0) & (ck < c0 + NA_KW)


def _attn_specs(length, d, tq, w128, cols):
    nb = length // tq
    qc, kc, vc = cols
    blk = lambda col, off: pl.BlockSpec(
        (tq, LANES), lambda hp, r, n: (jnp.clip(n + off, 0, nb - 1), r * w128 + col + hp))
    return [blk(qc, 0)] + [blk(kc, o) for o in (-1, 0, 1)] + [blk(vc, o) for o in (-1, 0, 1)]


def _attn_fwd(hv, bias, *, kind, d, tq, w128, cols, name):
    length = hv.shape[0]
    nb = length // tq
    scale = HEAD_DIM ** -0.5

    def body(q_ref, k0, k1, k2, v0, v1, v2, b_ref, o_ref, l_ref):
        n = pl.program_id(2)
        q = q_ref[...].astype(F32) * scale
        kcat = jnp.concatenate([k0[...], k1[...], k2[...]], axis=0).astype(BF16)
        vcat = jnp.concatenate([v0[...], v1[...], v2[...]], axis=0).astype(BF16)
        mask = _attn_mask(kind, n, tq, length)
        low = lax.broadcasted_iota(jnp.int32, (1, LANES), 1) < HEAD_DIM
        outs, lses = [], []
        for hh in range(2):
            qh = jnp.where(low if hh == 0 else ~low, q, 0.0).astype(BF16)
            sc = lax.dot_general(qh, kcat, (((1,), (1,)), ((), ())), preferred_element_type=F32) + b_ref[hh]
            sc = jnp.where(mask, sc, NEG)
            m = jnp.max(sc, axis=1, keepdims=True)
            p = jnp.exp(sc - m)
            den = jnp.sum(p, axis=1, keepdims=True)
            outs.append(jnp.dot((p / den).astype(BF16), vcat, preferred_element_type=F32))
            lses.append(m + jnp.log(den))
        o_ref[...] = jnp.where(low, outs[0], outs[1])
        l_ref[...] = jnp.where(low, lses[0], lses[1])

    out = pl.BlockSpec((tq, LANES), lambda hp, r, n: (n, r * 4 + hp))
    specs = _attn_specs(length, d, tq, w128, cols) + [pl.BlockSpec((2, tq, 3 * tq), lambda hp, r, n: (hp, 0, 0))]
    shape = jax.ShapeDtypeStruct((length, d * 4 * LANES), F32)
    return pl.pallas_call(body, name=name, grid=(4, d, nb), in_specs=specs, out_specs=[out, out],
                          out_shape=[shape, shape],
                          compiler_params=_params(("parallel", "parallel", "parallel")))(
                              hv, hv, hv, hv, hv, hv, hv, bias)


def _attn_bwd(hv, bias, dyv, yv, lse, *, kind, d, tq, w128, cols, yw128, ycol, name):
    length = hv.shape[0]
    nb = length // tq
    scale = HEAD_DIM ** -0.5

    def body(q_ref, k0, k1, k2, v0, v1, v2, b_ref, dy_ref, y_ref, l_ref, dq_ref, dk_ref, dv_ref, db_ref):
        r, n = pl.program_id(1), pl.program_id(2)

        @pl.when(n == 0)
        def _():
            dk_ref[...] = jnp.zeros_like(dk_ref)
            dv_ref[...] = jnp.zeros_like(dv_ref)

        @pl.when((n == 0) & (r == 0))
        def _():
            db_ref[...] = jnp.zeros_like(db_ref)

        q = q_ref[...].astype(F32) * scale
        kcat = jnp.concatenate([k0[...], k1[...], k2[...]], axis=0).astype(BF16)
        vcat = jnp.concatenate([v0[...], v1[...], v2[...]], axis=0).astype(BF16)
        mask = _attn_mask(kind, n, tq, length)
        low = lax.broadcasted_iota(jnp.int32, (1, LANES), 1) < HEAD_DIM
        dy = dy_ref[...]
        dyy = dy * y_ref[...]
        lse_all = l_ref[...]
        dq = jnp.zeros((tq, LANES), F32)
        dk = jnp.zeros((3 * tq, LANES), F32)
        dv = jnp.zeros((3 * tq, LANES), F32)
        for hh in range(2):
            sel = low if hh == 0 else ~low
            qh = jnp.where(sel, q, 0.0).astype(BF16)
            dyh = jnp.where(sel, dy, 0.0).astype(BF16)
            dsum = jnp.sum(jnp.where(sel, dyy, 0.0), axis=1, keepdims=True)
            lse_h = lse_all[:, hh * HEAD_DIM:hh * HEAD_DIM + 1]
            sc = lax.dot_general(qh, kcat, (((1,), (1,)), ((), ())), preferred_element_type=F32) + b_ref[hh]
            p = jnp.where(mask, jnp.exp(jnp.where(mask, sc, NEG) - lse_h), 0.0)
            dp = lax.dot_general(dyh, vcat, (((1,), (1,)), ((), ())), preferred_element_type=F32)
            ds = p * (dp - dsum)
            db_ref[hh] += ds
            pb, dsb = p.astype(BF16), ds.astype(BF16)
            dv += lax.dot_general(pb, dyh, (((0,), (0,)), ((), ())), preferred_element_type=F32)
            dk += lax.dot_general(dsb, qh, (((0,), (0,)), ((), ())), preferred_element_type=F32)
            dq += jnp.where(sel, jnp.dot(dsb, kcat, preferred_element_type=F32), 0.0)
        dq_ref[...] = dq * scale
        for j in range(3):
            start = pl.multiple_of(jnp.clip(n - 1 + j, 0, nb - 1) * tq, tq)
            dk_ref[pl.ds(start, tq), :] += dk[j * tq:(j + 1) * tq]
            dv_ref[pl.ds(start, tq), :] += dv[j * tq:(j + 1) * tq]

    blk = pl.BlockSpec((tq, LANES), lambda hp, r, n: (n, r * 4 + hp))
    yblk = pl.BlockSpec((tq, LANES), lambda hp, r, n: (n, r * yw128 + ycol + hp))
    slab = pl.BlockSpec((length, LANES), lambda hp, r, n: (0, r * 4 + hp))
    bspec = pl.BlockSpec((2, tq, 3 * tq), lambda hp, r, n: (hp, 0, 0))
    specs = _attn_specs(length, d, tq, w128, cols) + [bspec, yblk, yblk, blk]
    shape = jax.ShapeDtypeStruct((length, d * 4 * LANES), F32)
    return pl.pallas_call(
        body, name=name, grid=(4, d, nb), in_specs=specs, out_specs=[blk, slab, slab, bspec],
        out_shape=[shape, shape, shape, jax.ShapeDtypeStruct(bias.shape, F32)],
        compiler_params=_params(("arbitrary", "arbitrary", "arbitrary"), VMEM_BIG))(
            hv, hv, hv, hv, hv, hv, hv, bias, dyv, yv, lse)


def _dil_combine(outs, lses, *, name):
    s, c = outs[0].shape
    t = _tile(s, (512, 256, 128, 64, 8))

    def body(o0, o1, o2, l0, l1, l2, y_ref, lt_ref):
        ls = [l0[...], l1[...], l2[...]]
        m = jnp.maximum(jnp.maximum(ls[0], ls[1]), ls[2])
        es = [jnp.exp(l - m) for l in ls]
        den = es[0] + es[1] + es[2]
        y_ref[...] = (es[0] / den) * o0[...] + (es[1] / den) * o1[...] + (es[2] / den) * o2[...]
        lt_ref[...] = m + jnp.log(den)

    row = pl.BlockSpec((t, c), lambda i: (i, 0))
    shape = jax.ShapeDtypeStruct((s, c), F32)
    return pl.pallas_call(body, name=name, grid=(s // t,), in_specs=[row] * 6, out_specs=[row, row],
                          out_shape=[shape, shape], compiler_params=_params(("parallel",)))(*outs, *lses)


def _attn_dh(na, dil, *, name):
    s, c = na[0].shape
    t = _tile(s, (256, 128, 64, 8))

    def body(*refs):
        ins, o_ref = refs[:-1], refs[-1]
        for a in range(3):
            o_ref[:, a * c:(a + 1) * c] = ins[a][...].astype(BF16)
            o_ref[:, (3 + a) * c:(4 + a) * c] = (ins[3 + a][...] + ins[6 + a][...] + ins[9 + a][...]).astype(BF16)

    row = pl.BlockSpec((t, c), lambda i: (i, 0))
    flat = list(na) + [g[a] for g in dil for a in range(3)]
    return pl.pallas_call(body, name=name, grid=(s // t,), in_specs=[row] * 12,
                          out_specs=pl.BlockSpec((t, 6 * c), lambda i: (i, 0)),
                          out_shape=jax.ShapeDtypeStruct((s, 6 * c), BF16),
                          compiler_params=_params(("parallel",)))(*flat)


def _t5_bucket(rel):
    nb = N_BUCKETS // 2
    max_exact = nb // 2
    ret = np.where(rel > 0, nb, 0)
    n = np.abs(rel)
    large = max_exact + (np.log(np.maximum(n, 1).astype(np.float32) / np.float32(max_exact))
                         / np.float32(math.log(T5_MAX_DIST / max_exact)) * np.float32(nb - max_exact)).astype(np.int32)
    large = np.minimum(large, nb - 1)
    return (ret + np.where(n < max_exact, n, large)).astype(np.int32)


def _band_bucket_index(dil):
    tq = DIL_TQ
    rel = np.arange(3 * tq)[None, :] - tq - np.arange(tq)[:, None]
    return _t5_bucket(rel * dil)


def _band_bias(t5, dil, *, name):
    tq = DIL_TQ
    buckets = [int(b) for b in _t5_bucket(np.arange(-DIL_HALF, DIL_HALF + 1) * dil)]

    def body(t_ref, o_ref):
        hh = pl.program_id(0)
        rel = (lax.broadcasted_iota(jnp.int32, (tq, 3 * tq), 1) - tq
               - lax.broadcasted_iota(jnp.int32, (tq, 3 * tq), 0))
        acc = jnp.zeros((tq, 3 * tq), F32)
        for r, b in zip(range(-DIL_HALF, DIL_HALF + 1), buckets):
            acc = jnp.where(rel == r, t_ref[b * 8 + hh], acc)
        o_ref[0] = acc

    return pl.pallas_call(body, name=name, grid=(8,),
                          in_specs=[pl.BlockSpec(memory_space=pltpu.SMEM)],
                          out_specs=pl.BlockSpec((1, tq, 3 * tq), lambda h: (h, 0, 0)),
                          out_shape=jax.ShapeDtypeStruct((8, tq, 3 * tq), F32),
                          compiler_params=_params(("parallel",)))(t5.reshape(-1))


def _na_bias(rpb, *, name):
    nr, nc = 2 * NA_KH - 1, 2 * NA_KW - 1
    tq = NA_QROWS * GRID_W
    w = GRID_W

    def body(r_ref, o_ref):
        base = pl.program_id(0) * (nr * nc)
        lane = lax.broadcasted_iota(jnp.int32, (w, LANES), 1)
        upper = lane >= w
        diff = (lane & (w - 1)) - lax.broadcasted_iota(jnp.int32, (w, LANES), 0) + NA_KW - 1
        tiles = {}
        for i in range(NA_QROWS):
            for m in range(3 * NA_QROWS // 2):
                lo = 2 * m - i + NA_KH - 1 - NA_QROWS
                if lo not in tiles:
                    acc = jnp.zeros((w, LANES), F32)
                    for dc in range(nc):
                        v_lo = r_ref[base + lo * nc + dc] if 0 <= lo < nr else 0.0
                        v_hi = r_ref[base + (lo + 1) * nc + dc] if 0 <= lo + 1 < nr else 0.0
                        acc = jnp.where(diff == dc, jnp.where(upper, v_hi, v_lo), acc)
                    tiles[lo] = acc
                o_ref[0, i * w:(i + 1) * w, m * LANES:(m + 1) * LANES] = tiles[lo]

    return pl.pallas_call(body, name=name, grid=(8,),
                          in_specs=[pl.BlockSpec(memory_space=pltpu.SMEM)],
                          out_specs=pl.BlockSpec((1, tq, 3 * tq), lambda h: (h, 0, 0)),
                          out_shape=jax.ShapeDtypeStruct((8, tq, 3 * tq), F32),
                          compiler_params=_params(("parallel",)))(rpb.reshape(-1))


def _t5_grad(dbs, idxs, *, name):
    def body(d0, d1, d2, i0, i1, i2, o_ref):
        lane = lax.broadcasted_iota(jnp.int32, (1, LANES), 1)
        lines = [jnp.zeros((1, LANES), F32) for _ in range(8)]
        for dref, iref in ((d0, i0), (d1, i1), (d2, i2)):
            idx = iref[...]
            for hh in range(8):
                xh = dref[hh]
                for b in range(N_BUCKETS):
                    val = jnp.sum(jnp.sum(jnp.where(idx == b, xh, 0.0), axis=1, keepdims=True), axis=0, keepdims=True)
                    lines[hh] = lines[hh] + jnp.where(lane == b, val, 0.0)
        for hh in range(8):
            o_ref[hh:hh + 1, :] = lines[hh]

    out = pl.pallas_call(body, name=name, out_shape=jax.ShapeDtypeStruct((8, LANES), F32))(*dbs, *idxs)
    return out[:, :N_BUCKETS].T


def _rpb_grad(db, *, name):
    nr, nc = 2 * NA_KH - 1, 2 * NA_KW - 1
    tq = NA_QROWS * GRID_W
    w = GRID_W

    def body(d_ref, o_ref):
        x = d_ref[0]
        rows = []
        for dr in range(nr):
            acc = jnp.zeros((w, w), F32)
            for i in range(NA_QROWS):
                j = i + dr - (NA_KH - 1 - NA_QROWS)
                if 0 <= j < 3 * NA_QROWS:
                    acc = acc + x[i * w:(i + 1) * w, j * w:(j + 1) * w]
            rows.append(acc)
        diff = (lax.broadcasted_iota(jnp.int32, (w, w), 1) - lax.broadcasted_iota(jnp.int32, (w, w), 0)
                + NA_KW - 1)
        lane = lax.broadcasted_iota(jnp.int32, (1, LANES), 1)
        for dr in range(nr):
            line = jnp.zeros((1, LANES), F32)
            for dc in range(nc):
                val = jnp.sum(jnp.sum(jnp.where(diff == dc, rows[dr], 0.0), axis=1, keepdims=True),
                              axis=0, keepdims=True)
                line = jnp.where(lane == dc, val, line)
            o_ref[0, dr:dr + 1, :] = line

    out = pl.pallas_call(body, name=name, grid=(8,),
                         in_specs=[pl.BlockSpec((1, tq, 3 * tq), lambda h: (h, 0, 0))],
                         out_specs=pl.BlockSpec((1, nr, LANES), lambda h: (h, 0, 0)),
                         out_shape=jax.ShapeDtypeStruct((8, nr, LANES), F32),
                         compiler_params=_params(("parallel",)))(db)
    return out[:, :, :nc]


def _exchange(src, *, gather, name):
    shape = src.shape if not gather else (N_DEV,) + src.shape

    def body(src_ref, out_ref, send_sems, recv_sems, local_sem):
        _exchange_start(src_ref, out_ref, send_sems, recv_sems, local_sem, gather)
        _exchange_wait(src_ref, out_ref, send_sems, recv_sems, local_sem, gather)

    any_spec = pl.BlockSpec(memory_space=pl.ANY)
    return pl.pallas_call(
        body, name=name, in_specs=[any_spec], out_specs=any_spec, out_shape=jax.ShapeDtypeStruct(shape, src.dtype),
        scratch_shapes=_exchange_sems())(src)


def _exchange_sems():
    return [pltpu.SemaphoreType.DMA((N_DEV - 1,)), pltpu.SemaphoreType.DMA((N_DEV - 1,)), pltpu.SemaphoreType.DMA]


def _exchange_copies(src_ref, out_ref, send_sems, recv_sems, local_sem, gather):
    x, y, c = lax.axis_index("x"), lax.axis_index("y"), lax.axis_index("c")
    me = 4 * x + 2 * y + c

    def outgoing(p):
        return src_ref if gather else src_ref.at[p]

    own = pltpu.make_async_copy(outgoing(me), out_ref.at[me], local_sem)
    sends, recvs = [], []
    for k in range(1, N_DEV):
        px = 1 - x if k & 4 else x
        py = 1 - y if k & 2 else y
        pc = 1 - c if k & 1 else c
        p = 4 * px + 2 * py + pc
        for dst, group in ((me, sends), (p, recvs)):
            group.append(pltpu.make_async_remote_copy(
                src_ref=outgoing(p), dst_ref=out_ref.at[dst], send_sem=send_sems.at[k - 1],
                recv_sem=recv_sems.at[k - 1], device_id=(px, py, pc), device_id_type=pl.DeviceIdType.MESH))
    return own, sends, recvs


def _exchange_start(*refs_and_mode):
    own, sends, _ = _exchange_copies(*refs_and_mode)
    own.start()
    for cp in sends:
        cp.start()


def _exchange_wait(*refs_and_mode):
    own, sends, recvs = _exchange_copies(*refs_and_mode)
    for cp in recvs:
        cp.wait_recv()
    for cp in sends:
        cp.wait_send()
    own.wait()


def _adamw(parts, w, m, v, *, name):
    rows, cols = w.shape
    t = _tile(rows, (FLAT_ROW_TILE, 128, 64, 32, 16, 8))

    def body(p_ref, w_ref, m_ref, v_ref, g_ref, d_ref, nm_ref, nv_ref):
        g = p_ref[0].astype(F32)
        for k in range(1, N_DEV):
            g = g + p_ref[k].astype(F32)
        nm = ADAM_B1 * m_ref[...] + (1.0 - ADAM_B1) * g
        nv = ADAM_B2 * v_ref[...] + (1.0 - ADAM_B2) * (g * g)
        m_hat = nm / (1.0 - ADAM_B1 ** ADAM_STEP)
        v_hat = nv / (1.0 - ADAM_B2 ** ADAM_STEP)
        g_ref[...] = g
        d_ref[...] = -ADAM_LR * (m_hat / (jnp.sqrt(v_hat) + ADAM_EPS) + ADAM_WD * w_ref[...])
        nm_ref[...] = nm
        nv_ref[...] = nv

    row = pl.BlockSpec((t, cols), lambda i: (i, 0))
    shape = jax.ShapeDtypeStruct((rows, cols), F32)
    return pl.pallas_call(body, name=name, grid=(rows // t,),
                          in_specs=[pl.BlockSpec((N_DEV, t, cols), lambda i: (0, i, 0)), row, row, row],
                          out_specs=[row] * 4, out_shape=[shape] * 4,
                          compiler_params=_params(("parallel",), VMEM_BIG))(parts, w, m, v)


def _flatten(arrays, dtype, row_mult):
    flat = jnp.concatenate([a.reshape(-1).astype(dtype) for a in arrays])
    chunk = FLAT_COLS * row_mult
    padded = -(-flat.shape[0] // chunk) * chunk
    return jnp.pad(flat, (0, padded - flat.shape[0])).reshape(padded // FLAT_COLS, FLAT_COLS)


def _unflatten(flat, shapes):
    flat = flat.reshape(-1)
    out, pos = [], 0
    for shp in shapes:
        size = int(np.prod(shp))
        out.append(flat[pos:pos + size].reshape(shp))
        pos += size
    return out


def _gather_full(names, local, dtype, row_mult, label):
    got = _exchange(_flatten([local[n] for n in names], dtype, row_mult), gather=True, name=label)
    per_dev = [_unflatten(got[p], [local[n].shape for n in names]) for p in range(N_DEV)]
    return {n: jnp.concatenate([per_dev[p][i] for p in range(N_DEV)], axis=SHARD_AXIS[n])
            for i, n in enumerate(names)}


def _from_shards(stacked, axis):
    _, a, b = stacked.shape
    if axis == 1:
        return jnp.transpose(stacked, (1, 0, 2)).reshape(a, N_DEV * b)
    return stacked.reshape(N_DEV * a, b)


def _to_shards(full, axis):
    ra, rb = full.shape
    if axis == 1:
        return jnp.transpose(full.reshape(ra, N_DEV, rb // N_DEV), (1, 0, 2))
    return full.reshape(N_DEV, ra // N_DEV, rb)


class _ShardedMatmulWeights:
    def __init__(self, local):
        self.local, self.full, self.parts = local, {}, {}

    def gather_src(self, n, l):
        return self.local[n][l].astype(BF16)

    def set_gathered(self, n, l, got):
        self.full[n, l] = _from_shards(got, SHARD_AXIS[n] - 1)

    def get(self, n, l):
        return self.full[n, l]

    def scatter_src(self, n, l, dw):
        return _to_shards(dw, SHARD_AXIS[n] - 1).astype(BF16)

    def set_scattered(self, n, l, parts):
        self.parts[n, l] = parts


def _role(role, i):
    mixer = 'attn_w_' if i % 2 == 0 else 'conv_w_'
    return {'in': (mixer + 'in', i // 2), 'out': (mixer + 'out', i // 2),
            'up': ('ffn_w_up', i), 'down': ('ffn_w_down', i)}[role]


def _local_step(x, tgt, w, big):
    s = x.shape[0]

    def project(a, role, i, **kw):
        carry = None
        if i + 1 < DEPTH:
            nxt = _role(role, i + 1)
            carry = (big.gather_src(*nxt), True)
        out, got = _mm(a, big.get(*_role(role, i)), carry=carry, name=role + "_fwd", **kw)
        if got is not None:
            big.set_gathered(*nxt, got)
        return out

    def project_back(a, d_out, role, i):
        key = _role(role, i)
        dw, _ = _mm(a, d_out, ta=True, name=role + "_dw")
        d_in, parts = _mm(d_out, big.get(*key), tb=True, carry=(big.scatter_src(*key, dw), False), name=role + "_dx")
        if parts is not None:
            big.set_scattered(*key, parts)
        return d_in

    na_tq = NA_QROWS * GRID_W
    band_idx = [_band_bucket_index(d) for _, d in DIL_PATTERNS]
    band_bias = [_band_bias(w['t5_bias'], d, name=f"band_bias_{d}") for _, d in DIL_PATTERNS]
    na_cols, dil_cols = (0, 4, 8), (12, 16, 20)
    grads = {n: [None] * w[n].shape[0] for n in SMALL_SHARDED + REPLICATED if n != 't5_bias'}
    saved = []
    x16 = x.astype(BF16)

    for i in range(DEPTH):
        j = i // 2
        st = {'x': x, 'x16': x16}
        if i % 2 == 0:
            h = project(x16, 'in', i, out_dtype=BF16)
            na_bias = _na_bias(w['na_rpb'][j], name="na_bias")
            o_na, l_na = _attn_fwd(h, na_bias, kind="na", d=1, tq=na_tq, w128=24, cols=na_cols, name="na_fwd")
            outs, lses = [], []
            for (_, d), bias in zip(DIL_PATTERNS, band_bias):
                o, l = _attn_fwd(h.reshape(s // d, d * h.shape[1]), bias, kind="band", d=d, tq=DIL_TQ, w128=24,
                                 cols=dil_cols, name=f"dil_fwd_{d}")
                outs.append(o.reshape(s, -1))
                lses.append(l.reshape(s, -1))
            y_dil, l_dil = _dil_combine(outs, lses, name="dil_combine")
            mid = jnp.concatenate([o_na, y_dil], axis=1)
            mid16 = mid.astype(BF16)
            ymix = project(mid16, 'out', i)
            st.update(h=h, mid=mid, mid16=mid16, l_na=l_na, l_dil=l_dil, na_bias=na_bias)
        else:
            h = project(x16, 'in', i)
            mid16, u2 = _conv_mid_fwd(h, w['conf_dw_w'][j], w['conf_dw_b'][j], w['conf_ln_g'][j], w['conf_ln_b'][j],
                                      w['sconv_w'][j], name="conv_mid_fwd")
            ymix = project(mid16, 'out', i)
            st.update(h=h, mid16=mid16, u2=u2)
        xa, xa16 = _ln_fwd(x, ymix, w['mix_ln_g'][i], w['mix_ln_b'][i], name="mix_ln")
        hu = project(xa16, 'up', i)
        act16 = _ffn_mid_fwd(hu, w['ffn_dw_w'][i], name="ffn_mid_fwd")
        yffn = project(act16, 'down', i)
        xb, xb16 = _ln_fwd(xa, yffn, w['ffn_ln_g'][i], w['ffn_ln_b'][i], name="ffn_ln")
        st.update(ymix=ymix, xa=xa, xa16=xa16, hu=hu, act16=act16, yffn=yffn)
        saved.append(st)
        x, x16 = xb, xb16

    loss, d1 = _loss_head(x, tgt, name="loss_head")
    d2 = None
    g_t5 = None
    for i in reversed(range(DEPTH)):
        j = i // 2
        st = saved[i]
        dz, dz16, dg, db = _ln_bwd(st['xa'], st['yffn'], w['ffn_ln_g'][i], d1, d2, name="ffn_ln_bwd")
        grads['ffn_ln_g'][i], grads['ffn_ln_b'][i] = dg, db
        dact = project_back(st['act16'], dz16, 'down', i)
        dhu, grads['ffn_dw_w'][i] = _ffn_mid_bwd(st['hu'], w['ffn_dw_w'][i], dact, name="ffn_mid_bwd")
        dxa = project_back(st['xa16'], dhu, 'up', i)
        dz, dz1, dg, db = _ln_bwd(st['x'], st['ymix'], w['mix_ln_g'][i], dz, dxa, name="mix_ln_bwd")
        grads['mix_ln_g'][i], grads['mix_ln_b'][i] = dg, db
        dmid = project_back(st['mid16'], dz1, 'out', i)
        if i % 2 == 0:
            h = st['h']
            dq, dk, dv, dbias = _attn_bwd(h, st['na_bias'], dmid, st['mid'], st['l_na'], kind="na", d=1, tq=na_tq,
                                          w128=24, cols=na_cols, yw128=8, ycol=0, name="na_bwd")
            grads['na_rpb'][j] = _rpb_grad(dbias, name="rpb_grad")
            dil, dbs = [], []
            for (_, d), bias in zip(DIL_PATTERNS, band_bias):
                view = lambda a: a.reshape(s // d, d * a.shape[1])
                g = _attn_bwd(view(h), bias, view(dmid), view(st['mid']), view(st['l_dil']), kind="band", d=d,
                              tq=DIL_TQ, w128=24, cols=dil_cols, yw128=8, ycol=4, name=f"dil_bwd_{d}")
                dil.append([a.reshape(s, -1) for a in g[:3]])
                dbs.append(g[3])
            t5 = _t5_grad(dbs, band_idx, name="t5_grad")
            g_t5 = t5 if g_t5 is None else g_t5 + t5
            dh = _attn_dh((dq, dk, dv), dil, name="attn_dh")
        else:
            dh, dw31, db31, dlg, dlb, dw3 = _conv_mid_bwd(st['h'], st['u2'], dmid, w['conf_dw_w'][j],
                                                          w['conf_ln_g'][j], w['conf_ln_b'][j], w['sconv_w'][j],
                                                          name="conv_mid_bwd")
            grads['conf_dw_w'][j], grads['conf_dw_b'][j] = dw31, db31
            grads['conf_ln_g'][j], grads['conf_ln_b'][j], grads['sconv_w'][j] = dlg, dlb, dw3
        d1, d2 = dz, project_back(st['x16'], dh, 'in', i)
    dx = _axpy(d1, d2, name="grad_x")
    full = {n: jnp.stack(g) for n, g in grads.items()}
    full['t5_bias'] = g_t5
    return loss, dx, full


def kernel(x, t5_bias, attn_w_in, attn_w_out, na_rpb, conv_w_in, conf_dw_w, conf_dw_b, conf_ln_g, conf_ln_b, sconv_w, conv_w_out, ffn_w_up, ffn_dw_w, ffn_w_down, mix_ln_g, mix_ln_b, ffn_ln_g, ffn_ln_b, loss_target, m_t5_bias, m_attn_w_in, m_attn_w_out, m_na_rpb, m_conv_w_in, m_conf_dw_w, m_conf_dw_b, m_conf_ln_g, m_conf_ln_b, m_sconv_w, m_conv_w_out, m_ffn_w_up, m_ffn_dw_w, m_ffn_w_down, m_mix_ln_g, m_mix_ln_b, m_ffn_ln_g, m_ffn_ln_b, v_t5_bias, v_attn_w_in, v_attn_w_out, v_na_rpb, v_conv_w_in, v_conf_dw_w, v_conf_dw_b, v_conf_ln_g, v_conf_ln_b, v_sconv_w, v_conv_w_out, v_ffn_w_up, v_ffn_dw_w, v_ffn_w_down, v_mix_ln_g, v_mix_ln_b, v_ffn_ln_g, v_ffn_ln_b):
    args = dict(locals())
    local = {n: args[n] for n in WEIGHTS}
    mom1 = {n: args['m_' + n] for n in WEIGHTS}
    mom2 = {n: args['v_' + n] for n in WEIGHTS}

    kinds = ('grad', 'delta', 'new_m', 'new_v')
    small = {n: local[n] for n in REPLICATED}
    small.update(_gather_full(SMALL_SHARDED, local, F32, 8, "gather_small_weights"))
    big = _ShardedMatmulWeights({n: local[n] for n in MATMUL_WEIGHTS})
    for role in ('in', 'out', 'up', 'down'):
        n, l = _role(role, 0)
        big.set_gathered(n, l, _exchange(big.gather_src(n, l), gather=True, name="gather_first_" + role))

    loss, dx, grads = _local_step(x[0], loss_target[0], small, big)
    loss = lax.psum(loss, MESH_AXES)

    out = {}
    for n in MATMUL_WEIGHTS:
        per_layer = [_adamw(big.parts[n, l], local[n][l], mom1[n][l], mom2[n][l], name="adamw_" + n)
                     for l in range(local[n].shape[0])]
        for k, kind in enumerate(kinds):
            out[kind + '_' + n] = jnp.stack([res[k] for res in per_layer])
    for names, sharded, label in ((SMALL_SHARDED, True, "small"), (REPLICATED, False, "replicated")):
        shapes = [local[n].shape for n in names]
        if sharded:
            per_dev = [_flatten([lax.slice_in_dim(grads[n], p * local[n].shape[SHARD_AXIS[n]],
                                                  (p + 1) * local[n].shape[SHARD_AXIS[n]], axis=SHARD_AXIS[n])
                                 for n in names], F32, 8) for p in range(N_DEV)]
            parts = _exchange(jnp.stack(per_dev), gather=False, name="scatter_small_grads")
        else:
            parts = _exchange(_flatten([grads[n] for n in names], F32, 8), gather=True, name="gather_replicated_grads")
        res = _adamw(parts, _flatten([local[n] for n in names], F32, 8), _flatten([mom1[n] for n in names], F32, 8),
                     _flatten([mom2[n] for n in names], F32, 8), name="adamw_" + label)
        for kind, flat in zip(kinds, res):
            for n, a in zip(names, _unflatten(flat, shapes)):
                out[kind + '_' + n] = a

    return (loss, dx[None], *[out[k + '_' + n] for k in ('grad', 'delta', 'new_m', 'new_v') for n in WEIGHTS])
```

```python
import functools
import math

import jax
import jax.numpy as jnp
import numpy as np
from jax import lax
from jax.experimental import pallas as pl
from jax.experimental.pallas import tpu as pltpu

F32 = jnp.float32
BF16 = jnp.bfloat16

N_DEV = 8
MESH_AXES = ("x", "y", "c")
DEPTH = 4
GRID_W = 64
GRID_SHIFT = 6
HEAD_DIM = 64
NA_KH = 8
NA_KW = 16
NA_QROWS = 4
DIL_PATTERNS = ((128, 1), (512, 4), (2048, 16))
DIL_HALF = 64
DIL_TQ = 128
DIL_NSUB = {1: 8, 4: 2, 16: 1}
NA_NSUB = 4
N_BUCKETS = 32
T5_MAX_DIST = 1024
CONF_CH = 512
CONF_K = 31
SC_K = 3
FFN_K = 3
LN_EPS = 1e-5
NEG = -1e30
ALPHA = (2 * DEPTH) ** 0.25
ADAM_LR = 0.001
ADAM_B1 = 0.9
ADAM_B2 = 0.999
ADAM_EPS = 1e-08
ADAM_WD = 0.01
ADAM_STEP = 10

LANES = 128
SUBLANES = 8
VMEM_BIG = 48 * 1024 * 1024
FLAT_COLS = 1024
FLAT_ROW_TILE = 256

WEIGHTS = ['t5_bias', 'attn_w_in', 'attn_w_out', 'na_rpb', 'conv_w_in', 'conf_dw_w', 'conf_dw_b', 'conf_ln_g',
           'conf_ln_b', 'sconv_w', 'conv_w_out', 'ffn_w_up', 'ffn_dw_w', 'ffn_w_down', 'mix_ln_g', 'mix_ln_b',
           'ffn_ln_g', 'ffn_ln_b']
SHARD_AXIS = {'attn_w_in': 2, 'attn_w_out': 1, 'conv_w_in': 2, 'conf_dw_w': 2, 'conf_dw_b': 1, 'conf_ln_g': 1,
              'conf_ln_b': 1, 'sconv_w': 2, 'conv_w_out': 1, 'ffn_w_up': 2, 'ffn_dw_w': 2, 'ffn_w_down': 1}
MATMUL_WEIGHTS = ['attn_w_in', 'attn_w_out', 'conv_w_in', 'conv_w_out', 'ffn_w_up', 'ffn_w_down']
SMALL_SHARDED = ['conf_dw_w', 'conf_dw_b', 'conf_ln_g', 'conf_ln_b', 'sconv_w', 'ffn_dw_w']
SHARDED = MATMUL_WEIGHTS + SMALL_SHARDED
REPLICATED = ['t5_bias', 'na_rpb', 'mix_ln_g', 'mix_ln_b', 'ffn_ln_g', 'ffn_ln_b']


def _tile(n, cands):
    for c in cands:
        if n % c == 0:
            return c
    return n


def _params(sem, vmem=None):
    return pltpu.CompilerParams(dimension_semantics=sem, vmem_limit_bytes=vmem)


def _sigmoid(x):
    return 1.0 / (1.0 + jnp.exp(-x))


MM_TILES = (1024, 512, 256, 128)
MM_FULL_K = 3072


def _mm(a, b, *, ta=False, tb=False, out_dtype=F32, carry=None, name):
    assert a.dtype == BF16 and b.dtype == BF16, (name, a.dtype, b.dtype)
    m, k = (a.shape[1], a.shape[0]) if ta else a.shape
    n = b.shape[0] if tb else b.shape[1]
    tm, tn = _tile(m, MM_TILES), _tile(n, MM_TILES)
    tk = k if k <= MM_FULL_K else _tile(k, MM_TILES)
    grid = (m // tm, n // tn, k // tk)
    nk = grid[2]
    dims = (((0 if ta else 1,), (1 if tb else 0,)), ((), ()))
    use_acc = nk > 1 and out_dtype != F32
    src, gather = carry if carry is not None else (None, False)
    carried = src is not None

    def body(*refs):
        if carried:
            a_ref, b_ref, src_ref, o_ref, x_ref = refs[:5]
            send_sems, recv_sems, local_sem = refs[-3:]
            scratch = refs[5:-3]
            ids = [pl.program_id(ax) for ax in range(3)]
            first = (ids[0] == 0) & (ids[1] == 0) & (ids[2] == 0)
            last = (ids[0] == grid[0] - 1) & (ids[1] == grid[1] - 1) & (ids[2] == grid[2] - 1)

            @pl.when(first)
            def _():
                _exchange_start(src_ref, x_ref, send_sems, recv_sems, local_sem, gather)
        else:
            a_ref, b_ref, o_ref = refs[:3]
            scratch = refs[3:]

        part = lax.dot_general(a_ref[...], b_ref[...], dims, preferred_element_type=F32)
        if nk == 1:
            o_ref[...] = part.astype(out_dtype)
        else:
            acc_ref = scratch[0] if use_acc else o_ref
            kk = pl.program_id(2)

            @pl.when(kk == 0)
            def _():
                acc_ref[...] = part

            @pl.when(kk > 0)
            def _():
                acc_ref[...] += part

            if use_acc:
                @pl.when(kk == nk - 1)
                def _():
                    o_ref[...] = acc_ref[...].astype(out_dtype)

        if carried:
            @pl.when(last)
            def _():
                _exchange_wait(src_ref, x_ref, send_sems, recv_sems, local_sem, gather)

    a_spec = pl.BlockSpec((tk, tm), lambda i, j, q: (q, i)) if ta else pl.BlockSpec((tm, tk), lambda i, j, q: (i, q))
    b_spec = pl.BlockSpec((tn, tk), lambda i, j, q: (j, q)) if tb else pl.BlockSpec((tk, tn), lambda i, j, q: (q, j))
    o_spec = pl.BlockSpec((tm, tn), lambda i, j, q: (i, j))
    o_shape = jax.ShapeDtypeStruct((m, n), out_dtype)
    scratch = [pltpu.VMEM((tm, tn), F32)] if use_acc else []
    if not carried:
        out = pl.pallas_call(
            body, name=name, grid=grid, in_specs=[a_spec, b_spec], out_specs=o_spec, out_shape=o_shape,
            scratch_shapes=scratch,
            compiler_params=_params(("parallel", "parallel", "arbitrary"), VMEM_BIG))(a, b)
        return out, None
    any_spec = pl.BlockSpec(memory_space=pl.ANY)
    x_shape = jax.ShapeDtypeStruct(((N_DEV,) + src.shape) if gather else src.shape, src.dtype)
    return pl.pallas_call(
        body, name=name, grid=grid, in_specs=[a_spec, b_spec, any_spec], out_specs=[o_spec, any_spec],
        out_shape=[o_shape, x_shape], scratch_shapes=scratch + _exchange_sems(),
        compiler_params=_params(("arbitrary", "arbitrary", "arbitrary"), VMEM_BIG))(a, b, src)


def _ln_stats(z):
    mu = jnp.mean(z, axis=-1, keepdims=True)
    zc = z - mu
    var = jnp.mean(zc * zc, axis=-1, keepdims=True)
    rstd = lax.rsqrt(var + LN_EPS)
    return zc * rstd, rstd


def _ln_fwd(x, y, g, b, *, name):
    s, d = x.shape
    t = _tile(s, (256, 128, 64, 8))

    def body(x_ref, y_ref, g_ref, b_ref, o_ref, o16_ref):
        xh, _ = _ln_stats(ALPHA * x_ref[...] + y_ref[...])
        out = xh * g_ref[...] + b_ref[...]
        o_ref[...] = out
        o16_ref[...] = out.astype(BF16)

    row = pl.BlockSpec((t, d), lambda i: (i, 0))
    vec = pl.BlockSpec((1, d), lambda i: (0, 0))
    return pl.pallas_call(body, name=name, grid=(s // t,), in_specs=[row, row, vec, vec], out_specs=[row, row],
                          out_shape=[jax.ShapeDtypeStruct((s, d), F32), jax.ShapeDtypeStruct((s, d), BF16)],
                          compiler_params=_params(("parallel",)))(x, y, g.reshape(1, d), b.reshape(1, d))


def _ln_bwd(x, y, g, d1, d2, *, name):
    s, d = x.shape
    t = _tile(s, (256, 128, 64, 8))
    two = d2 is not None

    def body(*refs):
        if two:
            x_ref, y_ref, g_ref, d1_ref, d2_ref, dz_ref, dz16_ref, dg_ref, db_ref = refs
            dout = ALPHA * d1_ref[...] + d2_ref[...]
        else:
            x_ref, y_ref, g_ref, d1_ref, dz_ref, dz16_ref, dg_ref, db_ref = refs
            dout = d1_ref[...]

        @pl.when(pl.program_id(0) == 0)
        def _():
            dg_ref[...] = jnp.zeros_like(dg_ref)
            db_ref[...] = jnp.zeros_like(db_ref)

        xh, rstd = _ln_stats(ALPHA * x_ref[...] + y_ref[...])
        dxh = dout * g_ref[...]
        dz = rstd * (dxh - jnp.mean(dxh, axis=-1, keepdims=True) - xh * jnp.mean(dxh * xh, axis=-1, keepdims=True))
        dz_ref[...] = dz
        dz16_ref[...] = dz.astype(BF16)
        dg_ref[...] += jnp.sum(dout * xh, axis=0, keepdims=True)
        db_ref[...] += jnp.sum(dout, axis=0, keepdims=True)

    row = pl.BlockSpec((t, d), lambda i: (i, 0))
    vec = pl.BlockSpec((1, d), lambda i: (0, 0))
    ins = [x, y, g.reshape(1, d), d1] + ([d2] if two else [])
    specs = [row, row, vec, row] + ([row] if two else [])
    dz, dz16, dg, db = pl.pallas_call(
        body, name=name, grid=(s // t,), in_specs=specs, out_specs=[row, row, vec, vec],
        out_shape=[jax.ShapeDtypeStruct((s, d), F32), jax.ShapeDtypeStruct((s, d), BF16),
                   jax.ShapeDtypeStruct((1, d), F32), jax.ShapeDtypeStruct((1, d), F32)],
        compiler_params=_params(("arbitrary",)))(*ins)
    return dz, dz16, dg.reshape(d), db.reshape(d)


def _axpy(d1, d2, *, name):
    s, d = d1.shape
    t = _tile(s, (256, 128, 64, 8))

    def body(a_ref, b_ref, o_ref):
        o_ref[...] = ALPHA * a_ref[...] + b_ref[...]

    row = pl.BlockSpec((t, d), lambda i: (i, 0))
    return pl.pallas_call(body, name=name, grid=(s // t,), in_specs=[row, row], out_specs=row,
                          out_shape=jax.ShapeDtypeStruct((s, d), F32), compiler_params=_params(("parallel",)))(d1, d2)


def _loss_head(y, tgt, *, name):
    s, d = y.shape
    t = _tile(s, (256, 128, 64, 8))

    def body(y_ref, t_ref, l_ref, dy_ref):
        @pl.when(pl.program_id(0) == 0)
        def _():
            l_ref[...] = jnp.zeros_like(l_ref)

        err = y_ref[...] - t_ref[...]
        dy_ref[...] = err * (1.0 / d)
        l_ref[...] += 0.5 * jnp.sum(jnp.sum(err * err, axis=1, keepdims=True), axis=0, keepdims=True) * (1.0 / d)

    row = pl.BlockSpec((t, d), lambda i: (i, 0))
    one = pl.BlockSpec((SUBLANES, LANES), lambda i: (0, 0))
    loss, dy = pl.pallas_call(
        body, name=name, grid=(s // t,), in_specs=[row, row], out_specs=[one, row],
        out_shape=[jax.ShapeDtypeStruct((SUBLANES, LANES), F32), jax.ShapeDtypeStruct((s, d), F32)],
        compiler_params=_params(("arbitrary",)))(y, tgt)
    return loss[0, 0], dy


def _halo_specs(s, t, halo, cb, col):
    per = t // halo
    last = s // halo - 1
    return [pl.BlockSpec((t, cb), lambda j, i: (i, col(j))),
            pl.BlockSpec((halo, cb), lambda j, i: (jnp.maximum(i * per - 1, 0), col(j))),
            pl.BlockSpec((halo, cb), lambda j, i: (jnp.minimum((i + 1) * per, last), col(j)))]


def _extended(main_ref, prev_ref, next_ref, i, n):
    prev = jnp.where(i > 0, prev_ref[...], 0.0)
    nxt = jnp.where(i < n - 1, next_ref[...], 0.0)
    return jnp.concatenate([prev, main_ref[...], nxt], axis=0)


def _shift(ext, o):
    if o == 0:
        return ext
    return pltpu.roll(ext, (-o) % ext.shape[0], 0)


def _conv(ext, w_ref, k, sign=1):
    acc = None
    for j in range(k):
        term = w_ref[j:j + 1, :] * _shift(ext, sign * (j - k // 2))
        acc = term if acc is None else acc + term
    return acc


def _conv_wgrad(dw_ref, d_main, x_ext, k, halo, t):
    for j in range(k):
        xs = _shift(x_ext, j - k // 2)[halo:halo + t]
        dw_ref[j:j + 1, :] += jnp.sum(d_main * xs, axis=0, keepdims=True)


def _ffn_mid_fwd(hu, w, *, name):
    s, f2 = hu.shape
    f = f2 // 2
    t, cb, halo = _tile(s, (512, 256, 128)), _tile(f, (256, 128)), SUBLANES
    nt, nc = s // t, f // cb

    def body(g_ref, gp_ref, gn_ref, u_ref, up_ref, un_ref, wg_ref, wu_ref, a_ref):
        i = pl.program_id(1)
        hg = _conv(_extended(g_ref, gp_ref, gn_ref, i, nt), wg_ref, FFN_K)[halo:halo + t]
        hu_ = _conv(_extended(u_ref, up_ref, un_ref, i, nt), wu_ref, FFN_K)[halo:halo + t]
        a_ref[...] = (hg * _sigmoid(hg) * hu_).astype(BF16)

    specs = (_halo_specs(s, t, halo, cb, lambda j: j) + _halo_specs(s, t, halo, cb, lambda j: j + nc)
             + [pl.BlockSpec((FFN_K, cb), lambda j, i: (0, j)), pl.BlockSpec((FFN_K, cb), lambda j, i: (0, j + nc))])
    return pl.pallas_call(body, name=name, grid=(nc, nt), in_specs=specs,
                          out_specs=pl.BlockSpec((t, cb), lambda j, i: (i, j)),
                          out_shape=jax.ShapeDtypeStruct((s, f), BF16),
                          compiler_params=_params(("parallel", "parallel")))(hu, hu, hu, hu, hu, hu, w, w)


def _ffn_mid_bwd(hu, w, da, *, name):
    s, f2 = hu.shape
    f = f2 // 2
    t, cb, halo = _tile(s, (512, 256, 128)), _tile(f, (256, 128)), SUBLANES
    nt, nc = s // t, f // cb

    def body(g_ref, gp_ref, gn_ref, u_ref, up_ref, un_ref, a_ref, ap_ref, an_ref, wg_ref, wu_ref,
             dg_ref, du_ref, dwg_ref, dwu_ref):
        i = pl.program_id(1)

        @pl.when(i == 0)
        def _():
            dwg_ref[...] = jnp.zeros_like(dwg_ref)
            dwu_ref[...] = jnp.zeros_like(dwu_ref)

        xg = _extended(g_ref, gp_ref, gn_ref, i, nt)
        xu = _extended(u_ref, up_ref, un_ref, i, nt)
        dae = _extended(a_ref, ap_ref, an_ref, i, nt)
        hg = _conv(xg, wg_ref, FFN_K)
        hu_ = _conv(xu, wu_ref, FFN_K)
        sg = _sigmoid(hg)
        d_hg = dae * hu_ * (sg * (1.0 + hg * (1.0 - sg)))
        d_hu = dae * (hg * sg)
        dg_ref[...] = _conv(d_hg, wg_ref, FFN_K, sign=-1)[halo:halo + t].astype(BF16)
        du_ref[...] = _conv(d_hu, wu_ref, FFN_K, sign=-1)[halo:halo + t].astype(BF16)
        _conv_wgrad(dwg_ref, d_hg[halo:halo + t], xg, FFN_K, halo, t)
        _conv_wgrad(dwu_ref, d_hu[halo:halo + t], xu, FFN_K, halo, t)

    wspec = lambda off: pl.BlockSpec((FFN_K, cb), lambda j, i: (0, j + off))
    specs = (_halo_specs(s, t, halo, cb, lambda j: j) + _halo_specs(s, t, halo, cb, lambda j: j + nc)
             + _halo_specs(s, t, halo, cb, lambda j: j) + [wspec(0), wspec(nc)])
    tile = pl.BlockSpec((t, cb), lambda j, i: (i, j))
    dg, du, dwg, dwu = pl.pallas_call(
        body, name=name, grid=(nc, nt), in_specs=specs, out_specs=[tile, tile, wspec(0), wspec(0)],
        out_shape=[jax.ShapeDtypeStruct((s, f), BF16), jax.ShapeDtypeStruct((s, f), BF16),
                   jax.ShapeDtypeStruct((FFN_K, f), F32), jax.ShapeDtypeStruct((FFN_K, f), F32)],
        compiler_params=_params(("parallel", "arbitrary")))(hu, hu, hu, hu, hu, hu, da, da, da, w, w)
    return jnp.concatenate([dg, du], axis=1), jnp.concatenate([dwg, dwu], axis=1)


CONV_HALO = 16


def _conv_mid_fwd(h, dw_w, dw_b, ln_g, ln_b, sc_w, *, name):
    s = h.shape[0]
    c = CONF_CH
    t, halo = _tile(s, (256, 128)), CONV_HALO
    nt = s // t

    def body(ca, cap, can, cg, cgp, cgn, gb, gc, gcp, gcn, hx, hxp, hxn, w31, b31, lg, lb, w3, o_ref, u2_ref):
        i = pl.program_id(1)
        u1 = _extended(ca, cap, can, i, nt) * _sigmoid(_extended(cg, cgp, cgn, i, nt))
        u2 = _conv(u1, w31, CONF_K)[halo:halo + t] + b31[...]
        u2_ref[...] = u2
        xh, _ = _ln_stats(u2)
        yl = xh * lg[...] + lb[...]
        o_ref[:, 0:c] = (yl * _sigmoid(yl)).astype(BF16)
        p = _extended(gc, gcp, gcn, i, nt) * _extended(hx, hxp, hxn, i, nt)
        o_ref[:, c:2 * c] = (gb[...] * _conv(p, w3, SC_K)[halo:halo + t]).astype(BF16)

    hs = lambda blk: _halo_specs(s, t, halo, c, lambda j: blk)
    vec = lambda r: pl.BlockSpec((r, c), lambda j, i: (0, 0))
    specs = hs(0) + hs(1) + hs(2)[:1] + hs(3) + hs(4) + [vec(CONF_K), vec(1), vec(1), vec(1), vec(SC_K)]
    return pl.pallas_call(
        body, name=name, grid=(1, nt), in_specs=specs,
        out_specs=[pl.BlockSpec((t, 2 * c), lambda j, i: (i, 0)), pl.BlockSpec((t, c), lambda j, i: (i, 0))],
        out_shape=[jax.ShapeDtypeStruct((s, 2 * c), BF16), jax.ShapeDtypeStruct((s, c), F32)],
        compiler_params=_params(("parallel", "parallel")))(
            h, h, h, h, h, h, h, h, h, h, h, h, h, dw_w, dw_b.reshape(1, c), ln_g.reshape(1, c),
            ln_b.reshape(1, c), sc_w)


def _conv_mid_bwd(h, u2, dm, dw_w, ln_g, ln_b, sc_w, *, name):
    s = h.shape[0]
    c = CONF_CH
    t, halo = _tile(s, (256, 128)), CONV_HALO
    nt = s // t

    def body(ca, cap, can, cg, cgp, cgn, gb, gbp, gbn, gc, gcp, gcn, hx, hxp, hxn, u2r, u2p, u2n,
             du, dup, dun, dz, dzp, dzn, w31, lg, lb, w3,
             dh_ref, dw31_ref, db31_ref, dlg_ref, dlb_ref, dw3_ref):
        i = pl.program_id(1)

        @pl.when(i == 0)
        def _():
            for r in (dw31_ref, db31_ref, dlg_ref, dlb_ref, dw3_ref):
                r[...] = jnp.zeros_like(r)

        main = slice(halo, halo + t)
        xh, rstd = _ln_stats(_extended(u2r, u2p, u2n, i, nt))
        yl = xh * lg[...] + lb[...]
        sg = _sigmoid(yl)
        d_yl = _extended(du, dup, dun, i, nt) * (sg * (1.0 + yl * (1.0 - sg)))
        dlg_ref[...] += jnp.sum((d_yl * xh)[main], axis=0, keepdims=True)
        dlb_ref[...] += jnp.sum(d_yl[main], axis=0, keepdims=True)
        dxh = d_yl * lg[...]
        du2 = rstd * (dxh - jnp.mean(dxh, axis=-1, keepdims=True) - xh * jnp.mean(dxh * xh, axis=-1, keepdims=True))
        db31_ref[...] += jnp.sum(du2[main], axis=0, keepdims=True)
        cae = _extended(ca, cap, can, i, nt)
        sc = _sigmoid(_extended(cg, cgp, cgn, i, nt))
        u1 = cae * sc
        _conv_wgrad(dw31_ref, du2[main], u1, CONF_K, halo, t)
        du1 = _conv(du2, w31, CONF_K, sign=-1)[main]
        dh_ref[:, 0:c] = (du1 * sc[main]).astype(BF16)
        dh_ref[:, c:2 * c] = (du1 * (cae * sc * (1.0 - sc))[main]).astype(BF16)
        gce = _extended(gc, gcp, gcn, i, nt)
        hxe = _extended(hx, hxp, hxn, i, nt)
        p = gce * hxe
        dze = _extended(dz, dzp, dzn, i, nt)
        d_c3 = dze * _extended(gb, gbp, gbn, i, nt)
        dh_ref[:, 2 * c:3 * c] = (dze[main] * _conv(p, w3, SC_K)[main]).astype(BF16)
        _conv_wgrad(dw3_ref, d_c3[main], p, SC_K, halo, t)
        dp = _conv(d_c3, w3, SC_K, sign=-1)[main]
        dh_ref[:, 3 * c:4 * c] = (dp * hxe[main]).astype(BF16)
        dh_ref[:, 4 * c:5 * c] = (dp * gce[main]).astype(BF16)

    hs = lambda blk: _halo_specs(s, t, halo, c, lambda j: blk)
    vec = lambda r: pl.BlockSpec((r, c), lambda j, i: (0, 0))
    specs = (hs(0) + hs(1) + hs(2) + hs(3) + hs(4) + hs(0) + hs(0) + hs(1)
             + [vec(CONF_K), vec(1), vec(1), vec(SC_K)])
    outs = pl.pallas_call(
        body, name=name, grid=(1, nt), in_specs=specs,
        out_specs=[pl.BlockSpec((t, 5 * c), lambda j, i: (i, 0)), vec(CONF_K), vec(1), vec(1), vec(1), vec(SC_K)],
        out_shape=[jax.ShapeDtypeStruct((s, 5 * c), BF16), jax.ShapeDtypeStruct((CONF_K, c), F32),
                   jax.ShapeDtypeStruct((1, c), F32), jax.ShapeDtypeStruct((1, c), F32),
                   jax.ShapeDtypeStruct((1, c), F32), jax.ShapeDtypeStruct((SC_K, c), F32)],
        compiler_params=_params(("arbitrary", "arbitrary")))(
            h, h, h, h, h, h, h, h, h, h, h, h, h, h, h, u2, u2, u2, dm, dm, dm, dm, dm, dm,
            dw_w, ln_g.reshape(1, c), ln_b.reshape(1, c), sc_w)
    dh, dw31, db31, dlg, dlb, dw3 = outs
    return dh, dw31, db31.reshape(c), dlg.reshape(c), dlb.reshape(c), dw3


def _attn_mask(kind, n, tq, length):
    iq = lax.broadcasted_iota(jnp.int32, (tq, 1), 0)
    ik = lax.broadcasted_iota(jnp.int32, (1, 3 * tq), 1)
    if kind == "band":
        rel = ik - tq - iq
        kpos = (n - 1) * tq + ik
        return (jnp.abs(rel) <= DIL_HALF) & (kpos >= 0) & (kpos < length)
    rows = length // GRID_W
    rq = n * NA_QROWS + (iq >> GRID_SHIFT)
    cq = iq & (GRID_W - 1)
    rk = (n - 1) * NA_QROWS + (ik >> GRID_SHIFT)
    ck = ik & (GRID_W - 1)
    r0 = jnp.clip(rq - NA_KH // 2, 0, rows - NA_KH)
    c0 = jnp.clip(cq - NA_KW // 2, 0, GRID_W - NA_KW)
    return (rk >= r0) & (rk < r0 + NA_KH) & (ck >= c0) & (ck < c0 + NA_KW)


def _mask_tiles(kind, tq, length, *, name):
    nb = length // tq

    def body(o_ref):
        v = pl.program_id(0)
        n = jnp.where((v == 1) | (v == 3), 0, jnp.where(v == 2, nb - 1, 1))
        o_ref[0] = jnp.where(_attn_mask(kind, n, tq, length), 0.0, NEG)

    return pl.pallas_call(body, name=name, grid=(4,), out_specs=pl.BlockSpec((1, tq, 3 * tq), lambda v: (v, 0, 0)),
                          out_shape=jax.ShapeDtypeStruct((4, tq, 3 * tq), F32),
                          compiler_params=_params(("parallel",)))()


class _AttnGeom:
    def __init__(self, s, d, tq, nsub):
        self.s, self.d, self.tq, self.nsub = s, d, tq, nsub
        self.halo = tq * d
        self.rows = nsub * self.halo
        self.nbig = s // self.rows
        self.ext = self.rows + 2 * self.halo
        assert s % self.rows == 0

    def main(self, col):
        return pl.BlockSpec((self.rows, LANES), lambda hp, n: (n, col + hp))

    def with_halos(self, col):
        last = self.s // self.halo - 1
        return [self.main(col),
                pl.BlockSpec((self.halo, LANES), lambda hp, n: (jnp.maximum(n * self.nsub - 1, 0), col + hp)),
                pl.BlockSpec((self.halo, LANES), lambda hp, n: (jnp.minimum((n + 1) * self.nsub, last), col + hp))]

    def fill_ext(self, ext_ref, main_ref, prev_ref, next_ref):
        ext_ref[0:self.halo] = prev_ref[...].astype(F32)
        ext_ref[self.halo:self.halo + self.rows] = main_ref[...].astype(F32)
        ext_ref[self.halo + self.rows:self.ext] = next_ref[...].astype(F32)

    def rows_of(self, r, pos, count):
        start = r + pos * self.d
        return pl.ds(start, count, stride=self.d) if self.d > 1 else pl.ds(start, count)

    def variant(self, n, sub):
        v = 0
        if sub == 0:
            v = v + jnp.where(n == 0, 1, 0)
        if sub == self.nsub - 1:
            v = v + jnp.where(n == self.nbig - 1, 2, 0)
        return v


def _attn_fwd(h, bias, mask, *, d, tq, nsub, cols, name):
    geo = _AttnGeom(h.shape[0], d, tq, nsub)
    scale = HEAD_DIM ** -0.5

    def body(q_ref, km, kp, kn, vm, vp, vn, b_ref, m_ref, o_ref, l_ref, kext, vext):
        n = pl.program_id(1)
        geo.fill_ext(kext, km, kp, kn)
        geo.fill_ext(vext, vm, vp, vn)
        low = lax.broadcasted_iota(jnp.int32, (1, LANES), 1) < HEAD_DIM
        for r in range(d):
            for sub in range(nsub):
                madd = m_ref[geo.variant(n, sub)]
                q = q_ref[geo.rows_of(r, sub * tq, tq), :].astype(F32) * scale
                ks = kext[geo.rows_of(r, sub * tq, 3 * tq), :].astype(BF16)
                vs = vext[geo.rows_of(r, sub * tq, 3 * tq), :].astype(BF16)
                outs, lses = [], []
                for hh in range(2):
                    qh = jnp.where(low if hh == 0 else ~low, q, 0.0).astype(BF16)
                    sc = (lax.dot_general(qh, ks, (((1,), (1,)), ((), ())), preferred_element_type=F32)
                          + b_ref[hh] + madd)
                    m = jnp.max(sc, axis=1, keepdims=True)
                    p = jnp.exp(sc - m)
                    den = jnp.sum(p, axis=1, keepdims=True)
                    outs.append(jnp.dot((p / den).astype(BF16), vs, preferred_element_type=F32))
                    lses.append(m + jnp.log(den))
                o_ref[geo.rows_of(r, sub * tq, tq), :] = jnp.where(low, outs[0], outs[1])
                l_ref[geo.rows_of(r, sub * tq, tq), :] = jnp.where(low, lses[0], lses[1])

    qc, kc, vc = cols
    specs = ([geo.main(qc)] + geo.with_halos(kc) + geo.with_halos(vc)
             + [pl.BlockSpec((2, tq, 3 * tq), lambda hp, n: (hp, 0, 0)),
                pl.BlockSpec((4, tq, 3 * tq), lambda hp, n: (0, 0, 0))])
    shape = jax.ShapeDtypeStruct((geo.s, 4 * LANES), F32)
    return pl.pallas_call(body, name=name, grid=(4, geo.nbig), in_specs=specs, out_specs=[geo.main(0), geo.main(0)],
                          out_shape=[shape, shape],
                          scratch_shapes=[pltpu.VMEM((geo.ext, LANES), F32), pltpu.VMEM((geo.ext, LANES), F32)],
                          compiler_params=_params(("parallel", "parallel"), VMEM_BIG))(
                              h, h, h, h, h, h, h, bias, mask)


def _attn_bwd(h, bias, mask, dy, y, lse, *, d, tq, nsub, cols, ycol, name):
    geo = _AttnGeom(h.shape[0], d, tq, nsub)
    scale = HEAD_DIM ** -0.5
    halo, rows = geo.halo, geo.rows

    def body(q_ref, km, kp, kn, vm, vp, vn, b_ref, m_ref, dy_ref, y_ref, l_ref, dq_ref, dk_hbm, dv_hbm, db_ref,
             kext, vext, dkext, dvext, dk_all, dv_all, sems):
        hp, n = pl.program_id(0), pl.program_id(1)

        @pl.when(n == 0)
        def _():
            dk_all[...] = jnp.zeros_like(dk_all)
            dv_all[...] = jnp.zeros_like(dv_all)
            db_ref[...] = jnp.zeros_like(db_ref)

        geo.fill_ext(kext, km, kp, kn)
        geo.fill_ext(vext, vm, vp, vn)
        dkext[...] = jnp.zeros_like(dkext)
        dvext[...] = jnp.zeros_like(dvext)
        low = lax.broadcasted_iota(jnp.int32, (1, LANES), 1) < HEAD_DIM
        for r in range(d):
            for sub in range(nsub):
                madd = m_ref[geo.variant(n, sub)]
                mine = geo.rows_of(r, sub * tq, tq)
                keys = geo.rows_of(r, sub * tq, 3 * tq)
                q = q_ref[mine, :].astype(F32) * scale
                ks = kext[keys, :].astype(BF16)
                vs = vext[keys, :].astype(BF16)
                dyv = dy_ref[mine, :]
                dyy = dyv * y_ref[mine, :]
                lse_all = l_ref[mine, :]
                dq = jnp.zeros((tq, LANES), F32)
                dk = jnp.zeros((3 * tq, LANES), F32)
                dv = jnp.zeros((3 * tq, LANES), F32)
                for hh in range(2):
                    sel = low if hh == 0 else ~low
                    qh = jnp.where(sel, q, 0.0).astype(BF16)
                    dyh = jnp.where(sel, dyv, 0.0).astype(BF16)
                    dsum = jnp.sum(jnp.where(sel, dyy, 0.0), axis=1, keepdims=True)
                    lse_h = lse_all[:, hh * HEAD_DIM:hh * HEAD_DIM + 1]
                    sc = (lax.dot_general(qh, ks, (((1,), (1,)), ((), ())), preferred_element_type=F32)
                          + b_ref[hh] + madd)
                    p = jnp.exp(sc - lse_h)
                    dp = lax.dot_general(dyh, vs, (((1,), (1,)), ((), ())), preferred_element_type=F32)
                    ds = p * (dp - dsum)
                    db_ref[hh] += ds
                    pb, dsb = p.astype(BF16), ds.astype(BF16)
                    dv += lax.dot_general(pb, dyh, (((0,), (0,)), ((), ())), preferred_element_type=F32)
                    dk += lax.dot_general(dsb, qh, (((0,), (0,)), ((), ())), preferred_element_type=F32)
                    dq += jnp.where(sel, jnp.dot(dsb, ks, preferred_element_type=F32), 0.0)
                dq_ref[mine, :] = dq * scale
                dkext[keys, :] += dk
                dvext[keys, :] += dv

        before = pl.multiple_of(jnp.maximum(n * rows - halo, 0), LANES)
        here = pl.multiple_of(n * rows, LANES)
        after = pl.multiple_of(jnp.minimum((n + 1) * rows, geo.s - halo), LANES)
        for ext, total in ((dkext, dk_all), (dvext, dv_all)):
            total[pl.ds(before, halo), :] += ext[0:halo]
            total[pl.ds(here, rows), :] += ext[halo:halo + rows]
            total[pl.ds(after, halo), :] += ext[halo + rows:geo.ext]

        @pl.when(n == geo.nbig - 1)
        def _():
            col = pl.ds(pl.multiple_of(hp * LANES, LANES), LANES)
            copies = [pltpu.make_async_copy(total, out.at[:, col], sems.at[i])
                      for i, (total, out) in enumerate(((dk_all, dk_hbm), (dv_all, dv_hbm)))]
            for cp in copies:
                cp.start()
            for cp in copies:
                cp.wait()

    qc, kc, vc = cols
    bspec = pl.BlockSpec((2, tq, 3 * tq), lambda hp, n: (hp, 0, 0))
    any_spec = pl.BlockSpec(memory_space=pl.ANY)
    specs = ([geo.main(qc)] + geo.with_halos(kc) + geo.with_halos(vc)
             + [bspec, pl.BlockSpec((4, tq, 3 * tq), lambda hp, n: (0, 0, 0)), geo.main(ycol), geo.main(ycol), geo.main(0)])
    shape = jax.ShapeDtypeStruct((geo.s, 4 * LANES), F32)
    ext = pltpu.VMEM((geo.ext, LANES), F32)
    whole = pltpu.VMEM((geo.s, LANES), F32)
    return pl.pallas_call(
        body, name=name, grid=(4, geo.nbig), in_specs=specs, out_specs=[geo.main(0), any_spec, any_spec, bspec],
        out_shape=[shape, shape, shape, jax.ShapeDtypeStruct(bias.shape, F32)],
        scratch_shapes=[ext, ext, ext, ext, whole, whole, pltpu.SemaphoreType.DMA((2,))],
        compiler_params=_params(("arbitrary", "arbitrary"), VMEM_BIG))(
            h, h, h, h, h, h, h, bias, mask, dy, y, lse)


def _dil_combine(outs, lses, *, name):
    s, c = outs[0].shape
    t = _tile(s, (512, 256, 128, 64, 8))

    def body(o0, o1, o2, l0, l1, l2, y_ref, lt_ref):
        ls = [l0[...], l1[...], l2[...]]
        m = jnp.maximum(jnp.maximum(ls[0], ls[1]), ls[2])
        es = [jnp.exp(l - m) for l in ls]
        den = es[0] + es[1] + es[2]
        y_ref[...] = (es[0] / den) * o0[...] + (es[1] / den) * o1[...] + (es[2] / den) * o2[...]
        lt_ref[...] = m + jnp.log(den)

    row = pl.BlockSpec((t, c), lambda i: (i, 0))
    shape = jax.ShapeDtypeStruct((s, c), F32)
    return pl.pallas_call(body, name=name, grid=(s // t,), in_specs=[row] * 6, out_specs=[row, row],
                          out_shape=[shape, shape], compiler_params=_params(("parallel",)))(*outs, *lses)


def _attn_dh(na, dil, *, name):
    s, c = na[0].shape
    t = _tile(s, (256, 128, 64, 8))

    def body(*refs):
        ins, o_ref = refs[:-1], refs[-1]
        for a in range(3):
            o_ref[:, a * c:(a + 1) * c] = ins[a][...].astype(BF16)
            o_ref[:, (3 + a) * c:(4 + a) * c] = (ins[3 + a][...] + ins[6 + a][...] + ins[9 + a][...]).astype(BF16)

    row = pl.BlockSpec((t, c), lambda i: (i, 0))
    flat = list(na) + [g[a] for g in dil for a in range(3)]
    return pl.pallas_call(body, name=name, grid=(s // t,), in_specs=[row] * 12,
                          out_specs=pl.BlockSpec((t, 6 * c), lambda i: (i, 0)),
                          out_shape=jax.ShapeDtypeStruct((s, 6 * c), BF16),
                          compiler_params=_params(("parallel",)))(*flat)


def _t5_bucket(rel):
    nb = N_BUCKETS // 2
    max_exact = nb // 2
    ret = np.where(rel > 0, nb, 0)
    n = np.abs(rel)
    large = max_exact + (np.log(np.maximum(n, 1).astype(np.float32) / np.float32(max_exact))
                         / np.float32(math.log(T5_MAX_DIST / max_exact)) * np.float32(nb - max_exact)).astype(np.int32)
    large = np.minimum(large, nb - 1)
    return (ret + np.where(n < max_exact, n, large)).astype(np.int32)


def _band_bucket_index(dil):
    tq = DIL_TQ
    rel = np.arange(3 * tq)[None, :] - tq - np.arange(tq)[:, None]
    return _t5_bucket(rel * dil)


def _band_bias(t5, dil, *, name):
    tq = DIL_TQ
    buckets = [int(b) for b in _t5_bucket(np.arange(-DIL_HALF, DIL_HALF + 1) * dil)]

    def body(t_ref, o_ref):
        hh = pl.program_id(0)
        rel = (lax.broadcasted_iota(jnp.int32, (tq, 3 * tq), 1) - tq
               - lax.broadcasted_iota(jnp.int32, (tq, 3 * tq), 0))
        acc = jnp.zeros((tq, 3 * tq), F32)
        for r, b in zip(range(-DIL_HALF, DIL_HALF + 1), buckets):
            acc = jnp.where(rel == r, t_ref[b * 8 + hh], acc)
        o_ref[0] = acc

    return pl.pallas_call(body, name=name, grid=(8,),
                          in_specs=[pl.BlockSpec(memory_space=pltpu.SMEM)],
                          out_specs=pl.BlockSpec((1, tq, 3 * tq), lambda h: (h, 0, 0)),
                          out_shape=jax.ShapeDtypeStruct((8, tq, 3 * tq), F32),
                          compiler_params=_params(("parallel",)))(t5.reshape(-1))


def _na_bias(rpb, *, name):
    nr, nc = 2 * NA_KH - 1, 2 * NA_KW - 1
    tq = NA_QROWS * GRID_W
    w = GRID_W

    def body(r_ref, o_ref):
        base = pl.program_id(0) * (nr * nc)
        lane = lax.broadcasted_iota(jnp.int32, (w, LANES), 1)
        upper = lane >= w
        diff = (lane & (w - 1)) - lax.broadcasted_iota(jnp.int32, (w, LANES), 0) + NA_KW - 1
        tiles = {}
        for i in range(NA_QROWS):
            for m in range(3 * NA_QROWS // 2):
                lo = 2 * m - i + NA_KH - 1 - NA_QROWS
                if lo not in tiles:
                    acc = jnp.zeros((w, LANES), F32)
                    for dc in range(nc):
                        v_lo = r_ref[base + lo * nc + dc] if 0 <= lo < nr else 0.0
                        v_hi = r_ref[base + (lo + 1) * nc + dc] if 0 <= lo + 1 < nr else 0.0
                        acc = jnp.where(diff == dc, jnp.where(upper, v_hi, v_lo), acc)
                    tiles[lo] = acc
                o_ref[0, i * w:(i + 1) * w, m * LANES:(m + 1) * LANES] = tiles[lo]

    return pl.pallas_call(body, name=name, grid=(8,),
                          in_specs=[pl.BlockSpec(memory_space=pltpu.SMEM)],
                          out_specs=pl.BlockSpec((1, tq, 3 * tq), lambda h: (h, 0, 0)),
                          out_shape=jax.ShapeDtypeStruct((8, tq, 3 * tq), F32),
                          compiler_params=_params(("parallel",)))(rpb.reshape(-1))


def _t5_grad(dbs, idxs, *, name):
    def body(d0, d1, d2, i0, i1, i2, o_ref):
        lane = lax.broadcasted_iota(jnp.int32, (1, LANES), 1)
        lines = [jnp.zeros((1, LANES), F32) for _ in range(8)]
        for dref, iref in ((d0, i0), (d1, i1), (d2, i2)):
            idx = iref[...]
            for hh in range(8):
                xh = dref[hh]
                for b in range(N_BUCKETS):
                    val = jnp.sum(jnp.sum(jnp.where(idx == b, xh, 0.0), axis=1, keepdims=True), axis=0, keepdims=True)
                    lines[hh] = lines[hh] + jnp.where(lane == b, val, 0.0)
        for hh in range(8):
            o_ref[hh:hh + 1, :] = lines[hh]

    out = pl.pallas_call(body, name=name, out_shape=jax.ShapeDtypeStruct((8, LANES), F32))(*dbs, *idxs)
    return out[:, :N_BUCKETS].T


def _rpb_grad(db, *, name):
    nr, nc = 2 * NA_KH - 1, 2 * NA_KW - 1
    tq = NA_QROWS * GRID_W
    w = GRID_W

    def body(d_ref, o_ref):
        x = d_ref[0]
        rows = []
        for dr in range(nr):
            acc = jnp.zeros((w, w), F32)
            for i in range(NA_QROWS):
                j = i + dr - (NA_KH - 1 - NA_QROWS)
                if 0 <= j < 3 * NA_QROWS:
                    acc = acc + x[i * w:(i + 1) * w, j * w:(j + 1) * w]
            rows.append(acc)
        diff = (lax.broadcasted_iota(jnp.int32, (w, w), 1) - lax.broadcasted_iota(jnp.int32, (w, w), 0)
                + NA_KW - 1)
        lane = lax.broadcasted_iota(jnp.int32, (1, LANES), 1)
        for dr in range(nr):
            line = jnp.zeros((1, LANES), F32)
            for dc in range(nc):
                val = jnp.sum(jnp.sum(jnp.where(diff == dc, rows[dr], 0.0), axis=1, keepdims=True),
                              axis=0, keepdims=True)
                line = jnp.where(lane == dc, val, line)
            o_ref[0, dr:dr + 1, :] = line

    out = pl.pallas_call(body, name=name, grid=(8,),
                         in_specs=[pl.BlockSpec((1, tq, 3 * tq), lambda h: (h, 0, 0))],
                         out_specs=pl.BlockSpec((1, nr, LANES), lambda h: (h, 0, 0)),
                         out_shape=jax.ShapeDtypeStruct((8, nr, LANES), F32),
                         compiler_params=_params(("parallel",)))(db)
    return out[:, :, :nc]


def _exchange(src, *, gather, name):
    shape = src.shape if not gather else (N_DEV,) + src.shape

    def body(src_ref, out_ref, send_sems, recv_sems, local_sem):
        _exchange_start(src_ref, out_ref, send_sems, recv_sems, local_sem, gather)
        _exchange_wait(src_ref, out_ref, send_sems, recv_sems, local_sem, gather)

    any_spec = pl.BlockSpec(memory_space=pl.ANY)
    return pl.pallas_call(
        body, name=name, in_specs=[any_spec], out_specs=any_spec, out_shape=jax.ShapeDtypeStruct(shape, src.dtype),
        scratch_shapes=_exchange_sems())(src)


def _exchange_sems():
    return [pltpu.SemaphoreType.DMA((N_DEV - 1,)), pltpu.SemaphoreType.DMA((N_DEV - 1,)), pltpu.SemaphoreType.DMA]


def _exchange_copies(src_ref, out_ref, send_sems, recv_sems, local_sem, gather):
    x, y, c = lax.axis_index("x"), lax.axis_index("y"), lax.axis_index("c")
    me = 4 * x + 2 * y + c

    def outgoing(p):
        return src_ref if gather else src_ref.at[p]

    own = pltpu.make_async_copy(outgoing(me), out_ref.at[me], local_sem)
    sends, recvs = [], []
    for k in range(1, N_DEV):
        px = 1 - x if k & 4 else x
        py = 1 - y if k & 2 else y
        pc = 1 - c if k & 1 else c
        p = 4 * px + 2 * py + pc
        for dst, group in ((me, sends), (p, recvs)):
            group.append(pltpu.make_async_remote_copy(
                src_ref=outgoing(p), dst_ref=out_ref.at[dst], send_sem=send_sems.at[k - 1],
                recv_sem=recv_sems.at[k - 1], device_id=(px, py, pc), device_id_type=pl.DeviceIdType.MESH))
    return own, sends, recvs


def _exchange_start(*refs_and_mode):
    own, sends, _ = _exchange_copies(*refs_and_mode)
    own.start()
    for cp in sends:
        cp.start()


def _exchange_wait(*refs_and_mode):
    own, sends, recvs = _exchange_copies(*refs_and_mode)
    for cp in recvs:
        cp.wait_recv()
    for cp in sends:
        cp.wait_send()
    own.wait()


def _adamw(parts, w, m, v, *, name):
    rows, cols = w.shape
    t = _tile(rows, (FLAT_ROW_TILE, 128, 64, 32, 16, 8))

    def body(p_ref, w_ref, m_ref, v_ref, g_ref, d_ref, nm_ref, nv_ref):
        g = p_ref[0].astype(F32)
        for k in range(1, N_DEV):
            g = g + p_ref[k].astype(F32)
        nm = ADAM_B1 * m_ref[...] + (1.0 - ADAM_B1) * g
        nv = ADAM_B2 * v_ref[...] + (1.0 - ADAM_B2) * (g * g)
        m_hat = nm / (1.0 - ADAM_B1 ** ADAM_STEP)
        v_hat = nv / (1.0 - ADAM_B2 ** ADAM_STEP)
        g_ref[...] = g
        d_ref[...] = -ADAM_LR * (m_hat / (jnp.sqrt(v_hat) + ADAM_EPS) + ADAM_WD * w_ref[...])
        nm_ref[...] = nm
        nv_ref[...] = nv

    row = pl.BlockSpec((t, cols), lambda i: (i, 0))
    shape = jax.ShapeDtypeStruct((rows, cols), F32)
    return pl.pallas_call(body, name=name, grid=(rows // t,),
                          in_specs=[pl.BlockSpec((N_DEV, t, cols), lambda i: (0, i, 0)), row, row, row],
                          out_specs=[row] * 4, out_shape=[shape] * 4,
                          compiler_params=_params(("parallel",), VMEM_BIG))(parts, w, m, v)


def _flatten(arrays, dtype, row_mult):
    flat = jnp.concatenate([a.reshape(-1).astype(dtype) for a in arrays])
    chunk = FLAT_COLS * row_mult
    padded = -(-flat.shape[0] // chunk) * chunk
    return jnp.pad(flat, (0, padded - flat.shape[0])).reshape(padded // FLAT_COLS, FLAT_COLS)


def _unflatten(flat, shapes):
    flat = flat.reshape(-1)
    out, pos = [], 0
    for shp in shapes:
        size = int(np.prod(shp))
        out.append(flat[pos:pos + size].reshape(shp))
        pos += size
    return out


def _gather_full(names, local, dtype, row_mult, label):
    got = _exchange(_flatten([local[n] for n in names], dtype, row_mult), gather=True, name=label)
    per_dev = [_unflatten(got[p], [local[n].shape for n in names]) for p in range(N_DEV)]
    return {n: jnp.concatenate([per_dev[p][i] for p in range(N_DEV)], axis=SHARD_AXIS[n])
            for i, n in enumerate(names)}


def _from_shards(stacked, axis):
    _, a, b = stacked.shape
    if axis == 1:
        return jnp.transpose(stacked, (1, 0, 2)).reshape(a, N_DEV * b)
    return stacked.reshape(N_DEV * a, b)


def _to_shards(full, axis):
    ra, rb = full.shape
    if axis == 1:
        return jnp.transpose(full.reshape(ra, N_DEV, rb // N_DEV), (1, 0, 2))
    return full.reshape(N_DEV, ra // N_DEV, rb)


class _ShardedMatmulWeights:
    def __init__(self, local):
        self.local, self.full, self.parts = local, {}, {}

    def gather_src(self, n, l):
        return self.local[n][l].astype(BF16)

    def set_gathered(self, n, l, got):
        self.full[n, l] = _from_shards(got, SHARD_AXIS[n] - 1)

    def get(self, n, l):
        return self.full[n, l]

    def scatter_src(self, n, l, dw):
        return _to_shards(dw, SHARD_AXIS[n] - 1).astype(BF16)

    def set_scattered(self, n, l, parts):
        self.parts[n, l] = parts


def _role(role, i):
    mixer = 'attn_w_' if i % 2 == 0 else 'conv_w_'
    return {'in': (mixer + 'in', i // 2), 'out': (mixer + 'out', i // 2),
            'up': ('ffn_w_up', i), 'down': ('ffn_w_down', i)}[role]


def _local_step(x, tgt, w, big):
    s = x.shape[0]

    def project(a, role, i, **kw):
        carry = None
        if i + 1 < DEPTH:
            nxt = _role(role, i + 1)
            carry = (big.gather_src(*nxt), True)
        out, got = _mm(a, big.get(*_role(role, i)), carry=carry, name=role + "_fwd", **kw)
        if got is not None:
            big.set_gathered(*nxt, got)
        return out

    def project_back(a, d_out, role, i):
        key = _role(role, i)
        dw, _ = _mm(a, d_out, ta=True, name=role + "_dw")
        d_in, parts = _mm(d_out, big.get(*key), tb=True, carry=(big.scatter_src(*key, dw), False), name=role + "_dx")
        if parts is not None:
            big.set_scattered(*key, parts)
        return d_in

    na_tq = NA_QROWS * GRID_W
    band_idx = [_band_bucket_index(d) for _, d in DIL_PATTERNS]
    band_bias = [_band_bias(w['t5_bias'], d, name=f"band_bias_{d}") for _, d in DIL_PATTERNS]
    band_mask = [_mask_tiles("band", DIL_TQ, s // d, name=f"band_mask_{d}") for _, d in DIL_PATTERNS]
    na_mask = _mask_tiles("na", na_tq, s, name="na_mask")
    na_cols, dil_cols = (0, 4, 8), (12, 16, 20)
    grads = {n: [None] * w[n].shape[0] for n in SMALL_SHARDED + REPLICATED if n != 't5_bias'}
    saved = []
    x16 = x.astype(BF16)

    for i in range(DEPTH):
        j = i // 2
        st = {'x': x, 'x16': x16}
        if i % 2 == 0:
            h = project(x16, 'in', i)
            na_bias = _na_bias(w['na_rpb'][j], name="na_bias")
            o_na, l_na = _attn_fwd(h, na_bias, na_mask, d=1, tq=na_tq, nsub=NA_NSUB, cols=na_cols, name="na_fwd")
            outs, lses = [], []
            for (_, d), bias, mask in zip(DIL_PATTERNS, band_bias, band_mask):
                o, l = _attn_fwd(h, bias, mask, d=d, tq=DIL_TQ, nsub=DIL_NSUB[d], cols=dil_cols, name=f"dil_fwd_{d}")
                outs.append(o)
                lses.append(l)
            y_dil, l_dil = _dil_combine(outs, lses, name="dil_combine")
            mid = jnp.concatenate([o_na, y_dil], axis=1)
            mid16 = mid.astype(BF16)
            ymix = project(mid16, 'out', i)
            st.update(h=h, mid=mid, mid16=mid16, l_na=l_na, l_dil=l_dil, na_bias=na_bias)
        else:
            h = project(x16, 'in', i)
            mid16, u2 = _conv_mid_fwd(h, w['conf_dw_w'][j], w['conf_dw_b'][j], w['conf_ln_g'][j], w['conf_ln_b'][j],
                                      w['sconv_w'][j], name="conv_mid_fwd")
            ymix = project(mid16, 'out', i)
            st.update(h=h, mid16=mid16, u2=u2)
        xa, xa16 = _ln_fwd(x, ymix, w['mix_ln_g'][i], w['mix_ln_b'][i], name="mix_ln")
        hu = project(xa16, 'up', i)
        act16 = _ffn_mid_fwd(hu, w['ffn_dw_w'][i], name="ffn_mid_fwd")
        yffn = project(act16, 'down', i)
        xb, xb16 = _ln_fwd(xa, yffn, w['ffn_ln_g'][i], w['ffn_ln_b'][i], name="ffn_ln")
        st.update(ymix=ymix, xa=xa, xa16=xa16, hu=hu, act16=act16, yffn=yffn)
        saved.append(st)
        x, x16 = xb, xb16

    loss, d1 = _loss_head(x, tgt, name="loss_head")
    d2 = None
    g_t5 = None
    for i in reversed(range(DEPTH)):
        j = i // 2
        st = saved[i]
        dz, dz16, dg, db = _ln_bwd(st['xa'], st['yffn'], w['ffn_ln_g'][i], d1, d2, name="ffn_ln_bwd")
        grads['ffn_ln_g'][i], grads['ffn_ln_b'][i] = dg, db
        dact = project_back(st['act16'], dz16, 'down', i)
        dhu, grads['ffn_dw_w'][i] = _ffn_mid_bwd(st['hu'], w['ffn_dw_w'][i], dact, name="ffn_mid_bwd")
        dxa = project_back(st['xa16'], dhu, 'up', i)
        dz, dz1, dg, db = _ln_bwd(st['x'], st['ymix'], w['mix_ln_g'][i], dz, dxa, name="mix_ln_bwd")
        grads['mix_ln_g'][i], grads['mix_ln_b'][i] = dg, db
        dmid = project_back(st['mid16'], dz1, 'out', i)
        if i % 2 == 0:
            h = st['h']
            dq, dk, dv, dbias = _attn_bwd(h, st['na_bias'], na_mask, dmid, st['mid'], st['l_na'], d=1, tq=na_tq,
                                          nsub=NA_NSUB, cols=na_cols, ycol=0, name="na_bwd")
            grads['na_rpb'][j] = _rpb_grad(dbias, name="rpb_grad")
            dil, dbs = [], []
            for (_, d), bias, mask in zip(DIL_PATTERNS, band_bias, band_mask):
                g = _attn_bwd(h, bias, mask, dmid, st['mid'], st['l_dil'], d=d, tq=DIL_TQ, nsub=DIL_NSUB[d],
                              cols=dil_cols, ycol=4, name=f"dil_bwd_{d}")
                dil.append(g[:3])
                dbs.append(g[3])
            t5 = _t5_grad(dbs, band_idx, name="t5_grad")
            g_t5 = t5 if g_t5 is None else g_t5 + t5
            dh = _attn_dh((dq, dk, dv), dil, name="attn_dh")
        else:
            dh, dw31, db31, dlg, dlb, dw3 = _conv_mid_bwd(st['h'], st['u2'], dmid, w['conf_dw_w'][j],
                                                          w['conf_ln_g'][j], w['conf_ln_b'][j], w['sconv_w'][j],
                                                          name="conv_mid_bwd")
            grads['conf_dw_w'][j], grads['conf_dw_b'][j] = dw31, db31
            grads['conf_ln_g'][j], grads['conf_ln_b'][j], grads['sconv_w'][j] = dlg, dlb, dw3
        d1, d2 = dz, project_back(st['x16'], dh, 'in', i)
    dx = _axpy(d1, d2, name="grad_x")
    full = {n: jnp.stack(g) for n, g in grads.items()}
    full['t5_bias'] = g_t5
    return loss, dx, full


def kernel(x, t5_bias, attn_w_in, attn_w_out, na_rpb, conv_w_in, conf_dw_w, conf_dw_b, conf_ln_g, conf_ln_b, sconv_w, conv_w_out, ffn_w_up, ffn_dw_w, ffn_w_down, mix_ln_g, mix_ln_b, ffn_ln_g, ffn_ln_b, loss_target, m_t5_bias, m_attn_w_in, m_attn_w_out, m_na_rpb, m_conv_w_in, m_conf_dw_w, m_conf_dw_b, m_conf_ln_g, m_conf_ln_b, m_sconv_w, m_conv_w_out, m_ffn_w_up, m_ffn_dw_w, m_ffn_w_down, m_mix_ln_g, m_mix_ln_b, m_ffn_ln_g, m_ffn_ln_b, v_t5_bias, v_attn_w_in, v_attn_w_out, v_na_rpb, v_conv_w_in, v_conf_dw_w, v_conf_dw_b, v_conf_ln_g, v_conf_ln_b, v_sconv_w, v_conv_w_out, v_ffn_w_up, v_ffn_dw_w, v_ffn_w_down, v_mix_ln_g, v_mix_ln_b, v_ffn_ln_g, v_ffn_ln_b):
    args = dict(locals())
    local = {n: args[n] for n in WEIGHTS}
    mom1 = {n: args['m_' + n] for n in WEIGHTS}
    mom2 = {n: args['v_' + n] for n in WEIGHTS}

    kinds = ('grad', 'delta', 'new_m', 'new_v')
    small = {n: local[n] for n in REPLICATED}
    small.update(_gather_full(SMALL_SHARDED, local, F32, 8, "gather_small_weights"))
    big = _ShardedMatmulWeights({n: local[n] for n in MATMUL_WEIGHTS})
    for role in ('in', 'out', 'up', 'down'):
        n, l = _role(role, 0)
        big.set_gathered(n, l, _exchange(big.gather_src(n, l), gather=True, name="gather_first_" + role))

    loss, dx, grads = _local_step(x[0], loss_target[0], small, big)
    loss = lax.psum(loss, MESH_AXES)

    out = {}
    for n in MATMUL_WEIGHTS:
        per_layer = [_adamw(big.parts[n, l], local[n][l], mom1[n][l], mom2[n][l], name="adamw_" + n)
                     for l in range(local[n].shape[0])]
        for k, kind in enumerate(kinds):
            out[kind + '_' + n] = jnp.stack([res[k] for res in per_layer])
    for names, sharded, label in ((SMALL_SHARDED, True, "small"), (REPLICATED, False, "replicated")):
        shapes = [local[n].shape for n in names]
        if sharded:
            per_dev = [_flatten([lax.slice_in_dim(grads[n], p * local[n].shape[SHARD_AXIS[n]],
                                                  (p + 1) * local[n].shape[SHARD_AXIS[n]], axis=SHARD_AXIS[n])
                                 for n in names], F32, 8) for p in range(N_DEV)]
            parts = _exchange(jnp.stack(per_dev), gather=False, name="scatter_small_grads")
        else:
            parts = _exchange(_flatten([grads[n] for n in names], F32, 8), gather=True, name="gather_replicated_grads")
        res = _adamw(parts, _flatten([local[n] for n in names], F32, 8), _flatten([mom1[n] for n in names], F32, 8),
                     _flatten([mom2[n] for n in names], F32, 8), name="adamw_" + label)
        for kind, flat in zip(kinds, res):
            for n, a in zip(names, _unflatten(flat, shapes)):
                out[kind + '_' + n] = a

    return (loss, dx[None], *[out[k + '_' + n] for k in ('grad', 'delta', 'new_m', 'new_v') for n in WEIGHTS])
```

```python
import functools
import math

import jax
import jax.numpy as jnp
import numpy as np
from jax import lax
from jax.experimental import pallas as pl
from jax.experimental.pallas import tpu as pltpu

F32 = jnp.float32
BF16 = jnp.bfloat16

N_DEV = 8
MESH_AXES = ("x", "y", "c")
DEPTH = 4
GRID_W = 64
GRID_SHIFT = 6
HEAD_DIM = 64
NA_KH = 8
NA_KW = 16
NA_QROWS = 4
DIL_PATTERNS = ((128, 1), (512, 4), (2048, 16))
DIL_HALF = 64
DIL_TQ = 128
DIL_KW = DIL_TQ + 2 * DIL_HALF
DIL_NSUB = {1: 8, 4: 2, 16: 1}
NA_NSUB = 4
N_BUCKETS = 32
T5_MAX_DIST = 1024
CONF_CH = 512
CONF_K = 31
SC_K = 3
FFN_K = 3
LN_EPS = 1e-5
NEG = -1e30
ALPHA = (2 * DEPTH) ** 0.25
ADAM_LR = 0.001
ADAM_B1 = 0.9
ADAM_B2 = 0.999
ADAM_EPS = 1e-08
ADAM_WD = 0.01
ADAM_STEP = 10

LANES = 128
SUBLANES = 8
VMEM_BIG = 48 * 1024 * 1024
FLAT_COLS = 1024
FLAT_ROW_TILE = 256

WEIGHTS = ['t5_bias', 'attn_w_in', 'attn_w_out', 'na_rpb', 'conv_w_in', 'conf_dw_w', 'conf_dw_b', 'conf_ln_g',
           'conf_ln_b', 'sconv_w', 'conv_w_out', 'ffn_w_up', 'ffn_dw_w', 'ffn_w_down', 'mix_ln_g', 'mix_ln_b',
           'ffn_ln_g', 'ffn_ln_b']
SHARD_AXIS = {'attn_w_in': 2, 'attn_w_out': 1, 'conv_w_in': 2, 'conf_dw_w': 2, 'conf_dw_b': 1, 'conf_ln_g': 1,
              'conf_ln_b': 1, 'sconv_w': 2, 'conv_w_out': 1, 'ffn_w_up': 2, 'ffn_dw_w': 2, 'ffn_w_down': 1}
MATMUL_WEIGHTS = ['attn_w_in', 'attn_w_out', 'conv_w_in', 'conv_w_out', 'ffn_w_up', 'ffn_w_down']
SMALL_SHARDED = ['conf_dw_w', 'conf_dw_b', 'conf_ln_g', 'conf_ln_b', 'sconv_w', 'ffn_dw_w']
SHARDED = MATMUL_WEIGHTS + SMALL_SHARDED
REPLICATED = ['t5_bias', 'na_rpb', 'mix_ln_g', 'mix_ln_b', 'ffn_ln_g', 'ffn_ln_b']


def _tile(n, cands):
    for c in cands:
        if n % c == 0:
            return c
    return n


def _params(sem, vmem=None):
    return pltpu.CompilerParams(dimension_semantics=sem, vmem_limit_bytes=vmem)


def _sigmoid(x):
    return 1.0 / (1.0 + jnp.exp(-x))


MM_MAX_TILE = 1408
MM_MAX_K = 3072


def _lane_tile(n, cap):
    best = None
    for t in range(LANES, min(n, cap) + 1, LANES):
        if n % t == 0:
            best = t
    return best or n


def _mm(a, b, *, ta=False, tb=False, out_dtype=F32, carry=None, name):
    assert a.dtype == BF16 and b.dtype == BF16, (name, a.dtype, b.dtype)
    m, k = (a.shape[1], a.shape[0]) if ta else a.shape
    n = b.shape[0] if tb else b.shape[1]
    tm, tn, tk = _lane_tile(m, MM_MAX_TILE), _lane_tile(n, MM_MAX_TILE), _lane_tile(k, MM_MAX_K)
    grid = (m // tm, n // tn, k // tk)
    nk = grid[2]
    dims = (((0 if ta else 1,), (1 if tb else 0,)), ((), ()))
    use_acc = nk > 1 and out_dtype != F32
    src, gather = carry if carry is not None else (None, False)
    carried = src is not None

    def body(*refs):
        if carried:
            a_ref, b_ref, src_ref, o_ref, x_ref = refs[:5]
            send_sems, recv_sems, local_sem = refs[-3:]
            scratch = refs[5:-3]
            ids = [pl.program_id(ax) for ax in range(3)]
            first = (ids[0] == 0) & (ids[1] == 0) & (ids[2] == 0)
            last = (ids[0] == grid[0] - 1) & (ids[1] == grid[1] - 1) & (ids[2] == grid[2] - 1)

            @pl.when(first)
            def _():
                _exchange_start(src_ref, x_ref, send_sems, recv_sems, local_sem, gather)
        else:
            a_ref, b_ref, o_ref = refs[:3]
            scratch = refs[3:]

        part = lax.dot_general(a_ref[...], b_ref[...], dims, preferred_element_type=F32)
        if nk == 1:
            o_ref[...] = part.astype(out_dtype)
        else:
            acc_ref = scratch[0] if use_acc else o_ref
            kk = pl.program_id(2)

            @pl.when(kk == 0)
            def _():
                acc_ref[...] = part

            @pl.when(kk > 0)
            def _():
                acc_ref[...] += part

            if use_acc:
                @pl.when(kk == nk - 1)
                def _():
                    o_ref[...] = acc_ref[...].astype(out_dtype)

        if carried:
            @pl.when(last)
            def _():
                _exchange_wait(src_ref, x_ref, send_sems, recv_sems, local_sem, gather)

    a_spec = pl.BlockSpec((tk, tm), lambda i, j, q: (q, i)) if ta else pl.BlockSpec((tm, tk), lambda i, j, q: (i, q))
    b_spec = pl.BlockSpec((tn, tk), lambda i, j, q: (j, q)) if tb else pl.BlockSpec((tk, tn), lambda i, j, q: (q, j))
    o_spec = pl.BlockSpec((tm, tn), lambda i, j, q: (i, j))
    o_shape = jax.ShapeDtypeStruct((m, n), out_dtype)
    scratch = [pltpu.VMEM((tm, tn), F32)] if use_acc else []
    if not carried:
        out = pl.pallas_call(
            body, name=name, grid=grid, in_specs=[a_spec, b_spec], out_specs=o_spec, out_shape=o_shape,
            scratch_shapes=scratch,
            compiler_params=_params(("parallel", "parallel", "arbitrary"), VMEM_BIG))(a, b)
        return out, None
    any_spec = pl.BlockSpec(memory_space=pl.ANY)
    x_shape = jax.ShapeDtypeStruct(((N_DEV,) + src.shape) if gather else src.shape, src.dtype)
    return pl.pallas_call(
        body, name=name, grid=grid, in_specs=[a_spec, b_spec, any_spec], out_specs=[o_spec, any_spec],
        out_shape=[o_shape, x_shape], scratch_shapes=scratch + _exchange_sems(),
        compiler_params=_params(("arbitrary", "arbitrary", "arbitrary"), VMEM_BIG))(a, b, src)


def _ln_stats(z):
    mu = jnp.mean(z, axis=-1, keepdims=True)
    zc = z - mu
    var = jnp.mean(zc * zc, axis=-1, keepdims=True)
    rstd = lax.rsqrt(var + LN_EPS)
    return zc * rstd, rstd


def _ln_fwd(x, y, g, b, *, name):
    s, d = x.shape
    t = _tile(s, (256, 128, 64, 8))

    def body(x_ref, y_ref, g_ref, b_ref, o_ref, o16_ref):
        xh, _ = _ln_stats(ALPHA * x_ref[...] + y_ref[...])
        out = xh * g_ref[...] + b_ref[...]
        o_ref[...] = out
        o16_ref[...] = out.astype(BF16)

    row = pl.BlockSpec((t, d), lambda i: (i, 0))
    vec = pl.BlockSpec((1, d), lambda i: (0, 0))
    return pl.pallas_call(body, name=name, grid=(s // t,), in_specs=[row, row, vec, vec], out_specs=[row, row],
                          out_shape=[jax.ShapeDtypeStruct((s, d), F32), jax.ShapeDtypeStruct((s, d), BF16)],
                          compiler_params=_params(("parallel",)))(x, y, g.reshape(1, d), b.reshape(1, d))


def _ln_bwd(x, y, g, d1, d2, *, name):
    s, d = x.shape
    t = _tile(s, (256, 128, 64, 8))
    two = d2 is not None

    def body(*refs):
        if two:
            x_ref, y_ref, g_ref, d1_ref, d2_ref, dz_ref, dz16_ref, dg_ref, db_ref = refs
            dout = ALPHA * d1_ref[...] + d2_ref[...]
        else:
            x_ref, y_ref, g_ref, d1_ref, dz_ref, dz16_ref, dg_ref, db_ref = refs
            dout = d1_ref[...]

        @pl.when(pl.program_id(0) == 0)
        def _():
            dg_ref[...] = jnp.zeros_like(dg_ref)
            db_ref[...] = jnp.zeros_like(db_ref)

        xh, rstd = _ln_stats(ALPHA * x_ref[...] + y_ref[...])
        dxh = dout * g_ref[...]
        dz = rstd * (dxh - jnp.mean(dxh, axis=-1, keepdims=True) - xh * jnp.mean(dxh * xh, axis=-1, keepdims=True))
        dz_ref[...] = dz
        dz16_ref[...] = dz.astype(BF16)
        dg_ref[...] += jnp.sum(dout * xh, axis=0, keepdims=True)
        db_ref[...] += jnp.sum(dout, axis=0, keepdims=True)

    row = pl.BlockSpec((t, d), lambda i: (i, 0))
    vec = pl.BlockSpec((1, d), lambda i: (0, 0))
    ins = [x, y, g.reshape(1, d), d1] + ([d2] if two else [])
    specs = [row, row, vec, row] + ([row] if two else [])
    dz, dz16, dg, db = pl.pallas_call(
        body, name=name, grid=(s // t,), in_specs=specs, out_specs=[row, row, vec, vec],
        out_shape=[jax.ShapeDtypeStruct((s, d), F32), jax.ShapeDtypeStruct((s, d), BF16),
                   jax.ShapeDtypeStruct((1, d), F32), jax.ShapeDtypeStruct((1, d), F32)],
        compiler_params=_params(("arbitrary",)))(*ins)
    return dz, dz16, dg.reshape(d), db.reshape(d)


def _axpy(d1, d2, *, name):
    s, d = d1.shape
    t = _tile(s, (256, 128, 64, 8))

    def body(a_ref, b_ref, o_ref):
        o_ref[...] = ALPHA * a_ref[...] + b_ref[...]

    row = pl.BlockSpec((t, d), lambda i: (i, 0))
    return pl.pallas_call(body, name=name, grid=(s // t,), in_specs=[row, row], out_specs=row,
                          out_shape=jax.ShapeDtypeStruct((s, d), F32), compiler_params=_params(("parallel",)))(d1, d2)


def _loss_head(y, tgt, *, name):
    s, d = y.shape
    t = _tile(s, (256, 128, 64, 8))

    def body(y_ref, t_ref, l_ref, dy_ref):
        @pl.when(pl.program_id(0) == 0)
        def _():
            l_ref[...] = jnp.zeros_like(l_ref)

        err = y_ref[...] - t_ref[...]
        dy_ref[...] = err * (1.0 / d)
        l_ref[...] += 0.5 * jnp.sum(jnp.sum(err * err, axis=1, keepdims=True), axis=0, keepdims=True) * (1.0 / d)

    row = pl.BlockSpec((t, d), lambda i: (i, 0))
    one = pl.BlockSpec((SUBLANES, LANES), lambda i: (0, 0))
    loss, dy = pl.pallas_call(
        body, name=name, grid=(s // t,), in_specs=[row, row], out_specs=[one, row],
        out_shape=[jax.ShapeDtypeStruct((SUBLANES, LANES), F32), jax.ShapeDtypeStruct((s, d), F32)],
        compiler_params=_params(("arbitrary",)))(y, tgt)
    return loss[0, 0], dy


def _halo_specs(s, t, halo, cb, col):
    per = t // halo
    last = s // halo - 1
    return [pl.BlockSpec((t, cb), lambda j, i: (i, col(j))),
            pl.BlockSpec((halo, cb), lambda j, i: (jnp.maximum(i * per - 1, 0), col(j))),
            pl.BlockSpec((halo, cb), lambda j, i: (jnp.minimum((i + 1) * per, last), col(j)))]


def _extended(main_ref, prev_ref, next_ref, i, n):
    prev = jnp.where(i > 0, prev_ref[...], 0.0)
    nxt = jnp.where(i < n - 1, next_ref[...], 0.0)
    return jnp.concatenate([prev, main_ref[...], nxt], axis=0)


def _shift(ext, o):
    if o == 0:
        return ext
    return pltpu.roll(ext, (-o) % ext.shape[0], 0)


def _taps(ext, k, sign=1):
    return [_shift(ext, sign * (j - k // 2)) for j in range(k)]


def _conv(ext, w_ref, k, sign=1, taps=None):
    taps = _taps(ext, k, sign) if taps is None else taps
    acc = None
    for j in range(k):
        term = w_ref[j:j + 1, :] * taps[j]
        acc = term if acc is None else acc + term
    return acc


def _conv_wgrad(dw_ref, d_main, x_ext, k, halo, t, taps=None):
    taps = _taps(x_ext, k) if taps is None else taps
    for j in range(k):
        dw_ref[j:j + 1, :] += jnp.sum(d_main * taps[j][halo:halo + t], axis=0, keepdims=True)


def _ffn_mid_fwd(hu, w, *, name):
    s, f2 = hu.shape
    f = f2 // 2
    t, cb, halo = _tile(s, (512, 256, 128)), _tile(f, (256, 128)), SUBLANES
    nt, nc = s // t, f // cb

    def body(g_ref, gp_ref, gn_ref, u_ref, up_ref, un_ref, wg_ref, wu_ref, a_ref):
        i = pl.program_id(1)
        hg = _conv(_extended(g_ref, gp_ref, gn_ref, i, nt), wg_ref, FFN_K)[halo:halo + t]
        hu_ = _conv(_extended(u_ref, up_ref, un_ref, i, nt), wu_ref, FFN_K)[halo:halo + t]
        a_ref[...] = (hg * _sigmoid(hg) * hu_).astype(BF16)

    specs = (_halo_specs(s, t, halo, cb, lambda j: j) + _halo_specs(s, t, halo, cb, lambda j: j + nc)
             + [pl.BlockSpec((FFN_K, cb), lambda j, i: (0, j)), pl.BlockSpec((FFN_K, cb), lambda j, i: (0, j + nc))])
    return pl.pallas_call(body, name=name, grid=(nc, nt), in_specs=specs,
                          out_specs=pl.BlockSpec((t, cb), lambda j, i: (i, j)),
                          out_shape=jax.ShapeDtypeStruct((s, f), BF16),
                          compiler_params=_params(("parallel", "parallel")))(hu, hu, hu, hu, hu, hu, w, w)


def _ffn_mid_bwd(hu, w, da, *, name):
    s, f2 = hu.shape
    f = f2 // 2
    t, cb, halo = _tile(s, (512, 256, 128)), _tile(f, (256, 128)), SUBLANES
    nt, nc = s // t, f // cb

    def body(g_ref, gp_ref, gn_ref, u_ref, up_ref, un_ref, a_ref, ap_ref, an_ref, wg_ref, wu_ref,
             dg_ref, du_ref, dwg_ref, dwu_ref):
        i = pl.program_id(1)

        @pl.when(i == 0)
        def _():
            dwg_ref[...] = jnp.zeros_like(dwg_ref)
            dwu_ref[...] = jnp.zeros_like(dwu_ref)

        xg = _extended(g_ref, gp_ref, gn_ref, i, nt)
        xu = _extended(u_ref, up_ref, un_ref, i, nt)
        dae = _extended(a_ref, ap_ref, an_ref, i, nt)
        xg_taps, xu_taps = _taps(xg, FFN_K), _taps(xu, FFN_K)
        hg = _conv(xg, wg_ref, FFN_K, taps=xg_taps)
        hu_ = _conv(xu, wu_ref, FFN_K, taps=xu_taps)
        sg = _sigmoid(hg)
        d_hg = dae * hu_ * (sg * (1.0 + hg * (1.0 - sg)))
        d_hu = dae * (hg * sg)
        dg_ref[...] = _conv(d_hg, wg_ref, FFN_K, sign=-1)[halo:halo + t].astype(BF16)
        du_ref[...] = _conv(d_hu, wu_ref, FFN_K, sign=-1)[halo:halo + t].astype(BF16)
        _conv_wgrad(dwg_ref, d_hg[halo:halo + t], xg, FFN_K, halo, t, taps=xg_taps)
        _conv_wgrad(dwu_ref, d_hu[halo:halo + t], xu, FFN_K, halo, t, taps=xu_taps)

    wspec = lambda off: pl.BlockSpec((FFN_K, cb), lambda j, i: (0, j + off))
    specs = (_halo_specs(s, t, halo, cb, lambda j: j) + _halo_specs(s, t, halo, cb, lambda j: j + nc)
             + _halo_specs(s, t, halo, cb, lambda j: j) + [wspec(0), wspec(nc)])
    tile = pl.BlockSpec((t, cb), lambda j, i: (i, j))
    dg, du, dwg, dwu = pl.pallas_call(
        body, name=name, grid=(nc, nt), in_specs=specs, out_specs=[tile, tile, wspec(0), wspec(0)],
        out_shape=[jax.ShapeDtypeStruct((s, f), BF16), jax.ShapeDtypeStruct((s, f), BF16),
                   jax.ShapeDtypeStruct((FFN_K, f), F32), jax.ShapeDtypeStruct((FFN_K, f), F32)],
        compiler_params=_params(("parallel", "arbitrary")))(hu, hu, hu, hu, hu, hu, da, da, da, w, w)
    return jnp.concatenate([dg, du], axis=1), jnp.concatenate([dwg, dwu], axis=1)


CONV_HALO = 16


def _conv_mid_fwd(h, dw_w, dw_b, ln_g, ln_b, sc_w, *, name):
    s = h.shape[0]
    c = CONF_CH
    t, halo = _tile(s, (256, 128)), CONV_HALO
    nt = s // t

    def body(ca, cap, can, cg, cgp, cgn, gb, gc, gcp, gcn, hx, hxp, hxn, w31, b31, lg, lb, w3, o_ref, u2_ref):
        i = pl.program_id(1)
        u1 = _extended(ca, cap, can, i, nt) * _sigmoid(_extended(cg, cgp, cgn, i, nt))
        u2 = _conv(u1, w31, CONF_K)[halo:halo + t] + b31[...]
        u2_ref[...] = u2
        xh, _ = _ln_stats(u2)
        yl = xh * lg[...] + lb[...]
        o_ref[:, 0:c] = (yl * _sigmoid(yl)).astype(BF16)
        p = _extended(gc, gcp, gcn, i, nt) * _extended(hx, hxp, hxn, i, nt)
        o_ref[:, c:2 * c] = (gb[...] * _conv(p, w3, SC_K)[halo:halo + t]).astype(BF16)

    hs = lambda blk: _halo_specs(s, t, halo, c, lambda j: blk)
    vec = lambda r: pl.BlockSpec((r, c), lambda j, i: (0, 0))
    specs = hs(0) + hs(1) + hs(2)[:1] + hs(3) + hs(4) + [vec(CONF_K), vec(1), vec(1), vec(1), vec(SC_K)]
    return pl.pallas_call(
        body, name=name, grid=(1, nt), in_specs=specs,
        out_specs=[pl.BlockSpec((t, 2 * c), lambda j, i: (i, 0)), pl.BlockSpec((t, c), lambda j, i: (i, 0))],
        out_shape=[jax.ShapeDtypeStruct((s, 2 * c), BF16), jax.ShapeDtypeStruct((s, c), F32)],
        compiler_params=_params(("parallel", "parallel")))(
            h, h, h, h, h, h, h, h, h, h, h, h, h, dw_w, dw_b.reshape(1, c), ln_g.reshape(1, c),
            ln_b.reshape(1, c), sc_w)


def _conv_mid_bwd(h, u2, dm, dw_w, ln_g, ln_b, sc_w, *, name):
    s = h.shape[0]
    c = CONF_CH
    t, halo = _tile(s, (256, 128)), CONV_HALO
    nt = s // t

    def body(ca, cap, can, cg, cgp, cgn, gb, gbp, gbn, gc, gcp, gcn, hx, hxp, hxn, u2r, u2p, u2n,
             du, dup, dun, dz, dzp, dzn, w31, lg, lb, w3,
             dh_ref, dw31_ref, db31_ref, dlg_ref, dlb_ref, dw3_ref):
        i = pl.program_id(1)

        @pl.when(i == 0)
        def _():
            for r in (dw31_ref, db31_ref, dlg_ref, dlb_ref, dw3_ref):
                r[...] = jnp.zeros_like(r)

        main = slice(halo, halo + t)
        xh, rstd = _ln_stats(_extended(u2r, u2p, u2n, i, nt))
        yl = xh * lg[...] + lb[...]
        sg = _sigmoid(yl)
        d_yl = _extended(du, dup, dun, i, nt) * (sg * (1.0 + yl * (1.0 - sg)))
        dlg_ref[...] += jnp.sum((d_yl * xh)[main], axis=0, keepdims=True)
        dlb_ref[...] += jnp.sum(d_yl[main], axis=0, keepdims=True)
        dxh = d_yl * lg[...]
        du2 = rstd * (dxh - jnp.mean(dxh, axis=-1, keepdims=True) - xh * jnp.mean(dxh * xh, axis=-1, keepdims=True))
        db31_ref[...] += jnp.sum(du2[main], axis=0, keepdims=True)
        cae = _extended(ca, cap, can, i, nt)
        sc = _sigmoid(_extended(cg, cgp, cgn, i, nt))
        u1 = cae * sc
        _conv_wgrad(dw31_ref, du2[main], u1, CONF_K, halo, t)
        du1 = _conv(du2, w31, CONF_K, sign=-1)[main]
        dh_ref[:, 0:c] = (du1 * sc[main]).astype(BF16)
        dh_ref[:, c:2 * c] = (du1 * (cae * sc * (1.0 - sc))[main]).astype(BF16)
        gce = _extended(gc, gcp, gcn, i, nt)
        hxe = _extended(hx, hxp, hxn, i, nt)
        p = gce * hxe
        dze = _extended(dz, dzp, dzn, i, nt)
        d_c3 = dze * _extended(gb, gbp, gbn, i, nt)
        dh_ref[:, 2 * c:3 * c] = (dze[main] * _conv(p, w3, SC_K)[main]).astype(BF16)
        _conv_wgrad(dw3_ref, d_c3[main], p, SC_K, halo, t)
        dp = _conv(d_c3, w3, SC_K, sign=-1)[main]
        dh_ref[:, 3 * c:4 * c] = (dp * hxe[main]).astype(BF16)
        dh_ref[:, 4 * c:5 * c] = (dp * gce[main]).astype(BF16)

    hs = lambda blk: _halo_specs(s, t, halo, c, lambda j: blk)
    vec = lambda r: pl.BlockSpec((r, c), lambda j, i: (0, 0))
    specs = (hs(0) + hs(1) + hs(2) + hs(3) + hs(4) + hs(0) + hs(0) + hs(1)
             + [vec(CONF_K), vec(1), vec(1), vec(SC_K)])
    outs = pl.pallas_call(
        body, name=name, grid=(1, nt), in_specs=specs,
        out_specs=[pl.BlockSpec((t, 5 * c), lambda j, i: (i, 0)), vec(CONF_K), vec(1), vec(1), vec(1), vec(SC_K)],
        out_shape=[jax.ShapeDtypeStruct((s, 5 * c), BF16), jax.ShapeDtypeStruct((CONF_K, c), F32),
                   jax.ShapeDtypeStruct((1, c), F32), jax.ShapeDtypeStruct((1, c), F32),
                   jax.ShapeDtypeStruct((1, c), F32), jax.ShapeDtypeStruct((SC_K, c), F32)],
        compiler_params=_params(("arbitrary", "arbitrary")))(
            h, h, h, h, h, h, h, h, h, h, h, h, h, h, h, u2, u2, u2, dm, dm, dm, dm, dm, dm,
            dw_w, ln_g.reshape(1, c), ln_b.reshape(1, c), sc_w)
    dh, dw31, db31, dlg, dlb, dw3 = outs
    return dh, dw31, db31.reshape(c), dlg.reshape(c), dlb.reshape(c), dw3


def _attn_mask(kind, n, tq, kw, length):
    pad = (kw - tq) // 2
    iq = lax.broadcasted_iota(jnp.int32, (tq, 1), 0)
    ik = lax.broadcasted_iota(jnp.int32, (1, kw), 1)
    if kind == "band":
        rel = ik - pad - iq
        kpos = n * tq - pad + ik
        return (jnp.abs(rel) <= DIL_HALF) & (kpos >= 0) & (kpos < length)
    rows = length // GRID_W
    rq = n * NA_QROWS + (iq >> GRID_SHIFT)
    cq = iq & (GRID_W - 1)
    rk = n * NA_QROWS - pad // GRID_W + (ik >> GRID_SHIFT)
    ck = ik & (GRID_W - 1)
    r0 = jnp.clip(rq - NA_KH // 2, 0, rows - NA_KH)
    c0 = jnp.clip(cq - NA_KW // 2, 0, GRID_W - NA_KW)
    return (rk >= r0) & (rk < r0 + NA_KH) & (ck >= c0) & (ck < c0 + NA_KW)


def _mask_tiles(kind, tq, kw, length, *, name):
    nb = length // tq

    def body(o_ref):
        v = pl.program_id(0)
        n = jnp.where((v == 1) | (v == 3), 0, jnp.where(v == 2, nb - 1, 1))
        o_ref[0] = jnp.where(_attn_mask(kind, n, tq, kw, length), 0.0, NEG)

    return pl.pallas_call(body, name=name, grid=(4,), out_specs=pl.BlockSpec((1, tq, kw), lambda v: (v, 0, 0)),
                          out_shape=jax.ShapeDtypeStruct((4, tq, kw), F32),
                          compiler_params=_params(("parallel",)))()


class _AttnGeom:
    def __init__(self, s, d, tq, nsub):
        self.s, self.d, self.tq, self.nsub = s, d, tq, nsub
        self.halo = tq * d
        self.rows = nsub * self.halo
        self.nbig = s // self.rows
        self.ext = self.rows + 2 * self.halo
        assert s % self.rows == 0

    def main(self, col):
        return pl.BlockSpec((self.rows, LANES), lambda hp, n: (n, col + hp))

    def with_halos(self, col):
        last = self.s // self.halo - 1
        return [self.main(col),
                pl.BlockSpec((self.halo, LANES), lambda hp, n: (jnp.maximum(n * self.nsub - 1, 0), col + hp)),
                pl.BlockSpec((self.halo, LANES), lambda hp, n: (jnp.minimum((n + 1) * self.nsub, last), col + hp))]

    def fill_ext(self, ext_ref, main_ref, prev_ref, next_ref):
        ext_ref[0:self.halo] = prev_ref[...].astype(F32)
        ext_ref[self.halo:self.halo + self.rows] = main_ref[...].astype(F32)
        ext_ref[self.halo + self.rows:self.ext] = next_ref[...].astype(F32)

    def rows_of(self, r, pos, count):
        start = r + pos * self.d
        return pl.ds(start, count, stride=self.d) if self.d > 1 else pl.ds(start, count)

    def variant(self, n, sub):
        v = 0
        if sub == 0:
            v = v + jnp.where(n == 0, 1, 0)
        if sub == self.nsub - 1:
            v = v + jnp.where(n == self.nbig - 1, 2, 0)
        return v


def _attn_fwd(h, bias, mask, *, d, tq, nsub, cols, name):
    geo = _AttnGeom(h.shape[0], d, tq, nsub)
    scale = HEAD_DIM ** -0.5
    kw = bias.shape[2]
    first_key = tq - (kw - tq) // 2

    def body(q_ref, km, kp, kn, vm, vp, vn, b_ref, m_ref, o_ref, l_ref, kext, vext):
        n = pl.program_id(1)
        geo.fill_ext(kext, km, kp, kn)
        geo.fill_ext(vext, vm, vp, vn)
        low = lax.broadcasted_iota(jnp.int32, (1, LANES), 1) < HEAD_DIM
        for r in range(d):
            for sub in range(nsub):
                madd = m_ref[geo.variant(n, sub)]
                q = q_ref[geo.rows_of(r, sub * tq, tq), :].astype(F32) * scale
                ks = kext[geo.rows_of(r, sub * tq + first_key, kw), :].astype(BF16)
                vs = vext[geo.rows_of(r, sub * tq + first_key, kw), :].astype(BF16)
                outs, lses = [], []
                for hh in range(2):
                    qh = jnp.where(low if hh == 0 else ~low, q, 0.0).astype(BF16)
                    sc = (lax.dot_general(qh, ks, (((1,), (1,)), ((), ())), preferred_element_type=F32)
                          + b_ref[hh] + madd)
                    m = jnp.max(sc, axis=1, keepdims=True)
                    p = jnp.exp(sc - m)
                    den = jnp.sum(p, axis=1, keepdims=True)
                    outs.append(jnp.dot((p / den).astype(BF16), vs, preferred_element_type=F32))
                    lses.append(m + jnp.log(den))
                o_ref[geo.rows_of(r, sub * tq, tq), :] = jnp.where(low, outs[0], outs[1])
                l_ref[geo.rows_of(r, sub * tq, tq), :] = jnp.where(low, lses[0], lses[1])

    qc, kc, vc = cols
    specs = ([geo.main(qc)] + geo.with_halos(kc) + geo.with_halos(vc)
             + [pl.BlockSpec((2, tq, kw), lambda hp, n: (hp, 0, 0)),
                pl.BlockSpec((4, tq, kw), lambda hp, n: (0, 0, 0))])
    shape = jax.ShapeDtypeStruct((geo.s, 4 * LANES), F32)
    return pl.pallas_call(body, name=name, grid=(4, geo.nbig), in_specs=specs, out_specs=[geo.main(0), geo.main(0)],
                          out_shape=[shape, shape],
                          scratch_shapes=[pltpu.VMEM((geo.ext, LANES), F32), pltpu.VMEM((geo.ext, LANES), F32)],
                          compiler_params=_params(("parallel", "parallel"), VMEM_BIG))(
                              h, h, h, h, h, h, h, bias, mask)


def _attn_bwd(h, bias, mask, dy, y, lse, *, d, tq, nsub, cols, ycol, name):
    geo = _AttnGeom(h.shape[0], d, tq, nsub)
    scale = HEAD_DIM ** -0.5
    halo, rows = geo.halo, geo.rows
    kw = bias.shape[2]
    first_key = tq - (kw - tq) // 2

    def body(q_ref, km, kp, kn, vm, vp, vn, b_ref, m_ref, dy_ref, y_ref, l_ref, dq_ref, dk_hbm, dv_hbm, db_ref,
             kext, vext, dkext, dvext, dk_all, dv_all, sems):
        hp, n = pl.program_id(0), pl.program_id(1)

        @pl.when(n == 0)
        def _():
            dk_all[...] = jnp.zeros_like(dk_all)
            dv_all[...] = jnp.zeros_like(dv_all)
            db_ref[...] = jnp.zeros_like(db_ref)

        geo.fill_ext(kext, km, kp, kn)
        geo.fill_ext(vext, vm, vp, vn)
        dkext[...] = jnp.zeros_like(dkext)
        dvext[...] = jnp.zeros_like(dvext)
        low = lax.broadcasted_iota(jnp.int32, (1, LANES), 1) < HEAD_DIM
        for r in range(d):
            for sub in range(nsub):
                madd = m_ref[geo.variant(n, sub)]
                mine = geo.rows_of(r, sub * tq, tq)
                keys = geo.rows_of(r, sub * tq + first_key, kw)
                q = q_ref[mine, :].astype(F32) * scale
                ks = kext[keys, :].astype(BF16)
                vs = vext[keys, :].astype(BF16)
                dyv = dy_ref[mine, :]
                dyy = dyv * y_ref[mine, :]
                lse_all = l_ref[mine, :]
                dq = jnp.zeros((tq, LANES), F32)
                dk = jnp.zeros((kw, LANES), F32)
                dv = jnp.zeros((kw, LANES), F32)
                for hh in range(2):
                    sel = low if hh == 0 else ~low
                    qh = jnp.where(sel, q, 0.0).astype(BF16)
                    dyh = jnp.where(sel, dyv, 0.0).astype(BF16)
                    dsum = jnp.sum(jnp.where(sel, dyy, 0.0), axis=1, keepdims=True)
                    lse_h = lse_all[:, hh * HEAD_DIM:hh * HEAD_DIM + 1]
                    sc = (lax.dot_general(qh, ks, (((1,), (1,)), ((), ())), preferred_element_type=F32)
                          + b_ref[hh] + madd)
                    p = jnp.exp(sc - lse_h)
                    dp = lax.dot_general(dyh, vs, (((1,), (1,)), ((), ())), preferred_element_type=F32)
                    ds = p * (dp - dsum)
                    db_ref[hh] += ds
                    pb, dsb = p.astype(BF16), ds.astype(BF16)
                    dv += lax.dot_general(pb, dyh, (((0,), (0,)), ((), ())), preferred_element_type=F32)
                    dk += lax.dot_general(dsb, qh, (((0,), (0,)), ((), ())), preferred_element_type=F32)
                    dq += jnp.where(sel, jnp.dot(dsb, ks, preferred_element_type=F32), 0.0)
                dq_ref[mine, :] = dq * scale
                dkext[keys, :] += dk
                dvext[keys, :] += dv

        before = pl.multiple_of(jnp.maximum(n * rows - halo, 0), LANES)
        here = pl.multiple_of(n * rows, LANES)
        after = pl.multiple_of(jnp.minimum((n + 1) * rows, geo.s - halo), LANES)
        for ext, total in ((dkext, dk_all), (dvext, dv_all)):
            total[pl.ds(before, halo), :] += ext[0:halo]
            total[pl.ds(here, rows), :] += ext[halo:halo + rows]
            total[pl.ds(after, halo), :] += ext[halo + rows:geo.ext]

        @pl.when(n == geo.nbig - 1)
        def _():
            col = pl.ds(pl.multiple_of(hp * LANES, LANES), LANES)
            copies = [pltpu.make_async_copy(total, out.at[:, col], sems.at[i])
                      for i, (total, out) in enumerate(((dk_all, dk_hbm), (dv_all, dv_hbm)))]
            for cp in copies:
                cp.start()
            for cp in copies:
                cp.wait()

    qc, kc, vc = cols
    bspec = pl.BlockSpec((2, tq, kw), lambda hp, n: (hp, 0, 0))
    any_spec = pl.BlockSpec(memory_space=pl.ANY)
    specs = ([geo.main(qc)] + geo.with_halos(kc) + geo.with_halos(vc)
             + [bspec, pl.BlockSpec((4, tq, kw), lambda hp, n: (0, 0, 0)), geo.main(ycol), geo.main(ycol), geo.main(0)])
    shape = jax.ShapeDtypeStruct((geo.s, 4 * LANES), F32)
    ext = pltpu.VMEM((geo.ext, LANES), F32)
    whole = pltpu.VMEM((geo.s, LANES), F32)
    return pl.pallas_call(
        body, name=name, grid=(4, geo.nbig), in_specs=specs, out_specs=[geo.main(0), any_spec, any_spec, bspec],
        out_shape=[shape, shape, shape, jax.ShapeDtypeStruct(bias.shape, F32)],
        scratch_shapes=[ext, ext, ext, ext, whole, whole, pltpu.SemaphoreType.DMA((2,))],
        compiler_params=_params(("arbitrary", "arbitrary"), VMEM_BIG))(
            h, h, h, h, h, h, h, bias, mask, dy, y, lse)


def _dil_combine(outs, lses, *, name):
    s, c = outs[0].shape
    t = _tile(s, (512, 256, 128, 64, 8))

    def body(o0, o1, o2, l0, l1, l2, y_ref, lt_ref):
        ls = [l0[...], l1[...], l2[...]]
        m = jnp.maximum(jnp.maximum(ls[0], ls[1]), ls[2])
        es = [jnp.exp(l - m) for l in ls]
        den = es[0] + es[1] + es[2]
        y_ref[...] = (es[0] / den) * o0[...] + (es[1] / den) * o1[...] + (es[2] / den) * o2[...]
        lt_ref[...] = m + jnp.log(den)

    row = pl.BlockSpec((t, c), lambda i: (i, 0))
    shape = jax.ShapeDtypeStruct((s, c), F32)
    return pl.pallas_call(body, name=name, grid=(s // t,), in_specs=[row] * 6, out_specs=[row, row],
                          out_shape=[shape, shape], compiler_params=_params(("parallel",)))(*outs, *lses)


def _attn_dh(na, dil, *, name):
    s, c = na[0].shape
    t = _tile(s, (256, 128, 64, 8))

    def body(*refs):
        ins, o_ref = refs[:-1], refs[-1]
        for a in range(3):
            o_ref[:, a * c:(a + 1) * c] = ins[a][...].astype(BF16)
            o_ref[:, (3 + a) * c:(4 + a) * c] = (ins[3 + a][...] + ins[6 + a][...] + ins[9 + a][...]).astype(BF16)

    row = pl.BlockSpec((t, c), lambda i: (i, 0))
    flat = list(na) + [g[a] for g in dil for a in range(3)]
    return pl.pallas_call(body, name=name, grid=(s // t,), in_specs=[row] * 12,
                          out_specs=pl.BlockSpec((t, 6 * c), lambda i: (i, 0)),
                          out_shape=jax.ShapeDtypeStruct((s, 6 * c), BF16),
                          compiler_params=_params(("parallel",)))(*flat)


def _t5_bucket(rel):
    nb = N_BUCKETS // 2
    max_exact = nb // 2
    ret = np.where(rel > 0, nb, 0)
    n = np.abs(rel)
    large = max_exact + (np.log(np.maximum(n, 1).astype(np.float32) / np.float32(max_exact))
                         / np.float32(math.log(T5_MAX_DIST / max_exact)) * np.float32(nb - max_exact)).astype(np.int32)
    large = np.minimum(large, nb - 1)
    return (ret + np.where(n < max_exact, n, large)).astype(np.int32)


def _band_bucket_index(dil):
    tq, kw = DIL_TQ, DIL_KW
    rel = np.arange(kw)[None, :] - (kw - tq) // 2 - np.arange(tq)[:, None]
    return _t5_bucket(rel * dil)


def _band_bias(t5, dil, *, name):
    tq, kw = DIL_TQ, DIL_KW
    buckets = [int(b) for b in _t5_bucket(np.arange(-DIL_HALF, DIL_HALF + 1) * dil)]

    def body(t_ref, o_ref):
        hh = pl.program_id(0)
        rel = (lax.broadcasted_iota(jnp.int32, (tq, kw), 1) - (kw - tq) // 2
               - lax.broadcasted_iota(jnp.int32, (tq, kw), 0))
        acc = jnp.zeros((tq, kw), F32)
        for r, b in zip(range(-DIL_HALF, DIL_HALF + 1), buckets):
            acc = jnp.where(rel == r, t_ref[b * 8 + hh], acc)
        o_ref[0] = acc

    return pl.pallas_call(body, name=name, grid=(8,),
                          in_specs=[pl.BlockSpec(memory_space=pltpu.SMEM)],
                          out_specs=pl.BlockSpec((1, tq, kw), lambda h: (h, 0, 0)),
                          out_shape=jax.ShapeDtypeStruct((8, tq, kw), F32),
                          compiler_params=_params(("parallel",)))(t5.reshape(-1))


def _na_bias(rpb, *, name):
    nr, nc = 2 * NA_KH - 1, 2 * NA_KW - 1
    tq = NA_QROWS * GRID_W
    w = GRID_W

    def body(r_ref, o_ref):
        base = pl.program_id(0) * (nr * nc)
        lane = lax.broadcasted_iota(jnp.int32, (w, LANES), 1)
        upper = lane >= w
        diff = (lane & (w - 1)) - lax.broadcasted_iota(jnp.int32, (w, LANES), 0) + NA_KW - 1
        tiles = {}
        for i in range(NA_QROWS):
            for m in range(3 * NA_QROWS // 2):
                lo = 2 * m - i + NA_KH - 1 - NA_QROWS
                if lo not in tiles:
                    acc = jnp.zeros((w, LANES), F32)
                    for dc in range(nc):
                        v_lo = r_ref[base + lo * nc + dc] if 0 <= lo < nr else 0.0
                        v_hi = r_ref[base + (lo + 1) * nc + dc] if 0 <= lo + 1 < nr else 0.0
                        acc = jnp.where(diff == dc, jnp.where(upper, v_hi, v_lo), acc)
                    tiles[lo] = acc
                o_ref[0, i * w:(i + 1) * w, m * LANES:(m + 1) * LANES] = tiles[lo]

    return pl.pallas_call(body, name=name, grid=(8,),
                          in_specs=[pl.BlockSpec(memory_space=pltpu.SMEM)],
                          out_specs=pl.BlockSpec((1, tq, 3 * tq), lambda h: (h, 0, 0)),
                          out_shape=jax.ShapeDtypeStruct((8, tq, 3 * tq), F32),
                          compiler_params=_params(("parallel",)))(rpb.reshape(-1))


def _t5_grad(dbs, idxs, *, name):
    def body(d0, d1, d2, i0, i1, i2, o_ref):
        lane = lax.broadcasted_iota(jnp.int32, (1, LANES), 1)
        lines = [jnp.zeros((1, LANES), F32) for _ in range(8)]
        for dref, iref in ((d0, i0), (d1, i1), (d2, i2)):
            idx = iref[...]
            for hh in range(8):
                xh = dref[hh]
                for b in range(N_BUCKETS):
                    val = jnp.sum(jnp.sum(jnp.where(idx == b, xh, 0.0), axis=1, keepdims=True), axis=0, keepdims=True)
                    lines[hh] = lines[hh] + jnp.where(lane == b, val, 0.0)
        for hh in range(8):
            o_ref[hh:hh + 1, :] = lines[hh]

    out = pl.pallas_call(body, name=name, out_shape=jax.ShapeDtypeStruct((8, LANES), F32))(*dbs, *idxs)
    return out[:, :N_BUCKETS].T


def _rpb_grad(db, *, name):
    nr, nc = 2 * NA_KH - 1, 2 * NA_KW - 1
    tq = NA_QROWS * GRID_W
    w = GRID_W

    def body(d_ref, o_ref):
        x = d_ref[0]
        rows = []
        for dr in range(nr):
            acc = jnp.zeros((w, w), F32)
            for i in range(NA_QROWS):
                j = i + dr - (NA_KH - 1 - NA_QROWS)
                if 0 <= j < 3 * NA_QROWS:
                    acc = acc + x[i * w:(i + 1) * w, j * w:(j + 1) * w]
            rows.append(acc)
        diff = (lax.broadcasted_iota(jnp.int32, (w, w), 1) - lax.broadcasted_iota(jnp.int32, (w, w), 0)
                + NA_KW - 1)
        lane = lax.broadcasted_iota(jnp.int32, (1, LANES), 1)
        for dr in range(nr):
            line = jnp.zeros((1, LANES), F32)
            for dc in range(nc):
                val = jnp.sum(jnp.sum(jnp.where(diff == dc, rows[dr], 0.0), axis=1, keepdims=True),
                              axis=0, keepdims=True)
                line = jnp.where(lane == dc, val, line)
            o_ref[0, dr:dr + 1, :] = line

    out = pl.pallas_call(body, name=name, grid=(8,),
                         in_specs=[pl.BlockSpec((1, tq, 3 * tq), lambda h: (h, 0, 0))],
                         out_specs=pl.BlockSpec((1, nr, LANES), lambda h: (h, 0, 0)),
                         out_shape=jax.ShapeDtypeStruct((8, nr, LANES), F32),
                         compiler_params=_params(("parallel",)))(db)
    return out[:, :, :nc]


def _exchange(src, *, gather, name):
    shape = src.shape if not gather else (N_DEV,) + src.shape

    def body(src_ref, out_ref, send_sems, recv_sems, local_sem):
        _exchange_start(src_ref, out_ref, send_sems, recv_sems, local_sem, gather)
        _exchange_wait(src_ref, out_ref, send_sems, recv_sems, local_sem, gather)

    any_spec = pl.BlockSpec(memory_space=pl.ANY)
    return pl.pallas_call(
        body, name=name, in_specs=[any_spec], out_specs=any_spec, out_shape=jax.ShapeDtypeStruct(shape, src.dtype),
        scratch_shapes=_exchange_sems())(src)


def _exchange_sems():
    return [pltpu.SemaphoreType.DMA((N_DEV - 1,)), pltpu.SemaphoreType.DMA((N_DEV - 1,)), pltpu.SemaphoreType.DMA]


def _exchange_copies(src_ref, out_ref, send_sems, recv_sems, local_sem, gather):
    x, y, c = lax.axis_index("x"), lax.axis_index("y"), lax.axis_index("c")
    me = 4 * x + 2 * y + c

    def outgoing(p):
        return src_ref if gather else src_ref.at[p]

    own = pltpu.make_async_copy(outgoing(me), out_ref.at[me], local_sem)
    sends, recvs = [], []
    for k in range(1, N_DEV):
        px = 1 - x if k & 4 else x
        py = 1 - y if k & 2 else y
        pc = 1 - c if k & 1 else c
        p = 4 * px + 2 * py + pc
        for dst, group in ((me, sends), (p, recvs)):
            group.append(pltpu.make_async_remote_copy(
                src_ref=outgoing(p), dst_ref=out_ref.at[dst], send_sem=send_sems.at[k - 1],
                recv_sem=recv_sems.at[k - 1], device_id=(px, py, pc), device_id_type=pl.DeviceIdType.MESH))
    return own, sends, recvs


def _exchange_start(*refs_and_mode):
    own, sends, _ = _exchange_copies(*refs_and_mode)
    own.start()
    for cp in sends:
        cp.start()


def _exchange_wait(*refs_and_mode):
    own, sends, recvs = _exchange_copies(*refs_and_mode)
    for cp in recvs:
        cp.wait_recv()
    for cp in sends:
        cp.wait_send()
    own.wait()


def _adamw(parts, w, m, v, *, name):
    rows, cols = w.shape
    t = _tile(rows, (FLAT_ROW_TILE, 128, 64, 32, 16, 8))

    def body(p_ref, w_ref, m_ref, v_ref, g_ref, d_ref, nm_ref, nv_ref):
        g = p_ref[0].astype(F32)
        for k in range(1, N_DEV):
            g = g + p_ref[k].astype(F32)
        nm = ADAM_B1 * m_ref[...] + (1.0 - ADAM_B1) * g
        nv = ADAM_B2 * v_ref[...] + (1.0 - ADAM_B2) * (g * g)
        m_hat = nm / (1.0 - ADAM_B1 ** ADAM_STEP)
        v_hat = nv / (1.0 - ADAM_B2 ** ADAM_STEP)
        g_ref[...] = g
        d_ref[...] = -ADAM_LR * (m_hat / (jnp.sqrt(v_hat) + ADAM_EPS) + ADAM_WD * w_ref[...])
        nm_ref[...] = nm
        nv_ref[...] = nv

    row = pl.BlockSpec((t, cols), lambda i: (i, 0))
    shape = jax.ShapeDtypeStruct((rows, cols), F32)
    return pl.pallas_call(body, name=name, grid=(rows // t,),
                          in_specs=[pl.BlockSpec((N_DEV, t, cols), lambda i: (0, i, 0)), row, row, row],
                          out_specs=[row] * 4, out_shape=[shape] * 4,
                          compiler_params=_params(("parallel",), VMEM_BIG))(parts, w, m, v)


def _flatten(arrays, dtype, row_mult):
    flat = jnp.concatenate([a.reshape(-1).astype(dtype) for a in arrays])
    chunk = FLAT_COLS * row_mult
    padded = -(-flat.shape[0] // chunk) * chunk
    return jnp.pad(flat, (0, padded - flat.shape[0])).reshape(padded // FLAT_COLS, FLAT_COLS)


def _unflatten(flat, shapes):
    flat = flat.reshape(-1)
    out, pos = [], 0
    for shp in shapes:
        size = int(np.prod(shp))
        out.append(flat[pos:pos + size].reshape(shp))
        pos += size
    return out


def _gather_full(names, local, dtype, row_mult, label):
    got = _exchange(_flatten([local[n] for n in names], dtype, row_mult), gather=True, name=label)
    per_dev = [_unflatten(got[p], [local[n].shape for n in names]) for p in range(N_DEV)]
    return {n: jnp.concatenate([per_dev[p][i] for p in range(N_DEV)], axis=SHARD_AXIS[n])
            for i, n in enumerate(names)}


def _from_shards(stacked, axis):
    _, a, b = stacked.shape
    if axis == 1:
        return jnp.transpose(stacked, (1, 0, 2)).reshape(a, N_DEV * b)
    return stacked.reshape(N_DEV * a, b)


def _to_shards(full, axis):
    ra, rb = full.shape
    if axis == 1:
        return jnp.transpose(full.reshape(ra, N_DEV, rb // N_DEV), (1, 0, 2))
    return full.reshape(N_DEV, ra // N_DEV, rb)


class _ShardedMatmulWeights:
    def __init__(self, local):
        self.local, self.full, self.parts = local, {}, {}

    def gather_src(self, n, l):
        return self.local[n][l].astype(BF16)

    def set_gathered(self, n, l, got):
        self.full[n, l] = _from_shards(got, SHARD_AXIS[n] - 1)

    def get(self, n, l):
        return self.full[n, l]

    def scatter_src(self, n, l, dw):
        return _to_shards(dw, SHARD_AXIS[n] - 1).astype(BF16)

    def set_scattered(self, n, l, parts):
        self.parts[n, l] = parts


def _role(role, i):
    mixer = 'attn_w_' if i % 2 == 0 else 'conv_w_'
    return {'in': (mixer + 'in', i // 2), 'out': (mixer + 'out', i // 2),
            'up': ('ffn_w_up', i), 'down': ('ffn_w_down', i)}[role]


def _local_step(x, tgt, w, big):
    s = x.shape[0]

    def project(a, role, i, **kw):
        carry = None
        if i + 1 < DEPTH:
            nxt = _role(role, i + 1)
            carry = (big.gather_src(*nxt), True)
        out, got = _mm(a, big.get(*_role(role, i)), carry=carry, name=role + "_fwd", **kw)
        if got is not None:
            big.set_gathered(*nxt, got)
        return out

    def project_back(a, d_out, role, i):
        key = _role(role, i)
        dw, _ = _mm(a, d_out, ta=True, name=role + "_dw")
        d_in, parts = _mm(d_out, big.get(*key), tb=True, carry=(big.scatter_src(*key, dw), False), name=role + "_dx")
        if parts is not None:
            big.set_scattered(*key, parts)
        return d_in

    na_tq = NA_QROWS * GRID_W
    band_idx = [_band_bucket_index(d) for _, d in DIL_PATTERNS]
    band_bias = [_band_bias(w['t5_bias'], d, name=f"band_bias_{d}") for _, d in DIL_PATTERNS]
    band_mask = [_mask_tiles("band", DIL_TQ, DIL_KW, s // d, name=f"band_mask_{d}") for _, d in DIL_PATTERNS]
    na_mask = _mask_tiles("na", na_tq, 3 * na_tq, s, name="na_mask")
    na_cols, dil_cols = (0, 4, 8), (12, 16, 20)
    grads = {n: [None] * w[n].shape[0] for n in SMALL_SHARDED + REPLICATED if n != 't5_bias'}
    saved = []
    x16 = x.astype(BF16)

    for i in range(DEPTH):
        j = i // 2
        st = {'x': x, 'x16': x16}
        if i % 2 == 0:
            h = project(x16, 'in', i)
            na_bias = _na_bias(w['na_rpb'][j], name="na_bias")
            o_na, l_na = _attn_fwd(h, na_bias, na_mask, d=1, tq=na_tq, nsub=NA_NSUB, cols=na_cols, name="na_fwd")
            outs, lses = [], []
            for (_, d), bias, mask in zip(DIL_PATTERNS, band_bias, band_mask):
                o, l = _attn_fwd(h, bias, mask, d=d, tq=DIL_TQ, nsub=DIL_NSUB[d], cols=dil_cols, name=f"dil_fwd_{d}")
                outs.append(o)
                lses.append(l)
            y_dil, l_dil = _dil_combine(outs, lses, name="dil_combine")
            mid = jnp.concatenate([o_na, y_dil], axis=1)
            mid16 = mid.astype(BF16)
            ymix = project(mid16, 'out', i)
            st.update(h=h, mid=mid, mid16=mid16, l_na=l_na, l_dil=l_dil, na_bias=na_bias)
        else:
            h = project(x16, 'in', i)
            mid16, u2 = _conv_mid_fwd(h, w['conf_dw_w'][j], w['conf_dw_b'][j], w['conf_ln_g'][j], w['conf_ln_b'][j],
                                      w['sconv_w'][j], name="conv_mid_fwd")
            ymix = project(mid16, 'out', i)
            st.update(h=h, mid16=mid16, u2=u2)
        xa, xa16 = _ln_fwd(x, ymix, w['mix_ln_g'][i], w['mix_ln_b'][i], name="mix_ln")
        hu = project(xa16, 'up', i)
        act16 = _ffn_mid_fwd(hu, w['ffn_dw_w'][i], name="ffn_mid_fwd")
        yffn = project(act16, 'down', i)
        xb, xb16 = _ln_fwd(xa, yffn, w['ffn_ln_g'][i], w['ffn_ln_b'][i], name="ffn_ln")
        st.update(ymix=ymix, xa=xa, xa16=xa16, hu=hu, act16=act16, yffn=yffn)
        saved.append(st)
        x, x16 = xb, xb16

    loss, d1 = _loss_head(x, tgt, name="loss_head")
    d2 = None
    g_t5 = None
    for i in reversed(range(DEPTH)):
        j = i // 2
        st = saved[i]
        dz, dz16, dg, db = _ln_bwd(st['xa'], st['yffn'], w['ffn_ln_g'][i], d1, d2, name="ffn_ln_bwd")
        grads['ffn_ln_g'][i], grads['ffn_ln_b'][i] = dg, db
        dact = project_back(st['act16'], dz16, 'down', i)
        dhu, grads['ffn_dw_w'][i] = _ffn_mid_bwd(st['hu'], w['ffn_dw_w'][i], dact, name="ffn_mid_bwd")
        dxa = project_back(st['xa16'], dhu, 'up', i)
        dz, dz1, dg, db = _ln_bwd(st['x'], st['ymix'], w['mix_ln_g'][i], dz, dxa, name="mix_ln_bwd")
        grads['mix_ln_g'][i], grads['mix_ln_b'][i] = dg, db
        dmid = project_back(st['mid16'], dz1, 'out', i)
        if i % 2 == 0:
            h = st['h']
            dq, dk, dv, dbias = _attn_bwd(h, st['na_bias'], na_mask, dmid, st['mid'], st['l_na'], d=1, tq=na_tq,
                                          nsub=NA_NSUB, cols=na_cols, ycol=0, name="na_bwd")
            grads['na_rpb'][j] = _rpb_grad(dbias, name="rpb_grad")
            dil, dbs = [], []
            for (_, d), bias, mask in zip(DIL_PATTERNS, band_bias, band_mask):
                g = _attn_bwd(h, bias, mask, dmid, st['mid'], st['l_dil'], d=d, tq=DIL_TQ, nsub=DIL_NSUB[d],
                              cols=dil_cols, ycol=4, name=f"dil_bwd_{d}")
                dil.append(g[:3])
                dbs.append(g[3])
            t5 = _t5_grad(dbs, band_idx, name="t5_grad")
            g_t5 = t5 if g_t5 is None else g_t5 + t5
            dh = _attn_dh((dq, dk, dv), dil, name="attn_dh")
        else:
            dh, dw31, db31, dlg, dlb, dw3 = _conv_mid_bwd(st['h'], st['u2'], dmid, w['conf_dw_w'][j],
                                                          w['conf_ln_g'][j], w['conf_ln_b'][j], w['sconv_w'][j],
                                                          name="conv_mid_bwd")
            grads['conf_dw_w'][j], grads['conf_dw_b'][j] = dw31, db31
            grads['conf_ln_g'][j], grads['conf_ln_b'][j], grads['sconv_w'][j] = dlg, dlb, dw3
        d1, d2 = dz, project_back(st['x16'], dh, 'in', i)
    dx = _axpy(d1, d2, name="grad_x")
    full = {n: jnp.stack(g) for n, g in grads.items()}
    full['t5_bias'] = g_t5
    return loss, dx, full


def kernel(x, t5_bias, attn_w_in, attn_w_out, na_rpb, conv_w_in, conf_dw_w, conf_dw_b, conf_ln_g, conf_ln_b, sconv_w, conv_w_out, ffn_w_up, ffn_dw_w, ffn_w_down, mix_ln_g, mix_ln_b, ffn_ln_g, ffn_ln_b, loss_target, m_t5_bias, m_attn_w_in, m_attn_w_out, m_na_rpb, m_conv_w_in, m_conf_dw_w, m_conf_dw_b, m_conf_ln_g, m_conf_ln_b, m_sconv_w, m_conv_w_out, m_ffn_w_up, m_ffn_dw_w, m_ffn_w_down, m_mix_ln_g, m_mix_ln_b, m_ffn_ln_g, m_ffn_ln_b, v_t5_bias, v_attn_w_in, v_attn_w_out, v_na_rpb, v_conv_w_in, v_conf_dw_w, v_conf_dw_b, v_conf_ln_g, v_conf_ln_b, v_sconv_w, v_conv_w_out, v_ffn_w_up, v_ffn_dw_w, v_ffn_w_down, v_mix_ln_g, v_mix_ln_b, v_ffn_ln_g, v_ffn_ln_b):
    args = dict(locals())
    local = {n: args[n] for n in WEIGHTS}
    mom1 = {n: args['m_' + n] for n in WEIGHTS}
    mom2 = {n: args['v_' + n] for n in WEIGHTS}

    kinds = ('grad', 'delta', 'new_m', 'new_v')
    small = {n: local[n] for n in REPLICATED}
    small.update(_gather_full(SMALL_SHARDED, local, F32, 8, "gather_small_weights"))
    big = _ShardedMatmulWeights({n: local[n] for n in MATMUL_WEIGHTS})
    for role in ('in', 'out', 'up', 'down'):
        n, l = _role(role, 0)
        big.set_gathered(n, l, _exchange(big.gather_src(n, l), gather=True, name="gather_first_" + role))

    loss, dx, grads = _local_step(x[0], loss_target[0], small, big)
    loss = lax.psum(loss, MESH_AXES)

    out = {}
    for n in MATMUL_WEIGHTS:
        per_layer = [_adamw(big.parts[n, l], local[n][l], mom1[n][l], mom2[n][l], name="adamw_" + n)
                     for l in range(local[n].shape[0])]
        for k, kind in enumerate(kinds):
            out[kind + '_' + n] = jnp.stack([res[k] for res in per_layer])
    for names, sharded, label in ((SMALL_SHARDED, True, "small"), (REPLICATED, False, "replicated")):
        shapes = [local[n].shape for n in names]
        if sharded:
            per_dev = [_flatten([lax.slice_in_dim(grads[n], p * local[n].shape[SHARD_AXIS[n]],
                                                  (p + 1) * local[n].shape[SHARD_AXIS[n]], axis=SHARD_AXIS[n])
                                 for n in names], F32, 8) for p in range(N_DEV)]
            parts = _exchange(jnp.stack(per_dev), gather=False, name="scatter_small_grads")
        else:
            parts = _exchange(_flatten([grads[n] for n in names], F32, 8), gather=True, name="gather_replicated_grads")
        res = _adamw(parts, _flatten([local[n] for n in names], F32, 8), _flatten([mom1[n] for n in names], F32, 8),
                     _flatten([mom2[n] for n in names], F32, 8), name="adamw_" + label)
        for kind, flat in zip(kinds, res):
            for n, a in zip(names, _unflatten(flat, shapes)):
                out[kind + '_' + n] = a

    return (loss, dx[None], *[out[k + '_' + n] for k in ('grad', 'delta', 'new_m', 'new_v') for n in WEIGHTS])
```

```python
import functools
import math

import jax
import jax.numpy as jnp
import numpy as np
from jax import lax
from jax.experimental import pallas as pl
from jax.experimental.pallas import tpu as pltpu

F32 = jnp.float32
BF16 = jnp.bfloat16

N_DEV = 8
MESH_AXES = ("x", "y", "c")
DEPTH = 4
GRID_W = 64
GRID_SHIFT = 6
HEAD_DIM = 64
NA_KH = 8
NA_KW = 16
NA_QROWS = 4
DIL_PATTERNS = ((128, 1), (512, 4), (2048, 16))
DIL_HALF = 64
DIL_TQ = 128
DIL_KW = DIL_TQ + 2 * DIL_HALF
DIL_NSUB = {1: 8, 4: 2, 16: 1}
NA_NSUB = 4
N_BUCKETS = 32
T5_MAX_DIST = 1024
CONF_CH = 512
CONF_K = 31
SC_K = 3
FFN_K = 3
FFN_ROW_TILE = 128
LN_EPS = 1e-5
NEG = -1e30
ALPHA = (2 * DEPTH) ** 0.25
ADAM_LR = 0.001
ADAM_B1 = 0.9
ADAM_B2 = 0.999
ADAM_EPS = 1e-08
ADAM_WD = 0.01
ADAM_STEP = 10

LANES = 128
SUBLANES = 8
VMEM_BIG = 48 * 1024 * 1024
FLAT_COLS = 1024
FLAT_ROW_TILE = 256

WEIGHTS = ['t5_bias', 'attn_w_in', 'attn_w_out', 'na_rpb', 'conv_w_in', 'conf_dw_w', 'conf_dw_b', 'conf_ln_g',
           'conf_ln_b', 'sconv_w', 'conv_w_out', 'ffn_w_up', 'ffn_dw_w', 'ffn_w_down', 'mix_ln_g', 'mix_ln_b',
           'ffn_ln_g', 'ffn_ln_b']
SHARD_AXIS = {'attn_w_in': 2, 'attn_w_out': 1, 'conv_w_in': 2, 'conf_dw_w': 2, 'conf_dw_b': 1, 'conf_ln_g': 1,
              'conf_ln_b': 1, 'sconv_w': 2, 'conv_w_out': 1, 'ffn_w_up': 2, 'ffn_dw_w': 2, 'ffn_w_down': 1}
MATMUL_WEIGHTS = ['attn_w_in', 'attn_w_out', 'conv_w_in', 'conv_w_out', 'ffn_w_up', 'ffn_w_down']
SMALL_SHARDED = ['conf_dw_w', 'conf_dw_b', 'conf_ln_g', 'conf_ln_b', 'sconv_w', 'ffn_dw_w']
SHARDED = MATMUL_WEIGHTS + SMALL_SHARDED
REPLICATED = ['t5_bias', 'na_rpb', 'mix_ln_g', 'mix_ln_b', 'ffn_ln_g', 'ffn_ln_b']


def _tile(n, cands):
    for c in cands:
        if n % c == 0:
            return c
    return n


def _params(sem, vmem=None):
    return pltpu.CompilerParams(dimension_semantics=sem, vmem_limit_bytes=vmem)


def _sigmoid(x):
    return 0.5 * jnp.tanh(0.5 * x) + 0.5


MM_MAX_TILE = 1408
MM_MAX_K = 3072


def _lane_tile(n, cap):
    best = None
    for t in range(LANES, min(n, cap) + 1, LANES):
        if n % t == 0:
            best = t
    return best or n


def _mm(a, b, *, ta=False, tb=False, out_dtype=F32, carry=None, name):
    assert a.dtype == BF16 and b.dtype == BF16, (name, a.dtype, b.dtype)
    m, k = (a.shape[1], a.shape[0]) if ta else a.shape
    n = b.shape[0] if tb else b.shape[1]
    tm, tn, tk = _lane_tile(m, MM_MAX_TILE), _lane_tile(n, MM_MAX_TILE), _lane_tile(k, MM_MAX_K)
    grid = (m // tm, n // tn, k // tk)
    nk = grid[2]
    dims = (((0 if ta else 1,), (1 if tb else 0,)), ((), ()))
    use_acc = nk > 1 and out_dtype != F32
    ride = _Carried(carry, grid)

    def body(*refs):
        (a_ref, b_ref), (o_ref,), scratch = ride.split(refs, 2, 1)
        ride.start()
        part = lax.dot_general(a_ref[...], b_ref[...], dims, preferred_element_type=F32)
        if nk == 1:
            o_ref[...] = part.astype(out_dtype)
        else:
            acc_ref = scratch[0] if use_acc else o_ref
            kk = pl.program_id(2)

            @pl.when(kk == 0)
            def _():
                acc_ref[...] = part

            @pl.when(kk > 0)
            def _():
                acc_ref[...] += part

            if use_acc:
                @pl.when(kk == nk - 1)
                def _():
                    o_ref[...] = acc_ref[...].astype(out_dtype)

        ride.wait()

    a_spec = pl.BlockSpec((tk, tm), lambda i, j, q: (q, i)) if ta else pl.BlockSpec((tm, tk), lambda i, j, q: (i, q))
    b_spec = pl.BlockSpec((tn, tk), lambda i, j, q: (j, q)) if tb else pl.BlockSpec((tk, tn), lambda i, j, q: (q, j))
    o_spec = pl.BlockSpec((tm, tn), lambda i, j, q: (i, j))
    o_shape = jax.ShapeDtypeStruct((m, n), out_dtype)
    scratch = [pltpu.VMEM((tm, tn), F32)] if use_acc else []
    sem = ("arbitrary",) * 3 if ride.on else ("parallel", "parallel", "arbitrary")
    out = pl.pallas_call(
        body, name=name, grid=grid, in_specs=[a_spec, b_spec] + ride.specs, out_specs=[o_spec] + ride.specs,
        out_shape=[o_shape] + ride.out_shapes, scratch_shapes=scratch + ride.scratch,
        compiler_params=_params(sem, VMEM_BIG))(a, b, *ride.operands)
    return out[0], (out[1] if ride.on else None)


MM_LN_ROWS = 512


def _mm_ln(a, b, x, g, beta, *, carry=None, name):
    assert a.dtype == BF16 and b.dtype == BF16, (name, a.dtype, b.dtype)
    m, k = a.shape
    n = b.shape[1]
    assert k <= MM_MAX_K, (name, k)
    tm = _tile(m, (MM_LN_ROWS, 256, 128, 64, 8))
    grid = (m // tm,)
    ride = _Carried(carry, grid)

    def body(*refs):
        (a_ref, b_ref, x_ref, g_ref, beta_ref), (z_ref, o_ref, o16_ref), _ = ride.split(refs, 5, 3)
        ride.start()
        z = ALPHA * x_ref[...] + jnp.dot(a_ref[...], b_ref[...], preferred_element_type=F32)
        z_ref[...] = z
        xh, _ = _ln_stats(z)
        out = xh * g_ref[...] + beta_ref[...]
        o_ref[...] = out
        o16_ref[...] = out.astype(BF16)
        ride.wait()

    row = lambda width: pl.BlockSpec((tm, width), lambda i: (i, 0))
    vec = pl.BlockSpec((1, n), lambda i: (0, 0))
    out = pl.pallas_call(
        body, name=name, grid=grid,
        in_specs=[row(k), pl.BlockSpec((k, n), lambda i: (0, 0)), row(n), vec, vec] + ride.specs,
        out_specs=[row(n)] * 3 + ride.specs,
        out_shape=[jax.ShapeDtypeStruct((m, n), F32), jax.ShapeDtypeStruct((m, n), F32),
                   jax.ShapeDtypeStruct((m, n), BF16)] + ride.out_shapes,
        scratch_shapes=ride.scratch,
        compiler_params=_params(("arbitrary",) if ride.on else ("parallel",), VMEM_BIG))(
            a, b, x, g.reshape(1, n), beta.reshape(1, n), *ride.operands)
    return out[0], out[1], out[2], (out[3] if ride.on else None)


def _ln_stats(z):
    mu = jnp.mean(z, axis=-1, keepdims=True)
    zc = z - mu
    var = jnp.mean(zc * zc, axis=-1, keepdims=True)
    rstd = lax.rsqrt(var + LN_EPS)
    return zc * rstd, rstd


def _ln_bwd(z, g, d1, d2, *, name):
    s, d = z.shape
    t = _tile(s, (256, 128, 64, 8))
    two = d2 is not None

    def body(*refs):
        if two:
            z_ref, g_ref, d1_ref, d2_ref, dz_ref, dz16_ref, dg_ref, db_ref = refs
            dout = ALPHA * d1_ref[...] + d2_ref[...]
        else:
            z_ref, g_ref, d1_ref, dz_ref, dz16_ref, dg_ref, db_ref = refs
            dout = d1_ref[...]

        @pl.when(pl.program_id(0) == 0)
        def _():
            dg_ref[...] = jnp.zeros_like(dg_ref)
            db_ref[...] = jnp.zeros_like(db_ref)

        xh, rstd = _ln_stats(z_ref[...])
        dxh = dout * g_ref[...]
        dz = rstd * (dxh - jnp.mean(dxh, axis=-1, keepdims=True) - xh * jnp.mean(dxh * xh, axis=-1, keepdims=True))
        dz_ref[...] = dz
        dz16_ref[...] = dz.astype(BF16)
        dg_ref[...] += jnp.sum(dout * xh, axis=0, keepdims=True)
        db_ref[...] += jnp.sum(dout, axis=0, keepdims=True)

    row = pl.BlockSpec((t, d), lambda i: (i, 0))
    vec = pl.BlockSpec((1, d), lambda i: (0, 0))
    ins = [z, g.reshape(1, d), d1] + ([d2] if two else [])
    specs = [row, vec, row] + ([row] if two else [])
    dz, dz16, dg, db = pl.pallas_call(
        body, name=name, grid=(s // t,), in_specs=specs, out_specs=[row, row, vec, vec],
        out_shape=[jax.ShapeDtypeStruct((s, d), F32), jax.ShapeDtypeStruct((s, d), BF16),
                   jax.ShapeDtypeStruct((1, d), F32), jax.ShapeDtypeStruct((1, d), F32)],
        compiler_params=_params(("arbitrary",)))(*ins)
    return dz, dz16, dg.reshape(d), db.reshape(d)


def _axpy(d1, d2, *, name):
    s, d = d1.shape
    t = _tile(s, (256, 128, 64, 8))

    def body(a_ref, b_ref, o_ref):
        o_ref[...] = ALPHA * a_ref[...] + b_ref[...]

    row = pl.BlockSpec((t, d), lambda i: (i, 0))
    return pl.pallas_call(body, name=name, grid=(s // t,), in_specs=[row, row], out_specs=row,
                          out_shape=jax.ShapeDtypeStruct((s, d), F32), compiler_params=_params(("parallel",)))(d1, d2)


def _loss_head(y, tgt, *, name):
    s, d = y.shape
    t = _tile(s, (256, 128, 64, 8))

    def body(y_ref, t_ref, l_ref, dy_ref):
        @pl.when(pl.program_id(0) == 0)
        def _():
            l_ref[...] = jnp.zeros_like(l_ref)

        err = y_ref[...] - t_ref[...]
        dy_ref[...] = err * (1.0 / d)
        l_ref[...] += 0.5 * jnp.sum(jnp.sum(err * err, axis=1, keepdims=True), axis=0, keepdims=True) * (1.0 / d)

    row = pl.BlockSpec((t, d), lambda i: (i, 0))
    one = pl.BlockSpec((SUBLANES, LANES), lambda i: (0, 0))
    loss, dy = pl.pallas_call(
        body, name=name, grid=(s // t,), in_specs=[row, row], out_specs=[one, row],
        out_shape=[jax.ShapeDtypeStruct((SUBLANES, LANES), F32), jax.ShapeDtypeStruct((s, d), F32)],
        compiler_params=_params(("arbitrary",)))(y, tgt)
    return loss[0, 0], dy


def _halo_specs(s, t, halo, cb, col):
    per = t // halo
    last = s // halo - 1
    return [pl.BlockSpec((t, cb), lambda j, i: (i, col(j))),
            pl.BlockSpec((halo, cb), lambda j, i: (jnp.maximum(i * per - 1, 0), col(j))),
            pl.BlockSpec((halo, cb), lambda j, i: (jnp.minimum((i + 1) * per, last), col(j)))]


def _extended(main_ref, prev_ref, next_ref, i, n):
    prev = jnp.where(i > 0, prev_ref[...], 0.0)
    nxt = jnp.where(i < n - 1, next_ref[...], 0.0)
    return jnp.concatenate([prev, main_ref[...], nxt], axis=0)


def _shift(ext, o):
    if o == 0:
        return ext
    return pltpu.roll(ext, (-o) % ext.shape[0], 0)


def _taps(ext, k, sign=1):
    return [_shift(ext, sign * (j - k // 2)) for j in range(k)]


def _conv(ext, w_ref, k, sign=1, taps=None):
    taps = _taps(ext, k, sign) if taps is None else taps
    acc = None
    for j in range(k):
        term = w_ref[j:j + 1, :] * taps[j]
        acc = term if acc is None else acc + term
    return acc


def _conv_wgrad(dw_ref, d_main, x_ext, k, halo, t, taps=None):
    taps = _taps(x_ext, k) if taps is None else taps
    for j in range(k):
        dw_ref[j:j + 1, :] += jnp.sum(d_main * taps[j][halo:halo + t], axis=0, keepdims=True)


def _ffn_mid_fwd(hu, w, *, name):
    s, f2 = hu.shape
    f = f2 // 2
    t, cb, halo = _tile(s, (FFN_ROW_TILE,)), _lane_tile(f, MM_MAX_TILE), SUBLANES
    nt, nc = s // t, f // cb

    def body(g_ref, gp_ref, gn_ref, u_ref, up_ref, un_ref, wg_ref, wu_ref, a_ref):
        i = pl.program_id(1)
        hg = _conv(_extended(g_ref, gp_ref, gn_ref, i, nt), wg_ref, FFN_K)[halo:halo + t]
        hu_ = _conv(_extended(u_ref, up_ref, un_ref, i, nt), wu_ref, FFN_K)[halo:halo + t]
        a_ref[...] = (hg * _sigmoid(hg) * hu_).astype(BF16)

    specs = (_halo_specs(s, t, halo, cb, lambda j: j) + _halo_specs(s, t, halo, cb, lambda j: j + nc)
             + [pl.BlockSpec((FFN_K, cb), lambda j, i: (0, j)), pl.BlockSpec((FFN_K, cb), lambda j, i: (0, j + nc))])
    return pl.pallas_call(body, name=name, grid=(nc, nt), in_specs=specs,
                          out_specs=pl.BlockSpec((t, cb), lambda j, i: (i, j)),
                          out_shape=jax.ShapeDtypeStruct((s, f), BF16),
                          compiler_params=_params(("parallel", "parallel"), VMEM_BIG))(hu, hu, hu, hu, hu, hu, w, w)


def _ffn_mid_bwd(hu, w, da, *, name):
    s, f2 = hu.shape
    f = f2 // 2
    t, cb, halo = _tile(s, (FFN_ROW_TILE,)), _lane_tile(f, MM_MAX_TILE), SUBLANES
    nt, nc = s // t, f // cb

    def body(g_ref, gp_ref, gn_ref, u_ref, up_ref, un_ref, a_ref, ap_ref, an_ref, wg_ref, wu_ref,
             dg_ref, du_ref, dwg_ref, dwu_ref):
        i = pl.program_id(1)

        @pl.when(i == 0)
        def _():
            dwg_ref[...] = jnp.zeros_like(dwg_ref)
            dwu_ref[...] = jnp.zeros_like(dwu_ref)

        xg = _extended(g_ref, gp_ref, gn_ref, i, nt)
        xu = _extended(u_ref, up_ref, un_ref, i, nt)
        dae = _extended(a_ref, ap_ref, an_ref, i, nt)
        xg_taps, xu_taps = _taps(xg, FFN_K), _taps(xu, FFN_K)
        hg = _conv(xg, wg_ref, FFN_K, taps=xg_taps)
        hu_ = _conv(xu, wu_ref, FFN_K, taps=xu_taps)
        sg = _sigmoid(hg)
        d_hg = dae * hu_ * (sg * (1.0 + hg * (1.0 - sg)))
        d_hu = dae * (hg * sg)
        dg_ref[...] = _conv(d_hg, wg_ref, FFN_K, sign=-1)[halo:halo + t].astype(BF16)
        du_ref[...] = _conv(d_hu, wu_ref, FFN_K, sign=-1)[halo:halo + t].astype(BF16)
        _conv_wgrad(dwg_ref, d_hg[halo:halo + t], xg, FFN_K, halo, t, taps=xg_taps)
        _conv_wgrad(dwu_ref, d_hu[halo:halo + t], xu, FFN_K, halo, t, taps=xu_taps)

    wspec = lambda off: pl.BlockSpec((FFN_K, cb), lambda j, i: (0, j + off))
    specs = (_halo_specs(s, t, halo, cb, lambda j: j) + _halo_specs(s, t, halo, cb, lambda j: j + nc)
             + _halo_specs(s, t, halo, cb, lambda j: j) + [wspec(0), wspec(nc)])
    tile = pl.BlockSpec((t, cb), lambda j, i: (i, j))
    dg, du, dwg, dwu = pl.pallas_call(
        body, name=name, grid=(nc, nt), in_specs=specs, out_specs=[tile, tile, wspec(0), wspec(0)],
        out_shape=[jax.ShapeDtypeStruct((s, f), BF16), jax.ShapeDtypeStruct((s, f), BF16),
                   jax.ShapeDtypeStruct((FFN_K, f), F32), jax.ShapeDtypeStruct((FFN_K, f), F32)],
        compiler_params=_params(("parallel", "arbitrary"), VMEM_BIG))(hu, hu, hu, hu, hu, hu, da, da, da, w, w)
    return jnp.concatenate([dg, du], axis=1), jnp.concatenate([dwg, dwu], axis=1)


CONV_HALO = 16


def _conv_mid_fwd(h, dw_w, dw_b, ln_g, ln_b, sc_w, *, name):
    s = h.shape[0]
    c = CONF_CH
    t, halo = _tile(s, (256, 128)), CONV_HALO
    nt = s // t

    def body(ca, cap, can, cg, cgp, cgn, gb, gc, gcp, gcn, hx, hxp, hxn, w31, b31, lg, lb, w3, o_ref, u2_ref):
        i = pl.program_id(1)
        u1 = _extended(ca, cap, can, i, nt) * _sigmoid(_extended(cg, cgp, cgn, i, nt))
        u2 = _conv(u1, w31, CONF_K)[halo:halo + t] + b31[...]
        u2_ref[...] = u2
        xh, _ = _ln_stats(u2)
        yl = xh * lg[...] + lb[...]
        o_ref[:, 0:c] = (yl * _sigmoid(yl)).astype(BF16)
        p = _extended(gc, gcp, gcn, i, nt) * _extended(hx, hxp, hxn, i, nt)
        o_ref[:, c:2 * c] = (gb[...] * _conv(p, w3, SC_K)[halo:halo + t]).astype(BF16)

    hs = lambda blk: _halo_specs(s, t, halo, c, lambda j: blk)
    vec = lambda r: pl.BlockSpec((r, c), lambda j, i: (0, 0))
    specs = hs(0) + hs(1) + hs(2)[:1] + hs(3) + hs(4) + [vec(CONF_K), vec(1), vec(1), vec(1), vec(SC_K)]
    return pl.pallas_call(
        body, name=name, grid=(1, nt), in_specs=specs,
        out_specs=[pl.BlockSpec((t, 2 * c), lambda j, i: (i, 0)), pl.BlockSpec((t, c), lambda j, i: (i, 0))],
        out_shape=[jax.ShapeDtypeStruct((s, 2 * c), BF16), jax.ShapeDtypeStruct((s, c), F32)],
        compiler_params=_params(("parallel", "parallel")))(
            h, h, h, h, h, h, h, h, h, h, h, h, h, dw_w, dw_b.reshape(1, c), ln_g.reshape(1, c),
            ln_b.reshape(1, c), sc_w)


def _conv_mid_bwd(h, u2, dm, dw_w, ln_g, ln_b, sc_w, *, name):
    s = h.shape[0]
    c = CONF_CH
    t, halo = _tile(s, (256, 128)), CONV_HALO
    nt = s // t

    def body(ca, cap, can, cg, cgp, cgn, gb, gbp, gbn, gc, gcp, gcn, hx, hxp, hxn, u2r, u2p, u2n,
             du, dup, dun, dz, dzp, dzn, w31, lg, lb, w3,
             dh_ref, dw31_ref, db31_ref, dlg_ref, dlb_ref, dw3_ref):
        i = pl.program_id(1)

        @pl.when(i == 0)
        def _():
            for r in (dw31_ref, db31_ref, dlg_ref, dlb_ref, dw3_ref):
                r[...] = jnp.zeros_like(r)

        main = slice(halo, halo + t)
        xh, rstd = _ln_stats(_extended(u2r, u2p, u2n, i, nt))
        yl = xh * lg[...] + lb[...]
        sg = _sigmoid(yl)
        d_yl = _extended(du, dup, dun, i, nt) * (sg * (1.0 + yl * (1.0 - sg)))
        dlg_ref[...] += jnp.sum((d_yl * xh)[main], axis=0, keepdims=True)
        dlb_ref[...] += jnp.sum(d_yl[main], axis=0, keepdims=True)
        dxh = d_yl * lg[...]
        du2 = rstd * (dxh - jnp.mean(dxh, axis=-1, keepdims=True) - xh * jnp.mean(dxh * xh, axis=-1, keepdims=True))
        db31_ref[...] += jnp.sum(du2[main], axis=0, keepdims=True)
        cae = _extended(ca, cap, can, i, nt)
        sc = _sigmoid(_extended(cg, cgp, cgn, i, nt))
        u1 = cae * sc
        _conv_wgrad(dw31_ref, du2[main], u1, CONF_K, halo, t)
        du1 = _conv(du2, w31, CONF_K, sign=-1)[main]
        dh_ref[:, 0:c] = (du1 * sc[main]).astype(BF16)
        dh_ref[:, c:2 * c] = (du1 * (cae * sc * (1.0 - sc))[main]).astype(BF16)
        gce = _extended(gc, gcp, gcn, i, nt)
        hxe = _extended(hx, hxp, hxn, i, nt)
        p = gce * hxe
        dze = _extended(dz, dzp, dzn, i, nt)
        d_c3 = dze * _extended(gb, gbp, gbn, i, nt)
        dh_ref[:, 2 * c:3 * c] = (dze[main] * _conv(p, w3, SC_K)[main]).astype(BF16)
        _conv_wgrad(dw3_ref, d_c3[main], p, SC_K, halo, t)
        dp = _conv(d_c3, w3, SC_K, sign=-1)[main]
        dh_ref[:, 3 * c:4 * c] = (dp * hxe[main]).astype(BF16)
        dh_ref[:, 4 * c:5 * c] = (dp * gce[main]).astype(BF16)

    hs = lambda blk: _halo_specs(s, t, halo, c, lambda j: blk)
    vec = lambda r: pl.BlockSpec((r, c), lambda j, i: (0, 0))
    specs = (hs(0) + hs(1) + hs(2) + hs(3) + hs(4) + hs(0) + hs(0) + hs(1)
             + [vec(CONF_K), vec(1), vec(1), vec(SC_K)])
    outs = pl.pallas_call(
        body, name=name, grid=(1, nt), in_specs=specs,
        out_specs=[pl.BlockSpec((t, 5 * c), lambda j, i: (i, 0)), vec(CONF_K), vec(1), vec(1), vec(1), vec(SC_K)],
        out_shape=[jax.ShapeDtypeStruct((s, 5 * c), BF16), jax.ShapeDtypeStruct((CONF_K, c), F32),
                   jax.ShapeDtypeStruct((1, c), F32), jax.ShapeDtypeStruct((1, c), F32),
                   jax.ShapeDtypeStruct((1, c), F32), jax.ShapeDtypeStruct((SC_K, c), F32)],
        compiler_params=_params(("arbitrary", "arbitrary")))(
            h, h, h, h, h, h, h, h, h, h, h, h, h, h, h, u2, u2, u2, dm, dm, dm, dm, dm, dm,
            dw_w, ln_g.reshape(1, c), ln_b.reshape(1, c), sc_w)
    dh, dw31, db31, dlg, dlb, dw3 = outs
    return dh, dw31, db31.reshape(c), dlg.reshape(c), dlb.reshape(c), dw3


def _attn_mask(kind, n, tq, kw, length):
    pad = (kw - tq) // 2
    iq = lax.broadcasted_iota(jnp.int32, (tq, 1), 0)
    ik = lax.broadcasted_iota(jnp.int32, (1, kw), 1)
    if kind == "band":
        rel = ik - pad - iq
        kpos = n * tq - pad + ik
        return (jnp.abs(rel) <= DIL_HALF) & (kpos >= 0) & (kpos < length)
    rows = length // GRID_W
    rq = n * NA_QROWS + (iq >> GRID_SHIFT)
    cq = iq & (GRID_W - 1)
    rk = n * NA_QROWS - pad // GRID_W + (ik >> GRID_SHIFT)
    ck = ik & (GRID_W - 1)
    r0 = jnp.clip(rq - NA_KH // 2, 0, rows - NA_KH)
    c0 = jnp.clip(cq - NA_KW // 2, 0, GRID_W - NA_KW)
    return (rk >= r0) & (rk < r0 + NA_KH) & (ck >= c0) & (ck < c0 + NA_KW)


def _mask_tiles(kind, tq, kw, length, *, name):
    nb = length // tq

    def body(o_ref):
        v = pl.program_id(0)
        n = jnp.where((v == 1) | (v == 3), 0, jnp.where(v == 2, nb - 1, 1))
        o_ref[0] = jnp.where(_attn_mask(kind, n, tq, kw, length), 0.0, NEG)

    return pl.pallas_call(body, name=name, grid=(4,), out_specs=pl.BlockSpec((1, tq, kw), lambda v: (v, 0, 0)),
                          out_shape=jax.ShapeDtypeStruct((4, tq, kw), F32),
                          compiler_params=_params(("parallel",)))()


class _AttnGeom:
    def __init__(self, s, d, tq, nsub):
        self.s, self.d, self.tq, self.nsub = s, d, tq, nsub
        self.halo = tq * d
        self.rows = nsub * self.halo
        self.nbig = s // self.rows
        self.ext = self.rows + 2 * self.halo
        assert s % self.rows == 0

    def main(self, col):
        return pl.BlockSpec((self.rows, LANES), lambda hp, n: (n, col + hp))

    def with_halos(self, col):
        last = self.s // self.halo - 1
        return [self.main(col),
                pl.BlockSpec((self.halo, LANES), lambda hp, n: (jnp.maximum(n * self.nsub - 1, 0), col + hp)),
                pl.BlockSpec((self.halo, LANES), lambda hp, n: (jnp.minimum((n + 1) * self.nsub, last), col + hp))]

    def fill_ext(self, ext_ref, main_ref, prev_ref, next_ref):
        ext_ref[0:self.halo] = prev_ref[...].astype(F32)
        ext_ref[self.halo:self.halo + self.rows] = main_ref[...].astype(F32)
        ext_ref[self.halo + self.rows:self.ext] = next_ref[...].astype(F32)

    def rows_of(self, r, pos, count):
        start = r + pos * self.d
        return pl.ds(start, count, stride=self.d) if self.d > 1 else pl.ds(start, count)

    def variant(self, n, sub):
        v = 0
        if sub == 0:
            v = v + jnp.where(n == 0, 1, 0)
        if sub == self.nsub - 1:
            v = v + jnp.where(n == self.nbig - 1, 2, 0)
        return v


def _attn_fwd(h, bias, mask, *, d, tq, nsub, cols, carry=None, name):
    geo = _AttnGeom(h.shape[0], d, tq, nsub)
    scale = HEAD_DIM ** -0.5
    kw = bias.shape[2]
    first_key = tq - (kw - tq) // 2

    ride = _Carried(carry, (4, geo.nbig))

    def body(*refs):
        (q_ref, km, kp, kn, vm, vp, vn, b_ref, m_ref), (o_ref, l_ref), (kext, vext) = ride.split(refs, 9, 2)
        ride.start()
        n = pl.program_id(1)
        geo.fill_ext(kext, km, kp, kn)
        geo.fill_ext(vext, vm, vp, vn)
        low = lax.broadcasted_iota(jnp.int32, (1, LANES), 1) < HEAD_DIM
        for r in range(d):
            for sub in range(nsub):
                madd = m_ref[geo.variant(n, sub)]
                q = q_ref[geo.rows_of(r, sub * tq, tq), :].astype(F32) * scale
                ks = kext[geo.rows_of(r, sub * tq + first_key, kw), :].astype(BF16)
                vs = vext[geo.rows_of(r, sub * tq + first_key, kw), :].astype(BF16)
                outs, lses = [], []
                for hh in range(2):
                    qh = jnp.where(low if hh == 0 else ~low, q, 0.0).astype(BF16)
                    sc = (lax.dot_general(qh, ks, (((1,), (1,)), ((), ())), preferred_element_type=F32)
                          + b_ref[hh] + madd)
                    m = jnp.max(sc, axis=1, keepdims=True)
                    p = jnp.exp(sc - m)
                    den = jnp.sum(p, axis=1, keepdims=True)
                    outs.append(jnp.dot((p / den).astype(BF16), vs, preferred_element_type=F32))
                    lses.append(m + jnp.log(den))
                o_ref[geo.rows_of(r, sub * tq, tq), :] = jnp.where(low, outs[0], outs[1])
                l_ref[geo.rows_of(r, sub * tq, tq), :] = jnp.where(low, lses[0], lses[1])
        ride.wait()

    qc, kc, vc = cols
    specs = ([geo.main(qc)] + geo.with_halos(kc) + geo.with_halos(vc)
             + [pl.BlockSpec((2, tq, kw), lambda hp, n: (hp, 0, 0)),
                pl.BlockSpec((4, tq, kw), lambda hp, n: (0, 0, 0))])
    shape = jax.ShapeDtypeStruct((geo.s, 4 * LANES), F32)
    ext = pltpu.VMEM((geo.ext, LANES), F32)
    out = pl.pallas_call(
        body, name=name, grid=(4, geo.nbig), in_specs=specs + ride.specs,
        out_specs=[geo.main(0), geo.main(0)] + ride.specs, out_shape=[shape, shape] + ride.out_shapes,
        scratch_shapes=[ext, ext] + ride.scratch,
        compiler_params=_params(("arbitrary", "arbitrary") if ride.on else ("parallel", "parallel"), VMEM_BIG))(
            h, h, h, h, h, h, h, bias, mask, *ride.operands)
    return out[0], out[1], (out[2] if ride.on else None)


def _attn_bwd(h, bias, mask, dy, y, lse, *, d, tq, nsub, cols, ycol, name):
    geo = _AttnGeom(h.shape[0], d, tq, nsub)
    scale = HEAD_DIM ** -0.5
    halo, rows = geo.halo, geo.rows
    kw = bias.shape[2]
    first_key = tq - (kw - tq) // 2

    def body(q_ref, km, kp, kn, vm, vp, vn, b_ref, m_ref, dy_ref, y_ref, l_ref, dq_ref, dk_hbm, dv_hbm, db_ref,
             kext, vext, dkext, dvext, dk_all, dv_all, sems):
        hp, n = pl.program_id(0), pl.program_id(1)

        @pl.when(n == 0)
        def _():
            dk_all[...] = jnp.zeros_like(dk_all)
            dv_all[...] = jnp.zeros_like(dv_all)
            db_ref[...] = jnp.zeros_like(db_ref)

        geo.fill_ext(kext, km, kp, kn)
        geo.fill_ext(vext, vm, vp, vn)
        dkext[...] = jnp.zeros_like(dkext)
        dvext[...] = jnp.zeros_like(dvext)
        low = lax.broadcasted_iota(jnp.int32, (1, LANES), 1) < HEAD_DIM
        for r in range(d):
            for sub in range(nsub):
                madd = m_ref[geo.variant(n, sub)]
                mine = geo.rows_of(r, sub * tq, tq)
                keys = geo.rows_of(r, sub * tq + first_key, kw)
                q = q_ref[mine, :].astype(F32) * scale
                ks = kext[keys, :].astype(BF16)
                vs = vext[keys, :].astype(BF16)
                dyv = dy_ref[mine, :]
                dyy = dyv * y_ref[mine, :]
                lse_all = l_ref[mine, :]
                dq = jnp.zeros((tq, LANES), F32)
                dk = jnp.zeros((kw, LANES), F32)
                dv = jnp.zeros((kw, LANES), F32)
                for hh in range(2):
                    sel = low if hh == 0 else ~low
                    qh = jnp.where(sel, q, 0.0).astype(BF16)
                    dyh = jnp.where(sel, dyv, 0.0).astype(BF16)
                    dsum = jnp.sum(jnp.where(sel, dyy, 0.0), axis=1, keepdims=True)
                    lse_h = lse_all[:, hh * HEAD_DIM:hh * HEAD_DIM + 1]
                    sc = (lax.dot_general(qh, ks, (((1,), (1,)), ((), ())), preferred_element_type=F32)
                          + b_ref[hh] + madd)
                    p = jnp.exp(sc - lse_h)
                    dp = lax.dot_general(dyh, vs, (((1,), (1,)), ((), ())), preferred_element_type=F32)
                    ds = p * (dp - dsum)
                    db_ref[hh] += ds
                    pb, dsb = p.astype(BF16), ds.astype(BF16)
                    dv += lax.dot_general(pb, dyh, (((0,), (0,)), ((), ())), preferred_element_type=F32)
                    dk += lax.dot_general(dsb, qh, (((0,), (0,)), ((), ())), preferred_element_type=F32)
                    dq += jnp.where(sel, jnp.dot(dsb, ks, preferred_element_type=F32), 0.0)
                dq_ref[mine, :] = dq * scale
                dkext[keys, :] += dk
                dvext[keys, :] += dv

        before = pl.multiple_of(jnp.maximum(n * rows - halo, 0), LANES)
        here = pl.multiple_of(n * rows, LANES)
        after = pl.multiple_of(jnp.minimum((n + 1) * rows, geo.s - halo), LANES)
        for ext, total in ((dkext, dk_all), (dvext, dv_all)):
            total[pl.ds(before, halo), :] += ext[0:halo]
            total[pl.ds(here, rows), :] += ext[halo:halo + rows]
            total[pl.ds(after, halo), :] += ext[halo + rows:geo.ext]

        @pl.when(n == geo.nbig - 1)
        def _():
            col = pl.ds(pl.multiple_of(hp * LANES, LANES), LANES)
            copies = [pltpu.make_async_copy(total, out.at[:, col], sems.at[i])
                      for i, (total, out) in enumerate(((dk_all, dk_hbm), (dv_all, dv_hbm)))]
            for cp in copies:
                cp.start()
            for cp in copies:
                cp.wait()

    qc, kc, vc = cols
    bspec = pl.BlockSpec((2, tq, kw), lambda hp, n: (hp, 0, 0))
    any_spec = pl.BlockSpec(memory_space=pl.ANY)
    specs = ([geo.main(qc)] + geo.with_halos(kc) + geo.with_halos(vc)
             + [bspec, pl.BlockSpec((4, tq, kw), lambda hp, n: (0, 0, 0)), geo.main(ycol), geo.main(ycol), geo.main(0)])
    shape = jax.ShapeDtypeStruct((geo.s, 4 * LANES), F32)
    ext = pltpu.VMEM((geo.ext, LANES), F32)
    whole = pltpu.VMEM((geo.s, LANES), F32)
    return pl.pallas_call(
        body, name=name, grid=(4, geo.nbig), in_specs=specs, out_specs=[geo.main(0), any_spec, any_spec, bspec],
        out_shape=[shape, shape, shape, jax.ShapeDtypeStruct(bias.shape, F32)],
        scratch_shapes=[ext, ext, ext, ext, whole, whole, pltpu.SemaphoreType.DMA((2,))],
        compiler_params=_params(("arbitrary", "arbitrary"), VMEM_BIG))(
            h, h, h, h, h, h, h, bias, mask, dy, y, lse)


def _dil_combine(outs, lses, *, name):
    s, c = outs[0].shape
    t = _tile(s, (512, 256, 128, 64, 8))

    def body(o0, o1, o2, l0, l1, l2, y_ref, lt_ref):
        ls = [l0[...], l1[...], l2[...]]
        m = jnp.maximum(jnp.maximum(ls[0], ls[1]), ls[2])
        es = [jnp.exp(l - m) for l in ls]
        den = es[0] + es[1] + es[2]
        y_ref[...] = (es[0] / den) * o0[...] + (es[1] / den) * o1[...] + (es[2] / den) * o2[...]
        lt_ref[...] = m + jnp.log(den)

    row = pl.BlockSpec((t, c), lambda i: (i, 0))
    shape = jax.ShapeDtypeStruct((s, c), F32)
    return pl.pallas_call(body, name=name, grid=(s // t,), in_specs=[row] * 6, out_specs=[row, row],
                          out_shape=[shape, shape], compiler_params=_params(("parallel",)))(*outs, *lses)


def _attn_dh(na, dil, *, name):
    s, c = na[0].shape
    t = _tile(s, (256, 128, 64, 8))

    def body(*refs):
        ins, o_ref = refs[:-1], refs[-1]
        for a in range(3):
            o_ref[:, a * c:(a + 1) * c] = ins[a][...].astype(BF16)
            o_ref[:, (3 + a) * c:(4 + a) * c] = (ins[3 + a][...] + ins[6 + a][...] + ins[9 + a][...]).astype(BF16)

    row = pl.BlockSpec((t, c), lambda i: (i, 0))
    flat = list(na) + [g[a] for g in dil for a in range(3)]
    return pl.pallas_call(body, name=name, grid=(s // t,), in_specs=[row] * 12,
                          out_specs=pl.BlockSpec((t, 6 * c), lambda i: (i, 0)),
                          out_shape=jax.ShapeDtypeStruct((s, 6 * c), BF16),
                          compiler_params=_params(("parallel",)))(*flat)


def _t5_bucket(rel):
    nb = N_BUCKETS // 2
    max_exact = nb // 2
    ret = np.where(rel > 0, nb, 0)
    n = np.abs(rel)
    large = max_exact + (np.log(np.maximum(n, 1).astype(np.float32) / np.float32(max_exact))
                         / np.float32(math.log(T5_MAX_DIST / max_exact)) * np.float32(nb - max_exact)).astype(np.int32)
    large = np.minimum(large, nb - 1)
    return (ret + np.where(n < max_exact, n, large)).astype(np.int32)


def _band_bucket_index(dil):
    tq, kw = DIL_TQ, DIL_KW
    rel = np.arange(kw)[None, :] - (kw - tq) // 2 - np.arange(tq)[:, None]
    return _t5_bucket(rel * dil)


def _band_bias(t5, dil, *, name):
    tq, kw = DIL_TQ, DIL_KW
    buckets = [int(b) for b in _t5_bucket(np.arange(-DIL_HALF, DIL_HALF + 1) * dil)]

    def body(t_ref, o_ref):
        hh = pl.program_id(0)
        rel = (lax.broadcasted_iota(jnp.int32, (tq, kw), 1) - (kw - tq) // 2
               - lax.broadcasted_iota(jnp.int32, (tq, kw), 0))
        acc = jnp.zeros((tq, kw), F32)
        for r, b in zip(range(-DIL_HALF, DIL_HALF + 1), buckets):
            acc = jnp.where(rel == r, t_ref[b * 8 + hh], acc)
        o_ref[0] = acc

    return pl.pallas_call(body, name=name, grid=(8,),
                          in_specs=[pl.BlockSpec(memory_space=pltpu.SMEM)],
                          out_specs=pl.BlockSpec((1, tq, kw), lambda h: (h, 0, 0)),
                          out_shape=jax.ShapeDtypeStruct((8, tq, kw), F32),
                          compiler_params=_params(("parallel",)))(t5.reshape(-1))


def _na_bias(rpb, *, name):
    nr, nc = 2 * NA_KH - 1, 2 * NA_KW - 1
    tq = NA_QROWS * GRID_W
    w = GRID_W

    def body(r_ref, o_ref):
        base = pl.program_id(0) * (nr * nc)
        lane = lax.broadcasted_iota(jnp.int32, (w, LANES), 1)
        upper = lane >= w
        diff = (lane & (w - 1)) - lax.broadcasted_iota(jnp.int32, (w, LANES), 0) + NA_KW - 1
        tiles = {}
        for i in range(NA_QROWS):
            for m in range(3 * NA_QROWS // 2):
                lo = 2 * m - i + NA_KH - 1 - NA_QROWS
                if lo not in tiles:
                    acc = jnp.zeros((w, LANES), F32)
                    for dc in range(nc):
                        v_lo = r_ref[base + lo * nc + dc] if 0 <= lo < nr else 0.0
                        v_hi = r_ref[base + (lo + 1) * nc + dc] if 0 <= lo + 1 < nr else 0.0
                        acc = jnp.where(diff == dc, jnp.where(upper, v_hi, v_lo), acc)
                    tiles[lo] = acc
                o_ref[0, i * w:(i + 1) * w, m * LANES:(m + 1) * LANES] = tiles[lo]

    return pl.pallas_call(body, name=name, grid=(8,),
                          in_specs=[pl.BlockSpec(memory_space=pltpu.SMEM)],
                          out_specs=pl.BlockSpec((1, tq, 3 * tq), lambda h: (h, 0, 0)),
                          out_shape=jax.ShapeDtypeStruct((8, tq, 3 * tq), F32),
                          compiler_params=_params(("parallel",)))(rpb.reshape(-1))


def _t5_grad(dbs, idxs, *, name):
    def body(d0, d1, d2, i0, i1, i2, o_ref):
        lane = lax.broadcasted_iota(jnp.int32, (1, LANES), 1)
        lines = [jnp.zeros((1, LANES), F32) for _ in range(8)]
        for dref, iref in ((d0, i0), (d1, i1), (d2, i2)):
            idx = iref[...]
            for hh in range(8):
                xh = dref[hh]
                for b in range(N_BUCKETS):
                    val = jnp.sum(jnp.sum(jnp.where(idx == b, xh, 0.0), axis=1, keepdims=True), axis=0, keepdims=True)
                    lines[hh] = lines[hh] + jnp.where(lane == b, val, 0.0)
        for hh in range(8):
            o_ref[hh:hh + 1, :] = lines[hh]

    out = pl.pallas_call(body, name=name, out_shape=jax.ShapeDtypeStruct((8, LANES), F32))(*dbs, *idxs)
    return out[:, :N_BUCKETS].T


def _rpb_grad(db, *, name):
    nr, nc = 2 * NA_KH - 1, 2 * NA_KW - 1
    tq = NA_QROWS * GRID_W
    w = GRID_W

    def body(d_ref, o_ref):
        x = d_ref[0]
        rows = []
        for dr in range(nr):
            acc = jnp.zeros((w, w), F32)
            for i in range(NA_QROWS):
                j = i + dr - (NA_KH - 1 - NA_QROWS)
                if 0 <= j < 3 * NA_QROWS:
                    acc = acc + x[i * w:(i + 1) * w, j * w:(j + 1) * w]
            rows.append(acc)
        diff = (lax.broadcasted_iota(jnp.int32, (w, w), 1) - lax.broadcasted_iota(jnp.int32, (w, w), 0)
                + NA_KW - 1)
        lane = lax.broadcasted_iota(jnp.int32, (1, LANES), 1)
        for dr in range(nr):
            line = jnp.zeros((1, LANES), F32)
            for dc in range(nc):
                val = jnp.sum(jnp.sum(jnp.where(diff == dc, rows[dr], 0.0), axis=1, keepdims=True),
                              axis=0, keepdims=True)
                line = jnp.where(lane == dc, val, line)
            o_ref[0, dr:dr + 1, :] = line

    out = pl.pallas_call(body, name=name, grid=(8,),
                         in_specs=[pl.BlockSpec((1, tq, 3 * tq), lambda h: (h, 0, 0))],
                         out_specs=pl.BlockSpec((1, nr, LANES), lambda h: (h, 0, 0)),
                         out_shape=jax.ShapeDtypeStruct((8, nr, LANES), F32),
                         compiler_params=_params(("parallel",)))(db)
    return out[:, :, :nc]


def _exchange(src, *, gather, name):
    shape = src.shape if not gather else (N_DEV,) + src.shape

    def body(src_ref, out_ref, send_sems, recv_sems, local_sem):
        _exchange_start(src_ref, out_ref, send_sems, recv_sems, local_sem, gather)
        _exchange_wait(src_ref, out_ref, send_sems, recv_sems, local_sem, gather)

    any_spec = pl.BlockSpec(memory_space=pl.ANY)
    return pl.pallas_call(
        body, name=name, in_specs=[any_spec], out_specs=any_spec, out_shape=jax.ShapeDtypeStruct(shape, src.dtype),
        scratch_shapes=_exchange_sems())(src)


class _Carried:
    def __init__(self, carry, grid):
        self.src, self.gather = carry if carry is not None else (None, False)
        self.on = self.src is not None
        self.grid = grid
        self.operands = [self.src] if self.on else []
        self.specs = [pl.BlockSpec(memory_space=pl.ANY)] if self.on else []
        self.scratch = _exchange_sems() if self.on else []
        self.out_shapes = []
        if self.on:
            shape = ((N_DEV,) + self.src.shape) if self.gather else self.src.shape
            self.out_shapes = [jax.ShapeDtypeStruct(shape, self.src.dtype)]

    def split(self, refs, n_in, n_out):
        refs = list(refs)
        if not self.on:
            return refs[:n_in], refs[n_in:n_in + n_out], refs[n_in + n_out:]
        self.refs = (refs[n_in], refs[n_in + 1 + n_out], *refs[-3:], self.gather)
        return refs[:n_in], refs[n_in + 1:n_in + 1 + n_out], refs[n_in + 2 + n_out:-3]

    def _at(self, last):
        hit = None
        for ax, size in enumerate(self.grid):
            here = pl.program_id(ax) == (size - 1 if last else 0)
            hit = here if hit is None else hit & here
        return hit

    def start(self):
        if self.on:
            pl.when(self._at(False))(lambda: _exchange_start(*self.refs))

    def wait(self):
        if self.on:
            pl.when(self._at(True))(lambda: _exchange_wait(*self.refs))


def _exchange_sems():
    return [pltpu.SemaphoreType.DMA((N_DEV - 1,)), pltpu.SemaphoreType.DMA((N_DEV - 1,)), pltpu.SemaphoreType.DMA]


def _exchange_copies(src_ref, out_ref, send_sems, recv_sems, local_sem, gather):
    x, y, c = lax.axis_index("x"), lax.axis_index("y"), lax.axis_index("c")
    me = 4 * x + 2 * y + c

    def outgoing(p):
        return src_ref if gather else src_ref.at[p]

    own = pltpu.make_async_copy(outgoing(me), out_ref.at[me], local_sem)
    sends, recvs = [], []
    for k in range(1, N_DEV):
        px = 1 - x if k & 4 else x
        py = 1 - y if k & 2 else y
        pc = 1 - c if k & 1 else c
        p = 4 * px + 2 * py + pc
        for dst, group in ((me, sends), (p, recvs)):
            group.append(pltpu.make_async_remote_copy(
                src_ref=outgoing(p), dst_ref=out_ref.at[dst], send_sem=send_sems.at[k - 1],
                recv_sem=recv_sems.at[k - 1], device_id=(px, py, pc), device_id_type=pl.DeviceIdType.MESH))
    return own, sends, recvs


def _exchange_start(*refs_and_mode):
    own, sends, _ = _exchange_copies(*refs_and_mode)
    own.start()
    for cp in sends:
        cp.start()


def _exchange_wait(*refs_and_mode):
    own, sends, recvs = _exchange_copies(*refs_and_mode)
    for cp in recvs:
        cp.wait_recv()
    for cp in sends:
        cp.wait_send()
    own.wait()


def _adamw(parts, w, m, v, *, name):
    layers = len(parts)
    rows, cols = w.shape
    per_layer = rows // layers
    t = _tile(per_layer, (FLAT_ROW_TILE, 128, 64, 32, 16, 8))
    nt = per_layer // t

    def body(*refs):
        p_refs = refs[:layers]
        w_ref, m_ref, v_ref, g_ref, d_ref, nm_ref, nv_ref = refs[layers:]
        layer = pl.program_id(0)
        g = None
        for l, p_ref in enumerate(p_refs):
            total = p_ref[0].astype(F32)
            for k in range(1, N_DEV):
                total = total + p_ref[k].astype(F32)
            g = total if g is None else jnp.where(layer == l, total, g)
        nm = ADAM_B1 * m_ref[...] + (1.0 - ADAM_B1) * g
        nv = ADAM_B2 * v_ref[...] + (1.0 - ADAM_B2) * (g * g)
        m_hat = nm / (1.0 - ADAM_B1 ** ADAM_STEP)
        v_hat = nv / (1.0 - ADAM_B2 ** ADAM_STEP)
        g_ref[...] = g
        d_ref[...] = -ADAM_LR * (m_hat / (jnp.sqrt(v_hat) + ADAM_EPS) + ADAM_WD * w_ref[...])
        nm_ref[...] = nm
        nv_ref[...] = nv

    def part_spec(l):
        return pl.BlockSpec((N_DEV, t, cols), lambda layer, i: (0, jnp.where(layer == l, i, 0), 0))

    row = pl.BlockSpec((t, cols), lambda layer, i: (layer * nt + i, 0))
    shape = jax.ShapeDtypeStruct((rows, cols), F32)
    return pl.pallas_call(body, name=name, grid=(layers, nt),
                          in_specs=[part_spec(l) for l in range(layers)] + [row, row, row],
                          out_specs=[row] * 4, out_shape=[shape] * 4,
                          compiler_params=_params(("parallel", "parallel"), VMEM_BIG))(*parts, w, m, v)


def _flatten(arrays, dtype, row_mult):
    flat = jnp.concatenate([a.reshape(-1).astype(dtype) for a in arrays])
    chunk = FLAT_COLS * row_mult
    padded = -(-flat.shape[0] // chunk) * chunk
    return jnp.pad(flat, (0, padded - flat.shape[0])).reshape(padded // FLAT_COLS, FLAT_COLS)


def _unflatten(flat, shapes):
    flat = flat.reshape(-1)
    out, pos = [], 0
    for shp in shapes:
        size = int(np.prod(shp))
        out.append(flat[pos:pos + size].reshape(shp))
        pos += size
    return out


def _gather_full(names, local, dtype, row_mult, label):
    got = _exchange(_flatten([local[n] for n in names], dtype, row_mult), gather=True, name=label)
    per_dev = [_unflatten(got[p], [local[n].shape for n in names]) for p in range(N_DEV)]
    return {n: jnp.concatenate([per_dev[p][i] for p in range(N_DEV)], axis=SHARD_AXIS[n])
            for i, n in enumerate(names)}


def _from_shards(stacked, axis):
    _, a, b = stacked.shape
    if axis == 1:
        return jnp.transpose(stacked, (1, 0, 2)).reshape(a, N_DEV * b)
    return stacked.reshape(N_DEV * a, b)


def _to_shards(full, axis):
    ra, rb = full.shape
    if axis == 1:
        return jnp.transpose(full.reshape(ra, N_DEV, rb // N_DEV), (1, 0, 2))
    return full.reshape(N_DEV, ra // N_DEV, rb)


class _ShardedMatmulWeights:
    def __init__(self, local):
        self.local, self.full, self.parts = local, {}, {}

    def gather_src(self, n, l):
        return self.local[n][l].astype(BF16)

    def set_gathered(self, n, l, got):
        self.full[n, l] = _from_shards(got, SHARD_AXIS[n] - 1)

    def get(self, n, l):
        return self.full[n, l]

    def scatter_src(self, n, l, dw):
        return _to_shards(dw, SHARD_AXIS[n] - 1).astype(BF16)

    def set_scattered(self, n, l, parts):
        self.parts[n, l] = parts


def _role(role, i):
    mixer = 'attn_w_' if i % 2 == 0 else 'conv_w_'
    return {'in': (mixer + 'in', i // 2), 'out': (mixer + 'out', i // 2),
            'up': ('ffn_w_up', i), 'down': ('ffn_w_down', i)}[role]


def _local_step(x, tgt, w, big):
    s = x.shape[0]

    def project(a, role, i, ln=None):
        carry = None
        if i + 1 < DEPTH:
            nxt = _role(role, i + 1)
            carry = (big.gather_src(*nxt), True)
        weight = big.get(*_role(role, i))
        if ln is None:
            out, got = _mm(a, weight, carry=carry, name=role + "_fwd")
        else:
            *out, got = _mm_ln(a, weight, *ln, carry=carry, name=role + "_fwd")
        if got is not None:
            big.set_gathered(*nxt, got)
        return out

    def project_back(a, d_out, role, i):
        key = _role(role, i)
        dw, _ = _mm(a, d_out, ta=True, name=role + "_dw")
        d_in, parts = _mm(d_out, big.get(*key), tb=True, carry=(big.scatter_src(*key, dw), False), name=role + "_dx")
        if parts is not None:
            big.set_scattered(*key, parts)
        return d_in

    na_tq = NA_QROWS * GRID_W
    band_idx = [_band_bucket_index(d) for _, d in DIL_PATTERNS]
    band_bias = [_band_bias(w['t5_bias'], d, name=f"band_bias_{d}") for _, d in DIL_PATTERNS]
    band_mask = [_mask_tiles("band", DIL_TQ, DIL_KW, s // d, name=f"band_mask_{d}") for _, d in DIL_PATTERNS]
    na_mask = _mask_tiles("na", na_tq, 3 * na_tq, s, name="na_mask")
    na_cols, dil_cols = (0, 4, 8), (12, 16, 20)
    grads = {n: [None] * w[n].shape[0] for n in SMALL_SHARDED + REPLICATED if n != 't5_bias'}
    saved = []
    x16 = x.astype(BF16)

    for i in range(DEPTH):
        j = i // 2
        st = {'x': x, 'x16': x16}
        if i % 2 == 0:
            h = project(x16, 'in', i)
            na_bias = _na_bias(w['na_rpb'][j], name="na_bias")
            late = [_role(role, 0) for role in ('up', 'down')] if i == 0 else []
            rides = [(key, (big.gather_src(*key), True)) for key in late] + [(None, None)] * 4
            outs, lses = [], []
            calls = [(na_bias, na_mask, 1, na_tq, NA_NSUB, na_cols, "na_fwd")] + [
                (bias, mask, d, DIL_TQ, DIL_NSUB[d], dil_cols, f"dil_fwd_{d}")
                for (_, d), bias, mask in zip(DIL_PATTERNS, band_bias, band_mask)]
            for (bias, mask, d, tq, nsub, cols, label), (key, carry) in zip(calls, rides):
                o, l, got = _attn_fwd(h, bias, mask, d=d, tq=tq, nsub=nsub, cols=cols, carry=carry, name=label)
                if got is not None:
                    big.set_gathered(*key, got)
                outs.append(o)
                lses.append(l)
            o_na, l_na = outs.pop(0), lses.pop(0)
            y_dil, l_dil = _dil_combine(outs, lses, name="dil_combine")
            mid = jnp.concatenate([o_na, y_dil], axis=1)
            mid16 = mid.astype(BF16)
            st.update(h=h, mid=mid, mid16=mid16, l_na=l_na, l_dil=l_dil, na_bias=na_bias)
        else:
            h = project(x16, 'in', i)
            mid16, u2 = _conv_mid_fwd(h, w['conf_dw_w'][j], w['conf_dw_b'][j], w['conf_ln_g'][j], w['conf_ln_b'][j],
                                      w['sconv_w'][j], name="conv_mid_fwd")
            st.update(h=h, mid16=mid16, u2=u2)
        z_mix, xa, xa16 = project(mid16, 'out', i, ln=(x, w['mix_ln_g'][i], w['mix_ln_b'][i]))
        hu = project(xa16, 'up', i)
        act16 = _ffn_mid_fwd(hu, w['ffn_dw_w'][i], name="ffn_mid_fwd")
        z_ffn, xb, xb16 = project(act16, 'down', i, ln=(xa, w['ffn_ln_g'][i], w['ffn_ln_b'][i]))
        st.update(z_mix=z_mix, xa16=xa16, hu=hu, act16=act16, z_ffn=z_ffn)
        saved.append(st)
        x, x16 = xb, xb16

    loss, d1 = _loss_head(x, tgt, name="loss_head")
    d2 = None
    g_t5 = None
    for i in reversed(range(DEPTH)):
        j = i // 2
        st = saved[i]
        dz, dz16, dg, db = _ln_bwd(st['z_ffn'], w['ffn_ln_g'][i], d1, d2, name="ffn_ln_bwd")
        grads['ffn_ln_g'][i], grads['ffn_ln_b'][i] = dg, db
        dact = project_back(st['act16'], dz16, 'down', i)
        dhu, grads['ffn_dw_w'][i] = _ffn_mid_bwd(st['hu'], w['ffn_dw_w'][i], dact, name="ffn_mid_bwd")
        dxa = project_back(st['xa16'], dhu, 'up', i)
        dz, dz1, dg, db = _ln_bwd(st['z_mix'], w['mix_ln_g'][i], dz, dxa, name="mix_ln_bwd")
        grads['mix_ln_g'][i], grads['mix_ln_b'][i] = dg, db
        dmid = project_back(st['mid16'], dz1, 'out', i)
        if i % 2 == 0:
            h = st['h']
            dq, dk, dv, dbias = _attn_bwd(h, st['na_bias'], na_mask, dmid, st['mid'], st['l_na'], d=1, tq=na_tq,
                                          nsub=NA_NSUB, cols=na_cols, ycol=0, name="na_bwd")
            grads['na_rpb'][j] = _rpb_grad(dbias, name="rpb_grad")
            dil, dbs = [], []
            for (_, d), bias, mask in zip(DIL_PATTERNS, band_bias, band_mask):
                g = _attn_bwd(h, bias, mask, dmid, st['mid'], st['l_dil'], d=d, tq=DIL_TQ, nsub=DIL_NSUB[d],
                              cols=dil_cols, ycol=4, name=f"dil_bwd_{d}")
                dil.append(g[:3])
                dbs.append(g[3])
            t5 = _t5_grad(dbs, band_idx, name="t5_grad")
            g_t5 = t5 if g_t5 is None else g_t5 + t5
            dh = _attn_dh((dq, dk, dv), dil, name="attn_dh")
        else:
            dh, dw31, db31, dlg, dlb, dw3 = _conv_mid_bwd(st['h'], st['u2'], dmid, w['conf_dw_w'][j],
                                                          w['conf_ln_g'][j], w['conf_ln_b'][j], w['sconv_w'][j],
                                                          name="conv_mid_bwd")
            grads['conf_dw_w'][j], grads['conf_dw_b'][j] = dw31, db31
            grads['conf_ln_g'][j], grads['conf_ln_b'][j], grads['sconv_w'][j] = dlg, dlb, dw3
        d1, d2 = dz, project_back(st['x16'], dh, 'in', i)
    dx = _axpy(d1, d2, name="grad_x")
    full = {n: jnp.stack(g) for n, g in grads.items()}
    full['t5_bias'] = g_t5
    return loss, dx, full


def kernel(x, t5_bias, attn_w_in, attn_w_out, na_rpb, conv_w_in, conf_dw_w, conf_dw_b, conf_ln_g, conf_ln_b, sconv_w, conv_w_out, ffn_w_up, ffn_dw_w, ffn_w_down, mix_ln_g, mix_ln_b, ffn_ln_g, ffn_ln_b, loss_target, m_t5_bias, m_attn_w_in, m_attn_w_out, m_na_rpb, m_conv_w_in, m_conf_dw_w, m_conf_dw_b, m_conf_ln_g, m_conf_ln_b, m_sconv_w, m_conv_w_out, m_ffn_w_up, m_ffn_dw_w, m_ffn_w_down, m_mix_ln_g, m_mix_ln_b, m_ffn_ln_g, m_ffn_ln_b, v_t5_bias, v_attn_w_in, v_attn_w_out, v_na_rpb, v_conv_w_in, v_conf_dw_w, v_conf_dw_b, v_conf_ln_g, v_conf_ln_b, v_sconv_w, v_conv_w_out, v_ffn_w_up, v_ffn_dw_w, v_ffn_w_down, v_mix_ln_g, v_mix_ln_b, v_ffn_ln_g, v_ffn_ln_b):
    args = dict(locals())
    local = {n: args[n] for n in WEIGHTS}
    mom1 = {n: args['m_' + n] for n in WEIGHTS}
    mom2 = {n: args['v_' + n] for n in WEIGHTS}

    kinds = ('grad', 'delta', 'new_m', 'new_v')
    small = {n: local[n] for n in REPLICATED}
    small.update(_gather_full(SMALL_SHARDED, local, F32, 8, "gather_small_weights"))
    big = _ShardedMatmulWeights({n: local[n] for n in MATMUL_WEIGHTS})
    for role in ('in', 'out'):
        n, l = _role(role, 0)
        big.set_gathered(n, l, _exchange(big.gather_src(n, l), gather=True, name="gather_first_" + role))

    loss, dx, grads = _local_step(x[0], loss_target[0], small, big)
    loss = lax.psum(loss, MESH_AXES)

    out = {}
    for n in MATMUL_WEIGHTS:
        layers, a, b = local[n].shape
        res = _adamw([big.parts[n, l] for l in range(layers)],
                     *[t.reshape(layers * a, b) for t in (local[n], mom1[n], mom2[n])], name="adamw_" + n)
        for kind, r in zip(kinds, res):
            out[kind + '_' + n] = r.reshape(layers, a, b)
    for names, sharded, label in ((SMALL_SHARDED, True, "small"), (REPLICATED, False, "replicated")):
        shapes = [local[n].shape for n in names]
        if sharded:
            per_dev = [_flatten([lax.slice_in_dim(grads[n], p * local[n].shape[SHARD_AXIS[n]],
                                                  (p + 1) * local[n].shape[SHARD_AXIS[n]], axis=SHARD_AXIS[n])
                                 for n in names], F32, 8) for p in range(N_DEV)]
            parts = _exchange(jnp.stack(per_dev), gather=False, name="scatter_small_grads")
        else:
            parts = _exchange(_flatten([grads[n] for n in names], F32, 8), gather=True, name="gather_replicated_grads")
        res = _adamw([parts], _flatten([local[n] for n in names], F32, 8), _flatten([mom1[n] for n in names], F32, 8),
                     _flatten([mom2[n] for n in names], F32, 8), name="adamw_" + label)
        for kind, flat in zip(kinds, res):
            for n, a in zip(names, _unflatten(flat, shapes)):
                out[kind + '_' + n] = a

    return (loss, dx[None], *[out[k + '_' + n] for k in ('grad', 'delta', 'new_m', 'new_v') for n in WEIGHTS])
```

```python
import functools
import math

import jax
import jax.numpy as jnp
import numpy as np
from jax import lax
from jax.experimental import pallas as pl
from jax.experimental.pallas import tpu as pltpu

F32 = jnp.float32
BF16 = jnp.bfloat16

N_DEV = 8
MESH_AXES = ("x", "y", "c")
DEPTH = 4
GRID_W = 64
GRID_SHIFT = 6
HEAD_DIM = 64
NA_KH = 8
NA_KW = 16
NA_QROWS = 4
DIL_PATTERNS = ((128, 1), (512, 4), (2048, 16))
DIL_HALF = 64
DIL_TQ = 128
DIL_KW = DIL_TQ + 2 * DIL_HALF
DIL_NSUB = {1: 8, 4: 2, 16: 1}
NA_NSUB = 4
N_BUCKETS = 32
T5_MAX_DIST = 1024
CONF_CH = 512
CONF_K = 31
SC_K = 3
FFN_K = 3
FFN_ROW_TILE = 128
LN_EPS = 1e-5
NEG = -1e30
ALPHA = (2 * DEPTH) ** 0.25
ADAM_LR = 0.001
ADAM_B1 = 0.9
ADAM_B2 = 0.999
ADAM_EPS = 1e-08
ADAM_WD = 0.01
ADAM_STEP = 10

LANES = 128
SUBLANES = 8
VMEM_BIG = 48 * 1024 * 1024
FLAT_COLS = 1024
FLAT_ROW_TILE = 256

WEIGHTS = ['t5_bias', 'attn_w_in', 'attn_w_out', 'na_rpb', 'conv_w_in', 'conf_dw_w', 'conf_dw_b', 'conf_ln_g',
           'conf_ln_b', 'sconv_w', 'conv_w_out', 'ffn_w_up', 'ffn_dw_w', 'ffn_w_down', 'mix_ln_g', 'mix_ln_b',
           'ffn_ln_g', 'ffn_ln_b']
SHARD_AXIS = {'attn_w_in': 2, 'attn_w_out': 1, 'conv_w_in': 2, 'conf_dw_w': 2, 'conf_dw_b': 1, 'conf_ln_g': 1,
              'conf_ln_b': 1, 'sconv_w': 2, 'conv_w_out': 1, 'ffn_w_up': 2, 'ffn_dw_w': 2, 'ffn_w_down': 1}
MATMUL_WEIGHTS = ['attn_w_in', 'attn_w_out', 'conv_w_in', 'conv_w_out', 'ffn_w_up', 'ffn_w_down']
SMALL_SHARDED = ['conf_dw_w', 'conf_dw_b', 'conf_ln_g', 'conf_ln_b', 'sconv_w', 'ffn_dw_w']
SHARDED = MATMUL_WEIGHTS + SMALL_SHARDED
REPLICATED = ['t5_bias', 'na_rpb', 'mix_ln_g', 'mix_ln_b', 'ffn_ln_g', 'ffn_ln_b']


def _tile(n, cands):
    for c in cands:
        if n % c == 0:
            return c
    return n


def _params(sem, vmem=None):
    return pltpu.CompilerParams(dimension_semantics=sem, vmem_limit_bytes=vmem)


def _sigmoid(x):
    return 0.5 * jnp.tanh(0.5 * x) + 0.5


MM_MAX_TILE = 1408
MM_MAX_K = 3072


def _lane_tile(n, cap):
    best = None
    for t in range(LANES, min(n, cap) + 1, LANES):
        if n % t == 0:
            best = t
    return best or n


def _mm(a, b, *, ta=False, tb=False, out_dtype=F32, carry=None, name):
    assert a.dtype == BF16 and b.dtype == BF16, (name, a.dtype, b.dtype)
    m, k = (a.shape[1], a.shape[0]) if ta else a.shape
    n = b.shape[0] if tb else b.shape[1]
    tm, tn, tk = _lane_tile(m, MM_MAX_TILE), _lane_tile(n, MM_MAX_TILE), _lane_tile(k, MM_MAX_K)
    grid = (m // tm, n // tn, k // tk)
    nk = grid[2]
    dims = (((0 if ta else 1,), (1 if tb else 0,)), ((), ()))
    use_acc = nk > 1 and out_dtype != F32
    ride = _Carried(carry, grid)

    def body(*refs):
        (a_ref, b_ref), (o_ref,), scratch = ride.split(refs, 2, 1)
        ride.start()
        part = lax.dot_general(a_ref[...], b_ref[...], dims, preferred_element_type=F32)
        if nk == 1:
            o_ref[...] = part.astype(out_dtype)
        else:
            acc_ref = scratch[0] if use_acc else o_ref
            kk = pl.program_id(2)

            @pl.when(kk == 0)
            def _():
                acc_ref[...] = part

            @pl.when(kk > 0)
            def _():
                acc_ref[...] += part

            if use_acc:
                @pl.when(kk == nk - 1)
                def _():
                    o_ref[...] = acc_ref[...].astype(out_dtype)

        ride.wait()

    a_spec = pl.BlockSpec((tk, tm), lambda i, j, q: (q, i)) if ta else pl.BlockSpec((tm, tk), lambda i, j, q: (i, q))
    b_spec = pl.BlockSpec((tn, tk), lambda i, j, q: (j, q)) if tb else pl.BlockSpec((tk, tn), lambda i, j, q: (q, j))
    o_spec = pl.BlockSpec((tm, tn), lambda i, j, q: (i, j))
    o_shape = jax.ShapeDtypeStruct((m, n), out_dtype)
    scratch = [pltpu.VMEM((tm, tn), F32)] if use_acc else []
    sem = ("arbitrary",) * 3 if ride.on else ("parallel", "parallel", "arbitrary")
    out = pl.pallas_call(
        body, name=name, grid=grid, in_specs=[a_spec, b_spec] + ride.specs, out_specs=[o_spec] + ride.specs,
        out_shape=[o_shape] + ride.out_shapes, scratch_shapes=scratch + ride.scratch,
        compiler_params=_params(sem, VMEM_BIG))(a, b, *ride.operands)
    return out[0], (out[1] if ride.on else None)


MM_LN_ROWS = 512


def _mm_ln(a, b, x, g, beta, *, carry=None, name):
    assert a.dtype == BF16 and b.dtype == BF16, (name, a.dtype, b.dtype)
    m, k = a.shape
    n = b.shape[1]
    assert k <= MM_MAX_K, (name, k)
    tm = _tile(m, (MM_LN_ROWS, 256, 128, 64, 8))
    grid = (m // tm,)
    ride = _Carried(carry, grid)

    def body(*refs):
        (a_ref, b_ref, x_ref, g_ref, beta_ref), (z_ref, o_ref, o16_ref), _ = ride.split(refs, 5, 3)
        ride.start()
        z = ALPHA * x_ref[...] + jnp.dot(a_ref[...], b_ref[...], preferred_element_type=F32)
        z_ref[...] = z
        xh, _ = _ln_stats(z)
        out = xh * g_ref[...] + beta_ref[...]
        o_ref[...] = out
        o16_ref[...] = out.astype(BF16)
        ride.wait()

    row = lambda width: pl.BlockSpec((tm, width), lambda i: (i, 0))
    vec = pl.BlockSpec((1, n), lambda i: (0, 0))
    out = pl.pallas_call(
        body, name=name, grid=grid,
        in_specs=[row(k), pl.BlockSpec((k, n), lambda i: (0, 0)), row(n), vec, vec] + ride.specs,
        out_specs=[row(n)] * 3 + ride.specs,
        out_shape=[jax.ShapeDtypeStruct((m, n), F32), jax.ShapeDtypeStruct((m, n), F32),
                   jax.ShapeDtypeStruct((m, n), BF16)] + ride.out_shapes,
        scratch_shapes=ride.scratch,
        compiler_params=_params(("arbitrary",) if ride.on else ("parallel",), VMEM_BIG))(
            a, b, x, g.reshape(1, n), beta.reshape(1, n), *ride.operands)
    return out[0], out[1], out[2], (out[3] if ride.on else None)


def _mm_ln_bwd(d_out, b, z, g, d1, *, carry=None, name):
    assert d_out.dtype == BF16 and b.dtype == BF16, (name, d_out.dtype, b.dtype)
    m, k = d_out.shape
    n = b.shape[0]
    tm, tk = _tile(m, (MM_LN_ROWS, 256, 128, 64, 8)), _lane_tile(k, MM_MAX_K)
    grid = (m // tm, k // tk)
    nk = grid[1]
    ride = _Carried(carry, grid)

    def body(*refs):
        (a_ref, b_ref, z_ref, g_ref, d1_ref), (dz_ref, dz16_ref, dg_ref, db_ref), scratch = ride.split(refs, 5, 4)
        ride.start()
        i, kk = pl.program_id(0), pl.program_id(1)
        part = lax.dot_general(a_ref[...], b_ref[...], (((1,), (1,)), ((), ())), preferred_element_type=F32)

        def finish(d2):
            @pl.when(i == 0)
            def _():
                dg_ref[...] = jnp.zeros_like(dg_ref)
                db_ref[...] = jnp.zeros_like(db_ref)

            dout = ALPHA * d1_ref[...] + d2
            xh, rstd = _ln_stats(z_ref[...])
            dxh = dout * g_ref[...]
            dz = rstd * (dxh - jnp.mean(dxh, axis=-1, keepdims=True) - xh * jnp.mean(dxh * xh, axis=-1, keepdims=True))
            dz_ref[...] = dz
            dz16_ref[...] = dz.astype(BF16)
            dg_ref[...] += jnp.sum(dout * xh, axis=0, keepdims=True)
            db_ref[...] += jnp.sum(dout, axis=0, keepdims=True)

        if nk == 1:
            finish(part)
        else:
            acc_ref = scratch[0]

            @pl.when(kk == 0)
            def _():
                acc_ref[...] = part

            @pl.when((kk > 0) & (kk < nk - 1))
            def _():
                acc_ref[...] += part

            @pl.when(kk == nk - 1)
            def _():
                finish(acc_ref[...] + part)

        ride.wait()

    row = pl.BlockSpec((tm, n), lambda i, q: (i, 0))
    vec = pl.BlockSpec((1, n), lambda i, q: (0, 0))
    out = pl.pallas_call(
        body, name=name, grid=grid,
        in_specs=[pl.BlockSpec((tm, tk), lambda i, q: (i, q)), pl.BlockSpec((n, tk), lambda i, q: (0, q)),
                  row, vec, row] + ride.specs,
        out_specs=[row, row, vec, vec] + ride.specs,
        out_shape=[jax.ShapeDtypeStruct((m, n), F32), jax.ShapeDtypeStruct((m, n), BF16),
                   jax.ShapeDtypeStruct((1, n), F32), jax.ShapeDtypeStruct((1, n), F32)] + ride.out_shapes,
        scratch_shapes=([pltpu.VMEM((tm, n), F32)] if nk > 1 else []) + ride.scratch,
        compiler_params=_params(("arbitrary", "arbitrary"), VMEM_BIG))(
            d_out, b, z, g.reshape(1, n), d1, *ride.operands)
    return out[0], out[1], out[2].reshape(n), out[3].reshape(n), (out[4] if ride.on else None)


def _ln_stats(z):
    mu = jnp.mean(z, axis=-1, keepdims=True)
    zc = z - mu
    var = jnp.mean(zc * zc, axis=-1, keepdims=True)
    rstd = lax.rsqrt(var + LN_EPS)
    return zc * rstd, rstd


def _ln_bwd(z, g, d1, d2, *, name):
    s, d = z.shape
    t = _tile(s, (256, 128, 64, 8))
    two = d2 is not None

    def body(*refs):
        if two:
            z_ref, g_ref, d1_ref, d2_ref, dz_ref, dz16_ref, dg_ref, db_ref = refs
            dout = ALPHA * d1_ref[...] + d2_ref[...]
        else:
            z_ref, g_ref, d1_ref, dz_ref, dz16_ref, dg_ref, db_ref = refs
            dout = d1_ref[...]

        @pl.when(pl.program_id(0) == 0)
        def _():
            dg_ref[...] = jnp.zeros_like(dg_ref)
            db_ref[...] = jnp.zeros_like(db_ref)

        xh, rstd = _ln_stats(z_ref[...])
        dxh = dout * g_ref[...]
        dz = rstd * (dxh - jnp.mean(dxh, axis=-1, keepdims=True) - xh * jnp.mean(dxh * xh, axis=-1, keepdims=True))
        dz_ref[...] = dz
        dz16_ref[...] = dz.astype(BF16)
        dg_ref[...] += jnp.sum(dout * xh, axis=0, keepdims=True)
        db_ref[...] += jnp.sum(dout, axis=0, keepdims=True)

    row = pl.BlockSpec((t, d), lambda i: (i, 0))
    vec = pl.BlockSpec((1, d), lambda i: (0, 0))
    ins = [z, g.reshape(1, d), d1] + ([d2] if two else [])
    specs = [row, vec, row] + ([row] if two else [])
    dz, dz16, dg, db = pl.pallas_call(
        body, name=name, grid=(s // t,), in_specs=specs, out_specs=[row, row, vec, vec],
        out_shape=[jax.ShapeDtypeStruct((s, d), F32), jax.ShapeDtypeStruct((s, d), BF16),
                   jax.ShapeDtypeStruct((1, d), F32), jax.ShapeDtypeStruct((1, d), F32)],
        compiler_params=_params(("arbitrary",)))(*ins)
    return dz, dz16, dg.reshape(d), db.reshape(d)


def _axpy(d1, d2, *, name):
    s, d = d1.shape
    t = _tile(s, (256, 128, 64, 8))

    def body(a_ref, b_ref, o_ref):
        o_ref[...] = ALPHA * a_ref[...] + b_ref[...]

    row = pl.BlockSpec((t, d), lambda i: (i, 0))
    return pl.pallas_call(body, name=name, grid=(s // t,), in_specs=[row, row], out_specs=row,
                          out_shape=jax.ShapeDtypeStruct((s, d), F32), compiler_params=_params(("parallel",)))(d1, d2)


def _loss_head(y, tgt, *, name):
    s, d = y.shape
    t = _tile(s, (256, 128, 64, 8))

    def body(y_ref, t_ref, l_ref, dy_ref):
        @pl.when(pl.program_id(0) == 0)
        def _():
            l_ref[...] = jnp.zeros_like(l_ref)

        err = y_ref[...] - t_ref[...]
        dy_ref[...] = err * (1.0 / d)
        l_ref[...] += 0.5 * jnp.sum(jnp.sum(err * err, axis=1, keepdims=True), axis=0, keepdims=True) * (1.0 / d)

    row = pl.BlockSpec((t, d), lambda i: (i, 0))
    one = pl.BlockSpec((SUBLANES, LANES), lambda i: (0, 0))
    loss, dy = pl.pallas_call(
        body, name=name, grid=(s // t,), in_specs=[row, row], out_specs=[one, row],
        out_shape=[jax.ShapeDtypeStruct((SUBLANES, LANES), F32), jax.ShapeDtypeStruct((s, d), F32)],
        compiler_params=_params(("arbitrary",)))(y, tgt)
    return loss[0, 0], dy


def _halo_specs(s, t, halo, cb, col):
    per = t // halo
    last = s // halo - 1
    return [pl.BlockSpec((t, cb), lambda j, i: (i, col(j))),
            pl.BlockSpec((halo, cb), lambda j, i: (jnp.maximum(i * per - 1, 0), col(j))),
            pl.BlockSpec((halo, cb), lambda j, i: (jnp.minimum((i + 1) * per, last), col(j)))]


def _extended(main_ref, prev_ref, next_ref, i, n):
    prev = jnp.where(i > 0, prev_ref[...], 0.0)
    nxt = jnp.where(i < n - 1, next_ref[...], 0.0)
    return jnp.concatenate([prev, main_ref[...], nxt], axis=0)


def _shift(ext, o):
    if o == 0:
        return ext
    return pltpu.roll(ext, (-o) % ext.shape[0], 0)


def _taps(ext, k, sign=1):
    return [_shift(ext, sign * (j - k // 2)) for j in range(k)]


def _conv(ext, w_ref, k, sign=1, taps=None):
    taps = _taps(ext, k, sign) if taps is None else taps
    acc = None
    for j in range(k):
        term = w_ref[j:j + 1, :] * taps[j]
        acc = term if acc is None else acc + term
    return acc


def _main_taps(ext, k, halo, t, sign=1):
    rolled, taps = {}, []
    for j in range(k):
        offset = sign * (j - k // 2)
        res = offset % SUBLANES
        if res not in rolled:
            rolled[res] = _shift(ext, res)
        start = halo + offset - res
        taps.append(rolled[res][start:start + t])
    return taps


def _conv_main(taps, w_ref):
    acc = None
    for j, tap in enumerate(taps):
        term = w_ref[j:j + 1, :] * tap
        acc = term if acc is None else acc + term
    return acc


def _wgrad_main(dw_ref, d_main, taps):
    for j, tap in enumerate(taps):
        dw_ref[j:j + 1, :] += jnp.sum(d_main * tap, axis=0, keepdims=True)


def _conv_wgrad(dw_ref, d_main, x_ext, k, halo, t, taps=None):
    taps = _taps(x_ext, k) if taps is None else taps
    for j in range(k):
        dw_ref[j:j + 1, :] += jnp.sum(d_main * taps[j][halo:halo + t], axis=0, keepdims=True)


def _ffn_mid_fwd(hu, w, *, name):
    s, f2 = hu.shape
    f = f2 // 2
    t, cb, halo = _tile(s, (FFN_ROW_TILE,)), _lane_tile(f, MM_MAX_TILE), SUBLANES
    nt, nc = s // t, f // cb

    def body(g_ref, gp_ref, gn_ref, u_ref, up_ref, un_ref, wg_ref, wu_ref, a_ref):
        i = pl.program_id(1)
        hg = _conv_main(_main_taps(_extended(g_ref, gp_ref, gn_ref, i, nt), FFN_K, halo, t), wg_ref)
        hu_ = _conv_main(_main_taps(_extended(u_ref, up_ref, un_ref, i, nt), FFN_K, halo, t), wu_ref)
        a_ref[...] = (hg * _sigmoid(hg) * hu_).astype(BF16)

    specs = (_halo_specs(s, t, halo, cb, lambda j: j) + _halo_specs(s, t, halo, cb, lambda j: j + nc)
             + [pl.BlockSpec((FFN_K, cb), lambda j, i: (0, j)), pl.BlockSpec((FFN_K, cb), lambda j, i: (0, j + nc))])
    return pl.pallas_call(body, name=name, grid=(nc, nt), in_specs=specs,
                          out_specs=pl.BlockSpec((t, cb), lambda j, i: (i, j)),
                          out_shape=jax.ShapeDtypeStruct((s, f), BF16),
                          compiler_params=_params(("parallel", "parallel"), VMEM_BIG))(hu, hu, hu, hu, hu, hu, w, w)


def _ffn_mid_bwd(hu, w, da, *, name):
    s, f2 = hu.shape
    f = f2 // 2
    t, cb, halo = _tile(s, (FFN_ROW_TILE,)), _lane_tile(f, MM_MAX_TILE), SUBLANES
    nt, nc = s // t, f // cb

    def body(g_ref, gp_ref, gn_ref, u_ref, up_ref, un_ref, a_ref, ap_ref, an_ref, wg_ref, wu_ref,
             dg_ref, du_ref, dwg_ref, dwu_ref):
        i = pl.program_id(1)

        @pl.when(i == 0)
        def _():
            dwg_ref[...] = jnp.zeros_like(dwg_ref)
            dwu_ref[...] = jnp.zeros_like(dwu_ref)

        xg = _extended(g_ref, gp_ref, gn_ref, i, nt)
        xu = _extended(u_ref, up_ref, un_ref, i, nt)
        dae = _extended(a_ref, ap_ref, an_ref, i, nt)
        xg_taps, xu_taps = _taps(xg, FFN_K), _taps(xu, FFN_K)
        hg = _conv(xg, wg_ref, FFN_K, taps=xg_taps)
        hu_ = _conv(xu, wu_ref, FFN_K, taps=xu_taps)
        sg = _sigmoid(hg)
        d_hg = dae * hu_ * (sg * (1.0 + hg * (1.0 - sg)))
        d_hu = dae * (hg * sg)
        dg_ref[...] = _conv(d_hg, wg_ref, FFN_K, sign=-1)[halo:halo + t].astype(BF16)
        du_ref[...] = _conv(d_hu, wu_ref, FFN_K, sign=-1)[halo:halo + t].astype(BF16)
        _conv_wgrad(dwg_ref, d_hg[halo:halo + t], xg, FFN_K, halo, t, taps=xg_taps)
        _conv_wgrad(dwu_ref, d_hu[halo:halo + t], xu, FFN_K, halo, t, taps=xu_taps)

    wspec = lambda off: pl.BlockSpec((FFN_K, cb), lambda j, i: (0, j + off))
    specs = (_halo_specs(s, t, halo, cb, lambda j: j) + _halo_specs(s, t, halo, cb, lambda j: j + nc)
             + _halo_specs(s, t, halo, cb, lambda j: j) + [wspec(0), wspec(nc)])
    tile = pl.BlockSpec((t, cb), lambda j, i: (i, j))
    dg, du, dwg, dwu = pl.pallas_call(
        body, name=name, grid=(nc, nt), in_specs=specs, out_specs=[tile, tile, wspec(0), wspec(0)],
        out_shape=[jax.ShapeDtypeStruct((s, f), BF16), jax.ShapeDtypeStruct((s, f), BF16),
                   jax.ShapeDtypeStruct((FFN_K, f), F32), jax.ShapeDtypeStruct((FFN_K, f), F32)],
        compiler_params=_params(("parallel", "arbitrary"), VMEM_BIG))(hu, hu, hu, hu, hu, hu, da, da, da, w, w)
    return jnp.concatenate([dg, du], axis=1), jnp.concatenate([dwg, dwu], axis=1)


CONV_HALO = 16


def _conv_mid_fwd(h, dw_w, dw_b, ln_g, ln_b, sc_w, *, name):
    s = h.shape[0]
    c = CONF_CH
    t, halo = _tile(s, (256, 128)), CONV_HALO
    nt = s // t

    def body(ca, cap, can, cg, cgp, cgn, gb, gc, gcp, gcn, hx, hxp, hxn, w31, b31, lg, lb, w3, o_ref, u2_ref):
        i = pl.program_id(1)
        u1 = _extended(ca, cap, can, i, nt) * _sigmoid(_extended(cg, cgp, cgn, i, nt))
        u2 = _conv_main(_main_taps(u1, CONF_K, halo, t), w31) + b31[...]
        u2_ref[...] = u2
        xh, _ = _ln_stats(u2)
        yl = xh * lg[...] + lb[...]
        o_ref[:, 0:c] = (yl * _sigmoid(yl)).astype(BF16)
        p = _extended(gc, gcp, gcn, i, nt) * _extended(hx, hxp, hxn, i, nt)
        o_ref[:, c:2 * c] = (gb[...] * _conv_main(_main_taps(p, SC_K, halo, t), w3)).astype(BF16)

    hs = lambda blk: _halo_specs(s, t, halo, c, lambda j: blk)
    vec = lambda r: pl.BlockSpec((r, c), lambda j, i: (0, 0))
    specs = hs(0) + hs(1) + hs(2)[:1] + hs(3) + hs(4) + [vec(CONF_K), vec(1), vec(1), vec(1), vec(SC_K)]
    return pl.pallas_call(
        body, name=name, grid=(1, nt), in_specs=specs,
        out_specs=[pl.BlockSpec((t, 2 * c), lambda j, i: (i, 0)), pl.BlockSpec((t, c), lambda j, i: (i, 0))],
        out_shape=[jax.ShapeDtypeStruct((s, 2 * c), BF16), jax.ShapeDtypeStruct((s, c), F32)],
        compiler_params=_params(("parallel", "parallel")))(
            h, h, h, h, h, h, h, h, h, h, h, h, h, dw_w, dw_b.reshape(1, c), ln_g.reshape(1, c),
            ln_b.reshape(1, c), sc_w)


def _conv_mid_bwd(h, u2, dm, dw_w, ln_g, ln_b, sc_w, *, name):
    s = h.shape[0]
    c = CONF_CH
    t, halo = _tile(s, (256, 128)), CONV_HALO
    nt = s // t

    def body(ca, cap, can, cg, cgp, cgn, gb, gbp, gbn, gc, gcp, gcn, hx, hxp, hxn, u2r, u2p, u2n,
             du, dup, dun, dz, dzp, dzn, w31, lg, lb, w3,
             dh_ref, dw31_ref, db31_ref, dlg_ref, dlb_ref, dw3_ref):
        i = pl.program_id(1)

        @pl.when(i == 0)
        def _():
            for r in (dw31_ref, db31_ref, dlg_ref, dlb_ref, dw3_ref):
                r[...] = jnp.zeros_like(r)

        main = slice(halo, halo + t)
        xh, rstd = _ln_stats(_extended(u2r, u2p, u2n, i, nt))
        yl = xh * lg[...] + lb[...]
        sg = _sigmoid(yl)
        d_yl = _extended(du, dup, dun, i, nt) * (sg * (1.0 + yl * (1.0 - sg)))
        dlg_ref[...] += jnp.sum((d_yl * xh)[main], axis=0, keepdims=True)
        dlb_ref[...] += jnp.sum(d_yl[main], axis=0, keepdims=True)
        dxh = d_yl * lg[...]
        du2 = rstd * (dxh - jnp.mean(dxh, axis=-1, keepdims=True) - xh * jnp.mean(dxh * xh, axis=-1, keepdims=True))
        db31_ref[...] += jnp.sum(du2[main], axis=0, keepdims=True)
        cae = _extended(ca, cap, can, i, nt)
        sc = _sigmoid(_extended(cg, cgp, cgn, i, nt))
        u1 = cae * sc
        _wgrad_main(dw31_ref, du2[main], _main_taps(u1, CONF_K, halo, t))
        du1 = _conv_main(_main_taps(du2, CONF_K, halo, t, sign=-1), w31)
        dh_ref[:, 0:c] = (du1 * sc[main]).astype(BF16)
        dh_ref[:, c:2 * c] = (du1 * (cae * sc * (1.0 - sc))[main]).astype(BF16)
        gce = _extended(gc, gcp, gcn, i, nt)
        hxe = _extended(hx, hxp, hxn, i, nt)
        p = gce * hxe
        dze = _extended(dz, dzp, dzn, i, nt)
        d_c3 = dze * _extended(gb, gbp, gbn, i, nt)
        p_taps = _main_taps(p, SC_K, halo, t)
        dh_ref[:, 2 * c:3 * c] = (dze[main] * _conv_main(p_taps, w3)).astype(BF16)
        _wgrad_main(dw3_ref, d_c3[main], p_taps)
        dp = _conv_main(_main_taps(d_c3, SC_K, halo, t, sign=-1), w3)
        dh_ref[:, 3 * c:4 * c] = (dp * hxe[main]).astype(BF16)
        dh_ref[:, 4 * c:5 * c] = (dp * gce[main]).astype(BF16)

    hs = lambda blk: _halo_specs(s, t, halo, c, lambda j: blk)
    vec = lambda r: pl.BlockSpec((r, c), lambda j, i: (0, 0))
    specs = (hs(0) + hs(1) + hs(2) + hs(3) + hs(4) + hs(0) + hs(0) + hs(1)
             + [vec(CONF_K), vec(1), vec(1), vec(SC_K)])
    outs = pl.pallas_call(
        body, name=name, grid=(1, nt), in_specs=specs,
        out_specs=[pl.BlockSpec((t, 5 * c), lambda j, i: (i, 0)), vec(CONF_K), vec(1), vec(1), vec(1), vec(SC_K)],
        out_shape=[jax.ShapeDtypeStruct((s, 5 * c), BF16), jax.ShapeDtypeStruct((CONF_K, c), F32),
                   jax.ShapeDtypeStruct((1, c), F32), jax.ShapeDtypeStruct((1, c), F32),
                   jax.ShapeDtypeStruct((1, c), F32), jax.ShapeDtypeStruct((SC_K, c), F32)],
        compiler_params=_params(("arbitrary", "arbitrary")))(
            h, h, h, h, h, h, h, h, h, h, h, h, h, h, h, u2, u2, u2, dm, dm, dm, dm, dm, dm,
            dw_w, ln_g.reshape(1, c), ln_b.reshape(1, c), sc_w)
    dh, dw31, db31, dlg, dlb, dw3 = outs
    return dh, dw31, db31.reshape(c), dlg.reshape(c), dlb.reshape(c), dw3


def _attn_mask(kind, n, tq, kw, length):
    pad = (kw - tq) // 2
    iq = lax.broadcasted_iota(jnp.int32, (tq, 1), 0)
    ik = lax.broadcasted_iota(jnp.int32, (1, kw), 1)
    if kind == "band":
        rel = ik - pad - iq
        kpos = n * tq - pad + ik
        return (jnp.abs(rel) <= DIL_HALF) & (kpos >= 0) & (kpos < length)
    rows = length // GRID_W
    rq = n * NA_QROWS + (iq >> GRID_SHIFT)
    cq = iq & (GRID_W - 1)
    rk = n * NA_QROWS - pad // GRID_W + (ik >> GRID_SHIFT)
    ck = ik & (GRID_W - 1)
    r0 = jnp.clip(rq - NA_KH // 2, 0, rows - NA_KH)
    c0 = jnp.clip(cq - NA_KW // 2, 0, GRID_W - NA_KW)
    return (rk >= r0) & (rk < r0 + NA_KH) & (ck >= c0) & (ck < c0 + NA_KW)


def _mask_tiles(kind, tq, kw, length, *, name):
    nb = length // tq

    def body(o_ref):
        v = pl.program_id(0)
        n = jnp.where((v == 1) | (v == 3), 0, jnp.where(v == 2, nb - 1, 1))
        o_ref[0] = jnp.where(_attn_mask(kind, n, tq, kw, length), 0.0, NEG)

    return pl.pallas_call(body, name=name, grid=(4,), out_specs=pl.BlockSpec((1, tq, kw), lambda v: (v, 0, 0)),
                          out_shape=jax.ShapeDtypeStruct((4, tq, kw), F32),
                          compiler_params=_params(("parallel",)))()


class _AttnGeom:
    def __init__(self, s, d, tq, nsub):
        self.s, self.d, self.tq, self.nsub = s, d, tq, nsub
        self.halo = tq * d
        self.rows = nsub * self.halo
        self.nbig = s // self.rows
        self.ext = self.rows + 2 * self.halo
        assert s % self.rows == 0

    def main(self, col):
        return pl.BlockSpec((self.rows, LANES), lambda hp, n: (n, col + hp))

    def with_halos(self, col):
        last = self.s // self.halo - 1
        return [self.main(col),
                pl.BlockSpec((self.halo, LANES), lambda hp, n: (jnp.maximum(n * self.nsub - 1, 0), col + hp)),
                pl.BlockSpec((self.halo, LANES), lambda hp, n: (jnp.minimum((n + 1) * self.nsub, last), col + hp))]

    def fill_ext(self, ext_ref, main_ref, prev_ref, next_ref):
        ext_ref[0:self.halo] = prev_ref[...].astype(F32)
        ext_ref[self.halo:self.halo + self.rows] = main_ref[...].astype(F32)
        ext_ref[self.halo + self.rows:self.ext] = next_ref[...].astype(F32)

    def rows_of(self, r, pos, count):
        start = r + pos * self.d
        return pl.ds(start, count, stride=self.d) if self.d > 1 else pl.ds(start, count)

    def variant(self, n, sub):
        v = 0
        if sub == 0:
            v = v + jnp.where(n == 0, 1, 0)
        if sub == self.nsub - 1:
            v = v + jnp.where(n == self.nbig - 1, 2, 0)
        return v


def _attn_fwd(h, bias, mask, *, d, tq, nsub, cols, carry=None, name):
    geo = _AttnGeom(h.shape[0], d, tq, nsub)
    scale = HEAD_DIM ** -0.5
    kw = bias.shape[2]
    first_key = tq - (kw - tq) // 2

    ride = _Carried(carry, (4, geo.nbig))

    def body(*refs):
        (q_ref, km, kp, kn, vm, vp, vn, b_ref, m_ref), (o_ref, l_ref), (kext, vext) = ride.split(refs, 9, 2)
        ride.start()
        n = pl.program_id(1)
        geo.fill_ext(kext, km, kp, kn)
        geo.fill_ext(vext, vm, vp, vn)
        low = lax.broadcasted_iota(jnp.int32, (1, LANES), 1) < HEAD_DIM
        for r in range(d):
            for sub in range(nsub):
                madd = m_ref[geo.variant(n, sub)]
                q = q_ref[geo.rows_of(r, sub * tq, tq), :].astype(F32) * scale
                ks = kext[geo.rows_of(r, sub * tq + first_key, kw), :].astype(BF16)
                vs = vext[geo.rows_of(r, sub * tq + first_key, kw), :].astype(BF16)
                outs, lses = [], []
                for hh in range(2):
                    qh = jnp.where(low if hh == 0 else ~low, q, 0.0).astype(BF16)
                    sc = (lax.dot_general(qh, ks, (((1,), (1,)), ((), ())), preferred_element_type=F32)
                          + b_ref[hh] + madd)
                    m = jnp.max(sc, axis=1, keepdims=True)
                    p = jnp.exp(sc - m)
                    den = jnp.sum(p, axis=1, keepdims=True)
                    outs.append(jnp.dot((p / den).astype(BF16), vs, preferred_element_type=F32))
                    lses.append(m + jnp.log(den))
                o_ref[geo.rows_of(r, sub * tq, tq), :] = jnp.where(low, outs[0], outs[1])
                l_ref[geo.rows_of(r, sub * tq, tq), :] = jnp.where(low, lses[0], lses[1])
        ride.wait()

    qc, kc, vc = cols
    specs = ([geo.main(qc)] + geo.with_halos(kc) + geo.with_halos(vc)
             + [pl.BlockSpec((2, tq, kw), lambda hp, n: (hp, 0, 0)),
                pl.BlockSpec((4, tq, kw), lambda hp, n: (0, 0, 0))])
    shape = jax.ShapeDtypeStruct((geo.s, 4 * LANES), F32)
    ext = pltpu.VMEM((geo.ext, LANES), F32)
    out = pl.pallas_call(
        body, name=name, grid=(4, geo.nbig), in_specs=specs + ride.specs,
        out_specs=[geo.main(0), geo.main(0)] + ride.specs, out_shape=[shape, shape] + ride.out_shapes,
        scratch_shapes=[ext, ext] + ride.scratch,
        compiler_params=_params(("arbitrary", "arbitrary") if ride.on else ("parallel", "parallel"), VMEM_BIG))(
            h, h, h, h, h, h, h, bias, mask, *ride.operands)
    return out[0], out[1], (out[2] if ride.on else None)


def _attn_bwd(h, bias, mask, dy, y, lse, *, d, tq, nsub, cols, ycol, name):
    geo = _AttnGeom(h.shape[0], d, tq, nsub)
    scale = HEAD_DIM ** -0.5
    halo, rows = geo.halo, geo.rows
    kw = bias.shape[2]
    first_key = tq - (kw - tq) // 2

    def body(q_ref, km, kp, kn, vm, vp, vn, b_ref, m_ref, dy_ref, y_ref, l_ref, dq_ref, dk_hbm, dv_hbm, db_ref,
             kext, vext, dkext, dvext, dk_all, dv_all, sems):
        hp, n = pl.program_id(0), pl.program_id(1)

        @pl.when(n == 0)
        def _():
            dk_all[...] = jnp.zeros_like(dk_all)
            dv_all[...] = jnp.zeros_like(dv_all)
            db_ref[...] = jnp.zeros_like(db_ref)

        geo.fill_ext(kext, km, kp, kn)
        geo.fill_ext(vext, vm, vp, vn)
        dkext[...] = jnp.zeros_like(dkext)
        dvext[...] = jnp.zeros_like(dvext)
        low = lax.broadcasted_iota(jnp.int32, (1, LANES), 1) < HEAD_DIM
        for r in range(d):
            for sub in range(nsub):
                madd = m_ref[geo.variant(n, sub)]
                mine = geo.rows_of(r, sub * tq, tq)
                keys = geo.rows_of(r, sub * tq + first_key, kw)
                q = q_ref[mine, :].astype(F32) * scale
                ks = kext[keys, :].astype(BF16)
                vs = vext[keys, :].astype(BF16)
                dyv = dy_ref[mine, :]
                dyy = dyv * y_ref[mine, :]
                lse_all = l_ref[mine, :]
                dq = jnp.zeros((tq, LANES), F32)
                dk = jnp.zeros((kw, LANES), F32)
                dv = jnp.zeros((kw, LANES), F32)
                for hh in range(2):
                    sel = low if hh == 0 else ~low
                    qh = jnp.where(sel, q, 0.0).astype(BF16)
                    dyh = jnp.where(sel, dyv, 0.0).astype(BF16)
                    dsum = jnp.sum(jnp.where(sel, dyy, 0.0), axis=1, keepdims=True)
                    lse_h = lse_all[:, hh * HEAD_DIM:hh * HEAD_DIM + 1]
                    sc = (lax.dot_general(qh, ks, (((1,), (1,)), ((), ())), preferred_element_type=F32)
                          + b_ref[hh] + madd)
                    p = jnp.exp(sc - lse_h)
                    dp = lax.dot_general(dyh, vs, (((1,), (1,)), ((), ())), preferred_element_type=F32)
                    ds = p * (dp - dsum)
                    db_ref[hh] += ds
                    pb, dsb = p.astype(BF16), ds.astype(BF16)
                    dv += lax.dot_general(pb, dyh, (((0,), (0,)), ((), ())), preferred_element_type=F32)
                    dk += lax.dot_general(dsb, qh, (((0,), (0,)), ((), ())), preferred_element_type=F32)
                    dq += jnp.where(sel, jnp.dot(dsb, ks, preferred_element_type=F32), 0.0)
                dq_ref[mine, :] = dq * scale
                dkext[keys, :] += dk
                dvext[keys, :] += dv

        before = pl.multiple_of(jnp.maximum(n * rows - halo, 0), LANES)
        here = pl.multiple_of(n * rows, LANES)
        after = pl.multiple_of(jnp.minimum((n + 1) * rows, geo.s - halo), LANES)
        for ext, total in ((dkext, dk_all), (dvext, dv_all)):
            total[pl.ds(before, halo), :] += ext[0:halo]
            total[pl.ds(here, rows), :] += ext[halo:halo + rows]
            total[pl.ds(after, halo), :] += ext[halo + rows:geo.ext]

        @pl.when(n == geo.nbig - 1)
        def _():
            col = pl.ds(pl.multiple_of(hp * LANES, LANES), LANES)
            copies = [pltpu.make_async_copy(total, out.at[:, col], sems.at[i])
                      for i, (total, out) in enumerate(((dk_all, dk_hbm), (dv_all, dv_hbm)))]
            for cp in copies:
                cp.start()
            for cp in copies:
                cp.wait()

    qc, kc, vc = cols
    bspec = pl.BlockSpec((2, tq, kw), lambda hp, n: (hp, 0, 0))
    any_spec = pl.BlockSpec(memory_space=pl.ANY)
    specs = ([geo.main(qc)] + geo.with_halos(kc) + geo.with_halos(vc)
             + [bspec, pl.BlockSpec((4, tq, kw), lambda hp, n: (0, 0, 0)), geo.main(ycol), geo.main(ycol), geo.main(0)])
    shape = jax.ShapeDtypeStruct((geo.s, 4 * LANES), F32)
    ext = pltpu.VMEM((geo.ext, LANES), F32)
    whole = pltpu.VMEM((geo.s, LANES), F32)
    return pl.pallas_call(
        body, name=name, grid=(4, geo.nbig), in_specs=specs, out_specs=[geo.main(0), any_spec, any_spec, bspec],
        out_shape=[shape, shape, shape, jax.ShapeDtypeStruct(bias.shape, F32)],
        scratch_shapes=[ext, ext, ext, ext, whole, whole, pltpu.SemaphoreType.DMA((2,))],
        compiler_params=_params(("arbitrary", "arbitrary"), VMEM_BIG))(
            h, h, h, h, h, h, h, bias, mask, dy, y, lse)


def _dil_combine(o_na, outs, lses, *, name):
    s, c = outs[0].shape
    t = _tile(s, (512, 256, 128, 64, 8))

    def body(na, o0, o1, o2, l0, l1, l2, y_ref, y16_ref, lt_ref):
        ls = [l0[...], l1[...], l2[...]]
        m = jnp.maximum(jnp.maximum(ls[0], ls[1]), ls[2])
        es = [jnp.exp(l - m) for l in ls]
        den = es[0] + es[1] + es[2]
        y = (es[0] / den) * o0[...] + (es[1] / den) * o1[...] + (es[2] / den) * o2[...]
        lt_ref[...] = m + jnp.log(den)
        y_ref[:, 0:c] = na[...]
        y_ref[:, c:2 * c] = y
        y16_ref[:, 0:c] = na[...].astype(BF16)
        y16_ref[:, c:2 * c] = y.astype(BF16)

    row = pl.BlockSpec((t, c), lambda i: (i, 0))
    wide = pl.BlockSpec((t, 2 * c), lambda i: (i, 0))
    return pl.pallas_call(body, name=name, grid=(s // t,), in_specs=[row] * 7, out_specs=[wide, wide, row],
                          out_shape=[jax.ShapeDtypeStruct((s, 2 * c), F32), jax.ShapeDtypeStruct((s, 2 * c), BF16),
                                     jax.ShapeDtypeStruct((s, c), F32)],
                          compiler_params=_params(("parallel",)))(o_na, *outs, *lses)


def _attn_dh(na, dil, *, name):
    s, c = na[0].shape
    t = _tile(s, (256, 128, 64, 8))

    def body(*refs):
        ins, o_ref = refs[:-1], refs[-1]
        for a in range(3):
            o_ref[:, a * c:(a + 1) * c] = ins[a][...].astype(BF16)
            o_ref[:, (3 + a) * c:(4 + a) * c] = (ins[3 + a][...] + ins[6 + a][...] + ins[9 + a][...]).astype(BF16)

    row = pl.BlockSpec((t, c), lambda i: (i, 0))
    flat = list(na) + [g[a] for g in dil for a in range(3)]
    return pl.pallas_call(body, name=name, grid=(s // t,), in_specs=[row] * 12,
                          out_specs=pl.BlockSpec((t, 6 * c), lambda i: (i, 0)),
                          out_shape=jax.ShapeDtypeStruct((s, 6 * c), BF16),
                          compiler_params=_params(("parallel",)))(*flat)


def _t5_bucket(rel):
    nb = N_BUCKETS // 2
    max_exact = nb // 2
    ret = np.where(rel > 0, nb, 0)
    n = np.abs(rel)
    large = max_exact + (np.log(np.maximum(n, 1).astype(np.float32) / np.float32(max_exact))
                         / np.float32(math.log(T5_MAX_DIST / max_exact)) * np.float32(nb - max_exact)).astype(np.int32)
    large = np.minimum(large, nb - 1)
    return (ret + np.where(n < max_exact, n, large)).astype(np.int32)


def _band_bucket_index(dil):
    tq, kw = DIL_TQ, DIL_KW
    rel = np.arange(kw)[None, :] - (kw - tq) // 2 - np.arange(tq)[:, None]
    return _t5_bucket(rel * dil)


def _band_bias(t5, dil, *, name):
    tq, kw = DIL_TQ, DIL_KW
    buckets = [int(b) for b in _t5_bucket(np.arange(-DIL_HALF, DIL_HALF + 1) * dil)]

    def body(t_ref, o_ref):
        hh = pl.program_id(0)
        rel = (lax.broadcasted_iota(jnp.int32, (tq, kw), 1) - (kw - tq) // 2
               - lax.broadcasted_iota(jnp.int32, (tq, kw), 0))
        acc = jnp.zeros((tq, kw), F32)
        for r, b in zip(range(-DIL_HALF, DIL_HALF + 1), buckets):
            acc = jnp.where(rel == r, t_ref[b * 8 + hh], acc)
        o_ref[0] = acc

    return pl.pallas_call(body, name=name, grid=(8,),
                          in_specs=[pl.BlockSpec(memory_space=pltpu.SMEM)],
                          out_specs=pl.BlockSpec((1, tq, kw), lambda h: (h, 0, 0)),
                          out_shape=jax.ShapeDtypeStruct((8, tq, kw), F32),
                          compiler_params=_params(("parallel",)))(t5.reshape(-1))


def _na_bias(rpb, *, name):
    nr, nc = 2 * NA_KH - 1, 2 * NA_KW - 1
    tq = NA_QROWS * GRID_W
    w = GRID_W

    def body(r_ref, o_ref):
        base = pl.program_id(0) * (nr * nc)
        lane = lax.broadcasted_iota(jnp.int32, (w, LANES), 1)
        upper = lane >= w
        diff = (lane & (w - 1)) - lax.broadcasted_iota(jnp.int32, (w, LANES), 0) + NA_KW - 1
        tiles = {}
        for i in range(NA_QROWS):
            for m in range(3 * NA_QROWS // 2):
                lo = 2 * m - i + NA_KH - 1 - NA_QROWS
                if lo not in tiles:
                    acc = jnp.zeros((w, LANES), F32)
                    for dc in range(nc):
                        v_lo = r_ref[base + lo * nc + dc] if 0 <= lo < nr else 0.0
                        v_hi = r_ref[base + (lo + 1) * nc + dc] if 0 <= lo + 1 < nr else 0.0
                        acc = jnp.where(diff == dc, jnp.where(upper, v_hi, v_lo), acc)
                    tiles[lo] = acc
                o_ref[0, i * w:(i + 1) * w, m * LANES:(m + 1) * LANES] = tiles[lo]

    return pl.pallas_call(body, name=name, grid=(8,),
                          in_specs=[pl.BlockSpec(memory_space=pltpu.SMEM)],
                          out_specs=pl.BlockSpec((1, tq, 3 * tq), lambda h: (h, 0, 0)),
                          out_shape=jax.ShapeDtypeStruct((8, tq, 3 * tq), F32),
                          compiler_params=_params(("parallel",)))(rpb.reshape(-1))


def _t5_grad(dbs, idxs, *, name):
    def body(d0, d1, d2, i0, i1, i2, o_ref):
        lane = lax.broadcasted_iota(jnp.int32, (1, LANES), 1)
        lines = [jnp.zeros((1, LANES), F32) for _ in range(8)]
        for dref, iref in ((d0, i0), (d1, i1), (d2, i2)):
            idx = iref[...]
            for hh in range(8):
                xh = dref[hh]
                for b in range(N_BUCKETS):
                    val = jnp.sum(jnp.sum(jnp.where(idx == b, xh, 0.0), axis=1, keepdims=True), axis=0, keepdims=True)
                    lines[hh] = lines[hh] + jnp.where(lane == b, val, 0.0)
        for hh in range(8):
            o_ref[hh:hh + 1, :] = lines[hh]

    out = pl.pallas_call(body, name=name, out_shape=jax.ShapeDtypeStruct((8, LANES), F32))(*dbs, *idxs)
    return out[:, :N_BUCKETS].T


def _rpb_grad(db, *, name):
    nr, nc = 2 * NA_KH - 1, 2 * NA_KW - 1
    tq = NA_QROWS * GRID_W
    w = GRID_W

    def body(d_ref, o_ref):
        x = d_ref[0]
        rows = []
        for dr in range(nr):
            acc = jnp.zeros((w, w), F32)
            for i in range(NA_QROWS):
                j = i + dr - (NA_KH - 1 - NA_QROWS)
                if 0 <= j < 3 * NA_QROWS:
                    acc = acc + x[i * w:(i + 1) * w, j * w:(j + 1) * w]
            rows.append(acc)
        diff = (lax.broadcasted_iota(jnp.int32, (w, w), 1) - lax.broadcasted_iota(jnp.int32, (w, w), 0)
                + NA_KW - 1)
        lane = lax.broadcasted_iota(jnp.int32, (1, LANES), 1)
        for dr in range(nr):
            line = jnp.zeros((1, LANES), F32)
            for dc in range(nc):
                val = jnp.sum(jnp.sum(jnp.where(diff == dc, rows[dr], 0.0), axis=1, keepdims=True),
                              axis=0, keepdims=True)
                line = jnp.where(lane == dc, val, line)
            o_ref[0, dr:dr + 1, :] = line

    out = pl.pallas_call(body, name=name, grid=(8,),
                         in_specs=[pl.BlockSpec((1, tq, 3 * tq), lambda h: (h, 0, 0))],
                         out_specs=pl.BlockSpec((1, nr, LANES), lambda h: (h, 0, 0)),
                         out_shape=jax.ShapeDtypeStruct((8, nr, LANES), F32),
                         compiler_params=_params(("parallel",)))(db)
    return out[:, :, :nc]


def _exchange(src, *, gather, name):
    shape = src.shape if not gather else (N_DEV,) + src.shape

    def body(src_ref, out_ref, send_sems, recv_sems, local_sem):
        _exchange_start(src_ref, out_ref, send_sems, recv_sems, local_sem, gather)
        _exchange_wait(src_ref, out_ref, send_sems, recv_sems, local_sem, gather)

    any_spec = pl.BlockSpec(memory_space=pl.ANY)
    return pl.pallas_call(
        body, name=name, in_specs=[any_spec], out_specs=any_spec, out_shape=jax.ShapeDtypeStruct(shape, src.dtype),
        scratch_shapes=_exchange_sems())(src)


class _Carried:
    def __init__(self, carry, grid):
        self.src, self.gather = carry if carry is not None else (None, False)
        self.on = self.src is not None
        self.grid = grid
        self.operands = [self.src] if self.on else []
        self.specs = [pl.BlockSpec(memory_space=pl.ANY)] if self.on else []
        self.scratch = _exchange_sems() if self.on else []
        self.out_shapes = []
        if self.on:
            shape = ((N_DEV,) + self.src.shape) if self.gather else self.src.shape
            self.out_shapes = [jax.ShapeDtypeStruct(shape, self.src.dtype)]

    def split(self, refs, n_in, n_out):
        refs = list(refs)
        if not self.on:
            return refs[:n_in], refs[n_in:n_in + n_out], refs[n_in + n_out:]
        self.refs = (refs[n_in], refs[n_in + 1 + n_out], *refs[-3:], self.gather)
        return refs[:n_in], refs[n_in + 1:n_in + 1 + n_out], refs[n_in + 2 + n_out:-3]

    def _at(self, last):
        hit = None
        for ax, size in enumerate(self.grid):
            here = pl.program_id(ax) == (size - 1 if last else 0)
            hit = here if hit is None else hit & here
        return hit

    def start(self):
        if self.on:
            pl.when(self._at(False))(lambda: _exchange_start(*self.refs))

    def wait(self):
        if self.on:
            pl.when(self._at(True))(lambda: _exchange_wait(*self.refs))


def _exchange_sems():
    return [pltpu.SemaphoreType.DMA((N_DEV - 1,)), pltpu.SemaphoreType.DMA((N_DEV - 1,)), pltpu.SemaphoreType.DMA]


def _exchange_copies(src_ref, out_ref, send_sems, recv_sems, local_sem, gather):
    x, y, c = lax.axis_index("x"), lax.axis_index("y"), lax.axis_index("c")
    me = 4 * x + 2 * y + c

    def outgoing(p):
        return src_ref if gather else src_ref.at[p]

    own = pltpu.make_async_copy(outgoing(me), out_ref.at[me], local_sem)
    sends, recvs = [], []
    for k in range(1, N_DEV):
        px = 1 - x if k & 4 else x
        py = 1 - y if k & 2 else y
        pc = 1 - c if k & 1 else c
        p = 4 * px + 2 * py + pc
        for dst, group in ((me, sends), (p, recvs)):
            group.append(pltpu.make_async_remote_copy(
                src_ref=outgoing(p), dst_ref=out_ref.at[dst], send_sem=send_sems.at[k - 1],
                recv_sem=recv_sems.at[k - 1], device_id=(px, py, pc), device_id_type=pl.DeviceIdType.MESH))
    return own, sends, recvs


def _exchange_start(*refs_and_mode):
    own, sends, _ = _exchange_copies(*refs_and_mode)
    own.start()
    for cp in sends:
        cp.start()


def _exchange_wait(*refs_and_mode):
    own, sends, recvs = _exchange_copies(*refs_and_mode)
    for cp in recvs:
        cp.wait_recv()
    for cp in sends:
        cp.wait_send()
    own.wait()


def _adamw(parts, w, m, v, *, name):
    layers = len(parts)
    rows, cols = w.shape
    per_layer = rows // layers
    t = _tile(per_layer, (FLAT_ROW_TILE, 128, 64, 32, 16, 8))
    nt = per_layer // t

    def body(*refs):
        p_refs = refs[:layers]
        w_ref, m_ref, v_ref, g_ref, d_ref, nm_ref, nv_ref = refs[layers:]
        layer = pl.program_id(0)
        g = None
        for l, p_ref in enumerate(p_refs):
            total = p_ref[0].astype(F32)
            for k in range(1, N_DEV):
                total = total + p_ref[k].astype(F32)
            g = total if g is None else jnp.where(layer == l, total, g)
        nm = ADAM_B1 * m_ref[...] + (1.0 - ADAM_B1) * g
        nv = ADAM_B2 * v_ref[...] + (1.0 - ADAM_B2) * (g * g)
        m_hat = nm / (1.0 - ADAM_B1 ** ADAM_STEP)
        v_hat = nv / (1.0 - ADAM_B2 ** ADAM_STEP)
        g_ref[...] = g
        d_ref[...] = -ADAM_LR * (m_hat / (jnp.sqrt(v_hat) + ADAM_EPS) + ADAM_WD * w_ref[...])
        nm_ref[...] = nm
        nv_ref[...] = nv

    def part_spec(l):
        return pl.BlockSpec((N_DEV, t, cols), lambda layer, i: (0, jnp.where(layer == l, i, 0), 0))

    row = pl.BlockSpec((t, cols), lambda layer, i: (layer * nt + i, 0))
    shape = jax.ShapeDtypeStruct((rows, cols), F32)
    return pl.pallas_call(body, name=name, grid=(layers, nt),
                          in_specs=[part_spec(l) for l in range(layers)] + [row, row, row],
                          out_specs=[row] * 4, out_shape=[shape] * 4,
                          compiler_params=_params(("parallel", "parallel"), VMEM_BIG))(*parts, w, m, v)


def _flatten(arrays, dtype, row_mult):
    flat = jnp.concatenate([a.reshape(-1).astype(dtype) for a in arrays])
    chunk = FLAT_COLS * row_mult
    padded = -(-flat.shape[0] // chunk) * chunk
    return jnp.pad(flat, (0, padded - flat.shape[0])).reshape(padded // FLAT_COLS, FLAT_COLS)


def _unflatten(flat, shapes):
    flat = flat.reshape(-1)
    out, pos = [], 0
    for shp in shapes:
        size = int(np.prod(shp))
        out.append(flat[pos:pos + size].reshape(shp))
        pos += size
    return out


def _gather_full(names, local, dtype, row_mult, label):
    got = _exchange(_flatten([local[n] for n in names], dtype, row_mult), gather=True, name=label)
    per_dev = [_unflatten(got[p], [local[n].shape for n in names]) for p in range(N_DEV)]
    return {n: jnp.concatenate([per_dev[p][i] for p in range(N_DEV)], axis=SHARD_AXIS[n])
            for i, n in enumerate(names)}


def _from_shards(stacked, axis):
    _, a, b = stacked.shape
    if axis == 1:
        return jnp.transpose(stacked, (1, 0, 2)).reshape(a, N_DEV * b)
    return stacked.reshape(N_DEV * a, b)


def _to_shards(full, axis):
    ra, rb = full.shape
    if axis == 1:
        return jnp.transpose(full.reshape(ra, N_DEV, rb // N_DEV), (1, 0, 2))
    return full.reshape(N_DEV, ra // N_DEV, rb)


class _ShardedMatmulWeights:
    def __init__(self, local):
        self.local, self.full, self.parts = local, {}, {}

    def gather_src(self, n, l):
        return self.local[n][l].astype(BF16)

    def set_gathered(self, n, l, got):
        self.full[n, l] = _from_shards(got, SHARD_AXIS[n] - 1)

    def get(self, n, l):
        return self.full[n, l]

    def scatter_src(self, n, l, dw):
        return _to_shards(dw, SHARD_AXIS[n] - 1).astype(BF16)

    def set_scattered(self, n, l, parts):
        self.parts[n, l] = parts


def _role(role, i):
    mixer = 'attn_w_' if i % 2 == 0 else 'conv_w_'
    return {'in': (mixer + 'in', i // 2), 'out': (mixer + 'out', i // 2),
            'up': ('ffn_w_up', i), 'down': ('ffn_w_down', i)}[role]


def _local_step(x, tgt, w, big):
    s = x.shape[0]

    def project(a, role, i, ln=None):
        carry = None
        if i + 1 < DEPTH:
            nxt = _role(role, i + 1)
            carry = (big.gather_src(*nxt), True)
        weight = big.get(*_role(role, i))
        if ln is None:
            out, got = _mm(a, weight, carry=carry, name=role + "_fwd")
        else:
            *out, got = _mm_ln(a, weight, *ln, carry=carry, name=role + "_fwd")
        if got is not None:
            big.set_gathered(*nxt, got)
        return out

    def project_back(a, d_out, role, i, ln=None):
        key = _role(role, i)
        dw, _ = _mm(a, d_out, ta=True, name=role + "_dw")
        carry = (big.scatter_src(*key, dw), False)
        if ln is None:
            d_in, parts = _mm(d_out, big.get(*key), tb=True, carry=carry, name=role + "_dx")
        else:
            *d_in, parts = _mm_ln_bwd(d_out, big.get(*key), *ln, carry=carry, name=role + "_dx")
        if parts is not None:
            big.set_scattered(*key, parts)
        return d_in

    na_tq = NA_QROWS * GRID_W
    band_idx = [_band_bucket_index(d) for _, d in DIL_PATTERNS]
    band_bias = [_band_bias(w['t5_bias'], d, name=f"band_bias_{d}") for _, d in DIL_PATTERNS]
    band_mask = [_mask_tiles("band", DIL_TQ, DIL_KW, s // d, name=f"band_mask_{d}") for _, d in DIL_PATTERNS]
    na_mask = _mask_tiles("na", na_tq, 3 * na_tq, s, name="na_mask")
    na_cols, dil_cols = (0, 4, 8), (12, 16, 20)
    grads = {n: [None] * w[n].shape[0] for n in SMALL_SHARDED + REPLICATED if n != 't5_bias'}
    saved = []
    x16 = x.astype(BF16)

    for i in range(DEPTH):
        j = i // 2
        st = {'x': x, 'x16': x16}
        if i % 2 == 0:
            h = project(x16, 'in', i)
            na_bias = _na_bias(w['na_rpb'][j], name="na_bias")
            late = [_role(role, 0) for role in ('up', 'down')] if i == 0 else []
            rides = [(key, (big.gather_src(*key), True)) for key in late] + [(None, None)] * 4
            outs, lses = [], []
            calls = [(na_bias, na_mask, 1, na_tq, NA_NSUB, na_cols, "na_fwd")] + [
                (bias, mask, d, DIL_TQ, DIL_NSUB[d], dil_cols, f"dil_fwd_{d}")
                for (_, d), bias, mask in zip(DIL_PATTERNS, band_bias, band_mask)]
            for (bias, mask, d, tq, nsub, cols, label), (key, carry) in zip(calls, rides):
                o, l, got = _attn_fwd(h, bias, mask, d=d, tq=tq, nsub=nsub, cols=cols, carry=carry, name=label)
                if got is not None:
                    big.set_gathered(*key, got)
                outs.append(o)
                lses.append(l)
            o_na, l_na = outs.pop(0), lses.pop(0)
            mid, mid16, l_dil = _dil_combine(o_na, outs, lses, name="dil_combine")
            st.update(h=h, mid=mid, mid16=mid16, l_na=l_na, l_dil=l_dil, na_bias=na_bias)
        else:
            h = project(x16, 'in', i)
            mid16, u2 = _conv_mid_fwd(h, w['conf_dw_w'][j], w['conf_dw_b'][j], w['conf_ln_g'][j], w['conf_ln_b'][j],
                                      w['sconv_w'][j], name="conv_mid_fwd")
            st.update(h=h, mid16=mid16, u2=u2)
        z_mix, xa, xa16 = project(mid16, 'out', i, ln=(x, w['mix_ln_g'][i], w['mix_ln_b'][i]))
        hu = project(xa16, 'up', i)
        act16 = _ffn_mid_fwd(hu, w['ffn_dw_w'][i], name="ffn_mid_fwd")
        z_ffn, xb, xb16 = project(act16, 'down', i, ln=(xa, w['ffn_ln_g'][i], w['ffn_ln_b'][i]))
        st.update(z_mix=z_mix, xa16=xa16, hu=hu, act16=act16, z_ffn=z_ffn)
        saved.append(st)
        x, x16 = xb, xb16

    loss, d_loss = _loss_head(x, tgt, name="loss_head")
    g_t5 = None
    dz, dz16, dg, db = _ln_bwd(saved[-1]['z_ffn'], w['ffn_ln_g'][-1], d_loss, None, name="last_ln_bwd")
    for i in reversed(range(DEPTH)):
        j = i // 2
        st = saved[i]
        grads['ffn_ln_g'][i], grads['ffn_ln_b'][i] = dg, db
        dact = project_back(st['act16'], dz16, 'down', i)
        dhu, grads['ffn_dw_w'][i] = _ffn_mid_bwd(st['hu'], w['ffn_dw_w'][i], dact, name="ffn_mid_bwd")
        dz, dz1, dg, db = project_back(st['xa16'], dhu, 'up', i, ln=(st['z_mix'], w['mix_ln_g'][i], dz))
        grads['mix_ln_g'][i], grads['mix_ln_b'][i] = dg, db
        dmid = project_back(st['mid16'], dz1, 'out', i)
        if i % 2 == 0:
            h = st['h']
            dq, dk, dv, dbias = _attn_bwd(h, st['na_bias'], na_mask, dmid, st['mid'], st['l_na'], d=1, tq=na_tq,
                                          nsub=NA_NSUB, cols=na_cols, ycol=0, name="na_bwd")
            grads['na_rpb'][j] = _rpb_grad(dbias, name="rpb_grad")
            dil, dbs = [], []
            for (_, d), bias, mask in zip(DIL_PATTERNS, band_bias, band_mask):
                g = _attn_bwd(h, bias, mask, dmid, st['mid'], st['l_dil'], d=d, tq=DIL_TQ, nsub=DIL_NSUB[d],
                              cols=dil_cols, ycol=4, name=f"dil_bwd_{d}")
                dil.append(g[:3])
                dbs.append(g[3])
            t5 = _t5_grad(dbs, band_idx, name="t5_grad")
            g_t5 = t5 if g_t5 is None else g_t5 + t5
            dh = _attn_dh((dq, dk, dv), dil, name="attn_dh")
        else:
            dh, dw31, db31, dlg, dlb, dw3 = _conv_mid_bwd(st['h'], st['u2'], dmid, w['conf_dw_w'][j],
                                                          w['conf_ln_g'][j], w['conf_ln_b'][j], w['sconv_w'][j],
                                                          name="conv_mid_bwd")
            grads['conf_dw_w'][j], grads['conf_dw_b'][j] = dw31, db31
            grads['conf_ln_g'][j], grads['conf_ln_b'][j], grads['sconv_w'][j] = dlg, dlb, dw3
        if i > 0:
            below = saved[i - 1]
            dz, dz16, dg, db = project_back(st['x16'], dh, 'in', i, ln=(below['z_ffn'], w['ffn_ln_g'][i - 1], dz))
        else:
            dx = _axpy(dz, project_back(st['x16'], dh, 'in', i), name="grad_x")
    full = {n: jnp.stack(g) for n, g in grads.items()}
    full['t5_bias'] = g_t5
    return loss, dx, full


def kernel(x, t5_bias, attn_w_in, attn_w_out, na_rpb, conv_w_in, conf_dw_w, conf_dw_b, conf_ln_g, conf_ln_b, sconv_w, conv_w_out, ffn_w_up, ffn_dw_w, ffn_w_down, mix_ln_g, mix_ln_b, ffn_ln_g, ffn_ln_b, loss_target, m_t5_bias, m_attn_w_in, m_attn_w_out, m_na_rpb, m_conv_w_in, m_conf_dw_w, m_conf_dw_b, m_conf_ln_g, m_conf_ln_b, m_sconv_w, m_conv_w_out, m_ffn_w_up, m_ffn_dw_w, m_ffn_w_down, m_mix_ln_g, m_mix_ln_b, m_ffn_ln_g, m_ffn_ln_b, v_t5_bias, v_attn_w_in, v_attn_w_out, v_na_rpb, v_conv_w_in, v_conf_dw_w, v_conf_dw_b, v_conf_ln_g, v_conf_ln_b, v_sconv_w, v_conv_w_out, v_ffn_w_up, v_ffn_dw_w, v_ffn_w_down, v_mix_ln_g, v_mix_ln_b, v_ffn_ln_g, v_ffn_ln_b):
    args = dict(locals())
    local = {n: args[n] for n in WEIGHTS}
    mom1 = {n: args['m_' + n] for n in WEIGHTS}
    mom2 = {n: args['v_' + n] for n in WEIGHTS}

    kinds = ('grad', 'delta', 'new_m', 'new_v')
    small = {n: local[n] for n in REPLICATED}
    small.update(_gather_full(SMALL_SHARDED, local, F32, 8, "gather_small_weights"))
    big = _ShardedMatmulWeights({n: local[n] for n in MATMUL_WEIGHTS})
    for role in ('in', 'out'):
        n, l = _role(role, 0)
        big.set_gathered(n, l, _exchange(big.gather_src(n, l), gather=True, name="gather_first_" + role))

    loss, dx, grads = _local_step(x[0], loss_target[0], small, big)
    loss = lax.psum(loss, MESH_AXES)

    out = {}
    for n in MATMUL_WEIGHTS:
        layers, a, b = local[n].shape
        res = _adamw([big.parts[n, l] for l in range(layers)],
                     *[t.reshape(layers * a, b) for t in (local[n], mom1[n], mom2[n])], name="adamw_" + n)
        for kind, r in zip(kinds, res):
            out[kind + '_' + n] = r.reshape(layers, a, b)
    for names, sharded, label in ((SMALL_SHARDED, True, "small"), (REPLICATED, False, "replicated")):
        shapes = [local[n].shape for n in names]
        if sharded:
            per_dev = [_flatten([lax.slice_in_dim(grads[n], p * local[n].shape[SHARD_AXIS[n]],
                                                  (p + 1) * local[n].shape[SHARD_AXIS[n]], axis=SHARD_AXIS[n])
                                 for n in names], F32, 8) for p in range(N_DEV)]
            parts = _exchange(jnp.stack(per_dev), gather=False, name="scatter_small_grads")
        else:
            parts = _exchange(_flatten([grads[n] for n in names], F32, 8), gather=True, name="gather_replicated_grads")
        res = _adamw([parts], _flatten([local[n] for n in names], F32, 8), _flatten([mom1[n] for n in names], F32, 8),
                     _flatten([mom2[n] for n in names], F32, 8), name="adamw_" + label)
        for kind, flat in zip(kinds, res):
            for n, a in zip(names, _unflatten(flat, shapes)):
                out[kind + '_' + n] = a

    return (loss, dx[None], *[out[k + '_' + n] for k in ('grad', 'delta', 'new_m', 'new_v') for n in WEIGHTS])
```

```python
import functools
import math

import jax
import jax.numpy as jnp
import numpy as np
from jax import lax
from jax.experimental import pallas as pl
from jax.experimental.pallas import tpu as pltpu

F32 = jnp.float32
BF16 = jnp.bfloat16

N_DEV = 8
MESH_AXES = ("x", "y", "c")
DEPTH = 4
GRID_W = 64
GRID_SHIFT = 6
HEAD_DIM = 64
NA_KH = 8
NA_KW = 16
NA_QROWS = 4
DIL_PATTERNS = ((128, 1), (512, 4), (2048, 16))
DIL_HALF = 64
DIL_TQ = 128
DIL_KW = DIL_TQ + 2 * DIL_HALF
DIL_NSUB = {1: 8, 4: 2, 16: 1}
NA_NSUB = 4
N_BUCKETS = 32
T5_MAX_DIST = 1024
CONF_CH = 512
CONF_K = 31
SC_K = 3
FFN_K = 3
FFN_ROW_TILE = 128
LN_EPS = 1e-5
NEG = -1e30
ALPHA = (2 * DEPTH) ** 0.25
ADAM_LR = 0.001
ADAM_B1 = 0.9
ADAM_B2 = 0.999
ADAM_EPS = 1e-08
ADAM_WD = 0.01
ADAM_STEP = 10

LANES = 128
SUBLANES = 8
VMEM_BIG = 48 * 1024 * 1024
FLAT_COLS = 1024
FLAT_ROW_TILE = 256

WEIGHTS = ['t5_bias', 'attn_w_in', 'attn_w_out', 'na_rpb', 'conv_w_in', 'conf_dw_w', 'conf_dw_b', 'conf_ln_g',
           'conf_ln_b', 'sconv_w', 'conv_w_out', 'ffn_w_up', 'ffn_dw_w', 'ffn_w_down', 'mix_ln_g', 'mix_ln_b',
           'ffn_ln_g', 'ffn_ln_b']
SHARD_AXIS = {'attn_w_in': 2, 'attn_w_out': 1, 'conv_w_in': 2, 'conf_dw_w': 2, 'conf_dw_b': 1, 'conf_ln_g': 1,
              'conf_ln_b': 1, 'sconv_w': 2, 'conv_w_out': 1, 'ffn_w_up': 2, 'ffn_dw_w': 2, 'ffn_w_down': 1}
MATMUL_WEIGHTS = ['attn_w_in', 'attn_w_out', 'conv_w_in', 'conv_w_out', 'ffn_w_up', 'ffn_w_down']
SMALL_SHARDED = ['conf_dw_w', 'conf_dw_b', 'conf_ln_g', 'conf_ln_b', 'sconv_w', 'ffn_dw_w']
SHARDED = MATMUL_WEIGHTS + SMALL_SHARDED
REPLICATED = ['t5_bias', 'na_rpb', 'mix_ln_g', 'mix_ln_b', 'ffn_ln_g', 'ffn_ln_b']


def _tile(n, cands):
    for c in cands:
        if n % c == 0:
            return c
    return n


def _params(sem, vmem=None):
    return pltpu.CompilerParams(dimension_semantics=sem, vmem_limit_bytes=vmem)


def _sigmoid(x):
    return 0.5 * jnp.tanh(0.5 * x) + 0.5


MM_MAX_TILE = 1408
MM_MAX_K = 3072


def _lane_tile(n, cap):
    best = None
    for t in range(LANES, min(n, cap) + 1, LANES):
        if n % t == 0:
            best = t
    return best or n


def _mm(a, b, *, ta=False, tb=False, out_dtype=F32, carry=None, name):
    assert a.dtype == BF16 and b.dtype == BF16, (name, a.dtype, b.dtype)
    m, k = (a.shape[1], a.shape[0]) if ta else a.shape
    n = b.shape[0] if tb else b.shape[1]
    tm, tn, tk = _lane_tile(m, MM_MAX_TILE), _lane_tile(n, MM_MAX_TILE), _lane_tile(k, MM_MAX_K)
    grid = (m // tm, n // tn, k // tk)
    nk = grid[2]
    dims = (((0 if ta else 1,), (1 if tb else 0,)), ((), ()))
    use_acc = nk > 1 and out_dtype != F32
    ride = _Carried(carry, grid)

    def body(*refs):
        (a_ref, b_ref), (o_ref,), scratch = ride.split(refs, 2, 1)
        ride.start()
        part = lax.dot_general(a_ref[...], b_ref[...], dims, preferred_element_type=F32)
        if nk == 1:
            o_ref[...] = part.astype(out_dtype)
        else:
            acc_ref = scratch[0] if use_acc else o_ref
            kk = pl.program_id(2)

            @pl.when(kk == 0)
            def _():
                acc_ref[...] = part

            @pl.when(kk > 0)
            def _():
                acc_ref[...] += part

            if use_acc:
                @pl.when(kk == nk - 1)
                def _():
                    o_ref[...] = acc_ref[...].astype(out_dtype)

        ride.wait()

    a_spec = pl.BlockSpec((tk, tm), lambda i, j, q: (q, i)) if ta else pl.BlockSpec((tm, tk), lambda i, j, q: (i, q))
    b_spec = pl.BlockSpec((tn, tk), lambda i, j, q: (j, q)) if tb else pl.BlockSpec((tk, tn), lambda i, j, q: (q, j))
    o_spec = pl.BlockSpec((tm, tn), lambda i, j, q: (i, j))
    o_shape = jax.ShapeDtypeStruct((m, n), out_dtype)
    scratch = [pltpu.VMEM((tm, tn), F32)] if use_acc else []
    sem = ("arbitrary",) * 3 if ride.on else ("parallel", "parallel", "arbitrary")
    out = pl.pallas_call(
        body, name=name, grid=grid, in_specs=[a_spec, b_spec] + ride.specs, out_specs=[o_spec] + ride.specs,
        out_shape=[o_shape] + ride.out_shapes, scratch_shapes=scratch + ride.scratch,
        compiler_params=_params(sem, VMEM_BIG))(a, b, *ride.operands)
    return out[0], (out[1] if ride.on else None)


MM_LN_ROWS = 512


def _mm_ln(a, b, x, g, beta, *, carry=None, name):
    assert a.dtype == BF16 and b.dtype == BF16, (name, a.dtype, b.dtype)
    m, k = a.shape
    n = b.shape[1]
    assert k <= MM_MAX_K, (name, k)
    tm = _tile(m, (MM_LN_ROWS, 256, 128, 64, 8))
    grid = (m // tm,)
    ride = _Carried(carry, grid)

    def body(*refs):
        (a_ref, b_ref, x_ref, g_ref, beta_ref), (z_ref, o_ref, o16_ref), _ = ride.split(refs, 5, 3)
        ride.start()
        z = ALPHA * x_ref[...] + jnp.dot(a_ref[...], b_ref[...], preferred_element_type=F32)
        z_ref[...] = z
        xh, _ = _ln_stats(z)
        out = xh * g_ref[...] + beta_ref[...]
        o_ref[...] = out
        o16_ref[...] = out.astype(BF16)
        ride.wait()

    row = lambda width: pl.BlockSpec((tm, width), lambda i: (i, 0))
    vec = pl.BlockSpec((1, n), lambda i: (0, 0))
    out = pl.pallas_call(
        body, name=name, grid=grid,
        in_specs=[row(k), pl.BlockSpec((k, n), lambda i: (0, 0)), row(n), vec, vec] + ride.specs,
        out_specs=[row(n)] * 3 + ride.specs,
        out_shape=[jax.ShapeDtypeStruct((m, n), F32), jax.ShapeDtypeStruct((m, n), F32),
                   jax.ShapeDtypeStruct((m, n), BF16)] + ride.out_shapes,
        scratch_shapes=ride.scratch,
        compiler_params=_params(("arbitrary",) if ride.on else ("parallel",), VMEM_BIG))(
            a, b, x, g.reshape(1, n), beta.reshape(1, n), *ride.operands)
    return out[0], out[1], out[2], (out[3] if ride.on else None)


def _mm_ln_bwd(d_out, b, z, g, d1, *, carry=None, name):
    assert d_out.dtype == BF16 and b.dtype == BF16, (name, d_out.dtype, b.dtype)
    m, k = d_out.shape
    n = b.shape[0]
    tm, tk = _tile(m, (MM_LN_ROWS, 256, 128, 64, 8)), _lane_tile(k, MM_MAX_K)
    grid = (m // tm, k // tk)
    nk = grid[1]
    ride = _Carried(carry, grid)

    def body(*refs):
        (a_ref, b_ref, z_ref, g_ref, d1_ref), (dz_ref, dz16_ref, dg_ref, db_ref), scratch = ride.split(refs, 5, 4)
        ride.start()
        i, kk = pl.program_id(0), pl.program_id(1)
        part = lax.dot_general(a_ref[...], b_ref[...], (((1,), (1,)), ((), ())), preferred_element_type=F32)

        def finish(d2):
            @pl.when(i == 0)
            def _():
                dg_ref[...] = jnp.zeros_like(dg_ref)
                db_ref[...] = jnp.zeros_like(db_ref)

            dout = ALPHA * d1_ref[...] + d2
            xh, rstd = _ln_stats(z_ref[...])
            dxh = dout * g_ref[...]
            dz = rstd * (dxh - jnp.mean(dxh, axis=-1, keepdims=True) - xh * jnp.mean(dxh * xh, axis=-1, keepdims=True))
            dz_ref[...] = dz
            dz16_ref[...] = dz.astype(BF16)
            dg_ref[...] += jnp.sum(dout * xh, axis=0, keepdims=True)
            db_ref[...] += jnp.sum(dout, axis=0, keepdims=True)

        if nk == 1:
            finish(part)
        else:
            acc_ref = scratch[0]

            @pl.when(kk == 0)
            def _():
                acc_ref[...] = part

            @pl.when((kk > 0) & (kk < nk - 1))
            def _():
                acc_ref[...] += part

            @pl.when(kk == nk - 1)
            def _():
                finish(acc_ref[...] + part)

        ride.wait()

    row = pl.BlockSpec((tm, n), lambda i, q: (i, 0))
    vec = pl.BlockSpec((1, n), lambda i, q: (0, 0))
    out = pl.pallas_call(
        body, name=name, grid=grid,
        in_specs=[pl.BlockSpec((tm, tk), lambda i, q: (i, q)), pl.BlockSpec((n, tk), lambda i, q: (0, q)),
                  row, vec, row] + ride.specs,
        out_specs=[row, row, vec, vec] + ride.specs,
        out_shape=[jax.ShapeDtypeStruct((m, n), F32), jax.ShapeDtypeStruct((m, n), BF16),
                   jax.ShapeDtypeStruct((1, n), F32), jax.ShapeDtypeStruct((1, n), F32)] + ride.out_shapes,
        scratch_shapes=([pltpu.VMEM((tm, n), F32)] if nk > 1 else []) + ride.scratch,
        compiler_params=_params(("arbitrary", "arbitrary"), VMEM_BIG))(
            d_out, b, z, g.reshape(1, n), d1, *ride.operands)
    return out[0], out[1], out[2].reshape(n), out[3].reshape(n), (out[4] if ride.on else None)


def _ln_stats(z):
    mu = jnp.mean(z, axis=-1, keepdims=True)
    zc = z - mu
    var = jnp.mean(zc * zc, axis=-1, keepdims=True)
    rstd = lax.rsqrt(var + LN_EPS)
    return zc * rstd, rstd


def _ln_bwd(z, g, d1, d2, *, name):
    s, d = z.shape
    t = _tile(s, (256, 128, 64, 8))
    two = d2 is not None

    def body(*refs):
        if two:
            z_ref, g_ref, d1_ref, d2_ref, dz_ref, dz16_ref, dg_ref, db_ref = refs
            dout = ALPHA * d1_ref[...] + d2_ref[...]
        else:
            z_ref, g_ref, d1_ref, dz_ref, dz16_ref, dg_ref, db_ref = refs
            dout = d1_ref[...]

        @pl.when(pl.program_id(0) == 0)
        def _():
            dg_ref[...] = jnp.zeros_like(dg_ref)
            db_ref[...] = jnp.zeros_like(db_ref)

        xh, rstd = _ln_stats(z_ref[...])
        dxh = dout * g_ref[...]
        dz = rstd * (dxh - jnp.mean(dxh, axis=-1, keepdims=True) - xh * jnp.mean(dxh * xh, axis=-1, keepdims=True))
        dz_ref[...] = dz
        dz16_ref[...] = dz.astype(BF16)
        dg_ref[...] += jnp.sum(dout * xh, axis=0, keepdims=True)
        db_ref[...] += jnp.sum(dout, axis=0, keepdims=True)

    row = pl.BlockSpec((t, d), lambda i: (i, 0))
    vec = pl.BlockSpec((1, d), lambda i: (0, 0))
    ins = [z, g.reshape(1, d), d1] + ([d2] if two else [])
    specs = [row, vec, row] + ([row] if two else [])
    dz, dz16, dg, db = pl.pallas_call(
        body, name=name, grid=(s // t,), in_specs=specs, out_specs=[row, row, vec, vec],
        out_shape=[jax.ShapeDtypeStruct((s, d), F32), jax.ShapeDtypeStruct((s, d), BF16),
                   jax.ShapeDtypeStruct((1, d), F32), jax.ShapeDtypeStruct((1, d), F32)],
        compiler_params=_params(("arbitrary",)))(*ins)
    return dz, dz16, dg.reshape(d), db.reshape(d)


def _axpy(d1, d2, *, name):
    s, d = d1.shape
    t = _tile(s, (256, 128, 64, 8))

    def body(a_ref, b_ref, o_ref):
        o_ref[...] = ALPHA * a_ref[...] + b_ref[...]

    row = pl.BlockSpec((t, d), lambda i: (i, 0))
    return pl.pallas_call(body, name=name, grid=(s // t,), in_specs=[row, row], out_specs=row,
                          out_shape=jax.ShapeDtypeStruct((s, d), F32), compiler_params=_params(("parallel",)))(d1, d2)


def _loss_head(y, tgt, *, name):
    s, d = y.shape
    t = _tile(s, (256, 128, 64, 8))

    def body(y_ref, t_ref, l_ref, dy_ref):
        @pl.when(pl.program_id(0) == 0)
        def _():
            l_ref[...] = jnp.zeros_like(l_ref)

        err = y_ref[...] - t_ref[...]
        dy_ref[...] = err * (1.0 / d)
        l_ref[...] += 0.5 * jnp.sum(jnp.sum(err * err, axis=1, keepdims=True), axis=0, keepdims=True) * (1.0 / d)

    row = pl.BlockSpec((t, d), lambda i: (i, 0))
    one = pl.BlockSpec((SUBLANES, LANES), lambda i: (0, 0))
    loss, dy = pl.pallas_call(
        body, name=name, grid=(s // t,), in_specs=[row, row], out_specs=[one, row],
        out_shape=[jax.ShapeDtypeStruct((SUBLANES, LANES), F32), jax.ShapeDtypeStruct((s, d), F32)],
        compiler_params=_params(("arbitrary",)))(y, tgt)
    return loss[0, 0], dy


def _halo_specs(s, t, halo, cb, col):
    per = t // halo
    last = s // halo - 1
    return [pl.BlockSpec((t, cb), lambda j, i: (i, col(j))),
            pl.BlockSpec((halo, cb), lambda j, i: (jnp.maximum(i * per - 1, 0), col(j))),
            pl.BlockSpec((halo, cb), lambda j, i: (jnp.minimum((i + 1) * per, last), col(j)))]


def _extended(main_ref, prev_ref, next_ref, i, n):
    prev = jnp.where(i > 0, prev_ref[...], 0.0)
    nxt = jnp.where(i < n - 1, next_ref[...], 0.0)
    return jnp.concatenate([prev, main_ref[...], nxt], axis=0)


def _shift(ext, o):
    if o == 0:
        return ext
    return pltpu.roll(ext, (-o) % ext.shape[0], 0)


def _taps(ext, k, sign=1):
    return [_shift(ext, sign * (j - k // 2)) for j in range(k)]


def _conv(ext, w_ref, k, sign=1, taps=None):
    taps = _taps(ext, k, sign) if taps is None else taps
    acc = None
    for j in range(k):
        term = w_ref[j:j + 1, :] * taps[j]
        acc = term if acc is None else acc + term
    return acc


def _main_taps(ext, k, halo, t, sign=1):
    rolled, taps = {}, []
    for j in range(k):
        offset = sign * (j - k // 2)
        res = offset % SUBLANES
        if res not in rolled:
            rolled[res] = _shift(ext, res)
        start = halo + offset - res
        taps.append(rolled[res][start:start + t])
    return taps


def _conv_main(taps, w_ref):
    acc = None
    for j, tap in enumerate(taps):
        term = w_ref[j:j + 1, :] * tap
        acc = term if acc is None else acc + term
    return acc


def _wgrad_main(dw_ref, d_main, taps):
    for j, tap in enumerate(taps):
        dw_ref[j:j + 1, :] += jnp.sum(d_main * tap, axis=0, keepdims=True)


def _conv_wgrad(dw_ref, d_main, x_ext, k, halo, t, taps=None):
    taps = _taps(x_ext, k) if taps is None else taps
    for j in range(k):
        dw_ref[j:j + 1, :] += jnp.sum(d_main * taps[j][halo:halo + t], axis=0, keepdims=True)


def _ffn_mid_fwd(hu, w, *, name):
    s, f2 = hu.shape
    f = f2 // 2
    t, cb, halo = _tile(s, (FFN_ROW_TILE,)), _lane_tile(f, MM_MAX_TILE), SUBLANES
    nt, nc = s // t, f // cb

    def body(g_ref, gp_ref, gn_ref, u_ref, up_ref, un_ref, wg_ref, wu_ref, a_ref):
        i = pl.program_id(1)
        hg = _conv_main(_main_taps(_extended(g_ref, gp_ref, gn_ref, i, nt), FFN_K, halo, t), wg_ref)
        hu_ = _conv_main(_main_taps(_extended(u_ref, up_ref, un_ref, i, nt), FFN_K, halo, t), wu_ref)
        a_ref[...] = (hg * _sigmoid(hg) * hu_).astype(BF16)

    specs = (_halo_specs(s, t, halo, cb, lambda j: j) + _halo_specs(s, t, halo, cb, lambda j: j + nc)
             + [pl.BlockSpec((FFN_K, cb), lambda j, i: (0, j)), pl.BlockSpec((FFN_K, cb), lambda j, i: (0, j + nc))])
    return pl.pallas_call(body, name=name, grid=(nc, nt), in_specs=specs,
                          out_specs=pl.BlockSpec((t, cb), lambda j, i: (i, j)),
                          out_shape=jax.ShapeDtypeStruct((s, f), BF16),
                          compiler_params=_params(("parallel", "parallel"), VMEM_BIG))(hu, hu, hu, hu, hu, hu, w, w)


def _ffn_mid_bwd(hu, w, da, *, name):
    s, f2 = hu.shape
    f = f2 // 2
    t, cb, halo = _tile(s, (FFN_ROW_TILE,)), _lane_tile(f, MM_MAX_TILE), SUBLANES
    nt, nc = s // t, f // cb

    def body(h_ref, hp_ref, hn_ref, a_ref, ap_ref, an_ref, w_ref, dh_ref, dw_ref):
        i = pl.program_id(1)

        @pl.when(i == 0)
        def _():
            dw_ref[...] = jnp.zeros_like(dw_ref)

        first, last = i == 0, i == nt - 1

        def extended(main, prev, nxt, cols):
            return jnp.concatenate([jnp.where(first, 0.0, prev[:, cols]), main[:, cols],
                                    jnp.where(last, 0.0, nxt[:, cols])], axis=0)

        for c in range(nc):
            gcols, ucols, acols = pl.ds(c * cb, cb), pl.ds(f + c * cb, cb), pl.ds(c * cb, cb)
            xg = extended(h_ref, hp_ref, hn_ref, gcols)
            xu = extended(h_ref, hp_ref, hn_ref, ucols)
            dae = extended(a_ref, ap_ref, an_ref, acols)
            wg, wu = w_ref.at[:, gcols], w_ref.at[:, ucols]
            xg_taps, xu_taps = _taps(xg, FFN_K), _taps(xu, FFN_K)
            hg = _conv(xg, wg, FFN_K, taps=xg_taps)
            hu_ = _conv(xu, wu, FFN_K, taps=xu_taps)
            sg = _sigmoid(hg)
            d_hg = dae * hu_ * (sg * (1.0 + hg * (1.0 - sg)))
            d_hu = dae * (hg * sg)
            dh_ref[:, gcols] = _conv(d_hg, wg, FFN_K, sign=-1)[halo:halo + t].astype(BF16)
            dh_ref[:, ucols] = _conv(d_hu, wu, FFN_K, sign=-1)[halo:halo + t].astype(BF16)
            _conv_wgrad(dw_ref.at[:, gcols], d_hg[halo:halo + t], xg, FFN_K, halo, t, taps=xg_taps)
            _conv_wgrad(dw_ref.at[:, ucols], d_hu[halo:halo + t], xu, FFN_K, halo, t, taps=xu_taps)

    whole = lambda j: 0
    specs = (_halo_specs(s, t, halo, f2, whole) + _halo_specs(s, t, halo, f, whole)
             + [pl.BlockSpec((FFN_K, f2), lambda j, i: (0, 0))])
    return pl.pallas_call(
        body, name=name, grid=(1, nt), in_specs=specs,
        out_specs=[pl.BlockSpec((t, f2), lambda j, i: (i, 0)), pl.BlockSpec((FFN_K, f2), lambda j, i: (0, 0))],
        out_shape=[jax.ShapeDtypeStruct((s, f2), BF16), jax.ShapeDtypeStruct((FFN_K, f2), F32)],
        compiler_params=_params(("arbitrary", "arbitrary"), VMEM_BIG))(hu, hu, hu, da, da, da, w)


CONV_HALO = 16


def _conv_mid_fwd(h, dw_w, dw_b, ln_g, ln_b, sc_w, *, name):
    s = h.shape[0]
    c = CONF_CH
    t, halo = _tile(s, (256, 128)), CONV_HALO
    nt = s // t

    def body(ca, cap, can, cg, cgp, cgn, gb, gc, gcp, gcn, hx, hxp, hxn, w31, b31, lg, lb, w3, o_ref, u2_ref):
        i = pl.program_id(1)
        u1 = _extended(ca, cap, can, i, nt) * _sigmoid(_extended(cg, cgp, cgn, i, nt))
        u2 = _conv_main(_main_taps(u1, CONF_K, halo, t), w31) + b31[...]
        u2_ref[...] = u2
        xh, _ = _ln_stats(u2)
        yl = xh * lg[...] + lb[...]
        o_ref[:, 0:c] = (yl * _sigmoid(yl)).astype(BF16)
        p = _extended(gc, gcp, gcn, i, nt) * _extended(hx, hxp, hxn, i, nt)
        o_ref[:, c:2 * c] = (gb[...] * _conv_main(_main_taps(p, SC_K, halo, t), w3)).astype(BF16)

    hs = lambda blk: _halo_specs(s, t, halo, c, lambda j: blk)
    vec = lambda r: pl.BlockSpec((r, c), lambda j, i: (0, 0))
    specs = hs(0) + hs(1) + hs(2)[:1] + hs(3) + hs(4) + [vec(CONF_K), vec(1), vec(1), vec(1), vec(SC_K)]
    return pl.pallas_call(
        body, name=name, grid=(1, nt), in_specs=specs,
        out_specs=[pl.BlockSpec((t, 2 * c), lambda j, i: (i, 0)), pl.BlockSpec((t, c), lambda j, i: (i, 0))],
        out_shape=[jax.ShapeDtypeStruct((s, 2 * c), BF16), jax.ShapeDtypeStruct((s, c), F32)],
        compiler_params=_params(("parallel", "parallel")))(
            h, h, h, h, h, h, h, h, h, h, h, h, h, dw_w, dw_b.reshape(1, c), ln_g.reshape(1, c),
            ln_b.reshape(1, c), sc_w)


def _conv_mid_bwd(h, u2, dm, dw_w, ln_g, ln_b, sc_w, *, name):
    s = h.shape[0]
    c = CONF_CH
    t, halo = _tile(s, (256, 128)), CONV_HALO
    nt = s // t

    def body(ca, cap, can, cg, cgp, cgn, gb, gbp, gbn, gc, gcp, gcn, hx, hxp, hxn, u2r, u2p, u2n,
             du, dup, dun, dz, dzp, dzn, w31, lg, lb, w3,
             dh_ref, dw31_ref, db31_ref, dlg_ref, dlb_ref, dw3_ref):
        i = pl.program_id(1)

        @pl.when(i == 0)
        def _():
            for r in (dw31_ref, db31_ref, dlg_ref, dlb_ref, dw3_ref):
                r[...] = jnp.zeros_like(r)

        main = slice(halo, halo + t)
        xh, rstd = _ln_stats(_extended(u2r, u2p, u2n, i, nt))
        yl = xh * lg[...] + lb[...]
        sg = _sigmoid(yl)
        d_yl = _extended(du, dup, dun, i, nt) * (sg * (1.0 + yl * (1.0 - sg)))
        dlg_ref[...] += jnp.sum((d_yl * xh)[main], axis=0, keepdims=True)
        dlb_ref[...] += jnp.sum(d_yl[main], axis=0, keepdims=True)
        dxh = d_yl * lg[...]
        du2 = rstd * (dxh - jnp.mean(dxh, axis=-1, keepdims=True) - xh * jnp.mean(dxh * xh, axis=-1, keepdims=True))
        db31_ref[...] += jnp.sum(du2[main], axis=0, keepdims=True)
        cae = _extended(ca, cap, can, i, nt)
        sc = _sigmoid(_extended(cg, cgp, cgn, i, nt))
        u1 = cae * sc
        _wgrad_main(dw31_ref, du2[main], _main_taps(u1, CONF_K, halo, t))
        du1 = _conv_main(_main_taps(du2, CONF_K, halo, t, sign=-1), w31)
        dh_ref[:, 0:c] = (du1 * sc[main]).astype(BF16)
        dh_ref[:, c:2 * c] = (du1 * (cae * sc * (1.0 - sc))[main]).astype(BF16)
        gce = _extended(gc, gcp, gcn, i, nt)
        hxe = _extended(hx, hxp, hxn, i, nt)
        p = gce * hxe
        dze = _extended(dz, dzp, dzn, i, nt)
        d_c3 = dze * _extended(gb, gbp, gbn, i, nt)
        p_taps = _main_taps(p, SC_K, halo, t)
        dh_ref[:, 2 * c:3 * c] = (dze[main] * _conv_main(p_taps, w3)).astype(BF16)
        _wgrad_main(dw3_ref, d_c3[main], p_taps)
        dp = _conv_main(_main_taps(d_c3, SC_K, halo, t, sign=-1), w3)
        dh_ref[:, 3 * c:4 * c] = (dp * hxe[main]).astype(BF16)
        dh_ref[:, 4 * c:5 * c] = (dp * gce[main]).astype(BF16)

    hs = lambda blk: _halo_specs(s, t, halo, c, lambda j: blk)
    vec = lambda r: pl.BlockSpec((r, c), lambda j, i: (0, 0))
    specs = (hs(0) + hs(1) + hs(2) + hs(3) + hs(4) + hs(0) + hs(0) + hs(1)
             + [vec(CONF_K), vec(1), vec(1), vec(SC_K)])
    outs = pl.pallas_call(
        body, name=name, grid=(1, nt), in_specs=specs,
        out_specs=[pl.BlockSpec((t, 5 * c), lambda j, i: (i, 0)), vec(CONF_K), vec(1), vec(1), vec(1), vec(SC_K)],
        out_shape=[jax.ShapeDtypeStruct((s, 5 * c), BF16), jax.ShapeDtypeStruct((CONF_K, c), F32),
                   jax.ShapeDtypeStruct((1, c), F32), jax.ShapeDtypeStruct((1, c), F32),
                   jax.ShapeDtypeStruct((1, c), F32), jax.ShapeDtypeStruct((SC_K, c), F32)],
        compiler_params=_params(("arbitrary", "arbitrary")))(
            h, h, h, h, h, h, h, h, h, h, h, h, h, h, h, u2, u2, u2, dm, dm, dm, dm, dm, dm,
            dw_w, ln_g.reshape(1, c), ln_b.reshape(1, c), sc_w)
    dh, dw31, db31, dlg, dlb, dw3 = outs
    return dh, dw31, db31.reshape(c), dlg.reshape(c), dlb.reshape(c), dw3


def _attn_mask(kind, n, tq, kw, length):
    pad = (kw - tq) // 2
    iq = lax.broadcasted_iota(jnp.int32, (tq, 1), 0)
    ik = lax.broadcasted_iota(jnp.int32, (1, kw), 1)
    if kind == "band":
        rel = ik - pad - iq
        kpos = n * tq - pad + ik
        return (jnp.abs(rel) <= DIL_HALF) & (kpos >= 0) & (kpos < length)
    rows = length // GRID_W
    rq = n * NA_QROWS + (iq >> GRID_SHIFT)
    cq = iq & (GRID_W - 1)
    rk = n * NA_QROWS - pad // GRID_W + (ik >> GRID_SHIFT)
    ck = ik & (GRID_W - 1)
    r0 = jnp.clip(rq - NA_KH // 2, 0, rows - NA_KH)
    c0 = jnp.clip(cq - NA_KW // 2, 0, GRID_W - NA_KW)
    return (rk >= r0) & (rk < r0 + NA_KH) & (ck >= c0) & (ck < c0 + NA_KW)


def _mask_tiles(kind, tq, kw, length, *, name):
    nb = length // tq

    def body(o_ref):
        v = pl.program_id(0)
        n = jnp.where((v == 1) | (v == 3), 0, jnp.where(v == 2, nb - 1, 1))
        o_ref[0] = jnp.where(_attn_mask(kind, n, tq, kw, length), 0.0, NEG)

    return pl.pallas_call(body, name=name, grid=(4,), out_specs=pl.BlockSpec((1, tq, kw), lambda v: (v, 0, 0)),
                          out_shape=jax.ShapeDtypeStruct((4, tq, kw), F32),
                          compiler_params=_params(("parallel",)))()


class _AttnGeom:
    def __init__(self, s, d, tq, nsub):
        self.s, self.d, self.tq, self.nsub = s, d, tq, nsub
        self.halo = tq * d
        self.rows = nsub * self.halo
        self.nbig = s // self.rows
        self.ext = self.rows + 2 * self.halo
        assert s % self.rows == 0

    def main(self, col):
        return pl.BlockSpec((self.rows, LANES), lambda hp, n: (n, col + hp))

    def with_halos(self, col):
        last = self.s // self.halo - 1
        return [self.main(col),
                pl.BlockSpec((self.halo, LANES), lambda hp, n: (jnp.maximum(n * self.nsub - 1, 0), col + hp)),
                pl.BlockSpec((self.halo, LANES), lambda hp, n: (jnp.minimum((n + 1) * self.nsub, last), col + hp))]

    def fill_ext(self, ext_ref, main_ref, prev_ref, next_ref):
        ext_ref[0:self.halo] = prev_ref[...].astype(F32)
        ext_ref[self.halo:self.halo + self.rows] = main_ref[...].astype(F32)
        ext_ref[self.halo + self.rows:self.ext] = next_ref[...].astype(F32)

    def rows_of(self, r, pos, count):
        start = r + pos * self.d
        return pl.ds(start, count, stride=self.d) if self.d > 1 else pl.ds(start, count)

    def variant(self, n, sub):
        v = 0
        if sub == 0:
            v = v + jnp.where(n == 0, 1, 0)
        if sub == self.nsub - 1:
            v = v + jnp.where(n == self.nbig - 1, 2, 0)
        return v


def _stack_heads(x, low):
    return jnp.concatenate([jnp.where(low, x, 0.0), jnp.where(low, 0.0, x)], axis=0)


def _attn_fwd(h, bias, mask, *, d, tq, nsub, cols, carry=None, name):
    geo = _AttnGeom(h.shape[0], d, tq, nsub)
    scale = HEAD_DIM ** -0.5
    kw = bias.shape[2]
    first_key = tq - (kw - tq) // 2

    ride = _Carried(carry, (4, geo.nbig))

    def body(*refs):
        (q_ref, km, kp, kn, vm, vp, vn, b_ref, m_ref), (o_ref, l_ref), (kext, vext) = ride.split(refs, 9, 2)
        ride.start()
        n = pl.program_id(1)
        geo.fill_ext(kext, km, kp, kn)
        geo.fill_ext(vext, vm, vp, vn)
        low = lax.broadcasted_iota(jnp.int32, (1, LANES), 1) < HEAD_DIM
        for r in range(d):
            for sub in range(nsub):
                madd = m_ref[geo.variant(n, sub)]
                q = q_ref[geo.rows_of(r, sub * tq, tq), :].astype(F32) * scale
                ks = kext[geo.rows_of(r, sub * tq + first_key, kw), :].astype(BF16)
                vs = vext[geo.rows_of(r, sub * tq + first_key, kw), :].astype(BF16)
                q2 = _stack_heads(q, low).astype(BF16)
                sc = (lax.dot_general(q2, ks, (((1,), (1,)), ((), ())), preferred_element_type=F32)
                      + (b_ref[...] + madd).reshape(2 * tq, kw))
                m = jnp.max(sc, axis=1, keepdims=True)
                p = jnp.exp(sc - m)
                den = jnp.sum(p, axis=1, keepdims=True)
                out2 = jnp.dot((p / den).astype(BF16), vs, preferred_element_type=F32)
                lse2 = m + jnp.log(den)
                o_ref[geo.rows_of(r, sub * tq, tq), :] = jnp.where(low, out2[0:tq], out2[tq:2 * tq])
                l_ref[geo.rows_of(r, sub * tq, tq), :] = jnp.where(low, lse2[0:tq], lse2[tq:2 * tq])
        ride.wait()

    qc, kc, vc = cols
    specs = ([geo.main(qc)] + geo.with_halos(kc) + geo.with_halos(vc)
             + [pl.BlockSpec((2, tq, kw), lambda hp, n: (hp, 0, 0)),
                pl.BlockSpec((4, tq, kw), lambda hp, n: (0, 0, 0))])
    shape = jax.ShapeDtypeStruct((geo.s, 4 * LANES), F32)
    ext = pltpu.VMEM((geo.ext, LANES), F32)
    out = pl.pallas_call(
        body, name=name, grid=(4, geo.nbig), in_specs=specs + ride.specs,
        out_specs=[geo.main(0), geo.main(0)] + ride.specs, out_shape=[shape, shape] + ride.out_shapes,
        scratch_shapes=[ext, ext] + ride.scratch,
        compiler_params=_params(("arbitrary", "arbitrary") if ride.on else ("parallel", "parallel"), VMEM_BIG))(
            h, h, h, h, h, h, h, bias, mask, *ride.operands)
    return out[0], out[1], (out[2] if ride.on else None)


def _attn_bwd(h, bias, mask, dy, y, lse, *, d, tq, nsub, cols, ycol, name):
    geo = _AttnGeom(h.shape[0], d, tq, nsub)
    scale = HEAD_DIM ** -0.5
    halo, rows = geo.halo, geo.rows
    kw = bias.shape[2]
    first_key = tq - (kw - tq) // 2

    def body(q_ref, km, kp, kn, vm, vp, vn, b_ref, m_ref, dy_ref, y_ref, l_ref, dq_ref, dk_hbm, dv_hbm, db_ref,
             kext, vext, dkext, dvext, dk_all, dv_all, sems):
        hp, n = pl.program_id(0), pl.program_id(1)

        @pl.when(n == 0)
        def _():
            dk_all[...] = jnp.zeros_like(dk_all)
            dv_all[...] = jnp.zeros_like(dv_all)
            db_ref[...] = jnp.zeros_like(db_ref)

        geo.fill_ext(kext, km, kp, kn)
        geo.fill_ext(vext, vm, vp, vn)
        dkext[...] = jnp.zeros_like(dkext)
        dvext[...] = jnp.zeros_like(dvext)
        low = lax.broadcasted_iota(jnp.int32, (1, LANES), 1) < HEAD_DIM
        for r in range(d):
            for sub in range(nsub):
                madd = m_ref[geo.variant(n, sub)]
                mine = geo.rows_of(r, sub * tq, tq)
                keys = geo.rows_of(r, sub * tq + first_key, kw)
                q = q_ref[mine, :].astype(F32) * scale
                ks = kext[keys, :].astype(BF16)
                vs = vext[keys, :].astype(BF16)
                dyv = dy_ref[mine, :]
                dyy = dyv * y_ref[mine, :]
                lse_all = l_ref[mine, :]
                q2 = _stack_heads(q, low).astype(BF16)
                dy2 = _stack_heads(dyv, low).astype(BF16)
                dsum = jnp.concatenate([jnp.sum(jnp.where(sel, dyy, 0.0), axis=1, keepdims=True)
                                        for sel in (low, ~low)], axis=0)
                lse2 = jnp.concatenate([lse_all[:, 0:1], lse_all[:, HEAD_DIM:HEAD_DIM + 1]], axis=0)
                sc = (lax.dot_general(q2, ks, (((1,), (1,)), ((), ())), preferred_element_type=F32)
                      + (b_ref[...] + madd).reshape(2 * tq, kw))
                p = jnp.exp(sc - lse2)
                dp = lax.dot_general(dy2, vs, (((1,), (1,)), ((), ())), preferred_element_type=F32)
                ds = p * (dp - dsum)
                db_ref[...] += ds.reshape(2, tq, kw)
                pb, dsb = p.astype(BF16), ds.astype(BF16)
                dv = lax.dot_general(pb, dy2, (((0,), (0,)), ((), ())), preferred_element_type=F32)
                dk = lax.dot_general(dsb, q2, (((0,), (0,)), ((), ())), preferred_element_type=F32)
                dq2 = jnp.dot(dsb, ks, preferred_element_type=F32)
                dq_ref[mine, :] = jnp.where(low, dq2[0:tq], dq2[tq:2 * tq]) * scale
                dkext[keys, :] += dk
                dvext[keys, :] += dv

        before = pl.multiple_of(jnp.maximum(n * rows - halo, 0), LANES)
        here = pl.multiple_of(n * rows, LANES)
        after = pl.multiple_of(jnp.minimum((n + 1) * rows, geo.s - halo), LANES)
        for ext, total in ((dkext, dk_all), (dvext, dv_all)):
            total[pl.ds(before, halo), :] += ext[0:halo]
            total[pl.ds(here, rows), :] += ext[halo:halo + rows]
            total[pl.ds(after, halo), :] += ext[halo + rows:geo.ext]

        @pl.when(n == geo.nbig - 1)
        def _():
            col = pl.ds(pl.multiple_of(hp * LANES, LANES), LANES)
            copies = [pltpu.make_async_copy(total, out.at[:, col], sems.at[i])
                      for i, (total, out) in enumerate(((dk_all, dk_hbm), (dv_all, dv_hbm)))]
            for cp in copies:
                cp.start()
            for cp in copies:
                cp.wait()

    qc, kc, vc = cols
    bspec = pl.BlockSpec((2, tq, kw), lambda hp, n: (hp, 0, 0))
    any_spec = pl.BlockSpec(memory_space=pl.ANY)
    specs = ([geo.main(qc)] + geo.with_halos(kc) + geo.with_halos(vc)
             + [bspec, pl.BlockSpec((4, tq, kw), lambda hp, n: (0, 0, 0)), geo.main(ycol), geo.main(ycol), geo.main(0)])
    shape = jax.ShapeDtypeStruct((geo.s, 4 * LANES), F32)
    ext = pltpu.VMEM((geo.ext, LANES), F32)
    whole = pltpu.VMEM((geo.s, LANES), F32)
    return pl.pallas_call(
        body, name=name, grid=(4, geo.nbig), in_specs=specs, out_specs=[geo.main(0), any_spec, any_spec, bspec],
        out_shape=[shape, shape, shape, jax.ShapeDtypeStruct(bias.shape, F32)],
        scratch_shapes=[ext, ext, ext, ext, whole, whole, pltpu.SemaphoreType.DMA((2,))],
        compiler_params=_params(("arbitrary", "arbitrary"), VMEM_BIG))(
            h, h, h, h, h, h, h, bias, mask, dy, y, lse)


def _dil_combine(o_na, outs, lses, *, name):
    s, c = outs[0].shape
    t = _tile(s, (512, 256, 128, 64, 8))

    def body(na, o0, o1, o2, l0, l1, l2, y_ref, y16_ref, lt_ref):
        ls = [l0[...], l1[...], l2[...]]
        m = jnp.maximum(jnp.maximum(ls[0], ls[1]), ls[2])
        es = [jnp.exp(l - m) for l in ls]
        den = es[0] + es[1] + es[2]
        y = (es[0] / den) * o0[...] + (es[1] / den) * o1[...] + (es[2] / den) * o2[...]
        lt_ref[...] = m + jnp.log(den)
        y_ref[:, 0:c] = na[...]
        y_ref[:, c:2 * c] = y
        y16_ref[:, 0:c] = na[...].astype(BF16)
        y16_ref[:, c:2 * c] = y.astype(BF16)

    row = pl.BlockSpec((t, c), lambda i: (i, 0))
    wide = pl.BlockSpec((t, 2 * c), lambda i: (i, 0))
    return pl.pallas_call(body, name=name, grid=(s // t,), in_specs=[row] * 7, out_specs=[wide, wide, row],
                          out_shape=[jax.ShapeDtypeStruct((s, 2 * c), F32), jax.ShapeDtypeStruct((s, 2 * c), BF16),
                                     jax.ShapeDtypeStruct((s, c), F32)],
                          compiler_params=_params(("parallel",)))(o_na, *outs, *lses)


def _attn_dh(na, dil, *, name):
    s, c = na[0].shape
    t = _tile(s, (256, 128, 64, 8))

    def body(*refs):
        ins, o_ref = refs[:-1], refs[-1]
        for a in range(3):
            o_ref[:, a * c:(a + 1) * c] = ins[a][...].astype(BF16)
            o_ref[:, (3 + a) * c:(4 + a) * c] = (ins[3 + a][...] + ins[6 + a][...] + ins[9 + a][...]).astype(BF16)

    row = pl.BlockSpec((t, c), lambda i: (i, 0))
    flat = list(na) + [g[a] for g in dil for a in range(3)]
    return pl.pallas_call(body, name=name, grid=(s // t,), in_specs=[row] * 12,
                          out_specs=pl.BlockSpec((t, 6 * c), lambda i: (i, 0)),
                          out_shape=jax.ShapeDtypeStruct((s, 6 * c), BF16),
                          compiler_params=_params(("parallel",)))(*flat)


def _t5_bucket(rel):
    nb = N_BUCKETS // 2
    max_exact = nb // 2
    ret = np.where(rel > 0, nb, 0)
    n = np.abs(rel)
    large = max_exact + (np.log(np.maximum(n, 1).astype(np.float32) / np.float32(max_exact))
                         / np.float32(math.log(T5_MAX_DIST / max_exact)) * np.float32(nb - max_exact)).astype(np.int32)
    large = np.minimum(large, nb - 1)
    return (ret + np.where(n < max_exact, n, large)).astype(np.int32)


def _band_bucket_index(dil):
    tq, kw = DIL_TQ, DIL_KW
    rel = np.arange(kw)[None, :] - (kw - tq) // 2 - np.arange(tq)[:, None]
    return _t5_bucket(rel * dil)


def _band_bias(t5, dil, *, name):
    tq, kw = DIL_TQ, DIL_KW
    buckets = [int(b) for b in _t5_bucket(np.arange(-DIL_HALF, DIL_HALF + 1) * dil)]

    def body(t_ref, o_ref):
        hh = pl.program_id(0)
        rel = (lax.broadcasted_iota(jnp.int32, (tq, kw), 1) - (kw - tq) // 2
               - lax.broadcasted_iota(jnp.int32, (tq, kw), 0))
        acc = jnp.zeros((tq, kw), F32)
        for r, b in zip(range(-DIL_HALF, DIL_HALF + 1), buckets):
            acc = jnp.where(rel == r, t_ref[b * 8 + hh], acc)
        o_ref[0] = acc

    return pl.pallas_call(body, name=name, grid=(8,),
                          in_specs=[pl.BlockSpec(memory_space=pltpu.SMEM)],
                          out_specs=pl.BlockSpec((1, tq, kw), lambda h: (h, 0, 0)),
                          out_shape=jax.ShapeDtypeStruct((8, tq, kw), F32),
                          compiler_params=_params(("parallel",)))(t5.reshape(-1))


def _na_bias(rpb, *, name):
    nr, nc = 2 * NA_KH - 1, 2 * NA_KW - 1
    tq = NA_QROWS * GRID_W
    w = GRID_W

    def body(r_ref, o_ref):
        base = pl.program_id(0) * (nr * nc)
        lane = lax.broadcasted_iota(jnp.int32, (w, LANES), 1)
        upper = lane >= w
        diff = (lane & (w - 1)) - lax.broadcasted_iota(jnp.int32, (w, LANES), 0) + NA_KW - 1
        tiles = {}
        for i in range(NA_QROWS):
            for m in range(3 * NA_QROWS // 2):
                lo = 2 * m - i + NA_KH - 1 - NA_QROWS
                if lo not in tiles:
                    acc = jnp.zeros((w, LANES), F32)
                    for dc in range(nc):
                        v_lo = r_ref[base + lo * nc + dc] if 0 <= lo < nr else 0.0
                        v_hi = r_ref[base + (lo + 1) * nc + dc] if 0 <= lo + 1 < nr else 0.0
                        acc = jnp.where(diff == dc, jnp.where(upper, v_hi, v_lo), acc)
                    tiles[lo] = acc
                o_ref[0, i * w:(i + 1) * w, m * LANES:(m + 1) * LANES] = tiles[lo]

    return pl.pallas_call(body, name=name, grid=(8,),
                          in_specs=[pl.BlockSpec(memory_space=pltpu.SMEM)],
                          out_specs=pl.BlockSpec((1, tq, 3 * tq), lambda h: (h, 0, 0)),
                          out_shape=jax.ShapeDtypeStruct((8, tq, 3 * tq), F32),
                          compiler_params=_params(("parallel",)))(rpb.reshape(-1))


def _t5_grad(dbs, idxs, *, name):
    def body(d0, d1, d2, i0, i1, i2, o_ref):
        lane = lax.broadcasted_iota(jnp.int32, (1, LANES), 1)
        lines = [jnp.zeros((1, LANES), F32) for _ in range(8)]
        for dref, iref in ((d0, i0), (d1, i1), (d2, i2)):
            idx = iref[...]
            for hh in range(8):
                xh = dref[hh]
                for b in range(N_BUCKETS):
                    val = jnp.sum(jnp.sum(jnp.where(idx == b, xh, 0.0), axis=1, keepdims=True), axis=0, keepdims=True)
                    lines[hh] = lines[hh] + jnp.where(lane == b, val, 0.0)
        for hh in range(8):
            o_ref[hh:hh + 1, :] = lines[hh]

    out = pl.pallas_call(body, name=name, out_shape=jax.ShapeDtypeStruct((8, LANES), F32))(*dbs, *idxs)
    return out[:, :N_BUCKETS].T


def _rpb_grad(db, *, name):
    nr, nc = 2 * NA_KH - 1, 2 * NA_KW - 1
    tq = NA_QROWS * GRID_W
    w = GRID_W

    def body(d_ref, o_ref):
        x = d_ref[0]
        rows = []
        for dr in range(nr):
            acc = jnp.zeros((w, w), F32)
            for i in range(NA_QROWS):
                j = i + dr - (NA_KH - 1 - NA_QROWS)
                if 0 <= j < 3 * NA_QROWS:
                    acc = acc + x[i * w:(i + 1) * w, j * w:(j + 1) * w]
            rows.append(acc)
        diff = (lax.broadcasted_iota(jnp.int32, (w, w), 1) - lax.broadcasted_iota(jnp.int32, (w, w), 0)
                + NA_KW - 1)
        lane = lax.broadcasted_iota(jnp.int32, (1, LANES), 1)
        for dr in range(nr):
            line = jnp.zeros((1, LANES), F32)
            for dc in range(nc):
                val = jnp.sum(jnp.sum(jnp.where(diff == dc, rows[dr], 0.0), axis=1, keepdims=True),
                              axis=0, keepdims=True)
                line = jnp.where(lane == dc, val, line)
            o_ref[0, dr:dr + 1, :] = line

    out = pl.pallas_call(body, name=name, grid=(8,),
                         in_specs=[pl.BlockSpec((1, tq, 3 * tq), lambda h: (h, 0, 0))],
                         out_specs=pl.BlockSpec((1, nr, LANES), lambda h: (h, 0, 0)),
                         out_shape=jax.ShapeDtypeStruct((8, nr, LANES), F32),
                         compiler_params=_params(("parallel",)))(db)
    return out[:, :, :nc]


def _exchange(src, *, gather, name):
    shape = src.shape if not gather else (N_DEV,) + src.shape

    def body(src_ref, out_ref, send_sems, recv_sems, local_sem):
        _exchange_start(src_ref, out_ref, send_sems, recv_sems, local_sem, gather)
        _exchange_wait(src_ref, out_ref, send_sems, recv_sems, local_sem, gather)

    any_spec = pl.BlockSpec(memory_space=pl.ANY)
    return pl.pallas_call(
        body, name=name, in_specs=[any_spec], out_specs=any_spec, out_shape=jax.ShapeDtypeStruct(shape, src.dtype),
        scratch_shapes=_exchange_sems())(src)


class _Carried:
    def __init__(self, carry, grid):
        self.src, self.gather = carry if carry is not None else (None, False)
        self.on = self.src is not None
        self.grid = grid
        self.operands = [self.src] if self.on else []
        self.specs = [pl.BlockSpec(memory_space=pl.ANY)] if self.on else []
        self.scratch = _exchange_sems() if self.on else []
        self.out_shapes = []
        if self.on:
            shape = ((N_DEV,) + self.src.shape) if self.gather else self.src.shape
            self.out_shapes = [jax.ShapeDtypeStruct(shape, self.src.dtype)]

    def split(self, refs, n_in, n_out):
        refs = list(refs)
        if not self.on:
            return refs[:n_in], refs[n_in:n_in + n_out], refs[n_in + n_out:]
        self.refs = (refs[n_in], refs[n_in + 1 + n_out], *refs[-3:], self.gather)
        return refs[:n_in], refs[n_in + 1:n_in + 1 + n_out], refs[n_in + 2 + n_out:-3]

    def _at(self, last):
        hit = None
        for ax, size in enumerate(self.grid):
            here = pl.program_id(ax) == (size - 1 if last else 0)
            hit = here if hit is None else hit & here
        return hit

    def start(self):
        if self.on:
            pl.when(self._at(False))(lambda: _exchange_start(*self.refs))

    def wait(self):
        if self.on:
            pl.when(self._at(True))(lambda: _exchange_wait(*self.refs))


def _exchange_sems():
    return [pltpu.SemaphoreType.DMA((N_DEV - 1,)), pltpu.SemaphoreType.DMA((N_DEV - 1,)), pltpu.SemaphoreType.DMA]


def _exchange_copies(src_ref, out_ref, send_sems, recv_sems, local_sem, gather):
    x, y, c = lax.axis_index("x"), lax.axis_index("y"), lax.axis_index("c")
    me = 4 * x + 2 * y + c

    def outgoing(p):
        return src_ref if gather else src_ref.at[p]

    own = pltpu.make_async_copy(outgoing(me), out_ref.at[me], local_sem)
    sends, recvs = [], []
    for k in range(1, N_DEV):
        px = 1 - x if k & 4 else x
        py = 1 - y if k & 2 else y
        pc = 1 - c if k & 1 else c
        p = 4 * px + 2 * py + pc
        for dst, group in ((me, sends), (p, recvs)):
            group.append(pltpu.make_async_remote_copy(
                src_ref=outgoing(p), dst_ref=out_ref.at[dst], send_sem=send_sems.at[k - 1],
                recv_sem=recv_sems.at[k - 1], device_id=(px, py, pc), device_id_type=pl.DeviceIdType.MESH))
    return own, sends, recvs


def _exchange_start(*refs_and_mode):
    own, sends, _ = _exchange_copies(*refs_and_mode)
    own.start()
    for cp in sends:
        cp.start()


def _exchange_wait(*refs_and_mode):
    own, sends, recvs = _exchange_copies(*refs_and_mode)
    for cp in recvs:
        cp.wait_recv()
    for cp in sends:
        cp.wait_send()
    own.wait()


def _adamw(parts, w, m, v, *, name):
    layers = len(parts)
    rows, cols = w.shape
    per_layer = rows // layers
    t = _tile(per_layer, (FLAT_ROW_TILE, 128, 64, 32, 16, 8))
    nt = per_layer // t

    def body(*refs):
        p_refs = refs[:layers]
        w_ref, m_ref, v_ref, g_ref, d_ref, nm_ref, nv_ref = refs[layers:]
        layer = pl.program_id(0)
        g = None
        for l, p_ref in enumerate(p_refs):
            total = p_ref[0].astype(F32)
            for k in range(1, N_DEV):
                total = total + p_ref[k].astype(F32)
            g = total if g is None else jnp.where(layer == l, total, g)
        nm = ADAM_B1 * m_ref[...] + (1.0 - ADAM_B1) * g
        nv = ADAM_B2 * v_ref[...] + (1.0 - ADAM_B2) * (g * g)
        m_hat = nm / (1.0 - ADAM_B1 ** ADAM_STEP)
        v_hat = nv / (1.0 - ADAM_B2 ** ADAM_STEP)
        g_ref[...] = g
        d_ref[...] = -ADAM_LR * (m_hat / (jnp.sqrt(v_hat) + ADAM_EPS) + ADAM_WD * w_ref[...])
        nm_ref[...] = nm
        nv_ref[...] = nv

    def part_spec(l):
        return pl.BlockSpec((N_DEV, t, cols), lambda layer, i: (0, jnp.where(layer == l, i, 0), 0))

    row = pl.BlockSpec((t, cols), lambda layer, i: (layer * nt + i, 0))
    shape = jax.ShapeDtypeStruct((rows, cols), F32)
    return pl.pallas_call(body, name=name, grid=(layers, nt),
                          in_specs=[part_spec(l) for l in range(layers)] + [row, row, row],
                          out_specs=[row] * 4, out_shape=[shape] * 4,
                          compiler_params=_params(("parallel", "parallel"), VMEM_BIG))(*parts, w, m, v)


def _flatten(arrays, dtype, row_mult):
    flat = jnp.concatenate([a.reshape(-1).astype(dtype) for a in arrays])
    chunk = FLAT_COLS * row_mult
    padded = -(-flat.shape[0] // chunk) * chunk
    return jnp.pad(flat, (0, padded - flat.shape[0])).reshape(padded // FLAT_COLS, FLAT_COLS)


def _unflatten(flat, shapes):
    flat = flat.reshape(-1)
    out, pos = [], 0
    for shp in shapes:
        size = int(np.prod(shp))
        out.append(flat[pos:pos + size].reshape(shp))
        pos += size
    return out


def _gather_full(names, local, dtype, row_mult, label):
    got = _exchange(_flatten([local[n] for n in names], dtype, row_mult), gather=True, name=label)
    got = got.reshape(N_DEV, -1)
    full, pos = {}, 0
    for n in names:
        shp = local[n].shape
        size = int(np.prod(shp))
        piece = got[:, pos:pos + size].reshape(N_DEV, size // shp[-1], shp[-1])
        full[n] = jnp.transpose(piece, (1, 0, 2)).reshape(shp[:-1] + (N_DEV * shp[-1],))
        pos += size
    return full


def _scatter_rows(names, grads, row_mult):
    pieces = []
    for n in names:
        c = grads[n].shape[-1] // N_DEV
        pieces.append(jnp.transpose(grads[n].reshape(-1, N_DEV, c), (1, 0, 2)).reshape(N_DEV, -1))
    flat = jnp.concatenate(pieces, axis=1)
    chunk = FLAT_COLS * row_mult
    padded = -(-flat.shape[1] // chunk) * chunk
    return jnp.pad(flat, ((0, 0), (0, padded - flat.shape[1]))).reshape(N_DEV, padded // FLAT_COLS, FLAT_COLS)


def _from_shards(stacked, axis):
    _, a, b = stacked.shape
    if axis == 1:
        return jnp.transpose(stacked, (1, 0, 2)).reshape(a, N_DEV * b)
    return stacked.reshape(N_DEV * a, b)


def _to_shards(full, axis):
    ra, rb = full.shape
    if axis == 1:
        return jnp.transpose(full.reshape(ra, N_DEV, rb // N_DEV), (1, 0, 2))
    return full.reshape(N_DEV, ra // N_DEV, rb)


class _ShardedMatmulWeights:
    def __init__(self, local):
        self.local, self.full, self.parts = local, {}, {}

    def gather_src(self, n, l):
        return self.local[n][l].astype(BF16)

    def set_gathered(self, n, l, got):
        self.full[n, l] = _from_shards(got, SHARD_AXIS[n] - 1)

    def get(self, n, l):
        return self.full[n, l]

    def scatter_src(self, n, l, dw):
        return _to_shards(dw, SHARD_AXIS[n] - 1).astype(BF16)

    def set_scattered(self, n, l, parts):
        self.parts[n, l] = parts


def _role(role, i):
    mixer = 'attn_w_' if i % 2 == 0 else 'conv_w_'
    return {'in': (mixer + 'in', i // 2), 'out': (mixer + 'out', i // 2),
            'up': ('ffn_w_up', i), 'down': ('ffn_w_down', i)}[role]


def _local_step(x, tgt, w, big):
    s = x.shape[0]

    def project(a, role, i, ln=None):
        carry = None
        if i + 1 < DEPTH:
            nxt = _role(role, i + 1)
            carry = (big.gather_src(*nxt), True)
        weight = big.get(*_role(role, i))
        if ln is None:
            out, got = _mm(a, weight, carry=carry, name=role + "_fwd")
        else:
            *out, got = _mm_ln(a, weight, *ln, carry=carry, name=role + "_fwd")
        if got is not None:
            big.set_gathered(*nxt, got)
        return out

    def project_back(a, d_out, role, i, ln=None):
        key = _role(role, i)
        dw, _ = _mm(a, d_out, ta=True, name=role + "_dw")
        carry = (big.scatter_src(*key, dw), False)
        if ln is None:
            d_in, parts = _mm(d_out, big.get(*key), tb=True, carry=carry, name=role + "_dx")
        else:
            *d_in, parts = _mm_ln_bwd(d_out, big.get(*key), *ln, carry=carry, name=role + "_dx")
        if parts is not None:
            big.set_scattered(*key, parts)
        return d_in

    na_tq = NA_QROWS * GRID_W
    band_idx = [_band_bucket_index(d) for _, d in DIL_PATTERNS]
    band_bias = [_band_bias(w['t5_bias'], d, name=f"band_bias_{d}") for _, d in DIL_PATTERNS]
    band_mask = [_mask_tiles("band", DIL_TQ, DIL_KW, s // d, name=f"band_mask_{d}") for _, d in DIL_PATTERNS]
    na_mask = _mask_tiles("na", na_tq, 3 * na_tq, s, name="na_mask")
    na_cols, dil_cols = (0, 4, 8), (12, 16, 20)
    grads = {n: [None] * w[n].shape[0] for n in SMALL_SHARDED + REPLICATED if n != 't5_bias'}
    saved = []
    x16 = x.astype(BF16)

    for i in range(DEPTH):
        j = i // 2
        st = {'x': x, 'x16': x16}
        if i % 2 == 0:
            h = project(x16, 'in', i)
            na_bias = _na_bias(w['na_rpb'][j], name="na_bias")
            late = [_role(role, 0) for role in ('up', 'down')] if i == 0 else []
            rides = [(key, (big.gather_src(*key), True)) for key in late] + [(None, None)] * 4
            outs, lses = [], []
            calls = [(na_bias, na_mask, 1, na_tq, NA_NSUB, na_cols, "na_fwd")] + [
                (bias, mask, d, DIL_TQ, DIL_NSUB[d], dil_cols, f"dil_fwd_{d}")
                for (_, d), bias, mask in zip(DIL_PATTERNS, band_bias, band_mask)]
            for (bias, mask, d, tq, nsub, cols, label), (key, carry) in zip(calls, rides):
                o, l, got = _attn_fwd(h, bias, mask, d=d, tq=tq, nsub=nsub, cols=cols, carry=carry, name=label)
                if got is not None:
                    big.set_gathered(*key, got)
                outs.append(o)
                lses.append(l)
            o_na, l_na = outs.pop(0), lses.pop(0)
            mid, mid16, l_dil = _dil_combine(o_na, outs, lses, name="dil_combine")
            st.update(h=h, mid=mid, mid16=mid16, l_na=l_na, l_dil=l_dil, na_bias=na_bias)
        else:
            h = project(x16, 'in', i)
            mid16, u2 = _conv_mid_fwd(h, w['conf_dw_w'][j], w['conf_dw_b'][j], w['conf_ln_g'][j], w['conf_ln_b'][j],
                                      w['sconv_w'][j], name="conv_mid_fwd")
            st.update(h=h, mid16=mid16, u2=u2)
        z_mix, xa, xa16 = project(mid16, 'out', i, ln=(x, w['mix_ln_g'][i], w['mix_ln_b'][i]))
        hu = project(xa16, 'up', i)
        act16 = _ffn_mid_fwd(hu, w['ffn_dw_w'][i], name="ffn_mid_fwd")
        z_ffn, xb, xb16 = project(act16, 'down', i, ln=(xa, w['ffn_ln_g'][i], w['ffn_ln_b'][i]))
        st.update(z_mix=z_mix, xa16=xa16, hu=hu, act16=act16, z_ffn=z_ffn)
        saved.append(st)
        x, x16 = xb, xb16

    loss, d_loss = _loss_head(x, tgt, name="loss_head")
    g_t5 = None
    dz, dz16, dg, db = _ln_bwd(saved[-1]['z_ffn'], w['ffn_ln_g'][-1], d_loss, None, name="last_ln_bwd")
    for i in reversed(range(DEPTH)):
        j = i // 2
        st = saved[i]
        grads['ffn_ln_g'][i], grads['ffn_ln_b'][i] = dg, db
        dact = project_back(st['act16'], dz16, 'down', i)
        dhu, grads['ffn_dw_w'][i] = _ffn_mid_bwd(st['hu'], w['ffn_dw_w'][i], dact, name="ffn_mid_bwd")
        dz, dz1, dg, db = project_back(st['xa16'], dhu, 'up', i, ln=(st['z_mix'], w['mix_ln_g'][i], dz))
        grads['mix_ln_g'][i], grads['mix_ln_b'][i] = dg, db
        dmid = project_back(st['mid16'], dz1, 'out', i)
        if i % 2 == 0:
            h = st['h']
            dq, dk, dv, dbias = _attn_bwd(h, st['na_bias'], na_mask, dmid, st['mid'], st['l_na'], d=1, tq=na_tq,
                                          nsub=NA_NSUB, cols=na_cols, ycol=0, name="na_bwd")
            grads['na_rpb'][j] = _rpb_grad(dbias, name="rpb_grad")
            dil, dbs = [], []
            for (_, d), bias, mask in zip(DIL_PATTERNS, band_bias, band_mask):
                g = _attn_bwd(h, bias, mask, dmid, st['mid'], st['l_dil'], d=d, tq=DIL_TQ, nsub=DIL_NSUB[d],
                              cols=dil_cols, ycol=4, name=f"dil_bwd_{d}")
                dil.append(g[:3])
                dbs.append(g[3])
            t5 = _t5_grad(dbs, band_idx, name="t5_grad")
            g_t5 = t5 if g_t5 is None else g_t5 + t5
            dh = _attn_dh((dq, dk, dv), dil, name="attn_dh")
        else:
            dh, dw31, db31, dlg, dlb, dw3 = _conv_mid_bwd(st['h'], st['u2'], dmid, w['conf_dw_w'][j],
                                                          w['conf_ln_g'][j], w['conf_ln_b'][j], w['sconv_w'][j],
                                                          name="conv_mid_bwd")
            grads['conf_dw_w'][j], grads['conf_dw_b'][j] = dw31, db31
            grads['conf_ln_g'][j], grads['conf_ln_b'][j], grads['sconv_w'][j] = dlg, dlb, dw3
        if i > 0:
            below = saved[i - 1]
            dz, dz16, dg, db = project_back(st['x16'], dh, 'in', i, ln=(below['z_ffn'], w['ffn_ln_g'][i - 1], dz))
        else:
            dx = _axpy(dz, project_back(st['x16'], dh, 'in', i), name="grad_x")
    full = {n: jnp.stack(g) for n, g in grads.items()}
    full['t5_bias'] = g_t5
    return loss, dx, full


def kernel(x, t5_bias, attn_w_in, attn_w_out, na_rpb, conv_w_in, conf_dw_w, conf_dw_b, conf_ln_g, conf_ln_b, sconv_w, conv_w_out, ffn_w_up, ffn_dw_w, ffn_w_down, mix_ln_g, mix_ln_b, ffn_ln_g, ffn_ln_b, loss_target, m_t5_bias, m_attn_w_in, m_attn_w_out, m_na_rpb, m_conv_w_in, m_conf_dw_w, m_conf_dw_b, m_conf_ln_g, m_conf_ln_b, m_sconv_w, m_conv_w_out, m_ffn_w_up, m_ffn_dw_w, m_ffn_w_down, m_mix_ln_g, m_mix_ln_b, m_ffn_ln_g, m_ffn_ln_b, v_t5_bias, v_attn_w_in, v_attn_w_out, v_na_rpb, v_conv_w_in, v_conf_dw_w, v_conf_dw_b, v_conf_ln_g, v_conf_ln_b, v_sconv_w, v_conv_w_out, v_ffn_w_up, v_ffn_dw_w, v_ffn_w_down, v_mix_ln_g, v_mix_ln_b, v_ffn_ln_g, v_ffn_ln_b):
    args = dict(locals())
    local = {n: args[n] for n in WEIGHTS}
    mom1 = {n: args['m_' + n] for n in WEIGHTS}
    mom2 = {n: args['v_' + n] for n in WEIGHTS}

    kinds = ('grad', 'delta', 'new_m', 'new_v')
    small = {n: local[n] for n in REPLICATED}
    small.update(_gather_full(SMALL_SHARDED, local, F32, 8, "gather_small_weights"))
    big = _ShardedMatmulWeights({n: local[n] for n in MATMUL_WEIGHTS})
    for role in ('in', 'out'):
        n, l = _role(role, 0)
        big.set_gathered(n, l, _exchange(big.gather_src(n, l), gather=True, name="gather_first_" + role))

    loss, dx, grads = _local_step(x[0], loss_target[0], small, big)
    loss = lax.psum(loss, MESH_AXES)

    out = {}
    for n in MATMUL_WEIGHTS:
        layers, a, b = local[n].shape
        res = _adamw([big.parts[n, l] for l in range(layers)],
                     *[t.reshape(layers * a, b) for t in (local[n], mom1[n], mom2[n])], name="adamw_" + n)
        for kind, r in zip(kinds, res):
            out[kind + '_' + n] = r.reshape(layers, a, b)
    for names, sharded, label in ((SMALL_SHARDED, True, "small"), (REPLICATED, False, "replicated")):
        shapes = [local[n].shape for n in names]
        if sharded:
            parts = _exchange(_scatter_rows(names, grads, 8), gather=False, name="scatter_small_grads")
        else:
            parts = _exchange(_flatten([grads[n] for n in names], F32, 8), gather=True, name="gather_replicated_grads")
        res = _adamw([parts], _flatten([local[n] for n in names], F32, 8), _flatten([mom1[n] for n in names], F32, 8),
                     _flatten([mom2[n] for n in names], F32, 8), name="adamw_" + label)
        for kind, flat in zip(kinds, res):
            for n, a in zip(names, _unflatten(flat, shapes)):
                out[kind + '_' + n] = a

    return (loss, dx[None], *[out[k + '_' + n] for k in ('grad', 'delta', 'new_m', 'new_v') for n in WEIGHTS])
```

```python
import functools
import math

import jax
import jax.numpy as jnp
import numpy as np
from jax import lax
from jax.experimental import pallas as pl
from jax.experimental.pallas import tpu as pltpu

F32 = jnp.float32
BF16 = jnp.bfloat16

N_DEV = 8
MESH_AXES = ("x", "y", "c")
DEPTH = 4
GRID_W = 64
GRID_SHIFT = 6
HEAD_DIM = 64
NA_KH = 8
NA_KW = 16
NA_QROWS = 4
DIL_PATTERNS = ((128, 1), (512, 4), (2048, 16))
DIL_HALF = 64
DIL_TQ = 128
DIL_KW = DIL_TQ + 2 * DIL_HALF
DIL_NSUB = {1: 8, 4: 2, 16: 1}
NA_NSUB = 4
N_BUCKETS = 32
T5_MAX_DIST = 1024
CONF_CH = 512
CONF_K = 31
SC_K = 3
FFN_K = 3
FFN_ROW_TILE = 128
FFN_STRIP = 128
LN_EPS = 1e-5
NEG = -1e30
ALPHA = (2 * DEPTH) ** 0.25
ADAM_LR = 0.001
ADAM_B1 = 0.9
ADAM_B2 = 0.999
ADAM_EPS = 1e-08
ADAM_WD = 0.01
ADAM_STEP = 10

LANES = 128
SUBLANES = 8
VMEM_BIG = 48 * 1024 * 1024
FLAT_COLS = 1024
FLAT_ROW_TILE = 256

WEIGHTS = ['t5_bias', 'attn_w_in', 'attn_w_out', 'na_rpb', 'conv_w_in', 'conf_dw_w', 'conf_dw_b', 'conf_ln_g',
           'conf_ln_b', 'sconv_w', 'conv_w_out', 'ffn_w_up', 'ffn_dw_w', 'ffn_w_down', 'mix_ln_g', 'mix_ln_b',
           'ffn_ln_g', 'ffn_ln_b']
SHARD_AXIS = {'attn_w_in': 2, 'attn_w_out': 1, 'conv_w_in': 2, 'conf_dw_w': 2, 'conf_dw_b': 1, 'conf_ln_g': 1,
              'conf_ln_b': 1, 'sconv_w': 2, 'conv_w_out': 1, 'ffn_w_up': 2, 'ffn_dw_w': 2, 'ffn_w_down': 1}
MATMUL_WEIGHTS = ['attn_w_in', 'attn_w_out', 'conv_w_in', 'conv_w_out', 'ffn_w_up', 'ffn_w_down']
SMALL_SHARDED = ['conf_dw_w', 'conf_dw_b', 'conf_ln_g', 'conf_ln_b', 'sconv_w', 'ffn_dw_w']
SHARDED = MATMUL_WEIGHTS + SMALL_SHARDED
REPLICATED = ['t5_bias', 'na_rpb', 'mix_ln_g', 'mix_ln_b', 'ffn_ln_g', 'ffn_ln_b']


def _tile(n, cands):
    for c in cands:
        if n % c == 0:
            return c
    return n


def _params(sem, vmem=None):
    return pltpu.CompilerParams(dimension_semantics=sem, vmem_limit_bytes=vmem)


def _sigmoid(x):
    return 0.5 * jnp.tanh(0.5 * x) + 0.5


MM_MAX_TILE = 1408
MM_MAX_K = 3072


def _lane_tile(n, cap):
    best = None
    for t in range(LANES, min(n, cap) + 1, LANES):
        if n % t == 0:
            best = t
    return best or n


def _mm(a, b, *, ta=False, tb=False, out_dtype=F32, carry=None, name):
    assert a.dtype == BF16 and b.dtype == BF16, (name, a.dtype, b.dtype)
    m, k = (a.shape[1], a.shape[0]) if ta else a.shape
    n = b.shape[0] if tb else b.shape[1]
    tm, tn, tk = _lane_tile(m, MM_MAX_TILE), _lane_tile(n, MM_MAX_TILE), _lane_tile(k, MM_MAX_K)
    grid = (m // tm, n // tn, k // tk)
    nk = grid[2]
    dims = (((0 if ta else 1,), (1 if tb else 0,)), ((), ()))
    use_acc = nk > 1 and out_dtype != F32
    ride = _Carried(carry, grid)

    def body(*refs):
        (a_ref, b_ref), (o_ref,), scratch = ride.split(refs, 2, 1)
        ride.start()
        part = lax.dot_general(a_ref[...], b_ref[...], dims, preferred_element_type=F32)
        if nk == 1:
            o_ref[...] = part.astype(out_dtype)
        else:
            acc_ref = scratch[0] if use_acc else o_ref
            kk = pl.program_id(2)

            @pl.when(kk == 0)
            def _():
                acc_ref[...] = part

            @pl.when(kk > 0)
            def _():
                acc_ref[...] += part

            if use_acc:
                @pl.when(kk == nk - 1)
                def _():
                    o_ref[...] = acc_ref[...].astype(out_dtype)

        ride.wait()

    a_spec = pl.BlockSpec((tk, tm), lambda i, j, q: (q, i)) if ta else pl.BlockSpec((tm, tk), lambda i, j, q: (i, q))
    b_spec = pl.BlockSpec((tn, tk), lambda i, j, q: (j, q)) if tb else pl.BlockSpec((tk, tn), lambda i, j, q: (q, j))
    o_spec = pl.BlockSpec((tm, tn), lambda i, j, q: (i, j))
    o_shape = jax.ShapeDtypeStruct((m, n), out_dtype)
    scratch = [pltpu.VMEM((tm, tn), F32)] if use_acc else []
    sem = ("arbitrary",) * 3 if ride.on else ("parallel", "parallel", "arbitrary")
    out = pl.pallas_call(
        body, name=name, grid=grid, in_specs=[a_spec, b_spec] + ride.specs, out_specs=[o_spec] + ride.specs,
        out_shape=[o_shape] + ride.out_shapes, scratch_shapes=scratch + ride.scratch,
        compiler_params=_params(sem, VMEM_BIG))(a, b, *ride.operands)
    return out[0], (out[1] if ride.on else None)


MM_LN_ROWS = 512


def _mm_ln(a, b, x, g, beta, *, carry=None, name):
    assert a.dtype == BF16 and b.dtype == BF16, (name, a.dtype, b.dtype)
    m, k = a.shape
    n = b.shape[1]
    assert k <= MM_MAX_K, (name, k)
    tm = _tile(m, (MM_LN_ROWS, 256, 128, 64, 8))
    grid = (m // tm,)
    ride = _Carried(carry, grid)

    def body(*refs):
        (a_ref, b_ref, x_ref, g_ref, beta_ref), (z_ref, o_ref, o16_ref), _ = ride.split(refs, 5, 3)
        ride.start()
        z = ALPHA * x_ref[...] + jnp.dot(a_ref[...], b_ref[...], preferred_element_type=F32)
        z_ref[...] = z
        xh, _ = _ln_stats(z)
        out = xh * g_ref[...] + beta_ref[...]
        o_ref[...] = out
        o16_ref[...] = out.astype(BF16)
        ride.wait()

    row = lambda width: pl.BlockSpec((tm, width), lambda i: (i, 0))
    vec = pl.BlockSpec((1, n), lambda i: (0, 0))
    out = pl.pallas_call(
        body, name=name, grid=grid,
        in_specs=[row(k), pl.BlockSpec((k, n), lambda i: (0, 0)), row(n), vec, vec] + ride.specs,
        out_specs=[row(n)] * 3 + ride.specs,
        out_shape=[jax.ShapeDtypeStruct((m, n), F32), jax.ShapeDtypeStruct((m, n), F32),
                   jax.ShapeDtypeStruct((m, n), BF16)] + ride.out_shapes,
        scratch_shapes=ride.scratch,
        compiler_params=_params(("arbitrary",) if ride.on else ("parallel",), VMEM_BIG))(
            a, b, x, g.reshape(1, n), beta.reshape(1, n), *ride.operands)
    return out[0], out[1], out[2], (out[3] if ride.on else None)


def _mm_ln_bwd(d_out, b, z, g, d1, *, carry=None, name):
    assert d_out.dtype == BF16 and b.dtype == BF16, (name, d_out.dtype, b.dtype)
    m, k = d_out.shape
    n = b.shape[0]
    tm, tk = _tile(m, (MM_LN_ROWS, 256, 128, 64, 8)), _lane_tile(k, MM_MAX_K)
    grid = (m // tm, k // tk)
    nk = grid[1]
    ride = _Carried(carry, grid)

    def body(*refs):
        (a_ref, b_ref, z_ref, g_ref, d1_ref), (dz_ref, dz16_ref, dg_ref, db_ref), scratch = ride.split(refs, 5, 4)
        ride.start()
        i, kk = pl.program_id(0), pl.program_id(1)
        part = lax.dot_general(a_ref[...], b_ref[...], (((1,), (1,)), ((), ())), preferred_element_type=F32)

        def finish(d2):
            @pl.when(i == 0)
            def _():
                dg_ref[...] = jnp.zeros_like(dg_ref)
                db_ref[...] = jnp.zeros_like(db_ref)

            dout = ALPHA * d1_ref[...] + d2
            xh, rstd = _ln_stats(z_ref[...])
            dxh = dout * g_ref[...]
            dz = rstd * (dxh - jnp.mean(dxh, axis=-1, keepdims=True) - xh * jnp.mean(dxh * xh, axis=-1, keepdims=True))
            dz_ref[...] = dz
            dz16_ref[...] = dz.astype(BF16)
            dg_ref[...] += jnp.sum(dout * xh, axis=0, keepdims=True)
            db_ref[...] += jnp.sum(dout, axis=0, keepdims=True)

        if nk == 1:
            finish(part)
        else:
            acc_ref = scratch[0]

            @pl.when(kk == 0)
            def _():
                acc_ref[...] = part

            @pl.when((kk > 0) & (kk < nk - 1))
            def _():
                acc_ref[...] += part

            @pl.when(kk == nk - 1)
            def _():
                finish(acc_ref[...] + part)

        ride.wait()

    row = pl.BlockSpec((tm, n), lambda i, q: (i, 0))
    vec = pl.BlockSpec((1, n), lambda i, q: (0, 0))
    out = pl.pallas_call(
        body, name=name, grid=grid,
        in_specs=[pl.BlockSpec((tm, tk), lambda i, q: (i, q)), pl.BlockSpec((n, tk), lambda i, q: (0, q)),
                  row, vec, row] + ride.specs,
        out_specs=[row, row, vec, vec] + ride.specs,
        out_shape=[jax.ShapeDtypeStruct((m, n), F32), jax.ShapeDtypeStruct((m, n), BF16),
                   jax.ShapeDtypeStruct((1, n), F32), jax.ShapeDtypeStruct((1, n), F32)] + ride.out_shapes,
        scratch_shapes=([pltpu.VMEM((tm, n), F32)] if nk > 1 else []) + ride.scratch,
        compiler_params=_params(("arbitrary", "arbitrary"), VMEM_BIG))(
            d_out, b, z, g.reshape(1, n), d1, *ride.operands)
    return out[0], out[1], out[2].reshape(n), out[3].reshape(n), (out[4] if ride.on else None)


def _ln_stats(z):
    mu = jnp.mean(z, axis=-1, keepdims=True)
    zc = z - mu
    var = jnp.mean(zc * zc, axis=-1, keepdims=True)
    rstd = lax.rsqrt(var + LN_EPS)
    return zc * rstd, rstd


def _ln_bwd(z, g, d1, d2, *, name):
    s, d = z.shape
    t = _tile(s, (256, 128, 64, 8))
    two = d2 is not None

    def body(*refs):
        if two:
            z_ref, g_ref, d1_ref, d2_ref, dz_ref, dz16_ref, dg_ref, db_ref = refs
            dout = ALPHA * d1_ref[...] + d2_ref[...]
        else:
            z_ref, g_ref, d1_ref, dz_ref, dz16_ref, dg_ref, db_ref = refs
            dout = d1_ref[...]

        @pl.when(pl.program_id(0) == 0)
        def _():
            dg_ref[...] = jnp.zeros_like(dg_ref)
            db_ref[...] = jnp.zeros_like(db_ref)

        xh, rstd = _ln_stats(z_ref[...])
        dxh = dout * g_ref[...]
        dz = rstd * (dxh - jnp.mean(dxh, axis=-1, keepdims=True) - xh * jnp.mean(dxh * xh, axis=-1, keepdims=True))
        dz_ref[...] = dz
        dz16_ref[...] = dz.astype(BF16)
        dg_ref[...] += jnp.sum(dout * xh, axis=0, keepdims=True)
        db_ref[...] += jnp.sum(dout, axis=0, keepdims=True)

    row = pl.BlockSpec((t, d), lambda i: (i, 0))
    vec = pl.BlockSpec((1, d), lambda i: (0, 0))
    ins = [z, g.reshape(1, d), d1] + ([d2] if two else [])
    specs = [row, vec, row] + ([row] if two else [])
    dz, dz16, dg, db = pl.pallas_call(
        body, name=name, grid=(s // t,), in_specs=specs, out_specs=[row, row, vec, vec],
        out_shape=[jax.ShapeDtypeStruct((s, d), F32), jax.ShapeDtypeStruct((s, d), BF16),
                   jax.ShapeDtypeStruct((1, d), F32), jax.ShapeDtypeStruct((1, d), F32)],
        compiler_params=_params(("arbitrary",)))(*ins)
    return dz, dz16, dg.reshape(d), db.reshape(d)


def _axpy(d1, d2, *, name):
    s, d = d1.shape
    t = _tile(s, (256, 128, 64, 8))

    def body(a_ref, b_ref, o_ref):
        o_ref[...] = ALPHA * a_ref[...] + b_ref[...]

    row = pl.BlockSpec((t, d), lambda i: (i, 0))
    return pl.pallas_call(body, name=name, grid=(s // t,), in_specs=[row, row], out_specs=row,
                          out_shape=jax.ShapeDtypeStruct((s, d), F32), compiler_params=_params(("parallel",)))(d1, d2)


def _loss_head(y, tgt, *, name):
    s, d = y.shape
    t = _tile(s, (256, 128, 64, 8))

    def body(y_ref, t_ref, l_ref, dy_ref):
        @pl.when(pl.program_id(0) == 0)
        def _():
            l_ref[...] = jnp.zeros_like(l_ref)

        err = y_ref[...] - t_ref[...]
        dy_ref[...] = err * (1.0 / d)
        l_ref[...] += 0.5 * jnp.sum(jnp.sum(err * err, axis=1, keepdims=True), axis=0, keepdims=True) * (1.0 / d)

    row = pl.BlockSpec((t, d), lambda i: (i, 0))
    one = pl.BlockSpec((SUBLANES, LANES), lambda i: (0, 0))
    loss, dy = pl.pallas_call(
        body, name=name, grid=(s // t,), in_specs=[row, row], out_specs=[one, row],
        out_shape=[jax.ShapeDtypeStruct((SUBLANES, LANES), F32), jax.ShapeDtypeStruct((s, d), F32)],
        compiler_params=_params(("arbitrary",)))(y, tgt)
    return loss[0, 0], dy


def _halo_specs(s, t, halo, cb, col):
    per = t // halo
    last = s // halo - 1
    return [pl.BlockSpec((t, cb), lambda j, i: (i, col(j))),
            pl.BlockSpec((halo, cb), lambda j, i: (jnp.maximum(i * per - 1, 0), col(j))),
            pl.BlockSpec((halo, cb), lambda j, i: (jnp.minimum((i + 1) * per, last), col(j)))]


def _extended(main_ref, prev_ref, next_ref, i, n):
    prev = jnp.where(i > 0, prev_ref[...], 0.0)
    nxt = jnp.where(i < n - 1, next_ref[...], 0.0)
    return jnp.concatenate([prev, main_ref[...], nxt], axis=0)


def _shift(ext, o):
    if o == 0:
        return ext
    return pltpu.roll(ext, (-o) % ext.shape[0], 0)


def _taps(ext, k, sign=1):
    return [_shift(ext, sign * (j - k // 2)) for j in range(k)]


def _conv(ext, w_ref, k, sign=1, taps=None):
    taps = _taps(ext, k, sign) if taps is None else taps
    acc = None
    for j in range(k):
        term = w_ref[j:j + 1, :] * taps[j]
        acc = term if acc is None else acc + term
    return acc


def _main_taps(ext, k, halo, t, sign=1):
    rolled, taps = {}, []
    for j in range(k):
        offset = sign * (j - k // 2)
        res = offset % SUBLANES
        if res not in rolled:
            rolled[res] = _shift(ext, res)
        start = halo + offset - res
        taps.append(rolled[res][start:start + t])
    return taps


def _conv_main(taps, w_ref):
    acc = None
    for j, tap in enumerate(taps):
        term = w_ref[j:j + 1, :] * tap
        acc = term if acc is None else acc + term
    return acc


def _wgrad_main(dw_ref, d_main, taps):
    for j, tap in enumerate(taps):
        dw_ref[j:j + 1, :] += jnp.sum(d_main * tap, axis=0, keepdims=True)


def _conv_wgrad(dw_ref, d_main, x_ext, k, halo, t, taps=None):
    taps = _taps(x_ext, k) if taps is None else taps
    for j in range(k):
        dw_ref[j:j + 1, :] += jnp.sum(d_main * taps[j][halo:halo + t], axis=0, keepdims=True)


def _ffn_mid_fwd(hu, w, *, carry=None, name):
    s, f2 = hu.shape
    f = f2 // 2
    t, cb, halo = _tile(s, (FFN_ROW_TILE,)), _lane_tile(f, FFN_STRIP), SUBLANES
    nt, nc = s // t, f // cb
    ride = _Carried(carry, (1, nt))

    def body(*refs):
        (h_ref, hp_ref, hn_ref, w_ref), (a_ref,), _ = ride.split(refs, 4, 1)
        ride.start()
        i = pl.program_id(1)
        first, last = i == 0, i == nt - 1

        def extended(cols):
            return jnp.concatenate([jnp.where(first, 0.0, hp_ref[:, cols]), h_ref[:, cols],
                                    jnp.where(last, 0.0, hn_ref[:, cols])], axis=0)

        for c in range(nc):
            gcols, ucols = pl.ds(c * cb, cb), pl.ds(f + c * cb, cb)
            hg = _conv_main(_main_taps(extended(gcols), FFN_K, halo, t), w_ref.at[:, gcols])
            hu_ = _conv_main(_main_taps(extended(ucols), FFN_K, halo, t), w_ref.at[:, ucols])
            a_ref[:, gcols] = (hg * _sigmoid(hg) * hu_).astype(BF16)
        ride.wait()

    specs = _halo_specs(s, t, halo, f2, lambda j: 0) + [pl.BlockSpec((FFN_K, f2), lambda j, i: (0, 0))]
    out = pl.pallas_call(
        body, name=name, grid=(1, nt), in_specs=specs + ride.specs,
        out_specs=[pl.BlockSpec((t, f), lambda j, i: (i, 0))] + ride.specs,
        out_shape=[jax.ShapeDtypeStruct((s, f), BF16)] + ride.out_shapes, scratch_shapes=ride.scratch,
        compiler_params=_params(("arbitrary", "arbitrary") if ride.on else ("parallel", "parallel"), VMEM_BIG))(
            hu, hu, hu, w, *ride.operands)
    return out[0], (out[1] if ride.on else None)


def _ffn_mid_bwd(hu, w, da, *, carry=None, name):
    s, f2 = hu.shape
    f = f2 // 2
    t, cb, halo = _tile(s, (FFN_ROW_TILE,)), _lane_tile(f, FFN_STRIP), SUBLANES
    nt, nc = s // t, f // cb
    ride = _Carried(carry, (1, nt))

    def body(*refs):
        (h_ref, hp_ref, hn_ref, a_ref, ap_ref, an_ref, w_ref), (dh_ref, dw_ref), _ = ride.split(refs, 7, 2)
        ride.start()
        i = pl.program_id(1)

        @pl.when(i == 0)
        def _():
            dw_ref[...] = jnp.zeros_like(dw_ref)

        first, last = i == 0, i == nt - 1

        def extended(main, prev, nxt, cols):
            return jnp.concatenate([jnp.where(first, 0.0, prev[:, cols]), main[:, cols],
                                    jnp.where(last, 0.0, nxt[:, cols])], axis=0)

        for c in range(nc):
            gcols, ucols, acols = pl.ds(c * cb, cb), pl.ds(f + c * cb, cb), pl.ds(c * cb, cb)
            xg = extended(h_ref, hp_ref, hn_ref, gcols)
            xu = extended(h_ref, hp_ref, hn_ref, ucols)
            dae = extended(a_ref, ap_ref, an_ref, acols)
            wg, wu = w_ref.at[:, gcols], w_ref.at[:, ucols]
            xg_taps, xu_taps = _taps(xg, FFN_K), _taps(xu, FFN_K)
            hg = _conv(xg, wg, FFN_K, taps=xg_taps)
            hu_ = _conv(xu, wu, FFN_K, taps=xu_taps)
            sg = _sigmoid(hg)
            d_hg = dae * hu_ * (sg * (1.0 + hg * (1.0 - sg)))
            d_hu = dae * (hg * sg)
            dh_ref[:, gcols] = _conv(d_hg, wg, FFN_K, sign=-1)[halo:halo + t].astype(BF16)
            dh_ref[:, ucols] = _conv(d_hu, wu, FFN_K, sign=-1)[halo:halo + t].astype(BF16)
            _conv_wgrad(dw_ref.at[:, gcols], d_hg[halo:halo + t], xg, FFN_K, halo, t, taps=xg_taps)
            _conv_wgrad(dw_ref.at[:, ucols], d_hu[halo:halo + t], xu, FFN_K, halo, t, taps=xu_taps)
        ride.wait()

    whole = lambda j: 0
    specs = (_halo_specs(s, t, halo, f2, whole) + _halo_specs(s, t, halo, f, whole)
             + [pl.BlockSpec((FFN_K, f2), lambda j, i: (0, 0))])
    out = pl.pallas_call(
        body, name=name, grid=(1, nt), in_specs=specs + ride.specs,
        out_specs=[pl.BlockSpec((t, f2), lambda j, i: (i, 0)), pl.BlockSpec((FFN_K, f2), lambda j, i: (0, 0))]
        + ride.specs,
        out_shape=[jax.ShapeDtypeStruct((s, f2), BF16), jax.ShapeDtypeStruct((FFN_K, f2), F32)] + ride.out_shapes,
        scratch_shapes=ride.scratch,
        compiler_params=_params(("arbitrary", "arbitrary"), VMEM_BIG))(hu, hu, hu, da, da, da, w, *ride.operands)
    return out[0], out[1], (out[2] if ride.on else None)


CONV_HALO = 16


def _conv_mid_fwd(h, dw_w, dw_b, ln_g, ln_b, sc_w, *, name):
    s = h.shape[0]
    c = CONF_CH
    t, halo = _tile(s, (256, 128)), CONV_HALO
    nt = s // t

    def body(ca, cap, can, cg, cgp, cgn, gb, gc, gcp, gcn, hx, hxp, hxn, w31, b31, lg, lb, w3, o_ref, u2_ref):
        i = pl.program_id(1)
        first, last = i == 0, i == nt - 1

        def extended(main, prev, nxt, cols):
            return jnp.concatenate([jnp.where(first, 0.0, prev[:, cols]), main[:, cols],
                                    jnp.where(last, 0.0, nxt[:, cols])], axis=0)

        for strip in range(c // LANES):
            cols = pl.ds(strip * LANES, LANES)
            u1 = extended(ca, cap, can, cols) * _sigmoid(extended(cg, cgp, cgn, cols))
            u2_ref[:, cols] = _conv_main(_main_taps(u1, CONF_K, halo, t), w31.at[:, cols]) + b31[:, cols]
            p = extended(gc, gcp, gcn, cols) * extended(hx, hxp, hxn, cols)
            conv = _conv_main(_main_taps(p, SC_K, halo, t), w3.at[:, cols])
            o_ref[:, pl.ds(c + strip * LANES, LANES)] = (gb[:, cols] * conv).astype(BF16)
        xh, _ = _ln_stats(u2_ref[...])
        yl = xh * lg[...] + lb[...]
        o_ref[:, 0:c] = (yl * _sigmoid(yl)).astype(BF16)

    hs = lambda blk: _halo_specs(s, t, halo, c, lambda j: blk)
    vec = lambda r: pl.BlockSpec((r, c), lambda j, i: (0, 0))
    specs = hs(0) + hs(1) + hs(2)[:1] + hs(3) + hs(4) + [vec(CONF_K), vec(1), vec(1), vec(1), vec(SC_K)]
    return pl.pallas_call(
        body, name=name, grid=(1, nt), in_specs=specs,
        out_specs=[pl.BlockSpec((t, 2 * c), lambda j, i: (i, 0)), pl.BlockSpec((t, c), lambda j, i: (i, 0))],
        out_shape=[jax.ShapeDtypeStruct((s, 2 * c), BF16), jax.ShapeDtypeStruct((s, c), F32)],
        compiler_params=_params(("parallel", "parallel")))(
            h, h, h, h, h, h, h, h, h, h, h, h, h, dw_w, dw_b.reshape(1, c), ln_g.reshape(1, c),
            ln_b.reshape(1, c), sc_w)


def _conv_mid_bwd(h, u2, dm, dw_w, ln_g, ln_b, sc_w, *, name):
    s = h.shape[0]
    c = CONF_CH
    t, halo = _tile(s, (256, 128)), CONV_HALO
    nt = s // t

    def body(ca, cap, can, cg, cgp, cgn, gb, gbp, gbn, gc, gcp, gcn, hx, hxp, hxn, u2r, u2p, u2n,
             du, dup, dun, dz, dzp, dzn, w31, lg, lb, w3,
             dh_ref, dw31_ref, db31_ref, dlg_ref, dlb_ref, dw3_ref, u1_s, du2_s, p_s, dc3_s):
        i = pl.program_id(1)

        @pl.when(i == 0)
        def _():
            for r in (dw31_ref, db31_ref, dlg_ref, dlb_ref, dw3_ref):
                r[...] = jnp.zeros_like(r)

        main = slice(halo, halo + t)
        xh, rstd = _ln_stats(_extended(u2r, u2p, u2n, i, nt))
        yl = xh * lg[...] + lb[...]
        sg = _sigmoid(yl)
        d_yl = _extended(du, dup, dun, i, nt) * (sg * (1.0 + yl * (1.0 - sg)))
        dlg_ref[...] += jnp.sum((d_yl * xh)[main], axis=0, keepdims=True)
        dlb_ref[...] += jnp.sum(d_yl[main], axis=0, keepdims=True)
        dxh = d_yl * lg[...]
        du2 = rstd * (dxh - jnp.mean(dxh, axis=-1, keepdims=True) - xh * jnp.mean(dxh * xh, axis=-1, keepdims=True))
        db31_ref[...] += jnp.sum(du2[main], axis=0, keepdims=True)
        du2_s[...] = du2
        u1_s[...] = _extended(ca, cap, can, i, nt) * _sigmoid(_extended(cg, cgp, cgn, i, nt))
        p_s[...] = _extended(gc, gcp, gcn, i, nt) * _extended(hx, hxp, hxn, i, nt)
        dc3_s[...] = _extended(dz, dzp, dzn, i, nt) * _extended(gb, gbp, gbn, i, nt)
        for strip in range(c // LANES):
            cols = pl.ds(strip * LANES, LANES)
            at = lambda part: pl.ds(part * c + strip * LANES, LANES)
            du2_c = du2_s[:, cols]
            _wgrad_main(dw31_ref.at[:, cols], du2_c[main], _main_taps(u1_s[:, cols], CONF_K, halo, t))
            du1 = _conv_main(_main_taps(du2_c, CONF_K, halo, t, sign=-1), w31.at[:, cols])
            sc = _sigmoid(cg[:, cols])
            dh_ref[:, at(0)] = (du1 * sc).astype(BF16)
            dh_ref[:, at(1)] = (du1 * (ca[:, cols] * sc * (1.0 - sc))).astype(BF16)
            dc3_c = dc3_s[:, cols]
            p_taps = _main_taps(p_s[:, cols], SC_K, halo, t)
            dh_ref[:, at(2)] = (dz[:, cols] * _conv_main(p_taps, w3.at[:, cols])).astype(BF16)
            _wgrad_main(dw3_ref.at[:, cols], dc3_c[main], p_taps)
            dp = _conv_main(_main_taps(dc3_c, SC_K, halo, t, sign=-1), w3.at[:, cols])
            dh_ref[:, at(3)] = (dp * hx[:, cols]).astype(BF16)
            dh_ref[:, at(4)] = (dp * gc[:, cols]).astype(BF16)

    hs = lambda blk: _halo_specs(s, t, halo, c, lambda j: blk)
    vec = lambda r: pl.BlockSpec((r, c), lambda j, i: (0, 0))
    specs = (hs(0) + hs(1) + hs(2) + hs(3) + hs(4) + hs(0) + hs(0) + hs(1)
             + [vec(CONF_K), vec(1), vec(1), vec(SC_K)])
    ext = pltpu.VMEM((t + 2 * halo, c), F32)
    outs = pl.pallas_call(
        body, name=name, grid=(1, nt), in_specs=specs,
        out_specs=[pl.BlockSpec((t, 5 * c), lambda j, i: (i, 0)), vec(CONF_K), vec(1), vec(1), vec(1), vec(SC_K)],
        out_shape=[jax.ShapeDtypeStruct((s, 5 * c), BF16), jax.ShapeDtypeStruct((CONF_K, c), F32),
                   jax.ShapeDtypeStruct((1, c), F32), jax.ShapeDtypeStruct((1, c), F32),
                   jax.ShapeDtypeStruct((1, c), F32), jax.ShapeDtypeStruct((SC_K, c), F32)],
        scratch_shapes=[ext, ext, ext, ext],
        compiler_params=_params(("arbitrary", "arbitrary")))(
            h, h, h, h, h, h, h, h, h, h, h, h, h, h, h, u2, u2, u2, dm, dm, dm, dm, dm, dm,
            dw_w, ln_g.reshape(1, c), ln_b.reshape(1, c), sc_w)
    dh, dw31, db31, dlg, dlb, dw3 = outs
    return dh, dw31, db31.reshape(c), dlg.reshape(c), dlb.reshape(c), dw3


def _attn_mask(kind, n, tq, kw, length):
    pad = (kw - tq) // 2
    iq = lax.broadcasted_iota(jnp.int32, (tq, 1), 0)
    ik = lax.broadcasted_iota(jnp.int32, (1, kw), 1)
    if kind == "band":
        rel = ik - pad - iq
        kpos = n * tq - pad + ik
        return (jnp.abs(rel) <= DIL_HALF) & (kpos >= 0) & (kpos < length)
    rows = length // GRID_W
    rq = n * NA_QROWS + (iq >> GRID_SHIFT)
    cq = iq & (GRID_W - 1)
    rk = n * NA_QROWS - pad // GRID_W + (ik >> GRID_SHIFT)
    ck = ik & (GRID_W - 1)
    r0 = jnp.clip(rq - NA_KH // 2, 0, rows - NA_KH)
    c0 = jnp.clip(cq - NA_KW // 2, 0, GRID_W - NA_KW)
    return (rk >= r0) & (rk < r0 + NA_KH) & (ck >= c0) & (ck < c0 + NA_KW)


def _mask_tiles(kind, tq, kw, length, *, name):
    nb = length // tq

    def body(o_ref):
        v = pl.program_id(0)
        n = jnp.where((v == 1) | (v == 3), 0, jnp.where(v == 2, nb - 1, 1))
        o_ref[0] = jnp.where(_attn_mask(kind, n, tq, kw, length), 0.0, NEG)

    return pl.pallas_call(body, name=name, grid=(4,), out_specs=pl.BlockSpec((1, tq, kw), lambda v: (v, 0, 0)),
                          out_shape=jax.ShapeDtypeStruct((4, tq, kw), F32),
                          compiler_params=_params(("parallel",)))()


class _AttnGeom:
    def __init__(self, s, d, tq, nsub):
        self.s, self.d, self.tq, self.nsub = s, d, tq, nsub
        self.halo = tq * d
        self.rows = nsub * self.halo
        self.nbig = s // self.rows
        self.ext = self.rows + 2 * self.halo
        assert s % self.rows == 0

    def main(self, col):
        return pl.BlockSpec((self.rows, LANES), lambda hp, n: (n, col + hp))

    def with_halos(self, col):
        last = self.s // self.halo - 1
        return [self.main(col),
                pl.BlockSpec((self.halo, LANES), lambda hp, n: (jnp.maximum(n * self.nsub - 1, 0), col + hp)),
                pl.BlockSpec((self.halo, LANES), lambda hp, n: (jnp.minimum((n + 1) * self.nsub, last), col + hp))]

    def fill_ext(self, ext_ref, main_ref, prev_ref, next_ref):
        ext_ref[0:self.halo] = prev_ref[...].astype(F32)
        ext_ref[self.halo:self.halo + self.rows] = main_ref[...].astype(F32)
        ext_ref[self.halo + self.rows:self.ext] = next_ref[...].astype(F32)

    def rows_of(self, r, pos, count):
        start = r + pos * self.d
        return pl.ds(start, count, stride=self.d) if self.d > 1 else pl.ds(start, count)

    def variant(self, n, sub):
        v = 0
        if sub == 0:
            v = v + jnp.where(n == 0, 1, 0)
        if sub == self.nsub - 1:
            v = v + jnp.where(n == self.nbig - 1, 2, 0)
        return v


def _stack_heads(x, low):
    return jnp.concatenate([jnp.where(low, x, 0.0), jnp.where(low, 0.0, x)], axis=0)


def _attn_fwd(h, bias, mask, *, d, tq, nsub, cols, carry=None, name):
    geo = _AttnGeom(h.shape[0], d, tq, nsub)
    scale = HEAD_DIM ** -0.5
    kw = bias.shape[2]
    first_key = tq - (kw - tq) // 2

    ride = _Carried(carry, (4, geo.nbig))

    def body(*refs):
        (q_ref, km, kp, kn, vm, vp, vn, b_ref, m_ref), (o_ref, l_ref), (kext, vext) = ride.split(refs, 9, 2)
        ride.start()
        n = pl.program_id(1)
        geo.fill_ext(kext, km, kp, kn)
        geo.fill_ext(vext, vm, vp, vn)
        low = lax.broadcasted_iota(jnp.int32, (1, LANES), 1) < HEAD_DIM
        for r in range(d):
            for sub in range(nsub):
                madd = m_ref[geo.variant(n, sub)]
                q = q_ref[geo.rows_of(r, sub * tq, tq), :].astype(F32) * scale
                ks = kext[geo.rows_of(r, sub * tq + first_key, kw), :].astype(BF16)
                vs = vext[geo.rows_of(r, sub * tq + first_key, kw), :].astype(BF16)
                q2 = _stack_heads(q, low).astype(BF16)
                sc = (lax.dot_general(q2, ks, (((1,), (1,)), ((), ())), preferred_element_type=F32)
                      + (b_ref[...] + madd).reshape(2 * tq, kw))
                m = jnp.max(sc, axis=1, keepdims=True)
                p = jnp.exp(sc - m)
                den = jnp.sum(p, axis=1, keepdims=True)
                out2 = jnp.dot((p / den).astype(BF16), vs, preferred_element_type=F32)
                lse2 = m + jnp.log(den)
                o_ref[geo.rows_of(r, sub * tq, tq), :] = jnp.where(low, out2[0:tq], out2[tq:2 * tq])
                l_ref[geo.rows_of(r, sub * tq, tq), :] = jnp.where(low, lse2[0:tq], lse2[tq:2 * tq])
        ride.wait()

    qc, kc, vc = cols
    specs = ([geo.main(qc)] + geo.with_halos(kc) + geo.with_halos(vc)
             + [pl.BlockSpec((2, tq, kw), lambda hp, n: (hp, 0, 0)),
                pl.BlockSpec((4, tq, kw), lambda hp, n: (0, 0, 0))])
    shape = jax.ShapeDtypeStruct((geo.s, 4 * LANES), F32)
    ext = pltpu.VMEM((geo.ext, LANES), F32)
    out = pl.pallas_call(
        body, name=name, grid=(4, geo.nbig), in_specs=specs + ride.specs,
        out_specs=[geo.main(0), geo.main(0)] + ride.specs, out_shape=[shape, shape] + ride.out_shapes,
        scratch_shapes=[ext, ext] + ride.scratch,
        compiler_params=_params(("arbitrary", "arbitrary") if ride.on else ("parallel", "parallel"), VMEM_BIG))(
            h, h, h, h, h, h, h, bias, mask, *ride.operands)
    return out[0], out[1], (out[2] if ride.on else None)


def _attn_bwd(h, bias, mask, dy, y, lse, *, d, tq, nsub, cols, ycol, name):
    geo = _AttnGeom(h.shape[0], d, tq, nsub)
    scale = HEAD_DIM ** -0.5
    halo, rows = geo.halo, geo.rows
    kw = bias.shape[2]
    first_key = tq - (kw - tq) // 2

    def body(q_ref, km, kp, kn, vm, vp, vn, b_ref, m_ref, dy_ref, y_ref, l_ref, dq_ref, dk_hbm, dv_hbm, db_ref,
             kext, vext, dkext, dvext, dk_all, dv_all, sems):
        hp, n = pl.program_id(0), pl.program_id(1)

        @pl.when(n == 0)
        def _():
            dk_all[...] = jnp.zeros_like(dk_all)
            dv_all[...] = jnp.zeros_like(dv_all)
            db_ref[...] = jnp.zeros_like(db_ref)

        geo.fill_ext(kext, km, kp, kn)
        geo.fill_ext(vext, vm, vp, vn)
        dkext[...] = jnp.zeros_like(dkext)
        dvext[...] = jnp.zeros_like(dvext)
        low = lax.broadcasted_iota(jnp.int32, (1, LANES), 1) < HEAD_DIM
        for r in range(d):
            for sub in range(nsub):
                madd = m_ref[geo.variant(n, sub)]
                mine = geo.rows_of(r, sub * tq, tq)
                keys = geo.rows_of(r, sub * tq + first_key, kw)
                q = q_ref[mine, :].astype(F32) * scale
                ks = kext[keys, :].astype(BF16)
                vs = vext[keys, :].astype(BF16)
                dyv = dy_ref[mine, :]
                dyy = dyv * y_ref[mine, :]
                lse_all = l_ref[mine, :]
                q2 = _stack_heads(q, low).astype(BF16)
                dy2 = _stack_heads(dyv, low).astype(BF16)
                dsum = jnp.concatenate([jnp.sum(jnp.where(sel, dyy, 0.0), axis=1, keepdims=True)
                                        for sel in (low, ~low)], axis=0)
                lse2 = jnp.concatenate([lse_all[:, 0:1], lse_all[:, HEAD_DIM:HEAD_DIM + 1]], axis=0)
                sc = (lax.dot_general(q2, ks, (((1,), (1,)), ((), ())), preferred_element_type=F32)
                      + (b_ref[...] + madd).reshape(2 * tq, kw))
                p = jnp.exp(sc - lse2)
                dp = lax.dot_general(dy2, vs, (((1,), (1,)), ((), ())), preferred_element_type=F32)
                ds = p * (dp - dsum)
                db_ref[...] += ds.reshape(2, tq, kw)
                pb, dsb = p.astype(BF16), ds.astype(BF16)
                dv = lax.dot_general(pb, dy2, (((0,), (0,)), ((), ())), preferred_element_type=F32)
                dk = lax.dot_general(dsb, q2, (((0,), (0,)), ((), ())), preferred_element_type=F32)
                dq2 = jnp.dot(dsb, ks, preferred_element_type=F32)
                dq_ref[mine, :] = jnp.where(low, dq2[0:tq], dq2[tq:2 * tq]) * scale
                dkext[keys, :] += dk
                dvext[keys, :] += dv

        before = pl.multiple_of(jnp.maximum(n * rows - halo, 0), LANES)
        here = pl.multiple_of(n * rows, LANES)
        after = pl.multiple_of(jnp.minimum((n + 1) * rows, geo.s - halo), LANES)
        for ext, total in ((dkext, dk_all), (dvext, dv_all)):
            total[pl.ds(before, halo), :] += ext[0:halo]
            total[pl.ds(here, rows), :] += ext[halo:halo + rows]
            total[pl.ds(after, halo), :] += ext[halo + rows:geo.ext]

        @pl.when(n == geo.nbig - 1)
        def _():
            col = pl.ds(pl.multiple_of(hp * LANES, LANES), LANES)
            copies = [pltpu.make_async_copy(total, out.at[:, col], sems.at[i])
                      for i, (total, out) in enumerate(((dk_all, dk_hbm), (dv_all, dv_hbm)))]
            for cp in copies:
                cp.start()
            for cp in copies:
                cp.wait()

    qc, kc, vc = cols
    bspec = pl.BlockSpec((2, tq, kw), lambda hp, n: (hp, 0, 0))
    any_spec = pl.BlockSpec(memory_space=pl.ANY)
    specs = ([geo.main(qc)] + geo.with_halos(kc) + geo.with_halos(vc)
             + [bspec, pl.BlockSpec((4, tq, kw), lambda hp, n: (0, 0, 0)), geo.main(ycol), geo.main(ycol), geo.main(0)])
    shape = jax.ShapeDtypeStruct((geo.s, 4 * LANES), F32)
    ext = pltpu.VMEM((geo.ext, LANES), F32)
    whole = pltpu.VMEM((geo.s, LANES), F32)
    return pl.pallas_call(
        body, name=name, grid=(4, geo.nbig), in_specs=specs, out_specs=[geo.main(0), any_spec, any_spec, bspec],
        out_shape=[shape, shape, shape, jax.ShapeDtypeStruct(bias.shape, F32)],
        scratch_shapes=[ext, ext, ext, ext, whole, whole, pltpu.SemaphoreType.DMA((2,))],
        compiler_params=_params(("arbitrary", "arbitrary"), VMEM_BIG))(
            h, h, h, h, h, h, h, bias, mask, dy, y, lse)


def _dil_combine(o_na, outs, lses, *, name):
    s, c = outs[0].shape
    t = _tile(s, (512, 256, 128, 64, 8))

    def body(na, o0, o1, o2, l0, l1, l2, y_ref, y16_ref, lt_ref):
        ls = [l0[...], l1[...], l2[...]]
        m = jnp.maximum(jnp.maximum(ls[0], ls[1]), ls[2])
        es = [jnp.exp(l - m) for l in ls]
        den = es[0] + es[1] + es[2]
        y = (es[0] / den) * o0[...] + (es[1] / den) * o1[...] + (es[2] / den) * o2[...]
        lt_ref[...] = m + jnp.log(den)
        y_ref[:, 0:c] = na[...]
        y_ref[:, c:2 * c] = y
        y16_ref[:, 0:c] = na[...].astype(BF16)
        y16_ref[:, c:2 * c] = y.astype(BF16)

    row = pl.BlockSpec((t, c), lambda i: (i, 0))
    wide = pl.BlockSpec((t, 2 * c), lambda i: (i, 0))
    return pl.pallas_call(body, name=name, grid=(s // t,), in_specs=[row] * 7, out_specs=[wide, wide, row],
                          out_shape=[jax.ShapeDtypeStruct((s, 2 * c), F32), jax.ShapeDtypeStruct((s, 2 * c), BF16),
                                     jax.ShapeDtypeStruct((s, c), F32)],
                          compiler_params=_params(("parallel",)))(o_na, *outs, *lses)


def _attn_dh(na, dil, *, name):
    s, c = na[0].shape
    t = _tile(s, (256, 128, 64, 8))

    def body(*refs):
        ins, o_ref = refs[:-1], refs[-1]
        for a in range(3):
            o_ref[:, a * c:(a + 1) * c] = ins[a][...].astype(BF16)
            o_ref[:, (3 + a) * c:(4 + a) * c] = (ins[3 + a][...] + ins[6 + a][...] + ins[9 + a][...]).astype(BF16)

    row = pl.BlockSpec((t, c), lambda i: (i, 0))
    flat = list(na) + [g[a] for g in dil for a in range(3)]
    return pl.pallas_call(body, name=name, grid=(s // t,), in_specs=[row] * 12,
                          out_specs=pl.BlockSpec((t, 6 * c), lambda i: (i, 0)),
                          out_shape=jax.ShapeDtypeStruct((s, 6 * c), BF16),
                          compiler_params=_params(("parallel",)))(*flat)


def _t5_bucket(rel):
    nb = N_BUCKETS // 2
    max_exact = nb // 2
    ret = np.where(rel > 0, nb, 0)
    n = np.abs(rel)
    large = max_exact + (np.log(np.maximum(n, 1).astype(np.float32) / np.float32(max_exact))
                         / np.float32(math.log(T5_MAX_DIST / max_exact)) * np.float32(nb - max_exact)).astype(np.int32)
    large = np.minimum(large, nb - 1)
    return (ret + np.where(n < max_exact, n, large)).astype(np.int32)


def _band_bucket_index(dil):
    tq, kw = DIL_TQ, DIL_KW
    rel = np.arange(kw)[None, :] - (kw - tq) // 2 - np.arange(tq)[:, None]
    return _t5_bucket(rel * dil)


def _band_bias(t5, dil, *, name):
    tq, kw = DIL_TQ, DIL_KW
    buckets = [int(b) for b in _t5_bucket(np.arange(-DIL_HALF, DIL_HALF + 1) * dil)]

    def body(t_ref, o_ref):
        hh = pl.program_id(0)
        rel = (lax.broadcasted_iota(jnp.int32, (tq, kw), 1) - (kw - tq) // 2
               - lax.broadcasted_iota(jnp.int32, (tq, kw), 0))
        acc = jnp.zeros((tq, kw), F32)
        for r, b in zip(range(-DIL_HALF, DIL_HALF + 1), buckets):
            acc = jnp.where(rel == r, t_ref[b * 8 + hh], acc)
        o_ref[0] = acc

    return pl.pallas_call(body, name=name, grid=(8,),
                          in_specs=[pl.BlockSpec(memory_space=pltpu.SMEM)],
                          out_specs=pl.BlockSpec((1, tq, kw), lambda h: (h, 0, 0)),
                          out_shape=jax.ShapeDtypeStruct((8, tq, kw), F32),
                          compiler_params=_params(("parallel",)))(t5.reshape(-1))


def _na_bias(rpb, *, name):
    nr, nc = 2 * NA_KH - 1, 2 * NA_KW - 1
    tq = NA_QROWS * GRID_W
    w = GRID_W

    def body(r_ref, o_ref):
        base = pl.program_id(0) * (nr * nc)
        lane = lax.broadcasted_iota(jnp.int32, (w, LANES), 1)
        upper = lane >= w
        diff = (lane & (w - 1)) - lax.broadcasted_iota(jnp.int32, (w, LANES), 0) + NA_KW - 1
        tiles = {}
        for i in range(NA_QROWS):
            for m in range(3 * NA_QROWS // 2):
                lo = 2 * m - i + NA_KH - 1 - NA_QROWS
                if lo not in tiles:
                    acc = jnp.zeros((w, LANES), F32)
                    for dc in range(nc):
                        v_lo = r_ref[base + lo * nc + dc] if 0 <= lo < nr else 0.0
                        v_hi = r_ref[base + (lo + 1) * nc + dc] if 0 <= lo + 1 < nr else 0.0
                        acc = jnp.where(diff == dc, jnp.where(upper, v_hi, v_lo), acc)
                    tiles[lo] = acc
                o_ref[0, i * w:(i + 1) * w, m * LANES:(m + 1) * LANES] = tiles[lo]

    return pl.pallas_call(body, name=name, grid=(8,),
                          in_specs=[pl.BlockSpec(memory_space=pltpu.SMEM)],
                          out_specs=pl.BlockSpec((1, tq, 3 * tq), lambda h: (h, 0, 0)),
                          out_shape=jax.ShapeDtypeStruct((8, tq, 3 * tq), F32),
                          compiler_params=_params(("parallel",)))(rpb.reshape(-1))


def _t5_grad(dbs, idxs, *, name):
    def body(d0, d1, d2, i0, i1, i2, o_ref):
        lane = lax.broadcasted_iota(jnp.int32, (1, LANES), 1)
        lines = [jnp.zeros((1, LANES), F32) for _ in range(8)]
        for dref, iref in ((d0, i0), (d1, i1), (d2, i2)):
            idx = iref[...]
            for hh in range(8):
                xh = dref[hh]
                for b in range(N_BUCKETS):
                    val = jnp.sum(jnp.sum(jnp.where(idx == b, xh, 0.0), axis=1, keepdims=True), axis=0, keepdims=True)
                    lines[hh] = lines[hh] + jnp.where(lane == b, val, 0.0)
        for hh in range(8):
            o_ref[hh:hh + 1, :] = lines[hh]

    out = pl.pallas_call(body, name=name, out_shape=jax.ShapeDtypeStruct((8, LANES), F32))(*dbs, *idxs)
    return out[:, :N_BUCKETS].T


def _rpb_grad(db, *, name):
    nr, nc = 2 * NA_KH - 1, 2 * NA_KW - 1
    tq = NA_QROWS * GRID_W
    w = GRID_W

    def body(d_ref, o_ref):
        x = d_ref[0]
        rows = []
        for dr in range(nr):
            acc = jnp.zeros((w, w), F32)
            for i in range(NA_QROWS):
                j = i + dr - (NA_KH - 1 - NA_QROWS)
                if 0 <= j < 3 * NA_QROWS:
                    acc = acc + x[i * w:(i + 1) * w, j * w:(j + 1) * w]
            rows.append(acc)
        diff = (lax.broadcasted_iota(jnp.int32, (w, w), 1) - lax.broadcasted_iota(jnp.int32, (w, w), 0)
                + NA_KW - 1)
        lane = lax.broadcasted_iota(jnp.int32, (1, LANES), 1)
        for dr in range(nr):
            line = jnp.zeros((1, LANES), F32)
            for dc in range(nc):
                val = jnp.sum(jnp.sum(jnp.where(diff == dc, rows[dr], 0.0), axis=1, keepdims=True),
                              axis=0, keepdims=True)
                line = jnp.where(lane == dc, val, line)
            o_ref[0, dr:dr + 1, :] = line

    out = pl.pallas_call(body, name=name, grid=(8,),
                         in_specs=[pl.BlockSpec((1, tq, 3 * tq), lambda h: (h, 0, 0))],
                         out_specs=pl.BlockSpec((1, nr, LANES), lambda h: (h, 0, 0)),
                         out_shape=jax.ShapeDtypeStruct((8, nr, LANES), F32),
                         compiler_params=_params(("parallel",)))(db)
    return out[:, :, :nc]


def _exchange(src, *, gather, name):
    shape = src.shape if not gather else (N_DEV,) + src.shape

    def body(src_ref, out_ref, send_sems, recv_sems, local_sem):
        _exchange_start(src_ref, out_ref, send_sems, recv_sems, local_sem, gather)
        _exchange_wait(src_ref, out_ref, send_sems, recv_sems, local_sem, gather)

    any_spec = pl.BlockSpec(memory_space=pl.ANY)
    return pl.pallas_call(
        body, name=name, in_specs=[any_spec], out_specs=any_spec, out_shape=jax.ShapeDtypeStruct(shape, src.dtype),
        scratch_shapes=_exchange_sems())(src)


class _Carried:
    def __init__(self, carry, grid):
        self.src, self.gather = carry if carry is not None else (None, False)
        self.on = self.src is not None
        self.grid = grid
        self.operands = [self.src] if self.on else []
        self.specs = [pl.BlockSpec(memory_space=pl.ANY)] if self.on else []
        self.scratch = _exchange_sems() if self.on else []
        self.out_shapes = []
        if self.on:
            shape = ((N_DEV,) + self.src.shape) if self.gather else self.src.shape
            self.out_shapes = [jax.ShapeDtypeStruct(shape, self.src.dtype)]

    def split(self, refs, n_in, n_out):
        refs = list(refs)
        if not self.on:
            return refs[:n_in], refs[n_in:n_in + n_out], refs[n_in + n_out:]
        self.refs = (refs[n_in], refs[n_in + 1 + n_out], *refs[-3:], self.gather)
        return refs[:n_in], refs[n_in + 1:n_in + 1 + n_out], refs[n_in + 2 + n_out:-3]

    def _at(self, last):
        hit = None
        for ax, size in enumerate(self.grid):
            here = pl.program_id(ax) == (size - 1 if last else 0)
            hit = here if hit is None else hit & here
        return hit

    def start(self):
        if self.on:
            pl.when(self._at(False))(lambda: _exchange_start(*self.refs))

    def wait(self):
        if self.on:
            pl.when(self._at(True))(lambda: _exchange_wait(*self.refs))


def _exchange_sems():
    return [pltpu.SemaphoreType.DMA((N_DEV - 1,)), pltpu.SemaphoreType.DMA((N_DEV - 1,)), pltpu.SemaphoreType.DMA]


def _exchange_copies(src_ref, out_ref, send_sems, recv_sems, local_sem, gather):
    x, y, c = lax.axis_index("x"), lax.axis_index("y"), lax.axis_index("c")
    me = 4 * x + 2 * y + c

    def outgoing(p):
        return src_ref if gather else src_ref.at[p]

    own = pltpu.make_async_copy(outgoing(me), out_ref.at[me], local_sem)
    sends, recvs = [], []
    for k in range(1, N_DEV):
        px = 1 - x if k & 4 else x
        py = 1 - y if k & 2 else y
        pc = 1 - c if k & 1 else c
        p = 4 * px + 2 * py + pc
        for dst, group in ((me, sends), (p, recvs)):
            group.append(pltpu.make_async_remote_copy(
                src_ref=outgoing(p), dst_ref=out_ref.at[dst], send_sem=send_sems.at[k - 1],
                recv_sem=recv_sems.at[k - 1], device_id=(px, py, pc), device_id_type=pl.DeviceIdType.MESH))
    return own, sends, recvs


def _exchange_start(*refs_and_mode):
    own, sends, _ = _exchange_copies(*refs_and_mode)
    own.start()
    for cp in sends:
        cp.start()


def _exchange_wait(*refs_and_mode):
    own, sends, recvs = _exchange_copies(*refs_and_mode)
    for cp in recvs:
        cp.wait_recv()
    for cp in sends:
        cp.wait_send()
    own.wait()


def _adamw(parts, w, m, v, *, name):
    layers = len(parts)
    rows, cols = w.shape
    per_layer = rows // layers
    t = _tile(per_layer, (FLAT_ROW_TILE, 128, 64, 32, 16, 8))
    nt = per_layer // t

    def body(*refs):
        p_refs = refs[:layers]
        w_ref, m_ref, v_ref, g_ref, d_ref, nm_ref, nv_ref = refs[layers:]
        layer = pl.program_id(0)
        g = None
        for l, p_ref in enumerate(p_refs):
            total = p_ref[0].astype(F32)
            for k in range(1, N_DEV):
                total = total + p_ref[k].astype(F32)
            g = total if g is None else jnp.where(layer == l, total, g)
        nm = ADAM_B1 * m_ref[...] + (1.0 - ADAM_B1) * g
        nv = ADAM_B2 * v_ref[...] + (1.0 - ADAM_B2) * (g * g)
        m_hat = nm / (1.0 - ADAM_B1 ** ADAM_STEP)
        v_hat = nv / (1.0 - ADAM_B2 ** ADAM_STEP)
        g_ref[...] = g
        d_ref[...] = -ADAM_LR * (m_hat / (jnp.sqrt(v_hat) + ADAM_EPS) + ADAM_WD * w_ref[...])
        nm_ref[...] = nm
        nv_ref[...] = nv

    def part_spec(l):
        return pl.BlockSpec((N_DEV, t, cols), lambda layer, i: (0, jnp.where(layer == l, i, 0), 0))

    row = pl.BlockSpec((t, cols), lambda layer, i: (layer * nt + i, 0))
    shape = jax.ShapeDtypeStruct((rows, cols), F32)
    return pl.pallas_call(body, name=name, grid=(layers, nt),
                          in_specs=[part_spec(l) for l in range(layers)] + [row, row, row],
                          out_specs=[row] * 4, out_shape=[shape] * 4,
                          compiler_params=_params(("parallel", "parallel"), VMEM_BIG))(*parts, w, m, v)


def _flatten(arrays, dtype, row_mult):
    flat = jnp.concatenate([a.reshape(-1).astype(dtype) for a in arrays])
    chunk = FLAT_COLS * row_mult
    padded = -(-flat.shape[0] // chunk) * chunk
    return jnp.pad(flat, (0, padded - flat.shape[0])).reshape(padded // FLAT_COLS, FLAT_COLS)


def _unflatten(flat, shapes):
    flat = flat.reshape(-1)
    out, pos = [], 0
    for shp in shapes:
        size = int(np.prod(shp))
        out.append(flat[pos:pos + size].reshape(shp))
        pos += size
    return out


def _gather_full(names, local, dtype, row_mult, label):
    got = _exchange(_flatten([local[n] for n in names], dtype, row_mult), gather=True, name=label)
    got = got.reshape(N_DEV, -1)
    full, pos = {}, 0
    for n in names:
        shp = local[n].shape
        size = int(np.prod(shp))
        piece = got[:, pos:pos + size].reshape(N_DEV, size // shp[-1], shp[-1])
        full[n] = jnp.transpose(piece, (1, 0, 2)).reshape(shp[:-1] + (N_DEV * shp[-1],))
        pos += size
    return full


def _scatter_rows(names, grads, row_mult):
    pieces = []
    for n in names:
        c = grads[n].shape[-1] // N_DEV
        pieces.append(jnp.transpose(grads[n].reshape(-1, N_DEV, c), (1, 0, 2)).reshape(N_DEV, -1))
    flat = jnp.concatenate(pieces, axis=1)
    chunk = FLAT_COLS * row_mult
    padded = -(-flat.shape[1] // chunk) * chunk
    return jnp.pad(flat, ((0, 0), (0, padded - flat.shape[1]))).reshape(N_DEV, padded // FLAT_COLS, FLAT_COLS)


def _from_shards(stacked, axis):
    _, a, b = stacked.shape
    if axis == 1:
        return jnp.transpose(stacked, (1, 0, 2)).reshape(a, N_DEV * b)
    return stacked.reshape(N_DEV * a, b)


def _to_shards(full, axis):
    ra, rb = full.shape
    if axis == 1:
        return jnp.transpose(full.reshape(ra, N_DEV, rb // N_DEV), (1, 0, 2))
    return full.reshape(N_DEV, ra // N_DEV, rb)


class _ShardedMatmulWeights:
    def __init__(self, local):
        self.local, self.full, self.parts = local, {}, {}

    def gather_src(self, n, l):
        return self.local[n][l].astype(BF16)

    def set_gathered(self, n, l, got):
        self.full[n, l] = _from_shards(got, SHARD_AXIS[n] - 1)

    def get(self, n, l):
        return self.full[n, l]

    def scatter_src(self, n, l, dw):
        return _to_shards(dw, SHARD_AXIS[n] - 1).astype(BF16)

    def set_scattered(self, n, l, parts):
        self.parts[n, l] = parts


def _role(role, i):
    mixer = 'attn_w_' if i % 2 == 0 else 'conv_w_'
    return {'in': (mixer + 'in', i // 2), 'out': (mixer + 'out', i // 2),
            'up': ('ffn_w_up', i), 'down': ('ffn_w_down', i)}[role]


def _local_step(x, tgt, w, big):
    s = x.shape[0]

    def gather_of(role, i):
        if role is None or i >= DEPTH:
            return None, None
        key = _role(role, i)
        return key, (big.gather_src(*key), True)

    def project(a, role, i, ln=None, gather=None):
        nxt, carry = gather_of(gather, i + 1)
        weight = big.get(*_role(role, i))
        if ln is None:
            out, got = _mm(a, weight, carry=carry, name=role + "_fwd")
        else:
            *out, got = _mm_ln(a, weight, *ln, carry=carry, name=role + "_fwd")
        if got is not None:
            big.set_gathered(*nxt, got)
        return out

    def project_back(a, d_out, role, i, ln=None, scatter=True):
        key = _role(role, i)
        dw, _ = _mm(a, d_out, ta=True, name=role + "_dw")
        carry = (big.scatter_src(*key, dw), False)
        if not scatter:
            return _mm(d_out, big.get(*key), tb=True, name=role + "_dx")[0], (key, carry)
        if ln is None:
            d_in, parts = _mm(d_out, big.get(*key), tb=True, carry=carry, name=role + "_dx")
        else:
            *d_in, parts = _mm_ln_bwd(d_out, big.get(*key), *ln, carry=carry, name=role + "_dx")
        if parts is not None:
            big.set_scattered(*key, parts)
        return d_in

    na_tq = NA_QROWS * GRID_W
    band_idx = [_band_bucket_index(d) for _, d in DIL_PATTERNS]
    band_bias = [_band_bias(w['t5_bias'], d, name=f"band_bias_{d}") for _, d in DIL_PATTERNS]
    band_mask = [_mask_tiles("band", DIL_TQ, DIL_KW, s // d, name=f"band_mask_{d}") for _, d in DIL_PATTERNS]
    na_mask = _mask_tiles("na", na_tq, 3 * na_tq, s, name="na_mask")
    na_cols, dil_cols = (0, 4, 8), (12, 16, 20)
    grads = {n: [None] * w[n].shape[0] for n in SMALL_SHARDED + REPLICATED if n != 't5_bias'}
    saved = []
    x16 = x.astype(BF16)

    for i in range(DEPTH):
        j = i // 2
        st = {'x': x, 'x16': x16}
        if i % 2 == 0:
            h = project(x16, 'in', i)
            na_bias = _na_bias(w['na_rpb'][j], name="na_bias")
            late = ('up', 'down', 'out') if i == 0 else ()
            rides = [gather_of(role, 0) for role in late] + [(None, None)] * 4
            outs, lses = [], []
            calls = [(na_bias, na_mask, 1, na_tq, NA_NSUB, na_cols, "na_fwd")] + [
                (bias, mask, d, DIL_TQ, DIL_NSUB[d], dil_cols, f"dil_fwd_{d}")
                for (_, d), bias, mask in zip(DIL_PATTERNS, band_bias, band_mask)]
            for (bias, mask, d, tq, nsub, cols, label), (key, carry) in zip(calls, rides):
                o, l, got = _attn_fwd(h, bias, mask, d=d, tq=tq, nsub=nsub, cols=cols, carry=carry, name=label)
                if got is not None:
                    big.set_gathered(*key, got)
                outs.append(o)
                lses.append(l)
            o_na, l_na = outs.pop(0), lses.pop(0)
            mid, mid16, l_dil = _dil_combine(o_na, outs, lses, name="dil_combine")
            st.update(h=h, mid=mid, mid16=mid16, l_na=l_na, l_dil=l_dil, na_bias=na_bias)
        else:
            h = project(x16, 'in', i)
            mid16, u2 = _conv_mid_fwd(h, w['conf_dw_w'][j], w['conf_dw_b'][j], w['conf_ln_g'][j], w['conf_ln_b'][j],
                                      w['sconv_w'][j], name="conv_mid_fwd")
            st.update(h=h, mid16=mid16, u2=u2)
        z_mix, xa, xa16 = project(mid16, 'out', i, ln=(x, w['mix_ln_g'][i], w['mix_ln_b'][i]), gather='out')
        hu = project(xa16, 'up', i, gather='in')
        nxt, carry = gather_of('up', i + 1)
        act16, got = _ffn_mid_fwd(hu, w['ffn_dw_w'][i], carry=carry, name="ffn_mid_fwd")
        if got is not None:
            big.set_gathered(*nxt, got)
        z_ffn, xb, xb16 = project(act16, 'down', i, ln=(xa, w['ffn_ln_g'][i], w['ffn_ln_b'][i]), gather='down')
        st.update(z_mix=z_mix, xa16=xa16, hu=hu, act16=act16, z_ffn=z_ffn)
        saved.append(st)
        x, x16 = xb, xb16

    loss, d_loss = _loss_head(x, tgt, name="loss_head")
    g_t5 = None
    dz, dz16, dg, db = _ln_bwd(saved[-1]['z_ffn'], w['ffn_ln_g'][-1], d_loss, None, name="last_ln_bwd")
    for i in reversed(range(DEPTH)):
        j = i // 2
        st = saved[i]
        grads['ffn_ln_g'][i], grads['ffn_ln_b'][i] = dg, db
        dact, (key, carry) = project_back(st['act16'], dz16, 'down', i, scatter=False)
        dhu, grads['ffn_dw_w'][i], parts = _ffn_mid_bwd(st['hu'], w['ffn_dw_w'][i], dact, carry=carry,
                                                       name="ffn_mid_bwd")
        if parts is not None:
            big.set_scattered(*key, parts)
        dz, dz1, dg, db = project_back(st['xa16'], dhu, 'up', i, ln=(st['z_mix'], w['mix_ln_g'][i], dz))
        grads['mix_ln_g'][i], grads['mix_ln_b'][i] = dg, db
        dmid = project_back(st['mid16'], dz1, 'out', i)
        if i % 2 == 0:
            h = st['h']
            dq, dk, dv, dbias = _attn_bwd(h, st['na_bias'], na_mask, dmid, st['mid'], st['l_na'], d=1, tq=na_tq,
                                          nsub=NA_NSUB, cols=na_cols, ycol=0, name="na_bwd")
            grads['na_rpb'][j] = _rpb_grad(dbias, name="rpb_grad")
            dil, dbs = [], []
            for (_, d), bias, mask in zip(DIL_PATTERNS, band_bias, band_mask):
                g = _attn_bwd(h, bias, mask, dmid, st['mid'], st['l_dil'], d=d, tq=DIL_TQ, nsub=DIL_NSUB[d],
                              cols=dil_cols, ycol=4, name=f"dil_bwd_{d}")
                dil.append(g[:3])
                dbs.append(g[3])
            t5 = _t5_grad(dbs, band_idx, name="t5_grad")
            g_t5 = t5 if g_t5 is None else g_t5 + t5
            dh = _attn_dh((dq, dk, dv), dil, name="attn_dh")
        else:
            dh, dw31, db31, dlg, dlb, dw3 = _conv_mid_bwd(st['h'], st['u2'], dmid, w['conf_dw_w'][j],
                                                          w['conf_ln_g'][j], w['conf_ln_b'][j], w['sconv_w'][j],
                                                          name="conv_mid_bwd")
            grads['conf_dw_w'][j], grads['conf_dw_b'][j] = dw31, db31
            grads['conf_ln_g'][j], grads['conf_ln_b'][j], grads['sconv_w'][j] = dlg, dlb, dw3
        if i > 0:
            below = saved[i - 1]
            dz, dz16, dg, db = project_back(st['x16'], dh, 'in', i, ln=(below['z_ffn'], w['ffn_ln_g'][i - 1], dz))
        else:
            dx = _axpy(dz, project_back(st['x16'], dh, 'in', i), name="grad_x")
    full = {n: jnp.stack(g) for n, g in grads.items()}
    full['t5_bias'] = g_t5
    return loss, dx, full


def kernel(x, t5_bias, attn_w_in, attn_w_out, na_rpb, conv_w_in, conf_dw_w, conf_dw_b, conf_ln_g, conf_ln_b, sconv_w, conv_w_out, ffn_w_up, ffn_dw_w, ffn_w_down, mix_ln_g, mix_ln_b, ffn_ln_g, ffn_ln_b, loss_target, m_t5_bias, m_attn_w_in, m_attn_w_out, m_na_rpb, m_conv_w_in, m_conf_dw_w, m_conf_dw_b, m_conf_ln_g, m_conf_ln_b, m_sconv_w, m_conv_w_out, m_ffn_w_up, m_ffn_dw_w, m_ffn_w_down, m_mix_ln_g, m_mix_ln_b, m_ffn_ln_g, m_ffn_ln_b, v_t5_bias, v_attn_w_in, v_attn_w_out, v_na_rpb, v_conv_w_in, v_conf_dw_w, v_conf_dw_b, v_conf_ln_g, v_conf_ln_b, v_sconv_w, v_conv_w_out, v_ffn_w_up, v_ffn_dw_w, v_ffn_w_down, v_mix_ln_g, v_mix_ln_b, v_ffn_ln_g, v_ffn_ln_b):
    args = dict(locals())
    local = {n: args[n] for n in WEIGHTS}
    mom1 = {n: args['m_' + n] for n in WEIGHTS}
    mom2 = {n: args['v_' + n] for n in WEIGHTS}

    kinds = ('grad', 'delta', 'new_m', 'new_v')
    small = {n: local[n] for n in REPLICATED}
    small.update(_gather_full(SMALL_SHARDED, local, F32, 8, "gather_small_weights"))
    big = _ShardedMatmulWeights({n: local[n] for n in MATMUL_WEIGHTS})
    n, l = _role('in', 0)
    big.set_gathered(n, l, _exchange(big.gather_src(n, l), gather=True, name="gather_first_in"))

    loss, dx, grads = _local_step(x[0], loss_target[0], small, big)
    loss = lax.psum(loss, MESH_AXES)

    out = {}
    for n in MATMUL_WEIGHTS:
        layers, a, b = local[n].shape
        res = _adamw([big.parts[n, l] for l in range(layers)],
                     *[t.reshape(layers * a, b) for t in (local[n], mom1[n], mom2[n])], name="adamw_" + n)
        for kind, r in zip(kinds, res):
            out[kind + '_' + n] = r.reshape(layers, a, b)
    for names, sharded, label in ((SMALL_SHARDED, True, "small"), (REPLICATED, False, "replicated")):
        shapes = [local[n].shape for n in names]
        if sharded:
            parts = _exchange(_scatter_rows(names, grads, 8), gather=False, name="scatter_small_grads")
        else:
            parts = _exchange(_flatten([grads[n] for n in names], F32, 8), gather=True, name="gather_replicated_grads")
        res = _adamw([parts], _flatten([local[n] for n in names], F32, 8), _flatten([mom1[n] for n in names], F32, 8),
                     _flatten([mom2[n] for n in names], F32, 8), name="adamw_" + label)
        for kind, flat in zip(kinds, res):
            for n, a in zip(names, _unflatten(flat, shapes)):
                out[kind + '_' + n] = a

    return (loss, dx[None], *[out[k + '_' + n] for k in ('grad', 'delta', 'new_m', 'new_v') for n in WEIGHTS])
```

```python
import math

import jax
import jax.numpy as jnp
import numpy as np
from jax import lax
from jax.experimental import pallas as pl
from jax.experimental.pallas import tpu as pltpu

F32 = jnp.float32
BF16 = jnp.bfloat16

N_DEV = 8
MESH_AXES = ("x", "y", "c")
DEPTH = 4
GRID_W = 64
GRID_SHIFT = 6
HEAD_DIM = 64
NA_KH = 8
NA_KW = 16
NA_QROWS = 4
DIL_PATTERNS = ((128, 1), (512, 4), (2048, 16))
DIL_HALF = 64
DIL_TQ = 128
DIL_KW = DIL_TQ + 2 * DIL_HALF
DIL_NSUB = {1: 8, 4: 2, 16: 1}
NA_NSUB = 4
N_BUCKETS = 32
T5_MAX_DIST = 1024
CONF_CH = 512
CONF_K = 31
SC_K = 3
FFN_K = 3
FFN_ROW_TILE = 128
FFN_STRIP = 128
LN_EPS = 1e-5
NEG = -1e30
ALPHA = (2 * DEPTH) ** 0.25
ADAM_LR = 0.001
ADAM_B1 = 0.9
ADAM_B2 = 0.999
ADAM_EPS = 1e-08
ADAM_WD = 0.01
ADAM_STEP = 10

LANES = 128
SUBLANES = 8
VMEM_BIG = 48 * 1024 * 1024
FLAT_COLS = 1024
FLAT_ROW_TILE = 256

WEIGHTS = ['t5_bias', 'attn_w_in', 'attn_w_out', 'na_rpb', 'conv_w_in', 'conf_dw_w', 'conf_dw_b', 'conf_ln_g',
           'conf_ln_b', 'sconv_w', 'conv_w_out', 'ffn_w_up', 'ffn_dw_w', 'ffn_w_down', 'mix_ln_g', 'mix_ln_b',
           'ffn_ln_g', 'ffn_ln_b']
SHARD_AXIS = {'attn_w_in': 2, 'attn_w_out': 1, 'conv_w_in': 2, 'conf_dw_w': 2, 'conf_dw_b': 1, 'conf_ln_g': 1,
              'conf_ln_b': 1, 'sconv_w': 2, 'conv_w_out': 1, 'ffn_w_up': 2, 'ffn_dw_w': 2, 'ffn_w_down': 1}
MATMUL_WEIGHTS = ['attn_w_in', 'attn_w_out', 'conv_w_in', 'conv_w_out', 'ffn_w_up', 'ffn_w_down']
SMALL_SHARDED = ['conf_dw_w', 'conf_dw_b', 'conf_ln_g', 'conf_ln_b', 'sconv_w', 'ffn_dw_w']
SHARDED = MATMUL_WEIGHTS + SMALL_SHARDED
REPLICATED = ['t5_bias', 'na_rpb', 'mix_ln_g', 'mix_ln_b', 'ffn_ln_g', 'ffn_ln_b']


def _tile(n, cands):
    for c in cands:
        if n % c == 0:
            return c
    return n


def _params(sem, vmem=None):
    return pltpu.CompilerParams(dimension_semantics=sem, vmem_limit_bytes=vmem)


def _sigmoid(x):
    return 0.5 * jnp.tanh(0.5 * x) + 0.5


MM_MAX_TILE = 1408
MM_MAX_K = 3072


def _lane_tile(n, cap):
    best = None
    for t in range(LANES, min(n, cap) + 1, LANES):
        if n % t == 0:
            best = t
    return best or n


def _mm(a, b, *, ta=False, tb=False, out_dtype=F32, carry=None, name):
    assert a.dtype == BF16 and b.dtype == BF16, (name, a.dtype, b.dtype)
    m, k = (a.shape[1], a.shape[0]) if ta else a.shape
    n = b.shape[0] if tb else b.shape[1]
    tm, tn, tk = _lane_tile(m, MM_MAX_TILE), _lane_tile(n, MM_MAX_TILE), _lane_tile(k, MM_MAX_K)
    grid = (m // tm, n // tn, k // tk)
    nk = grid[2]
    dims = (((0 if ta else 1,), (1 if tb else 0,)), ((), ()))
    use_acc = nk > 1 and out_dtype != F32
    ride = _Carried(carry, grid)

    def body(*refs):
        (a_ref, b_ref), (o_ref,), scratch = ride.split(refs, 2, 1)
        ride.start()
        part = lax.dot_general(a_ref[...], b_ref[...], dims, preferred_element_type=F32)
        if nk == 1:
            o_ref[...] = part.astype(out_dtype)
        else:
            acc_ref = scratch[0] if use_acc else o_ref
            kk = pl.program_id(2)

            @pl.when(kk == 0)
            def _():
                acc_ref[...] = part

            @pl.when(kk > 0)
            def _():
                acc_ref[...] += part

            if use_acc:
                @pl.when(kk == nk - 1)
                def _():
                    o_ref[...] = acc_ref[...].astype(out_dtype)

        ride.wait()

    a_spec = pl.BlockSpec((tk, tm), lambda i, j, q: (q, i)) if ta else pl.BlockSpec((tm, tk), lambda i, j, q: (i, q))
    b_spec = pl.BlockSpec((tn, tk), lambda i, j, q: (j, q)) if tb else pl.BlockSpec((tk, tn), lambda i, j, q: (q, j))
    o_spec = pl.BlockSpec((tm, tn), lambda i, j, q: (i, j))
    o_shape = jax.ShapeDtypeStruct((m, n), out_dtype)
    scratch = [pltpu.VMEM((tm, tn), F32)] if use_acc else []
    sem = ("arbitrary",) * 3 if ride.on else ("parallel", "parallel", "arbitrary")
    out = pl.pallas_call(
        body, name=name, grid=grid, in_specs=[a_spec, b_spec] + ride.specs, out_specs=[o_spec] + ride.specs,
        out_shape=[o_shape] + ride.out_shapes, scratch_shapes=scratch + ride.scratch,
        compiler_params=_params(sem, VMEM_BIG))(a, b, *ride.operands)
    return out[0], (out[1] if ride.on else None)


MM_LN_ROWS = 512


def _mm_ln(a, b, x, g, beta, *, carry=None, name):
    assert a.dtype == BF16 and b.dtype == BF16, (name, a.dtype, b.dtype)
    m, k = a.shape
    n = b.shape[1]
    assert k <= MM_MAX_K, (name, k)
    tm = _tile(m, (MM_LN_ROWS, 256, 128, 64, 8))
    grid = (m // tm,)
    ride = _Carried(carry, grid)

    def body(*refs):
        (a_ref, b_ref, x_ref, g_ref, beta_ref), (z_ref, o_ref, o16_ref), _ = ride.split(refs, 5, 3)
        ride.start()
        z = ALPHA * x_ref[...] + jnp.dot(a_ref[...], b_ref[...], preferred_element_type=F32)
        z_ref[...] = z
        xh, _ = _ln_stats(z)
        out = xh * g_ref[...] + beta_ref[...]
        o_ref[...] = out
        o16_ref[...] = out.astype(BF16)
        ride.wait()

    row = lambda width: pl.BlockSpec((tm, width), lambda i: (i, 0))
    vec = pl.BlockSpec((1, n), lambda i: (0, 0))
    out = pl.pallas_call(
        body, name=name, grid=grid,
        in_specs=[row(k), pl.BlockSpec((k, n), lambda i: (0, 0)), row(n), vec, vec] + ride.specs,
        out_specs=[row(n)] * 3 + ride.specs,
        out_shape=[jax.ShapeDtypeStruct((m, n), F32), jax.ShapeDtypeStruct((m, n), F32),
                   jax.ShapeDtypeStruct((m, n), BF16)] + ride.out_shapes,
        scratch_shapes=ride.scratch,
        compiler_params=_params(("arbitrary",) if ride.on else ("parallel",), VMEM_BIG))(
            a, b, x, g.reshape(1, n), beta.reshape(1, n), *ride.operands)
    return out[0], out[1], out[2], (out[3] if ride.on else None)


def _mm_ln_bwd(d_out, b, z, g, d1, *, carry=None, name):
    assert d_out.dtype == BF16 and b.dtype == BF16, (name, d_out.dtype, b.dtype)
    m, k = d_out.shape
    n = b.shape[0]
    tm, tk = _tile(m, (MM_LN_ROWS, 256, 128, 64, 8)), _lane_tile(k, MM_MAX_K)
    grid = (m // tm, k // tk)
    nk = grid[1]
    ride = _Carried(carry, grid)

    def body(*refs):
        (a_ref, b_ref, z_ref, g_ref, d1_ref), (dz_ref, dz16_ref, dg_ref, db_ref), scratch = ride.split(refs, 5, 4)
        ride.start()
        i, kk = pl.program_id(0), pl.program_id(1)
        part = lax.dot_general(a_ref[...], b_ref[...], (((1,), (1,)), ((), ())), preferred_element_type=F32)

        def finish(d2):
            @pl.when(i == 0)
            def _():
                dg_ref[...] = jnp.zeros_like(dg_ref)
                db_ref[...] = jnp.zeros_like(db_ref)

            dout = ALPHA * d1_ref[...] + d2
            xh, rstd = _ln_stats(z_ref[...])
            dxh = dout * g_ref[...]
            dz = rstd * (dxh - jnp.mean(dxh, axis=-1, keepdims=True) - xh * jnp.mean(dxh * xh, axis=-1, keepdims=True))
            dz_ref[...] = dz
            dz16_ref[...] = dz.astype(BF16)
            dg_ref[...] += jnp.sum(dout * xh, axis=0, keepdims=True)
            db_ref[...] += jnp.sum(dout, axis=0, keepdims=True)

        if nk == 1:
            finish(part)
        else:
            acc_ref = scratch[0]

            @pl.when(kk == 0)
            def _():
                acc_ref[...] = part

            @pl.when((kk > 0) & (kk < nk - 1))
            def _():
                acc_ref[...] += part

            @pl.when(kk == nk - 1)
            def _():
                finish(acc_ref[...] + part)

        ride.wait()

    row = pl.BlockSpec((tm, n), lambda i, q: (i, 0))
    vec = pl.BlockSpec((1, n), lambda i, q: (0, 0))
    out = pl.pallas_call(
        body, name=name, grid=grid,
        in_specs=[pl.BlockSpec((tm, tk), lambda i, q: (i, q)), pl.BlockSpec((n, tk), lambda i, q: (0, q)),
                  row, vec, row] + ride.specs,
        out_specs=[row, row, vec, vec] + ride.specs,
        out_shape=[jax.ShapeDtypeStruct((m, n), F32), jax.ShapeDtypeStruct((m, n), BF16),
                   jax.ShapeDtypeStruct((1, n), F32), jax.ShapeDtypeStruct((1, n), F32)] + ride.out_shapes,
        scratch_shapes=([pltpu.VMEM((tm, n), F32)] if nk > 1 else []) + ride.scratch,
        compiler_params=_params(("arbitrary", "arbitrary"), VMEM_BIG))(
            d_out, b, z, g.reshape(1, n), d1, *ride.operands)
    return out[0], out[1], out[2].reshape(n), out[3].reshape(n), (out[4] if ride.on else None)


def _ln_stats(z):
    mu = jnp.mean(z, axis=-1, keepdims=True)
    zc = z - mu
    var = jnp.mean(zc * zc, axis=-1, keepdims=True)
    rstd = lax.rsqrt(var + LN_EPS)
    return zc * rstd, rstd


def _ln_bwd(z, g, d1, d2, *, name):
    s, d = z.shape
    t = _tile(s, (256, 128, 64, 8))
    two = d2 is not None

    def body(*refs):
        if two:
            z_ref, g_ref, d1_ref, d2_ref, dz_ref, dz16_ref, dg_ref, db_ref = refs
            dout = ALPHA * d1_ref[...] + d2_ref[...]
        else:
            z_ref, g_ref, d1_ref, dz_ref, dz16_ref, dg_ref, db_ref = refs
            dout = d1_ref[...]

        @pl.when(pl.program_id(0) == 0)
        def _():
            dg_ref[...] = jnp.zeros_like(dg_ref)
            db_ref[...] = jnp.zeros_like(db_ref)

        xh, rstd = _ln_stats(z_ref[...])
        dxh = dout * g_ref[...]
        dz = rstd * (dxh - jnp.mean(dxh, axis=-1, keepdims=True) - xh * jnp.mean(dxh * xh, axis=-1, keepdims=True))
        dz_ref[...] = dz
        dz16_ref[...] = dz.astype(BF16)
        dg_ref[...] += jnp.sum(dout * xh, axis=0, keepdims=True)
        db_ref[...] += jnp.sum(dout, axis=0, keepdims=True)

    row = pl.BlockSpec((t, d), lambda i: (i, 0))
    vec = pl.BlockSpec((1, d), lambda i: (0, 0))
    ins = [z, g.reshape(1, d), d1] + ([d2] if two else [])
    specs = [row, vec, row] + ([row] if two else [])
    dz, dz16, dg, db = pl.pallas_call(
        body, name=name, grid=(s // t,), in_specs=specs, out_specs=[row, row, vec, vec],
        out_shape=[jax.ShapeDtypeStruct((s, d), F32), jax.ShapeDtypeStruct((s, d), BF16),
                   jax.ShapeDtypeStruct((1, d), F32), jax.ShapeDtypeStruct((1, d), F32)],
        compiler_params=_params(("arbitrary",)))(*ins)
    return dz, dz16, dg.reshape(d), db.reshape(d)


def _axpy(d1, d2, *, name):
    s, d = d1.shape
    t = _tile(s, (256, 128, 64, 8))

    def body(a_ref, b_ref, o_ref):
        o_ref[...] = ALPHA * a_ref[...] + b_ref[...]

    row = pl.BlockSpec((t, d), lambda i: (i, 0))
    return pl.pallas_call(body, name=name, grid=(s // t,), in_specs=[row, row], out_specs=row,
                          out_shape=jax.ShapeDtypeStruct((s, d), F32), compiler_params=_params(("parallel",)))(d1, d2)


def _loss_head(y, tgt, *, name):
    s, d = y.shape
    t = _tile(s, (256, 128, 64, 8))

    def body(y_ref, t_ref, l_ref, dy_ref):
        @pl.when(pl.program_id(0) == 0)
        def _():
            l_ref[...] = jnp.zeros_like(l_ref)

        err = y_ref[...] - t_ref[...]
        dy_ref[...] = err * (1.0 / d)
        l_ref[...] += 0.5 * jnp.sum(jnp.sum(err * err, axis=1, keepdims=True), axis=0, keepdims=True) * (1.0 / d)

    row = pl.BlockSpec((t, d), lambda i: (i, 0))
    one = pl.BlockSpec((SUBLANES, LANES), lambda i: (0, 0))
    loss, dy = pl.pallas_call(
        body, name=name, grid=(s // t,), in_specs=[row, row], out_specs=[one, row],
        out_shape=[jax.ShapeDtypeStruct((SUBLANES, LANES), F32), jax.ShapeDtypeStruct((s, d), F32)],
        compiler_params=_params(("arbitrary",)))(y, tgt)
    return loss[0, 0], dy


def _halo_specs(s, t, halo, cb, col):
    per = t // halo
    last = s // halo - 1
    return [pl.BlockSpec((t, cb), lambda j, i: (i, col(j))),
            pl.BlockSpec((halo, cb), lambda j, i: (jnp.maximum(i * per - 1, 0), col(j))),
            pl.BlockSpec((halo, cb), lambda j, i: (jnp.minimum((i + 1) * per, last), col(j)))]


def _extended(main_ref, prev_ref, next_ref, i, n):
    prev = jnp.where(i > 0, prev_ref[...], 0.0)
    nxt = jnp.where(i < n - 1, next_ref[...], 0.0)
    return jnp.concatenate([prev, main_ref[...], nxt], axis=0)


def _shift(ext, o):
    if o == 0:
        return ext
    return pltpu.roll(ext, (-o) % ext.shape[0], 0)


def _taps(ext, k, sign=1):
    return [_shift(ext, sign * (j - k // 2)) for j in range(k)]


def _conv(ext, w_ref, k, sign=1, taps=None):
    taps = _taps(ext, k, sign) if taps is None else taps
    acc = None
    for j in range(k):
        term = w_ref[j:j + 1, :] * taps[j]
        acc = term if acc is None else acc + term
    return acc


def _main_taps(ext, k, halo, t, sign=1):
    rolled, taps = {}, []
    for j in range(k):
        offset = sign * (j - k // 2)
        res = offset % SUBLANES
        if res not in rolled:
            rolled[res] = _shift(ext, res)
        start = halo + offset - res
        taps.append(rolled[res][start:start + t])
    return taps


def _conv_main(taps, w_ref):
    acc = None
    for j, tap in enumerate(taps):
        term = w_ref[j:j + 1, :] * tap
        acc = term if acc is None else acc + term
    return acc


def _wgrad_main(dw_ref, d_main, taps):
    for j, tap in enumerate(taps):
        dw_ref[j:j + 1, :] += jnp.sum(d_main * tap, axis=0, keepdims=True)


def _conv_wgrad(dw_ref, d_main, x_ext, k, halo, t, taps=None):
    taps = _taps(x_ext, k) if taps is None else taps
    for j in range(k):
        dw_ref[j:j + 1, :] += jnp.sum(d_main * taps[j][halo:halo + t], axis=0, keepdims=True)


def _ffn_mid_fwd(hu, w, *, carry=None, name):
    s, f2 = hu.shape
    f = f2 // 2
    t, cb, halo = _tile(s, (FFN_ROW_TILE,)), _lane_tile(f, FFN_STRIP), SUBLANES
    nt, nc = s // t, f // cb
    ride = _Carried(carry, (1, nt))

    def body(*refs):
        (h_ref, hp_ref, hn_ref, w_ref), (a_ref,), _ = ride.split(refs, 4, 1)
        ride.start()
        i = pl.program_id(1)
        first, last = i == 0, i == nt - 1

        def extended(cols):
            return jnp.concatenate([jnp.where(first, 0.0, hp_ref[:, cols]), h_ref[:, cols],
                                    jnp.where(last, 0.0, hn_ref[:, cols])], axis=0)

        for c in range(nc):
            gcols, ucols = pl.ds(c * cb, cb), pl.ds(f + c * cb, cb)
            hg = _conv_main(_main_taps(extended(gcols), FFN_K, halo, t), w_ref.at[:, gcols])
            hu_ = _conv_main(_main_taps(extended(ucols), FFN_K, halo, t), w_ref.at[:, ucols])
            a_ref[:, gcols] = (hg * _sigmoid(hg) * hu_).astype(BF16)
        ride.wait()

    specs = _halo_specs(s, t, halo, f2, lambda j: 0) + [pl.BlockSpec((FFN_K, f2), lambda j, i: (0, 0))]
    out = pl.pallas_call(
        body, name=name, grid=(1, nt), in_specs=specs + ride.specs,
        out_specs=[pl.BlockSpec((t, f), lambda j, i: (i, 0))] + ride.specs,
        out_shape=[jax.ShapeDtypeStruct((s, f), BF16)] + ride.out_shapes, scratch_shapes=ride.scratch,
        compiler_params=_params(("arbitrary", "arbitrary") if ride.on else ("parallel", "parallel"), VMEM_BIG))(
            hu, hu, hu, w, *ride.operands)
    return out[0], (out[1] if ride.on else None)


def _ffn_mid_bwd(hu, w, da, *, carry=None, name):
    s, f2 = hu.shape
    f = f2 // 2
    t, cb, halo = _tile(s, (FFN_ROW_TILE,)), _lane_tile(f, FFN_STRIP), SUBLANES
    nt, nc = s // t, f // cb
    ride = _Carried(carry, (1, nt))

    def body(*refs):
        (h_ref, hp_ref, hn_ref, a_ref, ap_ref, an_ref, w_ref), (dh_ref, dw_ref), _ = ride.split(refs, 7, 2)
        ride.start()
        i = pl.program_id(1)

        @pl.when(i == 0)
        def _():
            dw_ref[...] = jnp.zeros_like(dw_ref)

        first, last = i == 0, i == nt - 1

        def extended(main, prev, nxt, cols):
            return jnp.concatenate([jnp.where(first, 0.0, prev[:, cols]), main[:, cols],
                                    jnp.where(last, 0.0, nxt[:, cols])], axis=0)

        for c in range(nc):
            gcols, ucols, acols = pl.ds(c * cb, cb), pl.ds(f + c * cb, cb), pl.ds(c * cb, cb)
            xg = extended(h_ref, hp_ref, hn_ref, gcols)
            xu = extended(h_ref, hp_ref, hn_ref, ucols)
            dae = extended(a_ref, ap_ref, an_ref, acols)
            wg, wu = w_ref.at[:, gcols], w_ref.at[:, ucols]
            xg_taps, xu_taps = _taps(xg, FFN_K), _taps(xu, FFN_K)
            hg = _conv(xg, wg, FFN_K, taps=xg_taps)
            hu_ = _conv(xu, wu, FFN_K, taps=xu_taps)
            sg = _sigmoid(hg)
            d_hg = dae * hu_ * (sg * (1.0 + hg * (1.0 - sg)))
            d_hu = dae * (hg * sg)
            dh_ref[:, gcols] = _conv(d_hg, wg, FFN_K, sign=-1)[halo:halo + t].astype(BF16)
            dh_ref[:, ucols] = _conv(d_hu, wu, FFN_K, sign=-1)[halo:halo + t].astype(BF16)
            _conv_wgrad(dw_ref.at[:, gcols], d_hg[halo:halo + t], xg, FFN_K, halo, t, taps=xg_taps)
            _conv_wgrad(dw_ref.at[:, ucols], d_hu[halo:halo + t], xu, FFN_K, halo, t, taps=xu_taps)
        ride.wait()

    whole = lambda j: 0
    specs = (_halo_specs(s, t, halo, f2, whole) + _halo_specs(s, t, halo, f, whole)
             + [pl.BlockSpec((FFN_K, f2), lambda j, i: (0, 0))])
    out = pl.pallas_call(
        body, name=name, grid=(1, nt), in_specs=specs + ride.specs,
        out_specs=[pl.BlockSpec((t, f2), lambda j, i: (i, 0)), pl.BlockSpec((FFN_K, f2), lambda j, i: (0, 0))]
        + ride.specs,
        out_shape=[jax.ShapeDtypeStruct((s, f2), BF16), jax.ShapeDtypeStruct((FFN_K, f2), F32)] + ride.out_shapes,
        scratch_shapes=ride.scratch,
        compiler_params=_params(("arbitrary", "arbitrary"), VMEM_BIG))(hu, hu, hu, da, da, da, w, *ride.operands)
    return out[0], out[1], (out[2] if ride.on else None)


CONV_HALO = 16


def _conv_mid_fwd(h, dw_w, dw_b, ln_g, ln_b, sc_w, *, name):
    s = h.shape[0]
    c = CONF_CH
    t, halo = _tile(s, (256, 128)), CONV_HALO
    nt = s // t

    def body(ca, cap, can, cg, cgp, cgn, gb, gc, gcp, gcn, hx, hxp, hxn, w31, b31, lg, lb, w3, o_ref, u2_ref):
        i = pl.program_id(1)
        first, last = i == 0, i == nt - 1

        def extended(main, prev, nxt, cols):
            return jnp.concatenate([jnp.where(first, 0.0, prev[:, cols]), main[:, cols],
                                    jnp.where(last, 0.0, nxt[:, cols])], axis=0)

        for strip in range(c // LANES):
            cols = pl.ds(strip * LANES, LANES)
            u1 = extended(ca, cap, can, cols) * _sigmoid(extended(cg, cgp, cgn, cols))
            u2_ref[:, cols] = _conv_main(_main_taps(u1, CONF_K, halo, t), w31.at[:, cols]) + b31[:, cols]
            p = extended(gc, gcp, gcn, cols) * extended(hx, hxp, hxn, cols)
            conv = _conv_main(_main_taps(p, SC_K, halo, t), w3.at[:, cols])
            o_ref[:, pl.ds(c + strip * LANES, LANES)] = (gb[:, cols] * conv).astype(BF16)
        xh, _ = _ln_stats(u2_ref[...])
        yl = xh * lg[...] + lb[...]
        o_ref[:, 0:c] = (yl * _sigmoid(yl)).astype(BF16)

    hs = lambda blk: _halo_specs(s, t, halo, c, lambda j: blk)
    vec = lambda r: pl.BlockSpec((r, c), lambda j, i: (0, 0))
    specs = hs(0) + hs(1) + hs(2)[:1] + hs(3) + hs(4) + [vec(CONF_K), vec(1), vec(1), vec(1), vec(SC_K)]
    return pl.pallas_call(
        body, name=name, grid=(1, nt), in_specs=specs,
        out_specs=[pl.BlockSpec((t, 2 * c), lambda j, i: (i, 0)), pl.BlockSpec((t, c), lambda j, i: (i, 0))],
        out_shape=[jax.ShapeDtypeStruct((s, 2 * c), BF16), jax.ShapeDtypeStruct((s, c), F32)],
        compiler_params=_params(("parallel", "parallel")))(
            h, h, h, h, h, h, h, h, h, h, h, h, h, dw_w, dw_b.reshape(1, c), ln_g.reshape(1, c),
            ln_b.reshape(1, c), sc_w)


def _conv_mid_bwd(h, u2, dm, dw_w, ln_g, ln_b, sc_w, *, name):
    s = h.shape[0]
    c = CONF_CH
    t, halo = _tile(s, (256, 128)), CONV_HALO
    nt = s // t

    def body(ca, cap, can, cg, cgp, cgn, gb, gbp, gbn, gc, gcp, gcn, hx, hxp, hxn, u2r, u2p, u2n,
             du, dup, dun, dz, dzp, dzn, w31, lg, lb, w3,
             dh_ref, dw31_ref, db31_ref, dlg_ref, dlb_ref, dw3_ref, u1_s, du2_s, p_s, dc3_s):
        i = pl.program_id(1)

        @pl.when(i == 0)
        def _():
            for r in (dw31_ref, db31_ref, dlg_ref, dlb_ref, dw3_ref):
                r[...] = jnp.zeros_like(r)

        main = slice(halo, halo + t)
        xh, rstd = _ln_stats(_extended(u2r, u2p, u2n, i, nt))
        yl = xh * lg[...] + lb[...]
        sg = _sigmoid(yl)
        d_yl = _extended(du, dup, dun, i, nt) * (sg * (1.0 + yl * (1.0 - sg)))
        dlg_ref[...] += jnp.sum((d_yl * xh)[main], axis=0, keepdims=True)
        dlb_ref[...] += jnp.sum(d_yl[main], axis=0, keepdims=True)
        dxh = d_yl * lg[...]
        du2 = rstd * (dxh - jnp.mean(dxh, axis=-1, keepdims=True) - xh * jnp.mean(dxh * xh, axis=-1, keepdims=True))
        db31_ref[...] += jnp.sum(du2[main], axis=0, keepdims=True)
        du2_s[...] = du2
        u1_s[...] = _extended(ca, cap, can, i, nt) * _sigmoid(_extended(cg, cgp, cgn, i, nt))
        p_s[...] = _extended(gc, gcp, gcn, i, nt) * _extended(hx, hxp, hxn, i, nt)
        dc3_s[...] = _extended(dz, dzp, dzn, i, nt) * _extended(gb, gbp, gbn, i, nt)
        for strip in range(c // LANES):
            cols = pl.ds(strip * LANES, LANES)
            at = lambda part: pl.ds(part * c + strip * LANES, LANES)
            du2_c = du2_s[:, cols]
            _wgrad_main(dw31_ref.at[:, cols], du2_c[main], _main_taps(u1_s[:, cols], CONF_K, halo, t))
            du1 = _conv_main(_main_taps(du2_c, CONF_K, halo, t, sign=-1), w31.at[:, cols])
            sc = _sigmoid(cg[:, cols])
            dh_ref[:, at(0)] = (du1 * sc).astype(BF16)
            dh_ref[:, at(1)] = (du1 * (ca[:, cols] * sc * (1.0 - sc))).astype(BF16)
            dc3_c = dc3_s[:, cols]
            p_taps = _main_taps(p_s[:, cols], SC_K, halo, t)
            dh_ref[:, at(2)] = (dz[:, cols] * _conv_main(p_taps, w3.at[:, cols])).astype(BF16)
            _wgrad_main(dw3_ref.at[:, cols], dc3_c[main], p_taps)
            dp = _conv_main(_main_taps(dc3_c, SC_K, halo, t, sign=-1), w3.at[:, cols])
            dh_ref[:, at(3)] = (dp * hx[:, cols]).astype(BF16)
            dh_ref[:, at(4)] = (dp * gc[:, cols]).astype(BF16)

    hs = lambda blk: _halo_specs(s, t, halo, c, lambda j: blk)
    vec = lambda r: pl.BlockSpec((r, c), lambda j, i: (0, 0))
    specs = (hs(0) + hs(1) + hs(2) + hs(3) + hs(4) + hs(0) + hs(0) + hs(1)
             + [vec(CONF_K), vec(1), vec(1), vec(SC_K)])
    ext = pltpu.VMEM((t + 2 * halo, c), F32)
    outs = pl.pallas_call(
        body, name=name, grid=(1, nt), in_specs=specs,
        out_specs=[pl.BlockSpec((t, 5 * c), lambda j, i: (i, 0)), vec(CONF_K), vec(1), vec(1), vec(1), vec(SC_K)],
        out_shape=[jax.ShapeDtypeStruct((s, 5 * c), BF16), jax.ShapeDtypeStruct((CONF_K, c), F32),
                   jax.ShapeDtypeStruct((1, c), F32), jax.ShapeDtypeStruct((1, c), F32),
                   jax.ShapeDtypeStruct((1, c), F32), jax.ShapeDtypeStruct((SC_K, c), F32)],
        scratch_shapes=[ext, ext, ext, ext],
        compiler_params=_params(("arbitrary", "arbitrary")))(
            h, h, h, h, h, h, h, h, h, h, h, h, h, h, h, u2, u2, u2, dm, dm, dm, dm, dm, dm,
            dw_w, ln_g.reshape(1, c), ln_b.reshape(1, c), sc_w)
    dh, dw31, db31, dlg, dlb, dw3 = outs
    return dh, dw31, db31.reshape(c), dlg.reshape(c), dlb.reshape(c), dw3


def _attn_mask(kind, n, tq, kw, length):
    pad = (kw - tq) // 2
    iq = lax.broadcasted_iota(jnp.int32, (tq, 1), 0)
    ik = lax.broadcasted_iota(jnp.int32, (1, kw), 1)
    if kind == "band":
        rel = ik - pad - iq
        kpos = n * tq - pad + ik
        return (jnp.abs(rel) <= DIL_HALF) & (kpos >= 0) & (kpos < length)
    rows = length // GRID_W
    rq = n * NA_QROWS + (iq >> GRID_SHIFT)
    cq = iq & (GRID_W - 1)
    rk = n * NA_QROWS - pad // GRID_W + (ik >> GRID_SHIFT)
    ck = ik & (GRID_W - 1)
    r0 = jnp.clip(rq - NA_KH // 2, 0, rows - NA_KH)
    c0 = jnp.clip(cq - NA_KW // 2, 0, GRID_W - NA_KW)
    return (rk >= r0) & (rk < r0 + NA_KH) & (ck >= c0) & (ck < c0 + NA_KW)


def _mask_tiles(kind, tq, kw, length, *, name):
    nb = length // tq

    def body(o_ref):
        v = pl.program_id(0)
        n = jnp.where((v == 1) | (v == 3), 0, jnp.where(v == 2, nb - 1, 1))
        o_ref[0] = jnp.where(_attn_mask(kind, n, tq, kw, length), 0.0, NEG)

    return pl.pallas_call(body, name=name, grid=(4,), out_specs=pl.BlockSpec((1, tq, kw), lambda v: (v, 0, 0)),
                          out_shape=jax.ShapeDtypeStruct((4, tq, kw), F32),
                          compiler_params=_params(("parallel",)))()


class _AttnGeom:
    def __init__(self, s, d, tq, nsub, kw):
        self.s, self.d, self.tq, self.nsub = s, d, tq, nsub
        self.halo = (kw - tq) // 2 * d
        self.rows = nsub * tq * d
        self.nbig = s // self.rows
        self.ext = self.rows + 2 * self.halo
        assert s % self.rows == 0 and self.rows % self.halo == 0

    def main(self, col):
        return pl.BlockSpec((self.rows, LANES), lambda hp, n: (n, col + hp))

    def with_halos(self, col):
        per, last = self.rows // self.halo, self.s // self.halo - 1
        return [self.main(col),
                pl.BlockSpec((self.halo, LANES), lambda hp, n: (jnp.maximum(n * per - 1, 0), col + hp)),
                pl.BlockSpec((self.halo, LANES), lambda hp, n: (jnp.minimum((n + 1) * per, last), col + hp))]

    def fill_ext(self, ext_ref, main_ref, prev_ref, next_ref):
        ext_ref[0:self.halo] = prev_ref[...].astype(F32)
        ext_ref[self.halo:self.halo + self.rows] = main_ref[...].astype(F32)
        ext_ref[self.halo + self.rows:self.ext] = next_ref[...].astype(F32)

    def rows_of(self, r, pos, count):
        start = r + pos * self.d
        return pl.ds(start, count, stride=self.d) if self.d > 1 else pl.ds(start, count)

    def variant(self, n, sub):
        v = 0
        if sub == 0:
            v = v + jnp.where(n == 0, 1, 0)
        if sub == self.nsub - 1:
            v = v + jnp.where(n == self.nbig - 1, 2, 0)
        return v


def _stack_heads(x, low):
    return jnp.concatenate([jnp.where(low, x, 0.0), jnp.where(low, 0.0, x)], axis=0)


def _attn_fwd(h, bias, mask, *, d, tq, nsub, cols, carry=None, name):
    kw = bias.shape[2]
    geo = _AttnGeom(h.shape[0], d, tq, nsub, kw)
    scale = HEAD_DIM ** -0.5

    ride = _Carried(carry, (4, geo.nbig))

    def body(*refs):
        (q_ref, km, kp, kn, vm, vp, vn, b_ref, m_ref), (o_ref, l_ref), (kext, vext) = ride.split(refs, 9, 2)
        ride.start()
        n = pl.program_id(1)
        geo.fill_ext(kext, km, kp, kn)
        geo.fill_ext(vext, vm, vp, vn)
        low = lax.broadcasted_iota(jnp.int32, (1, LANES), 1) < HEAD_DIM
        for r in range(d):
            for sub in range(nsub):
                madd = m_ref[geo.variant(n, sub)]
                q = q_ref[geo.rows_of(r, sub * tq, tq), :].astype(F32) * scale
                ks = kext[geo.rows_of(r, sub * tq, kw), :].astype(BF16)
                vs = vext[geo.rows_of(r, sub * tq, kw), :].astype(BF16)
                q2 = _stack_heads(q, low).astype(BF16)
                sc = (lax.dot_general(q2, ks, (((1,), (1,)), ((), ())), preferred_element_type=F32)
                      + (b_ref[...] + madd).reshape(2 * tq, kw))
                m = jnp.max(sc, axis=1, keepdims=True)
                p = jnp.exp(sc - m)
                den = jnp.sum(p, axis=1, keepdims=True)
                out2 = jnp.dot((p / den).astype(BF16), vs, preferred_element_type=F32)
                lse2 = m + jnp.log(den)
                o_ref[geo.rows_of(r, sub * tq, tq), :] = jnp.where(low, out2[0:tq], out2[tq:2 * tq])
                l_ref[geo.rows_of(r, sub * tq, tq), :] = jnp.where(low, lse2[0:tq], lse2[tq:2 * tq])
        ride.wait()

    qc, kc, vc = cols
    specs = ([geo.main(qc)] + geo.with_halos(kc) + geo.with_halos(vc)
             + [pl.BlockSpec((2, tq, kw), lambda hp, n: (hp, 0, 0)),
                pl.BlockSpec((4, tq, kw), lambda hp, n: (0, 0, 0))])
    shape = jax.ShapeDtypeStruct((geo.s, 4 * LANES), F32)
    ext = pltpu.VMEM((geo.ext, LANES), F32)
    out = pl.pallas_call(
        body, name=name, grid=(4, geo.nbig), in_specs=specs + ride.specs,
        out_specs=[geo.main(0), geo.main(0)] + ride.specs, out_shape=[shape, shape] + ride.out_shapes,
        scratch_shapes=[ext, ext] + ride.scratch,
        compiler_params=_params(("arbitrary", "arbitrary") if ride.on else ("parallel", "parallel"), VMEM_BIG))(
            h, h, h, h, h, h, h, bias, mask, *ride.operands)
    return out[0], out[1], (out[2] if ride.on else None)


def _attn_bwd(h, bias, mask, dy, y, lse, *, d, tq, nsub, cols, ycol, name):
    kw = bias.shape[2]
    geo = _AttnGeom(h.shape[0], d, tq, nsub, kw)
    scale = HEAD_DIM ** -0.5
    halo, rows = geo.halo, geo.rows
    align = math.gcd(halo, LANES)

    def body(q_ref, km, kp, kn, vm, vp, vn, b_ref, m_ref, dy_ref, y_ref, l_ref, dq_ref, dk_hbm, dv_hbm, db_ref,
             kext, vext, dkext, dvext, dk_all, dv_all, sems):
        hp, n = pl.program_id(0), pl.program_id(1)

        @pl.when(n == 0)
        def _():
            dk_all[...] = jnp.zeros_like(dk_all)
            dv_all[...] = jnp.zeros_like(dv_all)
            db_ref[...] = jnp.zeros_like(db_ref)

        geo.fill_ext(kext, km, kp, kn)
        geo.fill_ext(vext, vm, vp, vn)
        dkext[...] = jnp.zeros_like(dkext)
        dvext[...] = jnp.zeros_like(dvext)
        low = lax.broadcasted_iota(jnp.int32, (1, LANES), 1) < HEAD_DIM
        for r in range(d):
            for sub in range(nsub):
                madd = m_ref[geo.variant(n, sub)]
                mine = geo.rows_of(r, sub * tq, tq)
                keys = geo.rows_of(r, sub * tq, kw)
                q = q_ref[mine, :].astype(F32) * scale
                ks = kext[keys, :].astype(BF16)
                vs = vext[keys, :].astype(BF16)
                dyv = dy_ref[mine, :]
                dyy = dyv * y_ref[mine, :]
                lse_all = l_ref[mine, :]
                q2 = _stack_heads(q, low).astype(BF16)
                dy2 = _stack_heads(dyv, low).astype(BF16)
                dsum = jnp.concatenate([jnp.sum(jnp.where(sel, dyy, 0.0), axis=1, keepdims=True)
                                        for sel in (low, ~low)], axis=0)
                lse2 = jnp.concatenate([lse_all[:, 0:1], lse_all[:, HEAD_DIM:HEAD_DIM + 1]], axis=0)
                sc = (lax.dot_general(q2, ks, (((1,), (1,)), ((), ())), preferred_element_type=F32)
                      + (b_ref[...] + madd).reshape(2 * tq, kw))
                p = jnp.exp(sc - lse2)
                dp = lax.dot_general(dy2, vs, (((1,), (1,)), ((), ())), preferred_element_type=F32)
                ds = p * (dp - dsum)
                db_ref[...] += ds.reshape(2, tq, kw)
                pb, dsb = p.astype(BF16), ds.astype(BF16)
                dv = lax.dot_general(pb, dy2, (((0,), (0,)), ((), ())), preferred_element_type=F32)
                dk = lax.dot_general(dsb, q2, (((0,), (0,)), ((), ())), preferred_element_type=F32)
                dq2 = jnp.dot(dsb, ks, preferred_element_type=F32)
                dq_ref[mine, :] = jnp.where(low, dq2[0:tq], dq2[tq:2 * tq]) * scale
                dkext[keys, :] += dk
                dvext[keys, :] += dv

        before = pl.multiple_of(jnp.maximum(n * rows - halo, 0), align)
        here = pl.multiple_of(n * rows, align)
        after = pl.multiple_of(jnp.minimum((n + 1) * rows, geo.s - halo), align)
        for ext, total in ((dkext, dk_all), (dvext, dv_all)):
            total[pl.ds(before, halo), :] += ext[0:halo]
            total[pl.ds(here, rows), :] += ext[halo:halo + rows]
            total[pl.ds(after, halo), :] += ext[halo + rows:geo.ext]

        @pl.when(n == geo.nbig - 1)
        def _():
            col = pl.ds(pl.multiple_of(hp * LANES, LANES), LANES)
            copies = [pltpu.make_async_copy(total, out.at[:, col], sems.at[i])
                      for i, (total, out) in enumerate(((dk_all, dk_hbm), (dv_all, dv_hbm)))]
            for cp in copies:
                cp.start()
            for cp in copies:
                cp.wait()

    qc, kc, vc = cols
    bspec = pl.BlockSpec((2, tq, kw), lambda hp, n: (hp, 0, 0))
    any_spec = pl.BlockSpec(memory_space=pl.ANY)
    specs = ([geo.main(qc)] + geo.with_halos(kc) + geo.with_halos(vc)
             + [bspec, pl.BlockSpec((4, tq, kw), lambda hp, n: (0, 0, 0)), geo.main(ycol), geo.main(ycol), geo.main(0)])
    shape = jax.ShapeDtypeStruct((geo.s, 4 * LANES), F32)
    ext = pltpu.VMEM((geo.ext, LANES), F32)
    whole = pltpu.VMEM((geo.s, LANES), F32)
    return pl.pallas_call(
        body, name=name, grid=(4, geo.nbig), in_specs=specs, out_specs=[geo.main(0), any_spec, any_spec, bspec],
        out_shape=[shape, shape, shape, jax.ShapeDtypeStruct(bias.shape, F32)],
        scratch_shapes=[ext, ext, ext, ext, whole, whole, pltpu.SemaphoreType.DMA((2,))],
        compiler_params=_params(("arbitrary", "arbitrary"), VMEM_BIG))(
            h, h, h, h, h, h, h, bias, mask, dy, y, lse)


def _dil_combine(o_na, outs, lses, *, name):
    s, c = outs[0].shape
    t = _tile(s, (512, 256, 128, 64, 8))

    def body(na, o0, o1, o2, l0, l1, l2, y_ref, y16_ref, lt_ref):
        ls = [l0[...], l1[...], l2[...]]
        m = jnp.maximum(jnp.maximum(ls[0], ls[1]), ls[2])
        es = [jnp.exp(l - m) for l in ls]
        den = es[0] + es[1] + es[2]
        y = (es[0] / den) * o0[...] + (es[1] / den) * o1[...] + (es[2] / den) * o2[...]
        lt_ref[...] = m + jnp.log(den)
        y_ref[:, 0:c] = na[...]
        y_ref[:, c:2 * c] = y
        y16_ref[:, 0:c] = na[...].astype(BF16)
        y16_ref[:, c:2 * c] = y.astype(BF16)

    row = pl.BlockSpec((t, c), lambda i: (i, 0))
    wide = pl.BlockSpec((t, 2 * c), lambda i: (i, 0))
    return pl.pallas_call(body, name=name, grid=(s // t,), in_specs=[row] * 7, out_specs=[wide, wide, row],
                          out_shape=[jax.ShapeDtypeStruct((s, 2 * c), F32), jax.ShapeDtypeStruct((s, 2 * c), BF16),
                                     jax.ShapeDtypeStruct((s, c), F32)],
                          compiler_params=_params(("parallel",)))(o_na, *outs, *lses)


def _attn_dh(na, dil, *, name):
    s, c = na[0].shape
    t = _tile(s, (256, 128, 64, 8))

    def body(*refs):
        ins, o_ref = refs[:-1], refs[-1]
        for a in range(3):
            o_ref[:, a * c:(a + 1) * c] = ins[a][...].astype(BF16)
            o_ref[:, (3 + a) * c:(4 + a) * c] = (ins[3 + a][...] + ins[6 + a][...] + ins[9 + a][...]).astype(BF16)

    row = pl.BlockSpec((t, c), lambda i: (i, 0))
    flat = list(na) + [g[a] for g in dil for a in range(3)]
    return pl.pallas_call(body, name=name, grid=(s // t,), in_specs=[row] * 12,
                          out_specs=pl.BlockSpec((t, 6 * c), lambda i: (i, 0)),
                          out_shape=jax.ShapeDtypeStruct((s, 6 * c), BF16),
                          compiler_params=_params(("parallel",)))(*flat)


def _t5_bucket(rel):
    nb = N_BUCKETS // 2
    max_exact = nb // 2
    ret = np.where(rel > 0, nb, 0)
    n = np.abs(rel)
    large = max_exact + (np.log(np.maximum(n, 1).astype(np.float32) / np.float32(max_exact))
                         / np.float32(math.log(T5_MAX_DIST / max_exact)) * np.float32(nb - max_exact)).astype(np.int32)
    large = np.minimum(large, nb - 1)
    return (ret + np.where(n < max_exact, n, large)).astype(np.int32)


def _band_bucket_index(dil):
    tq, kw = DIL_TQ, DIL_KW
    rel = np.arange(kw)[None, :] - (kw - tq) // 2 - np.arange(tq)[:, None]
    return _t5_bucket(rel * dil)


def _band_bias(t5, dil, *, name):
    tq, kw = DIL_TQ, DIL_KW
    buckets = [int(b) for b in _t5_bucket(np.arange(-DIL_HALF, DIL_HALF + 1) * dil)]

    def body(t_ref, o_ref):
        hh = pl.program_id(0)
        rel = (lax.broadcasted_iota(jnp.int32, (tq, kw), 1) - (kw - tq) // 2
               - lax.broadcasted_iota(jnp.int32, (tq, kw), 0))
        acc = jnp.zeros((tq, kw), F32)
        for r, b in zip(range(-DIL_HALF, DIL_HALF + 1), buckets):
            acc = jnp.where(rel == r, t_ref[b * 8 + hh], acc)
        o_ref[0] = acc

    return pl.pallas_call(body, name=name, grid=(8,),
                          in_specs=[pl.BlockSpec(memory_space=pltpu.SMEM)],
                          out_specs=pl.BlockSpec((1, tq, kw), lambda h: (h, 0, 0)),
                          out_shape=jax.ShapeDtypeStruct((8, tq, kw), F32),
                          compiler_params=_params(("parallel",)))(t5.reshape(-1))


def _na_bias(rpb, *, name):
    nr, nc = 2 * NA_KH - 1, 2 * NA_KW - 1
    tq = NA_QROWS * GRID_W
    w = GRID_W

    def body(r_ref, o_ref):
        base = pl.program_id(0) * (nr * nc)
        lane = lax.broadcasted_iota(jnp.int32, (w, LANES), 1)
        upper = lane >= w
        diff = (lane & (w - 1)) - lax.broadcasted_iota(jnp.int32, (w, LANES), 0) + NA_KW - 1
        tiles = {}
        for i in range(NA_QROWS):
            for m in range(3 * NA_QROWS // 2):
                lo = 2 * m - i + NA_KH - 1 - NA_QROWS
                if lo not in tiles:
                    acc = jnp.zeros((w, LANES), F32)
                    for dc in range(nc):
                        v_lo = r_ref[base + lo * nc + dc] if 0 <= lo < nr else 0.0
                        v_hi = r_ref[base + (lo + 1) * nc + dc] if 0 <= lo + 1 < nr else 0.0
                        acc = jnp.where(diff == dc, jnp.where(upper, v_hi, v_lo), acc)
                    tiles[lo] = acc
                o_ref[0, i * w:(i + 1) * w, m * LANES:(m + 1) * LANES] = tiles[lo]

    return pl.pallas_call(body, name=name, grid=(8,),
                          in_specs=[pl.BlockSpec(memory_space=pltpu.SMEM)],
                          out_specs=pl.BlockSpec((1, tq, 3 * tq), lambda h: (h, 0, 0)),
                          out_shape=jax.ShapeDtypeStruct((8, tq, 3 * tq), F32),
                          compiler_params=_params(("parallel",)))(rpb.reshape(-1))


def _t5_grad(dbs, idxs, *, name):
    def body(d0, d1, d2, i0, i1, i2, o_ref):
        lane = lax.broadcasted_iota(jnp.int32, (1, LANES), 1)
        lines = [jnp.zeros((1, LANES), F32) for _ in range(8)]
        for dref, iref in ((d0, i0), (d1, i1), (d2, i2)):
            idx = iref[...]
            for hh in range(8):
                xh = dref[hh]
                for b in range(N_BUCKETS):
                    val = jnp.sum(jnp.sum(jnp.where(idx == b, xh, 0.0), axis=1, keepdims=True), axis=0, keepdims=True)
                    lines[hh] = lines[hh] + jnp.where(lane == b, val, 0.0)
        for hh in range(8):
            o_ref[hh:hh + 1, :] = lines[hh]

    out = pl.pallas_call(body, name=name, out_shape=jax.ShapeDtypeStruct((8, LANES), F32))(*dbs, *idxs)
    return out[:, :N_BUCKETS].T


def _rpb_grad(db, *, name):
    nr, nc = 2 * NA_KH - 1, 2 * NA_KW - 1
    tq = NA_QROWS * GRID_W
    w = GRID_W

    def body(d_ref, o_ref):
        x = d_ref[0]
        rows = []
        for dr in range(nr):
            acc = jnp.zeros((w, w), F32)
            for i in range(NA_QROWS):
                j = i + dr - (NA_KH - 1 - NA_QROWS)
                if 0 <= j < 3 * NA_QROWS:
                    acc = acc + x[i * w:(i + 1) * w, j * w:(j + 1) * w]
            rows.append(acc)
        diff = (lax.broadcasted_iota(jnp.int32, (w, w), 1) - lax.broadcasted_iota(jnp.int32, (w, w), 0)
                + NA_KW - 1)
        lane = lax.broadcasted_iota(jnp.int32, (1, LANES), 1)
        for dr in range(nr):
            line = jnp.zeros((1, LANES), F32)
            for dc in range(nc):
                val = jnp.sum(jnp.sum(jnp.where(diff == dc, rows[dr], 0.0), axis=1, keepdims=True),
                              axis=0, keepdims=True)
                line = jnp.where(lane == dc, val, line)
            o_ref[0, dr:dr + 1, :] = line

    out = pl.pallas_call(body, name=name, grid=(8,),
                         in_specs=[pl.BlockSpec((1, tq, 3 * tq), lambda h: (h, 0, 0))],
                         out_specs=pl.BlockSpec((1, nr, LANES), lambda h: (h, 0, 0)),
                         out_shape=jax.ShapeDtypeStruct((8, nr, LANES), F32),
                         compiler_params=_params(("parallel",)))(db)
    return out[:, :, :nc]


def _exchange(src, *, gather, name):
    shape = src.shape if not gather else (N_DEV,) + src.shape

    def body(src_ref, out_ref, send_sems, recv_sems, local_sem):
        _exchange_start(src_ref, out_ref, send_sems, recv_sems, local_sem, gather)
        _exchange_wait(src_ref, out_ref, send_sems, recv_sems, local_sem, gather)

    any_spec = pl.BlockSpec(memory_space=pl.ANY)
    return pl.pallas_call(
        body, name=name, in_specs=[any_spec], out_specs=any_spec, out_shape=jax.ShapeDtypeStruct(shape, src.dtype),
        scratch_shapes=_exchange_sems())(src)


class _Carried:
    def __init__(self, carry, grid):
        self.src, self.gather = carry if carry is not None else (None, False)
        self.on = self.src is not None
        self.grid = grid
        self.operands = [self.src] if self.on else []
        self.specs = [pl.BlockSpec(memory_space=pl.ANY)] if self.on else []
        self.scratch = _exchange_sems() if self.on else []
        self.out_shapes = []
        if self.on:
            shape = ((N_DEV,) + self.src.shape) if self.gather else self.src.shape
            self.out_shapes = [jax.ShapeDtypeStruct(shape, self.src.dtype)]

    def split(self, refs, n_in, n_out):
        refs = list(refs)
        if not self.on:
            return refs[:n_in], refs[n_in:n_in + n_out], refs[n_in + n_out:]
        self.refs = (refs[n_in], refs[n_in + 1 + n_out], *refs[-3:], self.gather)
        return refs[:n_in], refs[n_in + 1:n_in + 1 + n_out], refs[n_in + 2 + n_out:-3]

    def _at(self, last):
        hit = None
        for ax, size in enumerate(self.grid):
            here = pl.program_id(ax) == (size - 1 if last else 0)
            hit = here if hit is None else hit & here
        return hit

    def start(self):
        if self.on:
            pl.when(self._at(False))(lambda: _exchange_start(*self.refs))

    def wait(self):
        if self.on:
            pl.when(self._at(True))(lambda: _exchange_wait(*self.refs))


def _exchange_sems():
    return [pltpu.SemaphoreType.DMA((N_DEV - 1,)), pltpu.SemaphoreType.DMA((N_DEV - 1,)), pltpu.SemaphoreType.DMA]


def _exchange_copies(src_ref, out_ref, send_sems, recv_sems, local_sem, gather):
    x, y, c = lax.axis_index("x"), lax.axis_index("y"), lax.axis_index("c")
    me = 4 * x + 2 * y + c

    def outgoing(p):
        return src_ref if gather else src_ref.at[p]

    own = pltpu.make_async_copy(outgoing(me), out_ref.at[me], local_sem)
    sends, recvs = [], []
    for k in range(1, N_DEV):
        px = 1 - x if k & 4 else x
        py = 1 - y if k & 2 else y
        pc = 1 - c if k & 1 else c
        p = 4 * px + 2 * py + pc
        for dst, group in ((me, sends), (p, recvs)):
            group.append(pltpu.make_async_remote_copy(
                src_ref=outgoing(p), dst_ref=out_ref.at[dst], send_sem=send_sems.at[k - 1],
                recv_sem=recv_sems.at[k - 1], device_id=(px, py, pc), device_id_type=pl.DeviceIdType.MESH))
    return own, sends, recvs


def _exchange_start(*refs_and_mode):
    own, sends, _ = _exchange_copies(*refs_and_mode)
    own.start()
    for cp in sends:
        cp.start()


def _exchange_wait(*refs_and_mode):
    own, sends, recvs = _exchange_copies(*refs_and_mode)
    for cp in recvs:
        cp.wait_recv()
    for cp in sends:
        cp.wait_send()
    own.wait()


def _adamw(parts, w, m, v, *, name):
    layers = len(parts)
    rows, cols = w.shape
    per_layer = rows // layers
    t = _tile(per_layer, (FLAT_ROW_TILE, 128, 64, 32, 16, 8))
    nt = per_layer // t

    def body(*refs):
        p_refs = refs[:layers]
        w_ref, m_ref, v_ref, g_ref, d_ref, nm_ref, nv_ref = refs[layers:]
        layer = pl.program_id(0)
        g = None
        for l, p_ref in enumerate(p_refs):
            total = p_ref[0].astype(F32)
            for k in range(1, N_DEV):
                total = total + p_ref[k].astype(F32)
            g = total if g is None else jnp.where(layer == l, total, g)
        nm = ADAM_B1 * m_ref[...] + (1.0 - ADAM_B1) * g
        nv = ADAM_B2 * v_ref[...] + (1.0 - ADAM_B2) * (g * g)
        m_hat = nm / (1.0 - ADAM_B1 ** ADAM_STEP)
        v_hat = nv / (1.0 - ADAM_B2 ** ADAM_STEP)
        g_ref[...] = g
        d_ref[...] = -ADAM_LR * (m_hat / (jnp.sqrt(v_hat) + ADAM_EPS) + ADAM_WD * w_ref[...])
        nm_ref[...] = nm
        nv_ref[...] = nv

    def part_spec(l):
        return pl.BlockSpec((N_DEV, t, cols), lambda layer, i: (0, jnp.where(layer == l, i, 0), 0))

    row = pl.BlockSpec((t, cols), lambda layer, i: (layer * nt + i, 0))
    shape = jax.ShapeDtypeStruct((rows, cols), F32)
    return pl.pallas_call(body, name=name, grid=(layers, nt),
                          in_specs=[part_spec(l) for l in range(layers)] + [row, row, row],
                          out_specs=[row] * 4, out_shape=[shape] * 4,
                          compiler_params=_params(("parallel", "parallel"), VMEM_BIG))(*parts, w, m, v)


def _flatten(arrays, dtype, row_mult):
    flat = jnp.concatenate([a.reshape(-1).astype(dtype) for a in arrays])
    chunk = FLAT_COLS * row_mult
    padded = -(-flat.shape[0] // chunk) * chunk
    return jnp.pad(flat, (0, padded - flat.shape[0])).reshape(padded // FLAT_COLS, FLAT_COLS)


def _unflatten(flat, shapes):
    flat = flat.reshape(-1)
    out, pos = [], 0
    for shp in shapes:
        size = int(np.prod(shp))
        out.append(flat[pos:pos + size].reshape(shp))
        pos += size
    return out


def _gather_full(names, local, dtype, row_mult, label):
    got = _exchange(_flatten([local[n] for n in names], dtype, row_mult), gather=True, name=label)
    got = got.reshape(N_DEV, -1)
    full, pos = {}, 0
    for n in names:
        shp = local[n].shape
        size = int(np.prod(shp))
        piece = got[:, pos:pos + size].reshape(N_DEV, size // shp[-1], shp[-1])
        full[n] = jnp.transpose(piece, (1, 0, 2)).reshape(shp[:-1] + (N_DEV * shp[-1],))
        pos += size
    return full


def _scatter_rows(names, grads, row_mult):
    pieces = []
    for n in names:
        c = grads[n].shape[-1] // N_DEV
        pieces.append(jnp.transpose(grads[n].reshape(-1, N_DEV, c), (1, 0, 2)).reshape(N_DEV, -1))
    flat = jnp.concatenate(pieces, axis=1)
    chunk = FLAT_COLS * row_mult
    padded = -(-flat.shape[1] // chunk) * chunk
    return jnp.pad(flat, ((0, 0), (0, padded - flat.shape[1]))).reshape(N_DEV, padded // FLAT_COLS, FLAT_COLS)


def _from_shards(stacked, axis):
    _, a, b = stacked.shape
    if axis == 1:
        return jnp.transpose(stacked, (1, 0, 2)).reshape(a, N_DEV * b)
    return stacked.reshape(N_DEV * a, b)


def _to_shards(full, axis):
    ra, rb = full.shape
    if axis == 1:
        return jnp.transpose(full.reshape(ra, N_DEV, rb // N_DEV), (1, 0, 2))
    return full.reshape(N_DEV, ra // N_DEV, rb)


class _ShardedMatmulWeights:
    def __init__(self, local):
        self.local, self.full, self.parts = local, {}, {}

    def gather_src(self, n, l):
        return self.local[n][l].astype(BF16)

    def set_gathered(self, n, l, got):
        self.full[n, l] = _from_shards(got, SHARD_AXIS[n] - 1)

    def get(self, n, l):
        return self.full[n, l]

    def scatter_src(self, n, l, dw):
        return _to_shards(dw, SHARD_AXIS[n] - 1).astype(BF16)

    def set_scattered(self, n, l, parts):
        self.parts[n, l] = parts


def _role(role, i):
    mixer = 'attn_w_' if i % 2 == 0 else 'conv_w_'
    return {'in': (mixer + 'in', i // 2), 'out': (mixer + 'out', i // 2),
            'up': ('ffn_w_up', i), 'down': ('ffn_w_down', i)}[role]


def _local_step(x, tgt, w, big):
    s = x.shape[0]

    def gather_of(role, i):
        if role is None or i >= DEPTH:
            return None, None
        key = _role(role, i)
        return key, (big.gather_src(*key), True)

    def project(a, role, i, ln=None, gather=None):
        nxt, carry = gather_of(gather, i + 1)
        weight = big.get(*_role(role, i))
        if ln is None:
            out, got = _mm(a, weight, carry=carry, name=role + "_fwd")
        else:
            *out, got = _mm_ln(a, weight, *ln, carry=carry, name=role + "_fwd")
        if got is not None:
            big.set_gathered(*nxt, got)
        return out

    def project_back(a, d_out, role, i, ln=None, scatter=True):
        key = _role(role, i)
        dw, _ = _mm(a, d_out, ta=True, name=role + "_dw")
        carry = (big.scatter_src(*key, dw), False)
        if not scatter:
            return _mm(d_out, big.get(*key), tb=True, name=role + "_dx")[0], (key, carry)
        if ln is None:
            d_in, parts = _mm(d_out, big.get(*key), tb=True, carry=carry, name=role + "_dx")
        else:
            *d_in, parts = _mm_ln_bwd(d_out, big.get(*key), *ln, carry=carry, name=role + "_dx")
        if parts is not None:
            big.set_scattered(*key, parts)
        return d_in

    na_tq = NA_QROWS * GRID_W
    band_idx = [_band_bucket_index(d) for _, d in DIL_PATTERNS]
    band_bias = [_band_bias(w['t5_bias'], d, name=f"band_bias_{d}") for _, d in DIL_PATTERNS]
    band_mask = [_mask_tiles("band", DIL_TQ, DIL_KW, s // d, name=f"band_mask_{d}") for _, d in DIL_PATTERNS]
    na_mask = _mask_tiles("na", na_tq, 3 * na_tq, s, name="na_mask")
    na_cols, dil_cols = (0, 4, 8), (12, 16, 20)
    grads = {n: [None] * w[n].shape[0] for n in SMALL_SHARDED + REPLICATED if n != 't5_bias'}
    saved = []
    x16 = x.astype(BF16)

    for i in range(DEPTH):
        j = i // 2
        st = {'x': x, 'x16': x16}
        if i % 2 == 0:
            h = project(x16, 'in', i)
            na_bias = _na_bias(w['na_rpb'][j], name="na_bias")
            late = ('up', 'down', 'out') if i == 0 else ()
            rides = [gather_of(role, 0) for role in late] + [(None, None)] * 4
            outs, lses = [], []
            calls = [(na_bias, na_mask, 1, na_tq, NA_NSUB, na_cols, "na_fwd")] + [
                (bias, mask, d, DIL_TQ, DIL_NSUB[d], dil_cols, f"dil_fwd_{d}")
                for (_, d), bias, mask in zip(DIL_PATTERNS, band_bias, band_mask)]
            for (bias, mask, d, tq, nsub, cols, label), (key, carry) in zip(calls, rides):
                o, l, got = _attn_fwd(h, bias, mask, d=d, tq=tq, nsub=nsub, cols=cols, carry=carry, name=label)
                if got is not None:
                    big.set_gathered(*key, got)
                outs.append(o)
                lses.append(l)
            o_na, l_na = outs.pop(0), lses.pop(0)
            mid, mid16, l_dil = _dil_combine(o_na, outs, lses, name="dil_combine")
            st.update(h=h, mid=mid, mid16=mid16, l_na=l_na, l_dil=l_dil, na_bias=na_bias)
        else:
            h = project(x16, 'in', i)
            mid16, u2 = _conv_mid_fwd(h, w['conf_dw_w'][j], w['conf_dw_b'][j], w['conf_ln_g'][j], w['conf_ln_b'][j],
                                      w['sconv_w'][j], name="conv_mid_fwd")
            st.update(h=h, mid16=mid16, u2=u2)
        z_mix, xa, xa16 = project(mid16, 'out', i, ln=(x, w['mix_ln_g'][i], w['mix_ln_b'][i]), gather='out')
        hu = project(xa16, 'up', i, gather='in')
        nxt, carry = gather_of('up', i + 1)
        act16, got = _ffn_mid_fwd(hu, w['ffn_dw_w'][i], carry=carry, name="ffn_mid_fwd")
        if got is not None:
            big.set_gathered(*nxt, got)
        z_ffn, xb, xb16 = project(act16, 'down', i, ln=(xa, w['ffn_ln_g'][i], w['ffn_ln_b'][i]), gather='down')
        st.update(z_mix=z_mix, xa16=xa16, hu=hu, act16=act16, z_ffn=z_ffn)
        saved.append(st)
        x, x16 = xb, xb16

    loss, d_loss = _loss_head(x, tgt, name="loss_head")
    g_t5 = None
    dz, dz16, dg, db = _ln_bwd(saved[-1]['z_ffn'], w['ffn_ln_g'][-1], d_loss, None, name="last_ln_bwd")
    for i in reversed(range(DEPTH)):
        j = i // 2
        st = saved[i]
        grads['ffn_ln_g'][i], grads['ffn_ln_b'][i] = dg, db
        dact, (key, carry) = project_back(st['act16'], dz16, 'down', i, scatter=False)
        dhu, grads['ffn_dw_w'][i], parts = _ffn_mid_bwd(st['hu'], w['ffn_dw_w'][i], dact, carry=carry,
                                                       name="ffn_mid_bwd")
        if parts is not None:
            big.set_scattered(*key, parts)
        dz, dz1, dg, db = project_back(st['xa16'], dhu, 'up', i, ln=(st['z_mix'], w['mix_ln_g'][i], dz))
        grads['mix_ln_g'][i], grads['mix_ln_b'][i] = dg, db
        dmid = project_back(st['mid16'], dz1, 'out', i)
        if i % 2 == 0:
            h = st['h']
            dq, dk, dv, dbias = _attn_bwd(h, st['na_bias'], na_mask, dmid, st['mid'], st['l_na'], d=1, tq=na_tq,
                                          nsub=NA_NSUB, cols=na_cols, ycol=0, name="na_bwd")
            grads['na_rpb'][j] = _rpb_grad(dbias, name="rpb_grad")
            dil, dbs = [], []
            for (_, d), bias, mask in zip(DIL_PATTERNS, band_bias, band_mask):
                g = _attn_bwd(h, bias, mask, dmid, st['mid'], st['l_dil'], d=d, tq=DIL_TQ, nsub=DIL_NSUB[d],
                              cols=dil_cols, ycol=4, name=f"dil_bwd_{d}")
                dil.append(g[:3])
                dbs.append(g[3])
            t5 = _t5_grad(dbs, band_idx, name="t5_grad")
            g_t5 = t5 if g_t5 is None else g_t5 + t5
            dh = _attn_dh((dq, dk, dv), dil, name="attn_dh")
        else:
            dh, dw31, db31, dlg, dlb, dw3 = _conv_mid_bwd(st['h'], st['u2'], dmid, w['conf_dw_w'][j],
                                                          w['conf_ln_g'][j], w['conf_ln_b'][j], w['sconv_w'][j],
                                                          name="conv_mid_bwd")
            grads['conf_dw_w'][j], grads['conf_dw_b'][j] = dw31, db31
            grads['conf_ln_g'][j], grads['conf_ln_b'][j], grads['sconv_w'][j] = dlg, dlb, dw3
        if i > 0:
            below = saved[i - 1]
            dz, dz16, dg, db = project_back(st['x16'], dh, 'in', i, ln=(below['z_ffn'], w['ffn_ln_g'][i - 1], dz))
        else:
            dx = _axpy(dz, project_back(st['x16'], dh, 'in', i), name="grad_x")
    full = {n: jnp.stack(g) for n, g in grads.items()}
    full['t5_bias'] = g_t5
    return loss, dx, full


def kernel(x, t5_bias, attn_w_in, attn_w_out, na_rpb, conv_w_in, conf_dw_w, conf_dw_b, conf_ln_g, conf_ln_b, sconv_w, conv_w_out, ffn_w_up, ffn_dw_w, ffn_w_down, mix_ln_g, mix_ln_b, ffn_ln_g, ffn_ln_b, loss_target, m_t5_bias, m_attn_w_in, m_attn_w_out, m_na_rpb, m_conv_w_in, m_conf_dw_w, m_conf_dw_b, m_conf_ln_g, m_conf_ln_b, m_sconv_w, m_conv_w_out, m_ffn_w_up, m_ffn_dw_w, m_ffn_w_down, m_mix_ln_g, m_mix_ln_b, m_ffn_ln_g, m_ffn_ln_b, v_t5_bias, v_attn_w_in, v_attn_w_out, v_na_rpb, v_conv_w_in, v_conf_dw_w, v_conf_dw_b, v_conf_ln_g, v_conf_ln_b, v_sconv_w, v_conv_w_out, v_ffn_w_up, v_ffn_dw_w, v_ffn_w_down, v_mix_ln_g, v_mix_ln_b, v_ffn_ln_g, v_ffn_ln_b):
    args = dict(locals())
    local = {n: args[n] for n in WEIGHTS}
    mom1 = {n: args['m_' + n] for n in WEIGHTS}
    mom2 = {n: args['v_' + n] for n in WEIGHTS}

    kinds = ('grad', 'delta', 'new_m', 'new_v')
    small = {n: local[n] for n in REPLICATED}
    small.update(_gather_full(SMALL_SHARDED, local, F32, 8, "gather_small_weights"))
    big = _ShardedMatmulWeights({n: local[n] for n in MATMUL_WEIGHTS})
    n, l = _role('in', 0)
    big.set_gathered(n, l, _exchange(big.gather_src(n, l), gather=True, name="gather_first_in"))

    loss, dx, grads = _local_step(x[0], loss_target[0], small, big)
    loss = lax.psum(loss, MESH_AXES)

    out = {}
    for n in MATMUL_WEIGHTS:
        layers, a, b = local[n].shape
        res = _adamw([big.parts[n, l] for l in range(layers)],
                     *[t.reshape(layers * a, b) for t in (local[n], mom1[n], mom2[n])], name="adamw_" + n)
        for kind, r in zip(kinds, res):
            out[kind + '_' + n] = r.reshape(layers, a, b)
    for names, sharded, label in ((SMALL_SHARDED, True, "small"), (REPLICATED, False, "replicated")):
        shapes = [local[n].shape for n in names]
        if sharded:
            parts = _exchange(_scatter_rows(names, grads, 8), gather=False, name="scatter_small_grads")
        else:
            parts = _exchange(_flatten([grads[n] for n in names], F32, 8), gather=True, name="gather_replicated_grads")
        res = _adamw([parts], _flatten([local[n] for n in names], F32, 8), _flatten([mom1[n] for n in names], F32, 8),
                     _flatten([mom2[n] for n in names], F32, 8), name="adamw_" + label)
        for kind, flat in zip(kinds, res):
            for n, a in zip(names, _unflatten(flat, shapes)):
                out[kind + '_' + n] = a

    return (loss, dx[None], *[out[k + '_' + n] for k in ('grad', 'delta', 'new_m', 'new_v') for n in WEIGHTS])
```

```python
import math

import jax
import jax.numpy as jnp
import numpy as np
from jax import lax
from jax.experimental import pallas as pl
from jax.experimental.pallas import tpu as pltpu

F32 = jnp.float32
BF16 = jnp.bfloat16

N_DEV = 8
MESH_AXES = ("x", "y", "c")
DEPTH = 4
GRID_W = 64
GRID_SHIFT = 6
HEAD_DIM = 64
NA_KH = 8
NA_KW = 16
NA_QROWS = 4
DIL_PATTERNS = ((128, 1), (512, 4), (2048, 16))
DIL_HALF = 64
DIL_TQ = 128
DIL_KW = DIL_TQ + 2 * DIL_HALF
DIL_NSUB = {1: 8, 4: 2, 16: 1}
NA_NSUB = 4
N_BUCKETS = 32
T5_MAX_DIST = 1024
CONF_CH = 512
CONF_K = 31
SC_K = 3
FFN_K = 3
FFN_ROW_TILE = 256
FFN_STRIP = 128
LN_EPS = 1e-5
NEG = -1e30
ALPHA = (2 * DEPTH) ** 0.25
ADAM_LR = 0.001
ADAM_B1 = 0.9
ADAM_B2 = 0.999
ADAM_EPS = 1e-08
ADAM_WD = 0.01
ADAM_STEP = 10

LANES = 128
SUBLANES = 8
VMEM_BIG = 48 * 1024 * 1024
FLAT_COLS = 1024
FLAT_ROW_TILE = 256

WEIGHTS = ['t5_bias', 'attn_w_in', 'attn_w_out', 'na_rpb', 'conv_w_in', 'conf_dw_w', 'conf_dw_b', 'conf_ln_g',
           'conf_ln_b', 'sconv_w', 'conv_w_out', 'ffn_w_up', 'ffn_dw_w', 'ffn_w_down', 'mix_ln_g', 'mix_ln_b',
           'ffn_ln_g', 'ffn_ln_b']
SHARD_AXIS = {'attn_w_in': 2, 'attn_w_out': 1, 'conv_w_in': 2, 'conf_dw_w': 2, 'conf_dw_b': 1, 'conf_ln_g': 1,
              'conf_ln_b': 1, 'sconv_w': 2, 'conv_w_out': 1, 'ffn_w_up': 2, 'ffn_dw_w': 2, 'ffn_w_down': 1}
MATMUL_WEIGHTS = ['attn_w_in', 'attn_w_out', 'conv_w_in', 'conv_w_out', 'ffn_w_up', 'ffn_w_down']
SMALL_SHARDED = ['conf_dw_w', 'conf_dw_b', 'conf_ln_g', 'conf_ln_b', 'sconv_w', 'ffn_dw_w']
SHARDED = MATMUL_WEIGHTS + SMALL_SHARDED
REPLICATED = ['t5_bias', 'na_rpb', 'mix_ln_g', 'mix_ln_b', 'ffn_ln_g', 'ffn_ln_b']


def _tile(n, cands):
    for c in cands:
        if n % c == 0:
            return c
    return n


def _params(sem, vmem=None):
    return pltpu.CompilerParams(dimension_semantics=sem, vmem_limit_bytes=vmem)


def _sigmoid(x):
    return 0.5 * jnp.tanh(0.5 * x) + 0.5


MM_MAX_TILE = 1408
MM_MAX_K = 3072


def _lane_tile(n, cap):
    best = None
    for t in range(LANES, min(n, cap) + 1, LANES):
        if n % t == 0:
            best = t
    return best or n


def _mm(a, b, *, ta=False, tb=False, out_dtype=F32, carry=None, name):
    assert a.dtype == BF16 and b.dtype == BF16, (name, a.dtype, b.dtype)
    m, k = (a.shape[1], a.shape[0]) if ta else a.shape
    n = b.shape[0] if tb else b.shape[1]
    tm, tn, tk = _lane_tile(m, MM_MAX_TILE), _lane_tile(n, MM_MAX_TILE), _lane_tile(k, MM_MAX_K)
    grid = (m // tm, n // tn, k // tk)
    nk = grid[2]
    dims = (((0 if ta else 1,), (1 if tb else 0,)), ((), ()))
    use_acc = nk > 1 and out_dtype != F32
    ride = _Carried(carry, grid)

    def body(*refs):
        (a_ref, b_ref), (o_ref,), scratch = ride.split(refs, 2, 1)
        ride.start()
        part = lax.dot_general(a_ref[...], b_ref[...], dims, preferred_element_type=F32)
        if nk == 1:
            o_ref[...] = part.astype(out_dtype)
        else:
            acc_ref = scratch[0] if use_acc else o_ref
            kk = pl.program_id(2)

            @pl.when(kk == 0)
            def _():
                acc_ref[...] = part

            @pl.when(kk > 0)
            def _():
                acc_ref[...] += part

            if use_acc:
                @pl.when(kk == nk - 1)
                def _():
                    o_ref[...] = acc_ref[...].astype(out_dtype)

        ride.wait()

    a_spec = pl.BlockSpec((tk, tm), lambda i, j, q: (q, i)) if ta else pl.BlockSpec((tm, tk), lambda i, j, q: (i, q))
    b_spec = pl.BlockSpec((tn, tk), lambda i, j, q: (j, q)) if tb else pl.BlockSpec((tk, tn), lambda i, j, q: (q, j))
    o_spec = pl.BlockSpec((tm, tn), lambda i, j, q: (i, j))
    o_shape = jax.ShapeDtypeStruct((m, n), out_dtype)
    scratch = [pltpu.VMEM((tm, tn), F32)] if use_acc else []
    sem = ("arbitrary",) * 3 if ride.on else ("parallel", "parallel", "arbitrary")
    out = pl.pallas_call(
        body, name=name, grid=grid, in_specs=[a_spec, b_spec] + ride.specs, out_specs=[o_spec] + ride.specs,
        out_shape=[o_shape] + ride.out_shapes, scratch_shapes=scratch + ride.scratch,
        compiler_params=_params(sem, VMEM_BIG))(a, b, *ride.operands)
    return out[0], (out[1] if ride.on else None)


MM_LN_ROWS = 512


def _mm_ln(a, b, x, g, beta, *, carry=None, name):
    assert a.dtype == BF16 and b.dtype == BF16, (name, a.dtype, b.dtype)
    m, k = a.shape
    n = b.shape[1]
    assert k <= MM_MAX_K, (name, k)
    tm = _tile(m, (MM_LN_ROWS, 256, 128, 64, 8))
    grid = (m // tm,)
    ride = _Carried(carry, grid)

    def body(*refs):
        (a_ref, b_ref, x_ref, g_ref, beta_ref), (z_ref, o_ref, o16_ref), _ = ride.split(refs, 5, 3)
        ride.start()
        z = ALPHA * x_ref[...] + jnp.dot(a_ref[...], b_ref[...], preferred_element_type=F32)
        z_ref[...] = z
        xh, _ = _ln_stats(z)
        out = xh * g_ref[...] + beta_ref[...]
        o_ref[...] = out
        o16_ref[...] = out.astype(BF16)
        ride.wait()

    row = lambda width: pl.BlockSpec((tm, width), lambda i: (i, 0))
    vec = pl.BlockSpec((1, n), lambda i: (0, 0))
    out = pl.pallas_call(
        body, name=name, grid=grid,
        in_specs=[row(k), pl.BlockSpec((k, n), lambda i: (0, 0)), row(n), vec, vec] + ride.specs,
        out_specs=[row(n)] * 3 + ride.specs,
        out_shape=[jax.ShapeDtypeStruct((m, n), F32), jax.ShapeDtypeStruct((m, n), F32),
                   jax.ShapeDtypeStruct((m, n), BF16)] + ride.out_shapes,
        scratch_shapes=ride.scratch,
        compiler_params=_params(("arbitrary",) if ride.on else ("parallel",), VMEM_BIG))(
            a, b, x, g.reshape(1, n), beta.reshape(1, n), *ride.operands)
    return out[0], out[1], out[2], (out[3] if ride.on else None)


def _mm_ln_bwd(d_out, b, z, g, d1, *, carry=None, name):
    assert d_out.dtype == BF16 and b.dtype == BF16, (name, d_out.dtype, b.dtype)
    m, k = d_out.shape
    n = b.shape[0]
    tm, tk = _tile(m, (MM_LN_ROWS, 256, 128, 64, 8)), _lane_tile(k, MM_MAX_K)
    grid = (m // tm, k // tk)
    nk = grid[1]
    ride = _Carried(carry, grid)

    def body(*refs):
        (a_ref, b_ref, z_ref, g_ref, d1_ref), (dz_ref, dz16_ref, dg_ref, db_ref), scratch = ride.split(refs, 5, 4)
        ride.start()
        i, kk = pl.program_id(0), pl.program_id(1)
        part = lax.dot_general(a_ref[...], b_ref[...], (((1,), (1,)), ((), ())), preferred_element_type=F32)

        def finish(d2):
            @pl.when(i == 0)
            def _():
                dg_ref[...] = jnp.zeros_like(dg_ref)
                db_ref[...] = jnp.zeros_like(db_ref)

            dout = ALPHA * d1_ref[...] + d2
            xh, rstd = _ln_stats(z_ref[...])
            dxh = dout * g_ref[...]
            dz = rstd * (dxh - jnp.mean(dxh, axis=-1, keepdims=True) - xh * jnp.mean(dxh * xh, axis=-1, keepdims=True))
            dz_ref[...] = dz
            dz16_ref[...] = dz.astype(BF16)
            dg_ref[...] += jnp.sum(dout * xh, axis=0, keepdims=True)
            db_ref[...] += jnp.sum(dout, axis=0, keepdims=True)

        if nk == 1:
            finish(part)
        else:
            acc_ref = scratch[0]

            @pl.when(kk == 0)
            def _():
                acc_ref[...] = part

            @pl.when((kk > 0) & (kk < nk - 1))
            def _():
                acc_ref[...] += part

            @pl.when(kk == nk - 1)
            def _():
                finish(acc_ref[...] + part)

        ride.wait()

    row = pl.BlockSpec((tm, n), lambda i, q: (i, 0))
    vec = pl.BlockSpec((1, n), lambda i, q: (0, 0))
    out = pl.pallas_call(
        body, name=name, grid=grid,
        in_specs=[pl.BlockSpec((tm, tk), lambda i, q: (i, q)), pl.BlockSpec((n, tk), lambda i, q: (0, q)),
                  row, vec, row] + ride.specs,
        out_specs=[row, row, vec, vec] + ride.specs,
        out_shape=[jax.ShapeDtypeStruct((m, n), F32), jax.ShapeDtypeStruct((m, n), BF16),
                   jax.ShapeDtypeStruct((1, n), F32), jax.ShapeDtypeStruct((1, n), F32)] + ride.out_shapes,
        scratch_shapes=([pltpu.VMEM((tm, n), F32)] if nk > 1 else []) + ride.scratch,
        compiler_params=_params(("arbitrary", "arbitrary"), VMEM_BIG))(
            d_out, b, z, g.reshape(1, n), d1, *ride.operands)
    return out[0], out[1], out[2].reshape(n), out[3].reshape(n), (out[4] if ride.on else None)


def _ln_stats(z):
    mu = jnp.mean(z, axis=-1, keepdims=True)
    zc = z - mu
    var = jnp.mean(zc * zc, axis=-1, keepdims=True)
    rstd = lax.rsqrt(var + LN_EPS)
    return zc * rstd, rstd


def _ln_bwd(z, g, d1, d2, *, name):
    s, d = z.shape
    t = _tile(s, (256, 128, 64, 8))
    two = d2 is not None

    def body(*refs):
        if two:
            z_ref, g_ref, d1_ref, d2_ref, dz_ref, dz16_ref, dg_ref, db_ref = refs
            dout = ALPHA * d1_ref[...] + d2_ref[...]
        else:
            z_ref, g_ref, d1_ref, dz_ref, dz16_ref, dg_ref, db_ref = refs
            dout = d1_ref[...]

        @pl.when(pl.program_id(0) == 0)
        def _():
            dg_ref[...] = jnp.zeros_like(dg_ref)
            db_ref[...] = jnp.zeros_like(db_ref)

        xh, rstd = _ln_stats(z_ref[...])
        dxh = dout * g_ref[...]
        dz = rstd * (dxh - jnp.mean(dxh, axis=-1, keepdims=True) - xh * jnp.mean(dxh * xh, axis=-1, keepdims=True))
        dz_ref[...] = dz
        dz16_ref[...] = dz.astype(BF16)
        dg_ref[...] += jnp.sum(dout * xh, axis=0, keepdims=True)
        db_ref[...] += jnp.sum(dout, axis=0, keepdims=True)

    row = pl.BlockSpec((t, d), lambda i: (i, 0))
    vec = pl.BlockSpec((1, d), lambda i: (0, 0))
    ins = [z, g.reshape(1, d), d1] + ([d2] if two else [])
    specs = [row, vec, row] + ([row] if two else [])
    dz, dz16, dg, db = pl.pallas_call(
        body, name=name, grid=(s // t,), in_specs=specs, out_specs=[row, row, vec, vec],
        out_shape=[jax.ShapeDtypeStruct((s, d), F32), jax.ShapeDtypeStruct((s, d), BF16),
                   jax.ShapeDtypeStruct((1, d), F32), jax.ShapeDtypeStruct((1, d), F32)],
        compiler_params=_params(("arbitrary",)))(*ins)
    return dz, dz16, dg.reshape(d), db.reshape(d)


def _axpy(d1, d2, *, name):
    s, d = d1.shape
    t = _tile(s, (256, 128, 64, 8))

    def body(a_ref, b_ref, o_ref):
        o_ref[...] = ALPHA * a_ref[...] + b_ref[...]

    row = pl.BlockSpec((t, d), lambda i: (i, 0))
    return pl.pallas_call(body, name=name, grid=(s // t,), in_specs=[row, row], out_specs=row,
                          out_shape=jax.ShapeDtypeStruct((s, d), F32), compiler_params=_params(("parallel",)))(d1, d2)


def _loss_head(y, tgt, *, name):
    s, d = y.shape
    t = _tile(s, (256, 128, 64, 8))

    def body(y_ref, t_ref, l_ref, dy_ref):
        @pl.when(pl.program_id(0) == 0)
        def _():
            l_ref[...] = jnp.zeros_like(l_ref)

        err = y_ref[...] - t_ref[...]
        dy_ref[...] = err * (1.0 / d)
        l_ref[...] += 0.5 * jnp.sum(jnp.sum(err * err, axis=1, keepdims=True), axis=0, keepdims=True) * (1.0 / d)

    row = pl.BlockSpec((t, d), lambda i: (i, 0))
    one = pl.BlockSpec((SUBLANES, LANES), lambda i: (0, 0))
    loss, dy = pl.pallas_call(
        body, name=name, grid=(s // t,), in_specs=[row, row], out_specs=[one, row],
        out_shape=[jax.ShapeDtypeStruct((SUBLANES, LANES), F32), jax.ShapeDtypeStruct((s, d), F32)],
        compiler_params=_params(("arbitrary",)))(y, tgt)
    return loss[0, 0], dy


def _halo_specs(s, t, halo, cb, col):
    per = t // halo
    last = s // halo - 1
    return [pl.BlockSpec((t, cb), lambda j, i: (i, col(j))),
            pl.BlockSpec((halo, cb), lambda j, i: (jnp.maximum(i * per - 1, 0), col(j))),
            pl.BlockSpec((halo, cb), lambda j, i: (jnp.minimum((i + 1) * per, last), col(j)))]


def _extended(main_ref, prev_ref, next_ref, i, n):
    prev = jnp.where(i > 0, prev_ref[...], 0.0)
    nxt = jnp.where(i < n - 1, next_ref[...], 0.0)
    return jnp.concatenate([prev, main_ref[...], nxt], axis=0)


def _shift(ext, o):
    if o == 0:
        return ext
    return pltpu.roll(ext, (-o) % ext.shape[0], 0)


def _taps(ext, k, sign=1):
    return [_shift(ext, sign * (j - k // 2)) for j in range(k)]


def _conv(ext, w_ref, k, sign=1, taps=None):
    taps = _taps(ext, k, sign) if taps is None else taps
    acc = None
    for j in range(k):
        term = w_ref[j:j + 1, :] * taps[j]
        acc = term if acc is None else acc + term
    return acc


def _main_taps(ext, k, halo, t, sign=1):
    rolled, taps = {}, []
    for j in range(k):
        offset = sign * (j - k // 2)
        res = offset % SUBLANES
        if res not in rolled:
            rolled[res] = _shift(ext, res)
        start = halo + offset - res
        taps.append(rolled[res][start:start + t])
    return taps


def _conv_main(taps, w_ref):
    acc = None
    for j, tap in enumerate(taps):
        term = w_ref[j:j + 1, :] * tap
        acc = term if acc is None else acc + term
    return acc


def _wgrad_main(dw_ref, d_main, taps):
    for j, tap in enumerate(taps):
        dw_ref[j:j + 1, :] += jnp.sum(d_main * tap, axis=0, keepdims=True)


def _conv_wgrad(dw_ref, d_main, x_ext, k, halo, t, taps=None):
    taps = _taps(x_ext, k) if taps is None else taps
    for j in range(k):
        dw_ref[j:j + 1, :] += jnp.sum(d_main * taps[j][halo:halo + t], axis=0, keepdims=True)


def _ffn_mid_fwd(hu, w, *, carry=None, name):
    s, f2 = hu.shape
    f = f2 // 2
    t, cb, halo = _tile(s, (FFN_ROW_TILE,)), _lane_tile(f, FFN_STRIP), SUBLANES
    nt, nc = s // t, f // cb
    ride = _Carried(carry, (1, nt))

    def body(*refs):
        (h_ref, hp_ref, hn_ref, w_ref), (a_ref,), _ = ride.split(refs, 4, 1)
        ride.start()
        i = pl.program_id(1)
        first, last = i == 0, i == nt - 1

        def extended(cols):
            return jnp.concatenate([jnp.where(first, 0.0, hp_ref[:, cols]), h_ref[:, cols],
                                    jnp.where(last, 0.0, hn_ref[:, cols])], axis=0)

        for c in range(nc):
            gcols, ucols = pl.ds(c * cb, cb), pl.ds(f + c * cb, cb)
            hg = _conv_main(_main_taps(extended(gcols), FFN_K, halo, t), w_ref.at[:, gcols])
            hu_ = _conv_main(_main_taps(extended(ucols), FFN_K, halo, t), w_ref.at[:, ucols])
            a_ref[:, gcols] = (hg * _sigmoid(hg) * hu_).astype(BF16)
        ride.wait()

    specs = _halo_specs(s, t, halo, f2, lambda j: 0) + [pl.BlockSpec((FFN_K, f2), lambda j, i: (0, 0))]
    out = pl.pallas_call(
        body, name=name, grid=(1, nt), in_specs=specs + ride.specs,
        out_specs=[pl.BlockSpec((t, f), lambda j, i: (i, 0))] + ride.specs,
        out_shape=[jax.ShapeDtypeStruct((s, f), BF16)] + ride.out_shapes, scratch_shapes=ride.scratch,
        compiler_params=_params(("arbitrary", "arbitrary") if ride.on else ("parallel", "parallel"), VMEM_BIG))(
            hu, hu, hu, w, *ride.operands)
    return out[0], (out[1] if ride.on else None)


def _ffn_mid_bwd(hu, w, da, *, carry=None, name):
    s, f2 = hu.shape
    f = f2 // 2
    t, cb, halo = _tile(s, (FFN_ROW_TILE,)), _lane_tile(f, FFN_STRIP), SUBLANES
    nt, nc = s // t, f // cb
    ride = _Carried(carry, (1, nt))

    def body(*refs):
        (h_ref, hp_ref, hn_ref, a_ref, ap_ref, an_ref, w_ref), (dh_ref, dw_ref), _ = ride.split(refs, 7, 2)
        ride.start()
        i = pl.program_id(1)

        @pl.when(i == 0)
        def _():
            dw_ref[...] = jnp.zeros_like(dw_ref)

        first, last = i == 0, i == nt - 1

        def extended(main, prev, nxt, cols):
            return jnp.concatenate([jnp.where(first, 0.0, prev[:, cols]), main[:, cols],
                                    jnp.where(last, 0.0, nxt[:, cols])], axis=0)

        for c in range(nc):
            gcols, ucols, acols = pl.ds(c * cb, cb), pl.ds(f + c * cb, cb), pl.ds(c * cb, cb)
            xg = extended(h_ref, hp_ref, hn_ref, gcols)
            xu = extended(h_ref, hp_ref, hn_ref, ucols)
            dae = extended(a_ref, ap_ref, an_ref, acols)
            wg, wu = w_ref.at[:, gcols], w_ref.at[:, ucols]
            xg_taps, xu_taps = _taps(xg, FFN_K), _taps(xu, FFN_K)
            hg = _conv(xg, wg, FFN_K, taps=xg_taps)
            hu_ = _conv(xu, wu, FFN_K, taps=xu_taps)
            sg = _sigmoid(hg)
            d_hg = dae * hu_ * (sg * (1.0 + hg * (1.0 - sg)))
            d_hu = dae * (hg * sg)
            dh_ref[:, gcols] = _conv(d_hg, wg, FFN_K, sign=-1)[halo:halo + t].astype(BF16)
            dh_ref[:, ucols] = _conv(d_hu, wu, FFN_K, sign=-1)[halo:halo + t].astype(BF16)
            _conv_wgrad(dw_ref.at[:, gcols], d_hg[halo:halo + t], xg, FFN_K, halo, t, taps=xg_taps)
            _conv_wgrad(dw_ref.at[:, ucols], d_hu[halo:halo + t], xu, FFN_K, halo, t, taps=xu_taps)
        ride.wait()

    whole = lambda j: 0
    specs = (_halo_specs(s, t, halo, f2, whole) + _halo_specs(s, t, halo, f, whole)
             + [pl.BlockSpec((FFN_K, f2), lambda j, i: (0, 0))])
    out = pl.pallas_call(
        body, name=name, grid=(1, nt), in_specs=specs + ride.specs,
        out_specs=[pl.BlockSpec((t, f2), lambda j, i: (i, 0)), pl.BlockSpec((FFN_K, f2), lambda j, i: (0, 0))]
        + ride.specs,
        out_shape=[jax.ShapeDtypeStruct((s, f2), BF16), jax.ShapeDtypeStruct((FFN_K, f2), F32)] + ride.out_shapes,
        scratch_shapes=ride.scratch,
        compiler_params=_params(("arbitrary", "arbitrary"), VMEM_BIG))(hu, hu, hu, da, da, da, w, *ride.operands)
    return out[0], out[1], (out[2] if ride.on else None)


CONV_HALO = 16


def _conv_mid_fwd(h, dw_w, dw_b, ln_g, ln_b, sc_w, *, name):
    s = h.shape[0]
    c = CONF_CH
    t, halo = _tile(s, (256, 128)), CONV_HALO
    nt = s // t

    def body(ca, cap, can, cg, cgp, cgn, gb, gc, gcp, gcn, hx, hxp, hxn, w31, b31, lg, lb, w3, o_ref, u2_ref):
        i = pl.program_id(1)
        first, last = i == 0, i == nt - 1

        def extended(main, prev, nxt, cols):
            return jnp.concatenate([jnp.where(first, 0.0, prev[:, cols]), main[:, cols],
                                    jnp.where(last, 0.0, nxt[:, cols])], axis=0)

        for strip in range(c // LANES):
            cols = pl.ds(strip * LANES, LANES)
            u1 = extended(ca, cap, can, cols) * _sigmoid(extended(cg, cgp, cgn, cols))
            u2_ref[:, cols] = _conv_main(_main_taps(u1, CONF_K, halo, t), w31.at[:, cols]) + b31[:, cols]
            p = extended(gc, gcp, gcn, cols) * extended(hx, hxp, hxn, cols)
            conv = _conv_main(_main_taps(p, SC_K, halo, t), w3.at[:, cols])
            o_ref[:, pl.ds(c + strip * LANES, LANES)] = (gb[:, cols] * conv).astype(BF16)
        xh, _ = _ln_stats(u2_ref[...])
        yl = xh * lg[...] + lb[...]
        o_ref[:, 0:c] = (yl * _sigmoid(yl)).astype(BF16)

    hs = lambda blk: _halo_specs(s, t, halo, c, lambda j: blk)
    vec = lambda r: pl.BlockSpec((r, c), lambda j, i: (0, 0))
    specs = hs(0) + hs(1) + hs(2)[:1] + hs(3) + hs(4) + [vec(CONF_K), vec(1), vec(1), vec(1), vec(SC_K)]
    return pl.pallas_call(
        body, name=name, grid=(1, nt), in_specs=specs,
        out_specs=[pl.BlockSpec((t, 2 * c), lambda j, i: (i, 0)), pl.BlockSpec((t, c), lambda j, i: (i, 0))],
        out_shape=[jax.ShapeDtypeStruct((s, 2 * c), BF16), jax.ShapeDtypeStruct((s, c), F32)],
        compiler_params=_params(("parallel", "parallel")))(
            h, h, h, h, h, h, h, h, h, h, h, h, h, dw_w, dw_b.reshape(1, c), ln_g.reshape(1, c),
            ln_b.reshape(1, c), sc_w)


def _conv_mid_bwd(h, u2, dm, dw_w, ln_g, ln_b, sc_w, *, name):
    s = h.shape[0]
    c = CONF_CH
    t, halo = _tile(s, (256, 128)), CONV_HALO
    nt = s // t

    def body(ca, cap, can, cg, cgp, cgn, gb, gbp, gbn, gc, gcp, gcn, hx, hxp, hxn, u2r, u2p, u2n,
             du, dup, dun, dz, dzp, dzn, w31, lg, lb, w3,
             dh_ref, dw31_ref, db31_ref, dlg_ref, dlb_ref, dw3_ref, u1_s, du2_s, p_s, dc3_s):
        i = pl.program_id(1)

        @pl.when(i == 0)
        def _():
            for r in (dw31_ref, db31_ref, dlg_ref, dlb_ref, dw3_ref):
                r[...] = jnp.zeros_like(r)

        main = slice(halo, halo + t)
        xh, rstd = _ln_stats(_extended(u2r, u2p, u2n, i, nt))
        yl = xh * lg[...] + lb[...]
        sg = _sigmoid(yl)
        d_yl = _extended(du, dup, dun, i, nt) * (sg * (1.0 + yl * (1.0 - sg)))
        dlg_ref[...] += jnp.sum((d_yl * xh)[main], axis=0, keepdims=True)
        dlb_ref[...] += jnp.sum(d_yl[main], axis=0, keepdims=True)
        dxh = d_yl * lg[...]
        du2 = rstd * (dxh - jnp.mean(dxh, axis=-1, keepdims=True) - xh * jnp.mean(dxh * xh, axis=-1, keepdims=True))
        db31_ref[...] += jnp.sum(du2[main], axis=0, keepdims=True)
        du2_s[...] = du2
        u1_s[...] = _extended(ca, cap, can, i, nt) * _sigmoid(_extended(cg, cgp, cgn, i, nt))
        p_s[...] = _extended(gc, gcp, gcn, i, nt) * _extended(hx, hxp, hxn, i, nt)
        dc3_s[...] = _extended(dz, dzp, dzn, i, nt) * _extended(gb, gbp, gbn, i, nt)
        for strip in range(c // LANES):
            cols = pl.ds(strip * LANES, LANES)
            at = lambda part: pl.ds(part * c + strip * LANES, LANES)
            du2_c = du2_s[:, cols]
            _wgrad_main(dw31_ref.at[:, cols], du2_c[main], _main_taps(u1_s[:, cols], CONF_K, halo, t))
            du1 = _conv_main(_main_taps(du2_c, CONF_K, halo, t, sign=-1), w31.at[:, cols])
            sc = _sigmoid(cg[:, cols])
            dh_ref[:, at(0)] = (du1 * sc).astype(BF16)
            dh_ref[:, at(1)] = (du1 * (ca[:, cols] * sc * (1.0 - sc))).astype(BF16)
            dc3_c = dc3_s[:, cols]
            p_taps = _main_taps(p_s[:, cols], SC_K, halo, t)
            dh_ref[:, at(2)] = (dz[:, cols] * _conv_main(p_taps, w3.at[:, cols])).astype(BF16)
            _wgrad_main(dw3_ref.at[:, cols], dc3_c[main], p_taps)
            dp = _conv_main(_main_taps(dc3_c, SC_K, halo, t, sign=-1), w3.at[:, cols])
            dh_ref[:, at(3)] = (dp * hx[:, cols]).astype(BF16)
            dh_ref[:, at(4)] = (dp * gc[:, cols]).astype(BF16)

    hs = lambda blk: _halo_specs(s, t, halo, c, lambda j: blk)
    vec = lambda r: pl.BlockSpec((r, c), lambda j, i: (0, 0))
    specs = (hs(0) + hs(1) + hs(2) + hs(3) + hs(4) + hs(0) + hs(0) + hs(1)
             + [vec(CONF_K), vec(1), vec(1), vec(SC_K)])
    ext = pltpu.VMEM((t + 2 * halo, c), F32)
    outs = pl.pallas_call(
        body, name=name, grid=(1, nt), in_specs=specs,
        out_specs=[pl.BlockSpec((t, 5 * c), lambda j, i: (i, 0)), vec(CONF_K), vec(1), vec(1), vec(1), vec(SC_K)],
        out_shape=[jax.ShapeDtypeStruct((s, 5 * c), BF16), jax.ShapeDtypeStruct((CONF_K, c), F32),
                   jax.ShapeDtypeStruct((1, c), F32), jax.ShapeDtypeStruct((1, c), F32),
                   jax.ShapeDtypeStruct((1, c), F32), jax.ShapeDtypeStruct((SC_K, c), F32)],
        scratch_shapes=[ext, ext, ext, ext],
        compiler_params=_params(("arbitrary", "arbitrary")))(
            h, h, h, h, h, h, h, h, h, h, h, h, h, h, h, u2, u2, u2, dm, dm, dm, dm, dm, dm,
            dw_w, ln_g.reshape(1, c), ln_b.reshape(1, c), sc_w)
    dh, dw31, db31, dlg, dlb, dw3 = outs
    return dh, dw31, db31.reshape(c), dlg.reshape(c), dlb.reshape(c), dw3


def _attn_mask(kind, n, tq, kw, length):
    pad = (kw - tq) // 2
    iq = lax.broadcasted_iota(jnp.int32, (tq, 1), 0)
    ik = lax.broadcasted_iota(jnp.int32, (1, kw), 1)
    if kind == "band":
        rel = ik - pad - iq
        kpos = n * tq - pad + ik
        return (jnp.abs(rel) <= DIL_HALF) & (kpos >= 0) & (kpos < length)
    rows = length // GRID_W
    rq = n * NA_QROWS + (iq >> GRID_SHIFT)
    cq = iq & (GRID_W - 1)
    rk = n * NA_QROWS - pad // GRID_W + (ik >> GRID_SHIFT)
    ck = ik & (GRID_W - 1)
    r0 = jnp.clip(rq - NA_KH // 2, 0, rows - NA_KH)
    c0 = jnp.clip(cq - NA_KW // 2, 0, GRID_W - NA_KW)
    return (rk >= r0) & (rk < r0 + NA_KH) & (ck >= c0) & (ck < c0 + NA_KW)


def _mask_tiles(kind, tq, kw, length, *, name):
    nb = length // tq

    def body(o_ref):
        v = pl.program_id(0)
        n = jnp.where((v == 1) | (v == 3), 0, jnp.where(v == 2, nb - 1, 1))
        o_ref[0] = jnp.where(_attn_mask(kind, n, tq, kw, length), 0.0, NEG)

    return pl.pallas_call(body, name=name, grid=(4,), out_specs=pl.BlockSpec((1, tq, kw), lambda v: (v, 0, 0)),
                          out_shape=jax.ShapeDtypeStruct((4, tq, kw), F32),
                          compiler_params=_params(("parallel",)))()


class _AttnGeom:
    def __init__(self, s, d, tq, nsub, kw):
        self.s, self.d, self.tq, self.nsub = s, d, tq, nsub
        self.halo = (kw - tq) // 2 * d
        self.rows = nsub * tq * d
        self.nbig = s // self.rows
        self.ext = self.rows + 2 * self.halo
        assert s % self.rows == 0 and self.rows % self.halo == 0

    def main(self, col):
        return pl.BlockSpec((self.rows, LANES), lambda hp, n: (n, col + hp))

    def with_halos(self, col):
        per, last = self.rows // self.halo, self.s // self.halo - 1
        return [self.main(col),
                pl.BlockSpec((self.halo, LANES), lambda hp, n: (jnp.maximum(n * per - 1, 0), col + hp)),
                pl.BlockSpec((self.halo, LANES), lambda hp, n: (jnp.minimum((n + 1) * per, last), col + hp))]

    def fill_ext(self, ext_ref, main_ref, prev_ref, next_ref):
        ext_ref[0:self.halo] = prev_ref[...].astype(F32)
        ext_ref[self.halo:self.halo + self.rows] = main_ref[...].astype(F32)
        ext_ref[self.halo + self.rows:self.ext] = next_ref[...].astype(F32)

    def rows_of(self, r, pos, count):
        start = r + pos * self.d
        return pl.ds(start, count, stride=self.d) if self.d > 1 else pl.ds(start, count)

    def variant(self, n, sub):
        v = 0
        if sub == 0:
            v = v + jnp.where(n == 0, 1, 0)
        if sub == self.nsub - 1:
            v = v + jnp.where(n == self.nbig - 1, 2, 0)
        return v


def _stack_heads(x, low):
    return jnp.concatenate([jnp.where(low, x, 0.0), jnp.where(low, 0.0, x)], axis=0)


def _attn_fwd(h, bias, mask, *, d, tq, nsub, cols, carry=None, name):
    kw = bias.shape[2]
    geo = _AttnGeom(h.shape[0], d, tq, nsub, kw)
    scale = HEAD_DIM ** -0.5

    ride = _Carried(carry, (4, geo.nbig))

    def body(*refs):
        (q_ref, km, kp, kn, vm, vp, vn, b_ref, m_ref), (o_ref, l_ref), (kext, vext) = ride.split(refs, 9, 2)
        ride.start()
        n = pl.program_id(1)
        geo.fill_ext(kext, km, kp, kn)
        geo.fill_ext(vext, vm, vp, vn)
        low = lax.broadcasted_iota(jnp.int32, (1, LANES), 1) < HEAD_DIM
        for r in range(d):
            for sub in range(nsub):
                madd = m_ref[geo.variant(n, sub)]
                q = q_ref[geo.rows_of(r, sub * tq, tq), :].astype(F32) * scale
                ks = kext[geo.rows_of(r, sub * tq, kw), :].astype(BF16)
                vs = vext[geo.rows_of(r, sub * tq, kw), :].astype(BF16)
                q2 = _stack_heads(q, low).astype(BF16)
                sc = (lax.dot_general(q2, ks, (((1,), (1,)), ((), ())), preferred_element_type=F32)
                      + (b_ref[...] + madd).reshape(2 * tq, kw))
                m = jnp.max(sc, axis=1, keepdims=True)
                p = jnp.exp(sc - m)
                den = jnp.sum(p, axis=1, keepdims=True)
                out2 = jnp.dot((p / den).astype(BF16), vs, preferred_element_type=F32)
                lse2 = m + jnp.log(den)
                o_ref[geo.rows_of(r, sub * tq, tq), :] = jnp.where(low, out2[0:tq], out2[tq:2 * tq])
                l_ref[geo.rows_of(r, sub * tq, tq), :] = jnp.where(low, lse2[0:tq], lse2[tq:2 * tq])
        ride.wait()

    qc, kc, vc = cols
    specs = ([geo.main(qc)] + geo.with_halos(kc) + geo.with_halos(vc)
             + [pl.BlockSpec((2, tq, kw), lambda hp, n: (hp, 0, 0)),
                pl.BlockSpec((4, tq, kw), lambda hp, n: (0, 0, 0))])
    shape = jax.ShapeDtypeStruct((geo.s, 4 * LANES), F32)
    ext = pltpu.VMEM((geo.ext, LANES), F32)
    out = pl.pallas_call(
        body, name=name, grid=(4, geo.nbig), in_specs=specs + ride.specs,
        out_specs=[geo.main(0), geo.main(0)] + ride.specs, out_shape=[shape, shape] + ride.out_shapes,
        scratch_shapes=[ext, ext] + ride.scratch,
        compiler_params=_params(("arbitrary", "arbitrary") if ride.on else ("parallel", "parallel"), VMEM_BIG))(
            h, h, h, h, h, h, h, bias, mask, *ride.operands)
    return out[0], out[1], (out[2] if ride.on else None)


def _attn_bwd(h, bias, mask, dy, y, lse, *, d, tq, nsub, cols, ycol, name):
    kw = bias.shape[2]
    geo = _AttnGeom(h.shape[0], d, tq, nsub, kw)
    scale = HEAD_DIM ** -0.5
    halo, rows = geo.halo, geo.rows
    align = math.gcd(halo, LANES)

    def body(q_ref, km, kp, kn, vm, vp, vn, b_ref, m_ref, dy_ref, y_ref, l_ref, dq_ref, dk_hbm, dv_hbm, db_ref,
             kext, vext, dkext, dvext, dk_all, dv_all, sems):
        hp, n = pl.program_id(0), pl.program_id(1)

        @pl.when(n == 0)
        def _():
            dk_all[...] = jnp.zeros_like(dk_all)
            dv_all[...] = jnp.zeros_like(dv_all)
            db_ref[...] = jnp.zeros_like(db_ref)

        geo.fill_ext(kext, km, kp, kn)
        geo.fill_ext(vext, vm, vp, vn)
        dkext[...] = jnp.zeros_like(dkext)
        dvext[...] = jnp.zeros_like(dvext)
        low = lax.broadcasted_iota(jnp.int32, (1, LANES), 1) < HEAD_DIM
        for r in range(d):
            for sub in range(nsub):
                madd = m_ref[geo.variant(n, sub)]
                mine = geo.rows_of(r, sub * tq, tq)
                keys = geo.rows_of(r, sub * tq, kw)
                q = q_ref[mine, :].astype(F32) * scale
                ks = kext[keys, :].astype(BF16)
                vs = vext[keys, :].astype(BF16)
                dyv = dy_ref[mine, :]
                dyy = dyv * y_ref[mine, :]
                lse_all = l_ref[mine, :]
                q2 = _stack_heads(q, low).astype(BF16)
                dy2 = _stack_heads(dyv, low).astype(BF16)
                dsum = jnp.concatenate([jnp.sum(jnp.where(sel, dyy, 0.0), axis=1, keepdims=True)
                                        for sel in (low, ~low)], axis=0)
                lse2 = jnp.concatenate([lse_all[:, 0:1], lse_all[:, HEAD_DIM:HEAD_DIM + 1]], axis=0)
                sc = (lax.dot_general(q2, ks, (((1,), (1,)), ((), ())), preferred_element_type=F32)
                      + (b_ref[...] + madd).reshape(2 * tq, kw))
                p = jnp.exp(sc - lse2)
                dp = lax.dot_general(dy2, vs, (((1,), (1,)), ((), ())), preferred_element_type=F32)
                ds = p * (dp - dsum)
                db_ref[...] += ds.reshape(2, tq, kw)
                pb, dsb = p.astype(BF16), ds.astype(BF16)
                dv = lax.dot_general(pb, dy2, (((0,), (0,)), ((), ())), preferred_element_type=F32)
                dk = lax.dot_general(dsb, q2, (((0,), (0,)), ((), ())), preferred_element_type=F32)
                dq2 = jnp.dot(dsb, ks, preferred_element_type=F32)
                dq_ref[mine, :] = jnp.where(low, dq2[0:tq], dq2[tq:2 * tq]) * scale
                dkext[keys, :] += dk
                dvext[keys, :] += dv

        before = pl.multiple_of(jnp.maximum(n * rows - halo, 0), align)
        here = pl.multiple_of(n * rows, align)
        after = pl.multiple_of(jnp.minimum((n + 1) * rows, geo.s - halo), align)
        for ext, total in ((dkext, dk_all), (dvext, dv_all)):
            total[pl.ds(before, halo), :] += ext[0:halo]
            total[pl.ds(here, rows), :] += ext[halo:halo + rows]
            total[pl.ds(after, halo), :] += ext[halo + rows:geo.ext]

        @pl.when(n == geo.nbig - 1)
        def _():
            col = pl.ds(pl.multiple_of(hp * LANES, LANES), LANES)
            copies = [pltpu.make_async_copy(total, out.at[:, col], sems.at[i])
                      for i, (total, out) in enumerate(((dk_all, dk_hbm), (dv_all, dv_hbm)))]
            for cp in copies:
                cp.start()
            for cp in copies:
                cp.wait()

    qc, kc, vc = cols
    bspec = pl.BlockSpec((2, tq, kw), lambda hp, n: (hp, 0, 0))
    any_spec = pl.BlockSpec(memory_space=pl.ANY)
    specs = ([geo.main(qc)] + geo.with_halos(kc) + geo.with_halos(vc)
             + [bspec, pl.BlockSpec((4, tq, kw), lambda hp, n: (0, 0, 0)), geo.main(ycol), geo.main(ycol), geo.main(0)])
    shape = jax.ShapeDtypeStruct((geo.s, 4 * LANES), F32)
    ext = pltpu.VMEM((geo.ext, LANES), F32)
    whole = pltpu.VMEM((geo.s, LANES), F32)
    return pl.pallas_call(
        body, name=name, grid=(4, geo.nbig), in_specs=specs, out_specs=[geo.main(0), any_spec, any_spec, bspec],
        out_shape=[shape, shape, shape, jax.ShapeDtypeStruct(bias.shape, F32)],
        scratch_shapes=[ext, ext, ext, ext, whole, whole, pltpu.SemaphoreType.DMA((2,))],
        compiler_params=_params(("arbitrary", "arbitrary"), VMEM_BIG))(
            h, h, h, h, h, h, h, bias, mask, dy, y, lse)


def _dil_combine(o_na, outs, lses, *, name):
    s, c = outs[0].shape
    t = _tile(s, (512, 256, 128, 64, 8))

    def body(na, o0, o1, o2, l0, l1, l2, y_ref, y16_ref, lt_ref):
        ls = [l0[...], l1[...], l2[...]]
        m = jnp.maximum(jnp.maximum(ls[0], ls[1]), ls[2])
        es = [jnp.exp(l - m) for l in ls]
        den = es[0] + es[1] + es[2]
        y = (es[0] / den) * o0[...] + (es[1] / den) * o1[...] + (es[2] / den) * o2[...]
        lt_ref[...] = m + jnp.log(den)
        y_ref[:, 0:c] = na[...]
        y_ref[:, c:2 * c] = y
        y16_ref[:, 0:c] = na[...].astype(BF16)
        y16_ref[:, c:2 * c] = y.astype(BF16)

    row = pl.BlockSpec((t, c), lambda i: (i, 0))
    wide = pl.BlockSpec((t, 2 * c), lambda i: (i, 0))
    return pl.pallas_call(body, name=name, grid=(s // t,), in_specs=[row] * 7, out_specs=[wide, wide, row],
                          out_shape=[jax.ShapeDtypeStruct((s, 2 * c), F32), jax.ShapeDtypeStruct((s, 2 * c), BF16),
                                     jax.ShapeDtypeStruct((s, c), F32)],
                          compiler_params=_params(("parallel",)))(o_na, *outs, *lses)


def _attn_dh(na, dil, *, name):
    s, c = na[0].shape
    t = _tile(s, (256, 128, 64, 8))

    def body(*refs):
        ins, o_ref = refs[:-1], refs[-1]
        for a in range(3):
            o_ref[:, a * c:(a + 1) * c] = ins[a][...].astype(BF16)
            o_ref[:, (3 + a) * c:(4 + a) * c] = (ins[3 + a][...] + ins[6 + a][...] + ins[9 + a][...]).astype(BF16)

    row = pl.BlockSpec((t, c), lambda i: (i, 0))
    flat = list(na) + [g[a] for g in dil for a in range(3)]
    return pl.pallas_call(body, name=name, grid=(s // t,), in_specs=[row] * 12,
                          out_specs=pl.BlockSpec((t, 6 * c), lambda i: (i, 0)),
                          out_shape=jax.ShapeDtypeStruct((s, 6 * c), BF16),
                          compiler_params=_params(("parallel",)))(*flat)


def _t5_bucket(rel):
    nb = N_BUCKETS // 2
    max_exact = nb // 2
    ret = np.where(rel > 0, nb, 0)
    n = np.abs(rel)
    large = max_exact + (np.log(np.maximum(n, 1).astype(np.float32) / np.float32(max_exact))
                         / np.float32(math.log(T5_MAX_DIST / max_exact)) * np.float32(nb - max_exact)).astype(np.int32)
    large = np.minimum(large, nb - 1)
    return (ret + np.where(n < max_exact, n, large)).astype(np.int32)


def _band_bucket_index(dil):
    tq, kw = DIL_TQ, DIL_KW
    rel = np.arange(kw)[None, :] - (kw - tq) // 2 - np.arange(tq)[:, None]
    return _t5_bucket(rel * dil)


def _band_bias(t5, dil, *, name):
    tq, kw = DIL_TQ, DIL_KW
    buckets = [int(b) for b in _t5_bucket(np.arange(-DIL_HALF, DIL_HALF + 1) * dil)]

    def body(t_ref, o_ref):
        hh = pl.program_id(0)
        rel = (lax.broadcasted_iota(jnp.int32, (tq, kw), 1) - (kw - tq) // 2
               - lax.broadcasted_iota(jnp.int32, (tq, kw), 0))
        acc = jnp.zeros((tq, kw), F32)
        for r, b in zip(range(-DIL_HALF, DIL_HALF + 1), buckets):
            acc = jnp.where(rel == r, t_ref[b * 8 + hh], acc)
        o_ref[0] = acc

    return pl.pallas_call(body, name=name, grid=(8,),
                          in_specs=[pl.BlockSpec(memory_space=pltpu.SMEM)],
                          out_specs=pl.BlockSpec((1, tq, kw), lambda h: (h, 0, 0)),
                          out_shape=jax.ShapeDtypeStruct((8, tq, kw), F32),
                          compiler_params=_params(("parallel",)))(t5.reshape(-1))


def _na_bias(rpb, *, name):
    nr, nc = 2 * NA_KH - 1, 2 * NA_KW - 1
    tq = NA_QROWS * GRID_W
    w = GRID_W

    def body(r_ref, o_ref):
        base = pl.program_id(0) * (nr * nc)
        lane = lax.broadcasted_iota(jnp.int32, (w, LANES), 1)
        upper = lane >= w
        diff = (lane & (w - 1)) - lax.broadcasted_iota(jnp.int32, (w, LANES), 0) + NA_KW - 1
        tiles = {}
        for i in range(NA_QROWS):
            for m in range(3 * NA_QROWS // 2):
                lo = 2 * m - i + NA_KH - 1 - NA_QROWS
                if lo not in tiles:
                    acc = jnp.zeros((w, LANES), F32)
                    for dc in range(nc):
                        v_lo = r_ref[base + lo * nc + dc] if 0 <= lo < nr else 0.0
                        v_hi = r_ref[base + (lo + 1) * nc + dc] if 0 <= lo + 1 < nr else 0.0
                        acc = jnp.where(diff == dc, jnp.where(upper, v_hi, v_lo), acc)
                    tiles[lo] = acc
                o_ref[0, i * w:(i + 1) * w, m * LANES:(m + 1) * LANES] = tiles[lo]

    return pl.pallas_call(body, name=name, grid=(8,),
                          in_specs=[pl.BlockSpec(memory_space=pltpu.SMEM)],
                          out_specs=pl.BlockSpec((1, tq, 3 * tq), lambda h: (h, 0, 0)),
                          out_shape=jax.ShapeDtypeStruct((8, tq, 3 * tq), F32),
                          compiler_params=_params(("parallel",)))(rpb.reshape(-1))


def _t5_grad(dbs, idxs, *, name):
    def body(d0, d1, d2, i0, i1, i2, o_ref):
        lane = lax.broadcasted_iota(jnp.int32, (1, LANES), 1)
        lines = [jnp.zeros((1, LANES), F32) for _ in range(8)]
        for dref, iref in ((d0, i0), (d1, i1), (d2, i2)):
            idx = iref[...]
            for hh in range(8):
                xh = dref[hh]
                for b in range(N_BUCKETS):
                    val = jnp.sum(jnp.sum(jnp.where(idx == b, xh, 0.0), axis=1, keepdims=True), axis=0, keepdims=True)
                    lines[hh] = lines[hh] + jnp.where(lane == b, val, 0.0)
        for hh in range(8):
            o_ref[hh:hh + 1, :] = lines[hh]

    out = pl.pallas_call(body, name=name, out_shape=jax.ShapeDtypeStruct((8, LANES), F32))(*dbs, *idxs)
    return out[:, :N_BUCKETS].T


def _rpb_grad(db, *, name):
    nr, nc = 2 * NA_KH - 1, 2 * NA_KW - 1
    tq = NA_QROWS * GRID_W
    w = GRID_W

    def body(d_ref, o_ref):
        x = d_ref[0]
        rows = []
        for dr in range(nr):
            acc = jnp.zeros((w, w), F32)
            for i in range(NA_QROWS):
                j = i + dr - (NA_KH - 1 - NA_QROWS)
                if 0 <= j < 3 * NA_QROWS:
                    acc = acc + x[i * w:(i + 1) * w, j * w:(j + 1) * w]
            rows.append(acc)
        diff = (lax.broadcasted_iota(jnp.int32, (w, w), 1) - lax.broadcasted_iota(jnp.int32, (w, w), 0)
                + NA_KW - 1)
        lane = lax.broadcasted_iota(jnp.int32, (1, LANES), 1)
        for dr in range(nr):
            line = jnp.zeros((1, LANES), F32)
            for dc in range(nc):
                val = jnp.sum(jnp.sum(jnp.where(diff == dc, rows[dr], 0.0), axis=1, keepdims=True),
                              axis=0, keepdims=True)
                line = jnp.where(lane == dc, val, line)
            o_ref[0, dr:dr + 1, :] = line

    out = pl.pallas_call(body, name=name, grid=(8,),
                         in_specs=[pl.BlockSpec((1, tq, 3 * tq), lambda h: (h, 0, 0))],
                         out_specs=pl.BlockSpec((1, nr, LANES), lambda h: (h, 0, 0)),
                         out_shape=jax.ShapeDtypeStruct((8, nr, LANES), F32),
                         compiler_params=_params(("parallel",)))(db)
    return out[:, :, :nc]


def _exchange(src, *, gather, name):
    shape = src.shape if not gather else (N_DEV,) + src.shape

    def body(src_ref, out_ref, send_sems, recv_sems, local_sem):
        _exchange_start(src_ref, out_ref, send_sems, recv_sems, local_sem, gather)
        _exchange_wait(src_ref, out_ref, send_sems, recv_sems, local_sem, gather)

    any_spec = pl.BlockSpec(memory_space=pl.ANY)
    return pl.pallas_call(
        body, name=name, in_specs=[any_spec], out_specs=any_spec, out_shape=jax.ShapeDtypeStruct(shape, src.dtype),
        scratch_shapes=_exchange_sems())(src)


class _Carried:
    def __init__(self, carry, grid):
        self.src, self.gather = carry if carry is not None else (None, False)
        self.on = self.src is not None
        self.grid = grid
        self.operands = [self.src] if self.on else []
        self.specs = [pl.BlockSpec(memory_space=pl.ANY)] if self.on else []
        self.scratch = _exchange_sems() if self.on else []
        self.out_shapes = []
        if self.on:
            shape = ((N_DEV,) + self.src.shape) if self.gather else self.src.shape
            self.out_shapes = [jax.ShapeDtypeStruct(shape, self.src.dtype)]

    def split(self, refs, n_in, n_out):
        refs = list(refs)
        if not self.on:
            return refs[:n_in], refs[n_in:n_in + n_out], refs[n_in + n_out:]
        self.refs = (refs[n_in], refs[n_in + 1 + n_out], *refs[-3:], self.gather)
        return refs[:n_in], refs[n_in + 1:n_in + 1 + n_out], refs[n_in + 2 + n_out:-3]

    def _at(self, last):
        hit = None
        for ax, size in enumerate(self.grid):
            here = pl.program_id(ax) == (size - 1 if last else 0)
            hit = here if hit is None else hit & here
        return hit

    def start(self):
        if self.on:
            pl.when(self._at(False))(lambda: _exchange_start(*self.refs))

    def wait(self):
        if self.on:
            pl.when(self._at(True))(lambda: _exchange_wait(*self.refs))


def _exchange_sems():
    return [pltpu.SemaphoreType.DMA((N_DEV - 1,)), pltpu.SemaphoreType.DMA((N_DEV - 1,)), pltpu.SemaphoreType.DMA]


def _exchange_copies(src_ref, out_ref, send_sems, recv_sems, local_sem, gather):
    x, y, c = lax.axis_index("x"), lax.axis_index("y"), lax.axis_index("c")
    me = 4 * x + 2 * y + c

    def outgoing(p):
        return src_ref if gather else src_ref.at[p]

    own = pltpu.make_async_copy(outgoing(me), out_ref.at[me], local_sem)
    sends, recvs = [], []
    for k in range(1, N_DEV):
        px = 1 - x if k & 4 else x
        py = 1 - y if k & 2 else y
        pc = 1 - c if k & 1 else c
        p = 4 * px + 2 * py + pc
        for dst, group in ((me, sends), (p, recvs)):
            group.append(pltpu.make_async_remote_copy(
                src_ref=outgoing(p), dst_ref=out_ref.at[dst], send_sem=send_sems.at[k - 1],
                recv_sem=recv_sems.at[k - 1], device_id=(px, py, pc), device_id_type=pl.DeviceIdType.MESH))
    return own, sends, recvs


def _exchange_start(*refs_and_mode):
    own, sends, _ = _exchange_copies(*refs_and_mode)
    own.start()
    for cp in sends:
        cp.start()


def _exchange_wait(*refs_and_mode):
    own, sends, recvs = _exchange_copies(*refs_and_mode)
    for cp in recvs:
        cp.wait_recv()
    for cp in sends:
        cp.wait_send()
    own.wait()


def _adamw(parts, w, m, v, *, name):
    layers = len(parts)
    rows, cols = w.shape
    per_layer = rows // layers
    t = _tile(per_layer, (FLAT_ROW_TILE, 128, 64, 32, 16, 8))
    nt = per_layer // t

    def body(*refs):
        p_refs = refs[:layers]
        w_ref, m_ref, v_ref, g_ref, d_ref, nm_ref, nv_ref = refs[layers:]
        layer = pl.program_id(0)
        g = None
        for l, p_ref in enumerate(p_refs):
            total = p_ref[0].astype(F32)
            for k in range(1, N_DEV):
                total = total + p_ref[k].astype(F32)
            g = total if g is None else jnp.where(layer == l, total, g)
        nm = ADAM_B1 * m_ref[...] + (1.0 - ADAM_B1) * g
        nv = ADAM_B2 * v_ref[...] + (1.0 - ADAM_B2) * (g * g)
        m_hat = nm / (1.0 - ADAM_B1 ** ADAM_STEP)
        v_hat = nv / (1.0 - ADAM_B2 ** ADAM_STEP)
        g_ref[...] = g
        d_ref[...] = -ADAM_LR * (m_hat / (jnp.sqrt(v_hat) + ADAM_EPS) + ADAM_WD * w_ref[...])
        nm_ref[...] = nm
        nv_ref[...] = nv

    def part_spec(l):
        return pl.BlockSpec((N_DEV, t, cols), lambda layer, i: (0, jnp.where(layer == l, i, 0), 0))

    row = pl.BlockSpec((t, cols), lambda layer, i: (layer * nt + i, 0))
    shape = jax.ShapeDtypeStruct((rows, cols), F32)
    return pl.pallas_call(body, name=name, grid=(layers, nt),
                          in_specs=[part_spec(l) for l in range(layers)] + [row, row, row],
                          out_specs=[row] * 4, out_shape=[shape] * 4,
                          compiler_params=_params(("parallel", "parallel"), VMEM_BIG))(*parts, w, m, v)


def _flatten(arrays, dtype, row_mult):
    flat = jnp.concatenate([a.reshape(-1).astype(dtype) for a in arrays])
    chunk = FLAT_COLS * row_mult
    padded = -(-flat.shape[0] // chunk) * chunk
    return jnp.pad(flat, (0, padded - flat.shape[0])).reshape(padded // FLAT_COLS, FLAT_COLS)


def _unflatten(flat, shapes):
    flat = flat.reshape(-1)
    out, pos = [], 0
    for shp in shapes:
        size = int(np.prod(shp))
        out.append(flat[pos:pos + size].reshape(shp))
        pos += size
    return out


def _gather_full(names, local, dtype, row_mult, label):
    got = _exchange(_flatten([local[n] for n in names], dtype, row_mult), gather=True, name=label)
    got = got.reshape(N_DEV, -1)
    full, pos = {}, 0
    for n in names:
        shp = local[n].shape
        size = int(np.prod(shp))
        piece = got[:, pos:pos + size].reshape(N_DEV, size // shp[-1], shp[-1])
        full[n] = jnp.transpose(piece, (1, 0, 2)).reshape(shp[:-1] + (N_DEV * shp[-1],))
        pos += size
    return full


def _scatter_rows(names, grads, row_mult):
    pieces = []
    for n in names:
        c = grads[n].shape[-1] // N_DEV
        pieces.append(jnp.transpose(grads[n].reshape(-1, N_DEV, c), (1, 0, 2)).reshape(N_DEV, -1))
    flat = jnp.concatenate(pieces, axis=1)
    chunk = FLAT_COLS * row_mult
    padded = -(-flat.shape[1] // chunk) * chunk
    return jnp.pad(flat, ((0, 0), (0, padded - flat.shape[1]))).reshape(N_DEV, padded // FLAT_COLS, FLAT_COLS)


def _from_shards(stacked, axis):
    _, a, b = stacked.shape
    if axis == 1:
        return jnp.transpose(stacked, (1, 0, 2)).reshape(a, N_DEV * b)
    return stacked.reshape(N_DEV * a, b)


def _to_shards(full, axis):
    ra, rb = full.shape
    if axis == 1:
        return jnp.transpose(full.reshape(ra, N_DEV, rb // N_DEV), (1, 0, 2))
    return full.reshape(N_DEV, ra // N_DEV, rb)


class _ShardedMatmulWeights:
    def __init__(self, local):
        self.local, self.full, self.parts = local, {}, {}

    def gather_src(self, n, l):
        return self.local[n][l].astype(BF16)

    def set_gathered(self, n, l, got):
        self.full[n, l] = _from_shards(got, SHARD_AXIS[n] - 1)

    def get(self, n, l):
        return self.full[n, l]

    def scatter_src(self, n, l, dw):
        return _to_shards(dw, SHARD_AXIS[n] - 1).astype(BF16)

    def set_scattered(self, n, l, parts):
        self.parts[n, l] = parts


def _role(role, i):
    mixer = 'attn_w_' if i % 2 == 0 else 'conv_w_'
    return {'in': (mixer + 'in', i // 2), 'out': (mixer + 'out', i // 2),
            'up': ('ffn_w_up', i), 'down': ('ffn_w_down', i)}[role]


def _local_step(x, tgt, w, big):
    s = x.shape[0]

    def gather_of(role, i):
        if role is None or i >= DEPTH:
            return None, None
        key = _role(role, i)
        return key, (big.gather_src(*key), True)

    def project(a, role, i, ln=None, gather=None):
        nxt, carry = gather_of(gather, i + 1)
        weight = big.get(*_role(role, i))
        if ln is None:
            out, got = _mm(a, weight, carry=carry, name=role + "_fwd")
        else:
            *out, got = _mm_ln(a, weight, *ln, carry=carry, name=role + "_fwd")
        if got is not None:
            big.set_gathered(*nxt, got)
        return out

    def project_back(a, d_out, role, i, ln=None, scatter=True):
        key = _role(role, i)
        dw, _ = _mm(a, d_out, ta=True, name=role + "_dw")
        carry = (big.scatter_src(*key, dw), False)
        if not scatter:
            return _mm(d_out, big.get(*key), tb=True, name=role + "_dx")[0], (key, carry)
        if ln is None:
            d_in, parts = _mm(d_out, big.get(*key), tb=True, carry=carry, name=role + "_dx")
        else:
            *d_in, parts = _mm_ln_bwd(d_out, big.get(*key), *ln, carry=carry, name=role + "_dx")
        if parts is not None:
            big.set_scattered(*key, parts)
        return d_in

    na_tq = NA_QROWS * GRID_W
    band_idx = [_band_bucket_index(d) for _, d in DIL_PATTERNS]
    band_bias = [_band_bias(w['t5_bias'], d, name=f"band_bias_{d}") for _, d in DIL_PATTERNS]
    band_mask = [_mask_tiles("band", DIL_TQ, DIL_KW, s // d, name=f"band_mask_{d}") for _, d in DIL_PATTERNS]
    na_mask = _mask_tiles("na", na_tq, 3 * na_tq, s, name="na_mask")
    na_cols, dil_cols = (0, 4, 8), (12, 16, 20)
    grads = {n: [None] * w[n].shape[0] for n in SMALL_SHARDED + REPLICATED if n != 't5_bias'}
    saved = []
    x16 = x.astype(BF16)

    for i in range(DEPTH):
        j = i // 2
        st = {'x': x, 'x16': x16}
        if i % 2 == 0:
            h = project(x16, 'in', i)
            na_bias = _na_bias(w['na_rpb'][j], name="na_bias")
            late = ('up', 'down', 'out') if i == 0 else ()
            rides = [gather_of(role, 0) for role in late] + [(None, None)] * 4
            outs, lses = [], []
            calls = [(na_bias, na_mask, 1, na_tq, NA_NSUB, na_cols, "na_fwd")] + [
                (bias, mask, d, DIL_TQ, DIL_NSUB[d], dil_cols, f"dil_fwd_{d}")
                for (_, d), bias, mask in zip(DIL_PATTERNS, band_bias, band_mask)]
            for (bias, mask, d, tq, nsub, cols, label), (key, carry) in zip(calls, rides):
                o, l, got = _attn_fwd(h, bias, mask, d=d, tq=tq, nsub=nsub, cols=cols, carry=carry, name=label)
                if got is not None:
                    big.set_gathered(*key, got)
                outs.append(o)
                lses.append(l)
            o_na, l_na = outs.pop(0), lses.pop(0)
            mid, mid16, l_dil = _dil_combine(o_na, outs, lses, name="dil_combine")
            st.update(h=h, mid=mid, mid16=mid16, l_na=l_na, l_dil=l_dil, na_bias=na_bias)
        else:
            h = project(x16, 'in', i)
            mid16, u2 = _conv_mid_fwd(h, w['conf_dw_w'][j], w['conf_dw_b'][j], w['conf_ln_g'][j], w['conf_ln_b'][j],
                                      w['sconv_w'][j], name="conv_mid_fwd")
            st.update(h=h, mid16=mid16, u2=u2)
        z_mix, xa, xa16 = project(mid16, 'out', i, ln=(x, w['mix_ln_g'][i], w['mix_ln_b'][i]), gather='out')
        hu = project(xa16, 'up', i, gather='in')
        nxt, carry = gather_of('up', i + 1)
        act16, got = _ffn_mid_fwd(hu, w['ffn_dw_w'][i], carry=carry, name="ffn_mid_fwd")
        if got is not None:
            big.set_gathered(*nxt, got)
        z_ffn, xb, xb16 = project(act16, 'down', i, ln=(xa, w['ffn_ln_g'][i], w['ffn_ln_b'][i]), gather='down')
        st.update(z_mix=z_mix, xa16=xa16, hu=hu, act16=act16, z_ffn=z_ffn)
        saved.append(st)
        x, x16 = xb, xb16

    loss, d_loss = _loss_head(x, tgt, name="loss_head")
    g_t5 = None
    dz, dz16, dg, db = _ln_bwd(saved[-1]['z_ffn'], w['ffn_ln_g'][-1], d_loss, None, name="last_ln_bwd")
    for i in reversed(range(DEPTH)):
        j = i // 2
        st = saved[i]
        grads['ffn_ln_g'][i], grads['ffn_ln_b'][i] = dg, db
        dact, (key, carry) = project_back(st['act16'], dz16, 'down', i, scatter=False)
        dhu, grads['ffn_dw_w'][i], parts = _ffn_mid_bwd(st['hu'], w['ffn_dw_w'][i], dact, carry=carry,
                                                       name="ffn_mid_bwd")
        if parts is not None:
            big.set_scattered(*key, parts)
        dz, dz1, dg, db = project_back(st['xa16'], dhu, 'up', i, ln=(st['z_mix'], w['mix_ln_g'][i], dz))
        grads['mix_ln_g'][i], grads['mix_ln_b'][i] = dg, db
        dmid = project_back(st['mid16'], dz1, 'out', i)
        if i % 2 == 0:
            h = st['h']
            dq, dk, dv, dbias = _attn_bwd(h, st['na_bias'], na_mask, dmid, st['mid'], st['l_na'], d=1, tq=na_tq,
                                          nsub=NA_NSUB, cols=na_cols, ycol=0, name="na_bwd")
            grads['na_rpb'][j] = _rpb_grad(dbias, name="rpb_grad")
            dil, dbs = [], []
            for (_, d), bias, mask in zip(DIL_PATTERNS, band_bias, band_mask):
                g = _attn_bwd(h, bias, mask, dmid, st['mid'], st['l_dil'], d=d, tq=DIL_TQ, nsub=DIL_NSUB[d],
                              cols=dil_cols, ycol=4, name=f"dil_bwd_{d}")
                dil.append(g[:3])
                dbs.append(g[3])
            t5 = _t5_grad(dbs, band_idx, name="t5_grad")
            g_t5 = t5 if g_t5 is None else g_t5 + t5
            dh = _attn_dh((dq, dk, dv), dil, name="attn_dh")
        else:
            dh, dw31, db31, dlg, dlb, dw3 = _conv_mid_bwd(st['h'], st['u2'], dmid, w['conf_dw_w'][j],
                                                          w['conf_ln_g'][j], w['conf_ln_b'][j], w['sconv_w'][j],
                                                          name="conv_mid_bwd")
            grads['conf_dw_w'][j], grads['conf_dw_b'][j] = dw31, db31
            grads['conf_ln_g'][j], grads['conf_ln_b'][j], grads['sconv_w'][j] = dlg, dlb, dw3
        if i > 0:
            below = saved[i - 1]
            dz, dz16, dg, db = project_back(st['x16'], dh, 'in', i, ln=(below['z_ffn'], w['ffn_ln_g'][i - 1], dz))
        else:
            dx = _axpy(dz, project_back(st['x16'], dh, 'in', i), name="grad_x")
    full = {n: jnp.stack(g) for n, g in grads.items()}
    full['t5_bias'] = g_t5
    return loss, dx, full


def kernel(x, t5_bias, attn_w_in, attn_w_out, na_rpb, conv_w_in, conf_dw_w, conf_dw_b, conf_ln_g, conf_ln_b, sconv_w, conv_w_out, ffn_w_up, ffn_dw_w, ffn_w_down, mix_ln_g, mix_ln_b, ffn_ln_g, ffn_ln_b, loss_target, m_t5_bias, m_attn_w_in, m_attn_w_out, m_na_rpb, m_conv_w_in, m_conf_dw_w, m_conf_dw_b, m_conf_ln_g, m_conf_ln_b, m_sconv_w, m_conv_w_out, m_ffn_w_up, m_ffn_dw_w, m_ffn_w_down, m_mix_ln_g, m_mix_ln_b, m_ffn_ln_g, m_ffn_ln_b, v_t5_bias, v_attn_w_in, v_attn_w_out, v_na_rpb, v_conv_w_in, v_conf_dw_w, v_conf_dw_b, v_conf_ln_g, v_conf_ln_b, v_sconv_w, v_conv_w_out, v_ffn_w_up, v_ffn_dw_w, v_ffn_w_down, v_mix_ln_g, v_mix_ln_b, v_ffn_ln_g, v_ffn_ln_b):
    args = dict(locals())
    local = {n: args[n] for n in WEIGHTS}
    mom1 = {n: args['m_' + n] for n in WEIGHTS}
    mom2 = {n: args['v_' + n] for n in WEIGHTS}

    kinds = ('grad', 'delta', 'new_m', 'new_v')
    small = {n: local[n] for n in REPLICATED}
    small.update(_gather_full(SMALL_SHARDED, local, F32, 8, "gather_small_weights"))
    big = _ShardedMatmulWeights({n: local[n] for n in MATMUL_WEIGHTS})
    n, l = _role('in', 0)
    big.set_gathered(n, l, _exchange(big.gather_src(n, l), gather=True, name="gather_first_in"))

    loss, dx, grads = _local_step(x[0], loss_target[0], small, big)
    loss = lax.psum(loss, MESH_AXES)

    out = {}
    for n in MATMUL_WEIGHTS:
        layers, a, b = local[n].shape
        res = _adamw([big.parts[n, l] for l in range(layers)],
                     *[t.reshape(layers * a, b) for t in (local[n], mom1[n], mom2[n])], name="adamw_" + n)
        for kind, r in zip(kinds, res):
            out[kind + '_' + n] = r.reshape(layers, a, b)
    for names, sharded, label in ((SMALL_SHARDED, True, "small"), (REPLICATED, False, "replicated")):
        shapes = [local[n].shape for n in names]
        if sharded:
            parts = _exchange(_scatter_rows(names, grads, 8), gather=False, name="scatter_small_grads")
        else:
            parts = _exchange(_flatten([grads[n] for n in names], F32, 8), gather=True, name="gather_replicated_grads")
        res = _adamw([parts], _flatten([local[n] for n in names], F32, 8), _flatten([mom1[n] for n in names], F32, 8),
                     _flatten([mom2[n] for n in names], F32, 8), name="adamw_" + label)
        for kind, flat in zip(kinds, res):
            for n, a in zip(names, _unflatten(flat, shapes)):
                out[kind + '_' + n] = a

    return (loss, dx[None], *[out[k + '_' + n] for k in ('grad', 'delta', 'new_m', 'new_v') for n in WEIGHTS])
```

```python
import math

import jax
import jax.numpy as jnp
import numpy as np
from jax import lax
from jax.experimental import pallas as pl
from jax.experimental.pallas import tpu as pltpu

F32 = jnp.float32
BF16 = jnp.bfloat16

N_DEV = 8
MESH_AXES = ("x", "y", "c")
DEPTH = 4
GRID_W = 64
GRID_SHIFT = 6
HEAD_DIM = 64
NA_KH = 8
NA_KW = 16
NA_QROWS = 4
DIL_PATTERNS = ((128, 1), (512, 4), (2048, 16))
DIL_HALF = 64
DIL_TQ = 128
DIL_KW = DIL_TQ + 2 * DIL_HALF
DIL_NSUB = {1: 8, 4: 2, 16: 1}
NA_NSUB = 4
N_BUCKETS = 32
T5_MAX_DIST = 1024
CONF_CH = 512
CONF_K = 31
SC_K = 3
FFN_K = 3
FFN_ROW_TILE = 256
FFN_STRIP = 128
LN_EPS = 1e-5
NEG = -1e30
ALPHA = (2 * DEPTH) ** 0.25
ADAM_LR = 0.001
ADAM_B1 = 0.9
ADAM_B2 = 0.999
ADAM_EPS = 1e-08
ADAM_WD = 0.01
ADAM_STEP = 10

LANES = 128
SUBLANES = 8
VMEM_BIG = 48 * 1024 * 1024
FLAT_COLS = 1024
FLAT_ROW_TILE = 256

WEIGHTS = ['t5_bias', 'attn_w_in', 'attn_w_out', 'na_rpb', 'conv_w_in', 'conf_dw_w', 'conf_dw_b', 'conf_ln_g',
           'conf_ln_b', 'sconv_w', 'conv_w_out', 'ffn_w_up', 'ffn_dw_w', 'ffn_w_down', 'mix_ln_g', 'mix_ln_b',
           'ffn_ln_g', 'ffn_ln_b']
SHARD_AXIS = {'attn_w_in': 2, 'attn_w_out': 1, 'conv_w_in': 2, 'conf_dw_w': 2, 'conf_dw_b': 1, 'conf_ln_g': 1,
              'conf_ln_b': 1, 'sconv_w': 2, 'conv_w_out': 1, 'ffn_w_up': 2, 'ffn_dw_w': 2, 'ffn_w_down': 1}
MATMUL_WEIGHTS = ['attn_w_in', 'attn_w_out', 'conv_w_in', 'conv_w_out', 'ffn_w_up', 'ffn_w_down']
SMALL_SHARDED = ['conf_dw_w', 'conf_dw_b', 'conf_ln_g', 'conf_ln_b', 'sconv_w', 'ffn_dw_w']
SHARDED = MATMUL_WEIGHTS + SMALL_SHARDED
REPLICATED = ['t5_bias', 'na_rpb', 'mix_ln_g', 'mix_ln_b', 'ffn_ln_g', 'ffn_ln_b']


def _tile(n, cands):
    for c in cands:
        if n % c == 0:
            return c
    return n


def _params(sem, vmem=None):
    return pltpu.CompilerParams(dimension_semantics=sem, vmem_limit_bytes=vmem)


def _sigmoid(x):
    return 0.5 * jnp.tanh(0.5 * x) + 0.5


MM_MAX_TILE = 1408
MM_MAX_K = 3072


def _lane_tile(n, cap):
    best = None
    for t in range(LANES, min(n, cap) + 1, LANES):
        if n % t == 0:
            best = t
    return best or n


def _mm(a, b, *, ta=False, tb=False, out_dtype=F32, carry=None, name):
    assert a.dtype == BF16 and b.dtype == BF16, (name, a.dtype, b.dtype)
    m, k = (a.shape[1], a.shape[0]) if ta else a.shape
    n = b.shape[0] if tb else b.shape[1]
    tm, tn, tk = _lane_tile(m, MM_MAX_TILE), _lane_tile(n, MM_MAX_TILE), _lane_tile(k, MM_MAX_K)
    grid = (m // tm, n // tn, k // tk)
    nk = grid[2]
    dims = (((0 if ta else 1,), (1 if tb else 0,)), ((), ()))
    use_acc = nk > 1 and out_dtype != F32
    ride = _Carried(carry, grid)

    def body(*refs):
        (a_ref, b_ref), (o_ref,), scratch = ride.split(refs, 2, 1)
        ride.start()
        part = lax.dot_general(a_ref[...], b_ref[...], dims, preferred_element_type=F32)
        if nk == 1:
            o_ref[...] = part.astype(out_dtype)
        else:
            acc_ref = scratch[0] if use_acc else o_ref
            kk = pl.program_id(2)

            @pl.when(kk == 0)
            def _():
                acc_ref[...] = part

            @pl.when(kk > 0)
            def _():
                acc_ref[...] += part

            if use_acc:
                @pl.when(kk == nk - 1)
                def _():
                    o_ref[...] = acc_ref[...].astype(out_dtype)

        ride.wait()

    a_spec = pl.BlockSpec((tk, tm), lambda i, j, q: (q, i)) if ta else pl.BlockSpec((tm, tk), lambda i, j, q: (i, q))
    b_spec = pl.BlockSpec((tn, tk), lambda i, j, q: (j, q)) if tb else pl.BlockSpec((tk, tn), lambda i, j, q: (q, j))
    o_spec = pl.BlockSpec((tm, tn), lambda i, j, q: (i, j))
    o_shape = jax.ShapeDtypeStruct((m, n), out_dtype)
    scratch = [pltpu.VMEM((tm, tn), F32)] if use_acc else []
    sem = ("arbitrary",) * 3 if ride.on else ("parallel", "parallel", "arbitrary")
    out = pl.pallas_call(
        body, name=name, grid=grid, in_specs=[a_spec, b_spec] + ride.specs, out_specs=[o_spec] + ride.specs,
        out_shape=[o_shape] + ride.out_shapes, scratch_shapes=scratch + ride.scratch,
        compiler_params=_params(sem, VMEM_BIG))(a, b, *ride.operands)
    return out[0], (out[1] if ride.on else None)


MM_LN_ROWS = 512


def _mm_ln(a, b, x, g, beta, *, carry=None, name):
    assert a.dtype == BF16 and b.dtype == BF16, (name, a.dtype, b.dtype)
    m, k = a.shape
    n = b.shape[1]
    assert k <= MM_MAX_K, (name, k)
    tm = _tile(m, (MM_LN_ROWS, 256, 128, 64, 8))
    grid = (m // tm,)
    ride = _Carried(carry, grid)

    def body(*refs):
        (a_ref, b_ref, x_ref, g_ref, beta_ref), (z_ref, o_ref, o16_ref), _ = ride.split(refs, 5, 3)
        ride.start()
        z = ALPHA * x_ref[...] + jnp.dot(a_ref[...], b_ref[...], preferred_element_type=F32)
        z_ref[...] = z
        xh, _ = _ln_stats(z)
        out = xh * g_ref[...] + beta_ref[...]
        o_ref[...] = out
        o16_ref[...] = out.astype(BF16)
        ride.wait()

    row = lambda width: pl.BlockSpec((tm, width), lambda i: (i, 0))
    vec = pl.BlockSpec((1, n), lambda i: (0, 0))
    out = pl.pallas_call(
        body, name=name, grid=grid,
        in_specs=[row(k), pl.BlockSpec((k, n), lambda i: (0, 0)), row(n), vec, vec] + ride.specs,
        out_specs=[row(n)] * 3 + ride.specs,
        out_shape=[jax.ShapeDtypeStruct((m, n), F32), jax.ShapeDtypeStruct((m, n), F32),
                   jax.ShapeDtypeStruct((m, n), BF16)] + ride.out_shapes,
        scratch_shapes=ride.scratch,
        compiler_params=_params(("arbitrary",) if ride.on else ("parallel",), VMEM_BIG))(
            a, b, x, g.reshape(1, n), beta.reshape(1, n), *ride.operands)
    return out[0], out[1], out[2], (out[3] if ride.on else None)


def _mm_ln_bwd(d_out, b, z, g, d1, *, carry=None, name):
    assert d_out.dtype == BF16 and b.dtype == BF16, (name, d_out.dtype, b.dtype)
    m, k = d_out.shape
    n = b.shape[0]
    tm, tk = _tile(m, (MM_LN_ROWS, 256, 128, 64, 8)), _lane_tile(k, MM_MAX_K)
    grid = (m // tm, k // tk)
    nk = grid[1]
    ride = _Carried(carry, grid)

    def body(*refs):
        (a_ref, b_ref, z_ref, g_ref, d1_ref), (dz_ref, dz16_ref, dg_ref, db_ref), scratch = ride.split(refs, 5, 4)
        ride.start()
        i, kk = pl.program_id(0), pl.program_id(1)
        part = lax.dot_general(a_ref[...], b_ref[...], (((1,), (1,)), ((), ())), preferred_element_type=F32)

        def finish(d2):
            @pl.when(i == 0)
            def _():
                dg_ref[...] = jnp.zeros_like(dg_ref)
                db_ref[...] = jnp.zeros_like(db_ref)

            dout = ALPHA * d1_ref[...] + d2
            xh, rstd = _ln_stats(z_ref[...])
            dxh = dout * g_ref[...]
            dz = rstd * (dxh - jnp.mean(dxh, axis=-1, keepdims=True) - xh * jnp.mean(dxh * xh, axis=-1, keepdims=True))
            dz_ref[...] = dz
            dz16_ref[...] = dz.astype(BF16)
            dg_ref[...] += jnp.sum(dout * xh, axis=0, keepdims=True)
            db_ref[...] += jnp.sum(dout, axis=0, keepdims=True)

        if nk == 1:
            finish(part)
        else:
            acc_ref = scratch[0]

            @pl.when(kk == 0)
            def _():
                acc_ref[...] = part

            @pl.when((kk > 0) & (kk < nk - 1))
            def _():
                acc_ref[...] += part

            @pl.when(kk == nk - 1)
            def _():
                finish(acc_ref[...] + part)

        ride.wait()

    row = pl.BlockSpec((tm, n), lambda i, q: (i, 0))
    vec = pl.BlockSpec((1, n), lambda i, q: (0, 0))
    out = pl.pallas_call(
        body, name=name, grid=grid,
        in_specs=[pl.BlockSpec((tm, tk), lambda i, q: (i, q)), pl.BlockSpec((n, tk), lambda i, q: (0, q)),
                  row, vec, row] + ride.specs,
        out_specs=[row, row, vec, vec] + ride.specs,
        out_shape=[jax.ShapeDtypeStruct((m, n), F32), jax.ShapeDtypeStruct((m, n), BF16),
                   jax.ShapeDtypeStruct((1, n), F32), jax.ShapeDtypeStruct((1, n), F32)] + ride.out_shapes,
        scratch_shapes=([pltpu.VMEM((tm, n), F32)] if nk > 1 else []) + ride.scratch,
        compiler_params=_params(("arbitrary", "arbitrary"), VMEM_BIG))(
            d_out, b, z, g.reshape(1, n), d1, *ride.operands)
    return out[0], out[1], out[2].reshape(n), out[3].reshape(n), (out[4] if ride.on else None)


def _ln_stats(z):
    mu = jnp.mean(z, axis=-1, keepdims=True)
    zc = z - mu
    var = jnp.mean(zc * zc, axis=-1, keepdims=True)
    rstd = lax.rsqrt(var + LN_EPS)
    return zc * rstd, rstd


def _ln_bwd(z, g, d1, d2, *, name):
    s, d = z.shape
    t = _tile(s, (256, 128, 64, 8))
    two = d2 is not None

    def body(*refs):
        if two:
            z_ref, g_ref, d1_ref, d2_ref, dz_ref, dz16_ref, dg_ref, db_ref = refs
            dout = ALPHA * d1_ref[...] + d2_ref[...]
        else:
            z_ref, g_ref, d1_ref, dz_ref, dz16_ref, dg_ref, db_ref = refs
            dout = d1_ref[...]

        @pl.when(pl.program_id(0) == 0)
        def _():
            dg_ref[...] = jnp.zeros_like(dg_ref)
            db_ref[...] = jnp.zeros_like(db_ref)

        xh, rstd = _ln_stats(z_ref[...])
        dxh = dout * g_ref[...]
        dz = rstd * (dxh - jnp.mean(dxh, axis=-1, keepdims=True) - xh * jnp.mean(dxh * xh, axis=-1, keepdims=True))
        dz_ref[...] = dz
        dz16_ref[...] = dz.astype(BF16)
        dg_ref[...] += jnp.sum(dout * xh, axis=0, keepdims=True)
        db_ref[...] += jnp.sum(dout, axis=0, keepdims=True)

    row = pl.BlockSpec((t, d), lambda i: (i, 0))
    vec = pl.BlockSpec((1, d), lambda i: (0, 0))
    ins = [z, g.reshape(1, d), d1] + ([d2] if two else [])
    specs = [row, vec, row] + ([row] if two else [])
    dz, dz16, dg, db = pl.pallas_call(
        body, name=name, grid=(s // t,), in_specs=specs, out_specs=[row, row, vec, vec],
        out_shape=[jax.ShapeDtypeStruct((s, d), F32), jax.ShapeDtypeStruct((s, d), BF16),
                   jax.ShapeDtypeStruct((1, d), F32), jax.ShapeDtypeStruct((1, d), F32)],
        compiler_params=_params(("arbitrary",)))(*ins)
    return dz, dz16, dg.reshape(d), db.reshape(d)


def _axpy(d1, d2, *, name):
    s, d = d1.shape
    t = _tile(s, (256, 128, 64, 8))

    def body(a_ref, b_ref, o_ref):
        o_ref[...] = ALPHA * a_ref[...] + b_ref[...]

    row = pl.BlockSpec((t, d), lambda i: (i, 0))
    return pl.pallas_call(body, name=name, grid=(s // t,), in_specs=[row, row], out_specs=row,
                          out_shape=jax.ShapeDtypeStruct((s, d), F32), compiler_params=_params(("parallel",)))(d1, d2)


def _loss_head(y, tgt, *, name):
    s, d = y.shape
    t = _tile(s, (256, 128, 64, 8))

    def body(y_ref, t_ref, l_ref, dy_ref):
        @pl.when(pl.program_id(0) == 0)
        def _():
            l_ref[...] = jnp.zeros_like(l_ref)

        err = y_ref[...] - t_ref[...]
        dy_ref[...] = err * (1.0 / d)
        l_ref[...] += 0.5 * jnp.sum(jnp.sum(err * err, axis=1, keepdims=True), axis=0, keepdims=True) * (1.0 / d)

    row = pl.BlockSpec((t, d), lambda i: (i, 0))
    one = pl.BlockSpec((SUBLANES, LANES), lambda i: (0, 0))
    loss, dy = pl.pallas_call(
        body, name=name, grid=(s // t,), in_specs=[row, row], out_specs=[one, row],
        out_shape=[jax.ShapeDtypeStruct((SUBLANES, LANES), F32), jax.ShapeDtypeStruct((s, d), F32)],
        compiler_params=_params(("arbitrary",)))(y, tgt)
    return loss[0, 0], dy


def _halo_specs(s, t, halo, cb, col):
    per = t // halo
    last = s // halo - 1
    return [pl.BlockSpec((t, cb), lambda j, i: (i, col(j))),
            pl.BlockSpec((halo, cb), lambda j, i: (jnp.maximum(i * per - 1, 0), col(j))),
            pl.BlockSpec((halo, cb), lambda j, i: (jnp.minimum((i + 1) * per, last), col(j)))]


def _extended(main_ref, prev_ref, next_ref, i, n):
    prev = jnp.where(i > 0, prev_ref[...], 0.0)
    nxt = jnp.where(i < n - 1, next_ref[...], 0.0)
    return jnp.concatenate([prev, main_ref[...], nxt], axis=0)


def _shift(ext, o):
    if o == 0:
        return ext
    return pltpu.roll(ext, (-o) % ext.shape[0], 0)


def _taps(ext, k, sign=1):
    return [_shift(ext, sign * (j - k // 2)) for j in range(k)]


def _conv(ext, w_ref, k, sign=1, taps=None):
    taps = _taps(ext, k, sign) if taps is None else taps
    acc = None
    for j in range(k):
        term = w_ref[j:j + 1, :] * taps[j]
        acc = term if acc is None else acc + term
    return acc


def _main_taps(ext, k, halo, t, sign=1):
    rolled, taps = {}, []
    for j in range(k):
        offset = sign * (j - k // 2)
        res = offset % SUBLANES
        if res not in rolled:
            rolled[res] = _shift(ext, res)
        start = halo + offset - res
        taps.append(rolled[res][start:start + t])
    return taps


def _conv_main(taps, w_ref):
    acc = None
    for j, tap in enumerate(taps):
        term = w_ref[j:j + 1, :] * tap
        acc = term if acc is None else acc + term
    return acc


def _wgrad_main(dw_ref, d_main, taps):
    for j, tap in enumerate(taps):
        dw_ref[j:j + 1, :] += jnp.sum(d_main * tap, axis=0, keepdims=True)


def _conv_wgrad(dw_ref, d_main, x_ext, k, halo, t, taps=None):
    taps = _taps(x_ext, k) if taps is None else taps
    for j in range(k):
        dw_ref[j:j + 1, :] += jnp.sum(d_main * taps[j][halo:halo + t], axis=0, keepdims=True)


def _ffn_mid_fwd(hu, w, *, carry=None, name):
    s, f2 = hu.shape
    f = f2 // 2
    t, cb, halo = _tile(s, (FFN_ROW_TILE,)), _lane_tile(f, FFN_STRIP), SUBLANES
    nt, nc = s // t, f // cb
    ride = _Carried(carry, (1, nt))

    def body(*refs):
        (h_ref, hp_ref, hn_ref, w_ref), (a_ref,), _ = ride.split(refs, 4, 1)
        ride.start()
        i = pl.program_id(1)
        first, last = i == 0, i == nt - 1

        def extended(cols):
            return jnp.concatenate([jnp.where(first, 0.0, hp_ref[:, cols]), h_ref[:, cols],
                                    jnp.where(last, 0.0, hn_ref[:, cols])], axis=0)

        for c in range(nc):
            gcols, ucols = pl.ds(c * cb, cb), pl.ds(f + c * cb, cb)
            hg = _conv_main(_main_taps(extended(gcols), FFN_K, halo, t), w_ref.at[:, gcols])
            hu_ = _conv_main(_main_taps(extended(ucols), FFN_K, halo, t), w_ref.at[:, ucols])
            a_ref[:, gcols] = (hg * _sigmoid(hg) * hu_).astype(BF16)
        ride.wait()

    specs = _halo_specs(s, t, halo, f2, lambda j: 0) + [pl.BlockSpec((FFN_K, f2), lambda j, i: (0, 0))]
    out = pl.pallas_call(
        body, name=name, grid=(1, nt), in_specs=specs + ride.specs,
        out_specs=[pl.BlockSpec((t, f), lambda j, i: (i, 0))] + ride.specs,
        out_shape=[jax.ShapeDtypeStruct((s, f), BF16)] + ride.out_shapes, scratch_shapes=ride.scratch,
        compiler_params=_params(("arbitrary", "arbitrary") if ride.on else ("parallel", "parallel"), VMEM_BIG))(
            hu, hu, hu, w, *ride.operands)
    return out[0], (out[1] if ride.on else None)


def _ffn_mid_bwd(hu, w, da, *, carry=None, name):
    s, f2 = hu.shape
    f = f2 // 2
    t, cb, halo = _tile(s, (FFN_ROW_TILE,)), _lane_tile(f, FFN_STRIP), SUBLANES
    nt, nc = s // t, f // cb
    ride = _Carried(carry, (1, nt))

    def body(*refs):
        (h_ref, hp_ref, hn_ref, a_ref, ap_ref, an_ref, w_ref), (dh_ref, dw_ref), _ = ride.split(refs, 7, 2)
        ride.start()
        i = pl.program_id(1)

        @pl.when(i == 0)
        def _():
            dw_ref[...] = jnp.zeros_like(dw_ref)

        first, last = i == 0, i == nt - 1

        def extended(main, prev, nxt, cols):
            return jnp.concatenate([jnp.where(first, 0.0, prev[:, cols]), main[:, cols],
                                    jnp.where(last, 0.0, nxt[:, cols])], axis=0)

        for c in range(nc):
            gcols, ucols, acols = pl.ds(c * cb, cb), pl.ds(f + c * cb, cb), pl.ds(c * cb, cb)
            xg = extended(h_ref, hp_ref, hn_ref, gcols)
            xu = extended(h_ref, hp_ref, hn_ref, ucols)
            dae = extended(a_ref, ap_ref, an_ref, acols)
            wg, wu = w_ref.at[:, gcols], w_ref.at[:, ucols]
            xg_taps, xu_taps = _taps(xg, FFN_K), _taps(xu, FFN_K)
            hg = _conv(xg, wg, FFN_K, taps=xg_taps)
            hu_ = _conv(xu, wu, FFN_K, taps=xu_taps)
            sg = _sigmoid(hg)
            d_hg = dae * hu_ * (sg * (1.0 + hg * (1.0 - sg)))
            d_hu = dae * (hg * sg)
            dh_ref[:, gcols] = _conv(d_hg, wg, FFN_K, sign=-1)[halo:halo + t].astype(BF16)
            dh_ref[:, ucols] = _conv(d_hu, wu, FFN_K, sign=-1)[halo:halo + t].astype(BF16)
            _conv_wgrad(dw_ref.at[:, gcols], d_hg[halo:halo + t], xg, FFN_K, halo, t, taps=xg_taps)
            _conv_wgrad(dw_ref.at[:, ucols], d_hu[halo:halo + t], xu, FFN_K, halo, t, taps=xu_taps)
        ride.wait()

    whole = lambda j: 0
    specs = (_halo_specs(s, t, halo, f2, whole) + _halo_specs(s, t, halo, f, whole)
             + [pl.BlockSpec((FFN_K, f2), lambda j, i: (0, 0))])
    out = pl.pallas_call(
        body, name=name, grid=(1, nt), in_specs=specs + ride.specs,
        out_specs=[pl.BlockSpec((t, f2), lambda j, i: (i, 0)), pl.BlockSpec((FFN_K, f2), lambda j, i: (0, 0))]
        + ride.specs,
        out_shape=[jax.ShapeDtypeStruct((s, f2), BF16), jax.ShapeDtypeStruct((FFN_K, f2), F32)] + ride.out_shapes,
        scratch_shapes=ride.scratch,
        compiler_params=_params(("arbitrary", "arbitrary"), VMEM_BIG))(hu, hu, hu, da, da, da, w, *ride.operands)
    return out[0], out[1], (out[2] if ride.on else None)


CONV_HALO = 16


def _conv_mid_fwd(h, dw_w, dw_b, ln_g, ln_b, sc_w, *, name):
    s = h.shape[0]
    c = CONF_CH
    t, halo = _tile(s, (256, 128)), CONV_HALO
    nt = s // t

    def body(ca, cap, can, cg, cgp, cgn, gb, gc, gcp, gcn, hx, hxp, hxn, w31, b31, lg, lb, w3, o_ref, u2_ref):
        i = pl.program_id(1)
        first, last = i == 0, i == nt - 1

        def extended(main, prev, nxt, cols):
            return jnp.concatenate([jnp.where(first, 0.0, prev[:, cols]), main[:, cols],
                                    jnp.where(last, 0.0, nxt[:, cols])], axis=0)

        for strip in range(c // LANES):
            cols = pl.ds(strip * LANES, LANES)
            u1 = extended(ca, cap, can, cols) * _sigmoid(extended(cg, cgp, cgn, cols))
            u2_ref[:, cols] = _conv_main(_main_taps(u1, CONF_K, halo, t), w31.at[:, cols]) + b31[:, cols]
            p = extended(gc, gcp, gcn, cols) * extended(hx, hxp, hxn, cols)
            conv = _conv_main(_main_taps(p, SC_K, halo, t), w3.at[:, cols])
            o_ref[:, pl.ds(c + strip * LANES, LANES)] = (gb[:, cols] * conv).astype(BF16)
        xh, _ = _ln_stats(u2_ref[...])
        yl = xh * lg[...] + lb[...]
        o_ref[:, 0:c] = (yl * _sigmoid(yl)).astype(BF16)

    hs = lambda blk: _halo_specs(s, t, halo, c, lambda j: blk)
    vec = lambda r: pl.BlockSpec((r, c), lambda j, i: (0, 0))
    specs = hs(0) + hs(1) + hs(2)[:1] + hs(3) + hs(4) + [vec(CONF_K), vec(1), vec(1), vec(1), vec(SC_K)]
    return pl.pallas_call(
        body, name=name, grid=(1, nt), in_specs=specs,
        out_specs=[pl.BlockSpec((t, 2 * c), lambda j, i: (i, 0)), pl.BlockSpec((t, c), lambda j, i: (i, 0))],
        out_shape=[jax.ShapeDtypeStruct((s, 2 * c), BF16), jax.ShapeDtypeStruct((s, c), F32)],
        compiler_params=_params(("parallel", "parallel")))(
            h, h, h, h, h, h, h, h, h, h, h, h, h, dw_w, dw_b.reshape(1, c), ln_g.reshape(1, c),
            ln_b.reshape(1, c), sc_w)


def _conv_mid_bwd(h, u2, dm, dw_w, ln_g, ln_b, sc_w, *, name):
    s = h.shape[0]
    c = CONF_CH
    t, halo = _tile(s, (256, 128)), CONV_HALO
    nt = s // t

    def body(ca, cap, can, cg, cgp, cgn, gb, gbp, gbn, gc, gcp, gcn, hx, hxp, hxn, u2r, u2p, u2n,
             du, dup, dun, dz, dzp, dzn, w31, lg, lb, w3,
             dh_ref, dw31_ref, db31_ref, dlg_ref, dlb_ref, dw3_ref, u1_s, du2_s, p_s, dc3_s):
        i = pl.program_id(1)

        @pl.when(i == 0)
        def _():
            for r in (dw31_ref, db31_ref, dlg_ref, dlb_ref, dw3_ref):
                r[...] = jnp.zeros_like(r)

        main = slice(halo, halo + t)
        xh, rstd = _ln_stats(_extended(u2r, u2p, u2n, i, nt))
        yl = xh * lg[...] + lb[...]
        sg = _sigmoid(yl)
        d_yl = _extended(du, dup, dun, i, nt) * (sg * (1.0 + yl * (1.0 - sg)))
        dlg_ref[...] += jnp.sum((d_yl * xh)[main], axis=0, keepdims=True)
        dlb_ref[...] += jnp.sum(d_yl[main], axis=0, keepdims=True)
        dxh = d_yl * lg[...]
        du2 = rstd * (dxh - jnp.mean(dxh, axis=-1, keepdims=True) - xh * jnp.mean(dxh * xh, axis=-1, keepdims=True))
        db31_ref[...] += jnp.sum(du2[main], axis=0, keepdims=True)
        du2_s[...] = du2
        u1_s[...] = _extended(ca, cap, can, i, nt) * _sigmoid(_extended(cg, cgp, cgn, i, nt))
        p_s[...] = _extended(gc, gcp, gcn, i, nt) * _extended(hx, hxp, hxn, i, nt)
        dc3_s[...] = _extended(dz, dzp, dzn, i, nt) * _extended(gb, gbp, gbn, i, nt)
        for strip in range(c // LANES):
            cols = pl.ds(strip * LANES, LANES)
            at = lambda part: pl.ds(part * c + strip * LANES, LANES)
            du2_c = du2_s[:, cols]
            _wgrad_main(dw31_ref.at[:, cols], du2_c[main], _main_taps(u1_s[:, cols], CONF_K, halo, t))
            du1 = _conv_main(_main_taps(du2_c, CONF_K, halo, t, sign=-1), w31.at[:, cols])
            sc = _sigmoid(cg[:, cols])
            dh_ref[:, at(0)] = (du1 * sc).astype(BF16)
            dh_ref[:, at(1)] = (du1 * (ca[:, cols] * sc * (1.0 - sc))).astype(BF16)
            dc3_c = dc3_s[:, cols]
            p_taps = _main_taps(p_s[:, cols], SC_K, halo, t)
            dh_ref[:, at(2)] = (dz[:, cols] * _conv_main(p_taps, w3.at[:, cols])).astype(BF16)
            _wgrad_main(dw3_ref.at[:, cols], dc3_c[main], p_taps)
            dp = _conv_main(_main_taps(dc3_c, SC_K, halo, t, sign=-1), w3.at[:, cols])
            dh_ref[:, at(3)] = (dp * hx[:, cols]).astype(BF16)
            dh_ref[:, at(4)] = (dp * gc[:, cols]).astype(BF16)

    hs = lambda blk: _halo_specs(s, t, halo, c, lambda j: blk)
    vec = lambda r: pl.BlockSpec((r, c), lambda j, i: (0, 0))
    specs = (hs(0) + hs(1) + hs(2) + hs(3) + hs(4) + hs(0) + hs(0) + hs(1)
             + [vec(CONF_K), vec(1), vec(1), vec(SC_K)])
    ext = pltpu.VMEM((t + 2 * halo, c), F32)
    outs = pl.pallas_call(
        body, name=name, grid=(1, nt), in_specs=specs,
        out_specs=[pl.BlockSpec((t, 5 * c), lambda j, i: (i, 0)), vec(CONF_K), vec(1), vec(1), vec(1), vec(SC_K)],
        out_shape=[jax.ShapeDtypeStruct((s, 5 * c), BF16), jax.ShapeDtypeStruct((CONF_K, c), F32),
                   jax.ShapeDtypeStruct((1, c), F32), jax.ShapeDtypeStruct((1, c), F32),
                   jax.ShapeDtypeStruct((1, c), F32), jax.ShapeDtypeStruct((SC_K, c), F32)],
        scratch_shapes=[ext, ext, ext, ext],
        compiler_params=_params(("arbitrary", "arbitrary")))(
            h, h, h, h, h, h, h, h, h, h, h, h, h, h, h, u2, u2, u2, dm, dm, dm, dm, dm, dm,
            dw_w, ln_g.reshape(1, c), ln_b.reshape(1, c), sc_w)
    dh, dw31, db31, dlg, dlb, dw3 = outs
    return dh, dw31, db31.reshape(c), dlg.reshape(c), dlb.reshape(c), dw3


def _attn_mask(kind, n, tq, kw, length):
    pad = (kw - tq) // 2
    iq = lax.broadcasted_iota(jnp.int32, (tq, 1), 0)
    ik = lax.broadcasted_iota(jnp.int32, (1, kw), 1)
    if kind == "band":
        rel = ik - pad - iq
        kpos = n * tq - pad + ik
        return (jnp.abs(rel) <= DIL_HALF) & (kpos >= 0) & (kpos < length)
    rows = length // GRID_W
    rq = n * NA_QROWS + (iq >> GRID_SHIFT)
    cq = iq & (GRID_W - 1)
    rk = n * NA_QROWS - pad // GRID_W + (ik >> GRID_SHIFT)
    ck = ik & (GRID_W - 1)
    r0 = jnp.clip(rq - NA_KH // 2, 0, rows - NA_KH)
    c0 = jnp.clip(cq - NA_KW // 2, 0, GRID_W - NA_KW)
    return (rk >= r0) & (rk < r0 + NA_KH) & (ck >= c0) & (ck < c0 + NA_KW)


def _mask_tiles(kind, tq, kw, length, *, name):
    nb = length // tq

    def body(o_ref):
        v = pl.program_id(0)
        n = jnp.where((v == 1) | (v == 3), 0, jnp.where(v == 2, nb - 1, 1))
        o_ref[0] = jnp.where(_attn_mask(kind, n, tq, kw, length), 0.0, NEG)

    return pl.pallas_call(body, name=name, grid=(4,), out_specs=pl.BlockSpec((1, tq, kw), lambda v: (v, 0, 0)),
                          out_shape=jax.ShapeDtypeStruct((4, tq, kw), F32),
                          compiler_params=_params(("parallel",)))()


class _AttnGeom:
    def __init__(self, s, d, tq, nsub, kw):
        self.s, self.d, self.tq, self.nsub = s, d, tq, nsub
        self.halo = (kw - tq) // 2 * d
        self.rows = nsub * tq * d
        self.nbig = s // self.rows
        self.ext = self.rows + 2 * self.halo
        assert s % self.rows == 0 and self.rows % self.halo == 0

    def main(self, col):
        return pl.BlockSpec((self.rows, LANES), lambda hp, n: (n, col + hp))

    def with_halos(self, col):
        per, last = self.rows // self.halo, self.s // self.halo - 1
        return [self.main(col),
                pl.BlockSpec((self.halo, LANES), lambda hp, n: (jnp.maximum(n * per - 1, 0), col + hp)),
                pl.BlockSpec((self.halo, LANES), lambda hp, n: (jnp.minimum((n + 1) * per, last), col + hp))]

    def keys_of(self, main_ref, prev_ref, next_ref, r, sub, kw):
        pad = self.halo // self.d
        start, stop, size = sub * self.tq - pad, sub * self.tq - pad + kw, self.nsub * self.tq
        parts = []
        if start < 0:
            parts.append(prev_ref[self.rows_of(r, 0, pad), :])
        lo, hi = max(start, 0), min(stop, size)
        parts.append(main_ref[self.rows_of(r, lo, hi - lo), :])
        if stop > size:
            parts.append(next_ref[self.rows_of(r, 0, pad), :])
        return parts[0] if len(parts) == 1 else jnp.concatenate(parts, axis=0)

    def rows_of(self, r, pos, count):
        start = r + pos * self.d
        return pl.ds(start, count, stride=self.d) if self.d > 1 else pl.ds(start, count)

    def variant(self, n, sub):
        v = 0
        if sub == 0:
            v = v + jnp.where(n == 0, 1, 0)
        if sub == self.nsub - 1:
            v = v + jnp.where(n == self.nbig - 1, 2, 0)
        return v


def _stack_heads(x, low):
    return jnp.concatenate([jnp.where(low, x, 0.0), jnp.where(low, 0.0, x)], axis=0)


def _attn_fwd(h, bias, mask, *, d, tq, nsub, cols, carry=None, name):
    kw = bias.shape[2]
    geo = _AttnGeom(h.shape[0], d, tq, nsub, kw)
    scale = HEAD_DIM ** -0.5

    ride = _Carried(carry, (4, geo.nbig))

    def body(*refs):
        (q_ref, km, kp, kn, vm, vp, vn, b_ref, m_ref), (o_ref, l_ref), _ = ride.split(refs, 9, 2)
        ride.start()
        n = pl.program_id(1)
        low = lax.broadcasted_iota(jnp.int32, (1, LANES), 1) < HEAD_DIM
        for r in range(d):
            for sub in range(nsub):
                madd = m_ref[geo.variant(n, sub)]
                q = q_ref[geo.rows_of(r, sub * tq, tq), :].astype(F32) * scale
                ks = geo.keys_of(km, kp, kn, r, sub, kw).astype(BF16)
                vs = geo.keys_of(vm, vp, vn, r, sub, kw).astype(BF16)
                q2 = _stack_heads(q, low).astype(BF16)
                sc = (lax.dot_general(q2, ks, (((1,), (1,)), ((), ())), preferred_element_type=F32)
                      + (b_ref[...] + madd).reshape(2 * tq, kw))
                m = jnp.max(sc, axis=1, keepdims=True)
                p = jnp.exp(sc - m)
                den = jnp.sum(p, axis=1, keepdims=True)
                out2 = jnp.dot((p / den).astype(BF16), vs, preferred_element_type=F32)
                lse2 = m + jnp.log(den)
                o_ref[geo.rows_of(r, sub * tq, tq), :] = jnp.where(low, out2[0:tq], out2[tq:2 * tq])
                l_ref[geo.rows_of(r, sub * tq, tq), :] = jnp.where(low, lse2[0:tq], lse2[tq:2 * tq])
        ride.wait()

    qc, kc, vc = cols
    specs = ([geo.main(qc)] + geo.with_halos(kc) + geo.with_halos(vc)
             + [pl.BlockSpec((2, tq, kw), lambda hp, n: (hp, 0, 0)),
                pl.BlockSpec((4, tq, kw), lambda hp, n: (0, 0, 0))])
    shape = jax.ShapeDtypeStruct((geo.s, 4 * LANES), F32)
    out = pl.pallas_call(
        body, name=name, grid=(4, geo.nbig), in_specs=specs + ride.specs,
        out_specs=[geo.main(0), geo.main(0)] + ride.specs, out_shape=[shape, shape] + ride.out_shapes,
        scratch_shapes=ride.scratch,
        compiler_params=_params(("arbitrary", "arbitrary") if ride.on else ("parallel", "parallel"), VMEM_BIG))(
            h, h, h, h, h, h, h, bias, mask, *ride.operands)
    return out[0], out[1], (out[2] if ride.on else None)


def _attn_bwd(h, bias, mask, dy, y, lse, *, d, tq, nsub, cols, ycol, name):
    kw = bias.shape[2]
    geo = _AttnGeom(h.shape[0], d, tq, nsub, kw)
    scale = HEAD_DIM ** -0.5
    halo, rows = geo.halo, geo.rows
    align = math.gcd(halo, LANES)

    def body(q_ref, km, kp, kn, vm, vp, vn, b_ref, m_ref, dy_ref, y_ref, l_ref, dq_ref, dk_hbm, dv_hbm, db_ref,
             dkext, dvext, dk_all, dv_all, sems):
        hp, n = pl.program_id(0), pl.program_id(1)

        @pl.when(n == 0)
        def _():
            dk_all[...] = jnp.zeros_like(dk_all)
            dv_all[...] = jnp.zeros_like(dv_all)
            db_ref[...] = jnp.zeros_like(db_ref)

        dkext[...] = jnp.zeros_like(dkext)
        dvext[...] = jnp.zeros_like(dvext)
        low = lax.broadcasted_iota(jnp.int32, (1, LANES), 1) < HEAD_DIM
        for r in range(d):
            for sub in range(nsub):
                madd = m_ref[geo.variant(n, sub)]
                mine = geo.rows_of(r, sub * tq, tq)
                keys = geo.rows_of(r, sub * tq, kw)
                q = q_ref[mine, :].astype(F32) * scale
                ks = geo.keys_of(km, kp, kn, r, sub, kw).astype(BF16)
                vs = geo.keys_of(vm, vp, vn, r, sub, kw).astype(BF16)
                dyv = dy_ref[mine, :]
                dyy = dyv * y_ref[mine, :]
                lse_all = l_ref[mine, :]
                q2 = _stack_heads(q, low).astype(BF16)
                dy2 = _stack_heads(dyv, low).astype(BF16)
                dsum = jnp.concatenate([jnp.sum(jnp.where(sel, dyy, 0.0), axis=1, keepdims=True)
                                        for sel in (low, ~low)], axis=0)
                lse2 = jnp.concatenate([lse_all[:, 0:1], lse_all[:, HEAD_DIM:HEAD_DIM + 1]], axis=0)
                sc = (lax.dot_general(q2, ks, (((1,), (1,)), ((), ())), preferred_element_type=F32)
                      + (b_ref[...] + madd).reshape(2 * tq, kw))
                p = jnp.exp(sc - lse2)
                dp = lax.dot_general(dy2, vs, (((1,), (1,)), ((), ())), preferred_element_type=F32)
                ds = p * (dp - dsum)
                db_ref[...] += ds.reshape(2, tq, kw)
                pb, dsb = p.astype(BF16), ds.astype(BF16)
                dv = lax.dot_general(pb, dy2, (((0,), (0,)), ((), ())), preferred_element_type=F32)
                dk = lax.dot_general(dsb, q2, (((0,), (0,)), ((), ())), preferred_element_type=F32)
                dq2 = jnp.dot(dsb, ks, preferred_element_type=F32)
                dq_ref[mine, :] = jnp.where(low, dq2[0:tq], dq2[tq:2 * tq]) * scale
                dkext[keys, :] += dk
                dvext[keys, :] += dv

        before = pl.multiple_of(jnp.maximum(n * rows - halo, 0), align)
        here = pl.multiple_of(n * rows, align)
        after = pl.multiple_of(jnp.minimum((n + 1) * rows, geo.s - halo), align)
        for ext, total in ((dkext, dk_all), (dvext, dv_all)):
            total[pl.ds(before, halo), :] += ext[0:halo]
            total[pl.ds(here, rows), :] += ext[halo:halo + rows]
            total[pl.ds(after, halo), :] += ext[halo + rows:geo.ext]

        @pl.when(n == geo.nbig - 1)
        def _():
            col = pl.ds(pl.multiple_of(hp * LANES, LANES), LANES)
            copies = [pltpu.make_async_copy(total, out.at[:, col], sems.at[i])
                      for i, (total, out) in enumerate(((dk_all, dk_hbm), (dv_all, dv_hbm)))]
            for cp in copies:
                cp.start()
            for cp in copies:
                cp.wait()

    qc, kc, vc = cols
    bspec = pl.BlockSpec((2, tq, kw), lambda hp, n: (hp, 0, 0))
    any_spec = pl.BlockSpec(memory_space=pl.ANY)
    specs = ([geo.main(qc)] + geo.with_halos(kc) + geo.with_halos(vc)
             + [bspec, pl.BlockSpec((4, tq, kw), lambda hp, n: (0, 0, 0)), geo.main(ycol), geo.main(ycol), geo.main(0)])
    shape = jax.ShapeDtypeStruct((geo.s, 4 * LANES), F32)
    ext = pltpu.VMEM((geo.ext, LANES), F32)
    whole = pltpu.VMEM((geo.s, LANES), F32)
    return pl.pallas_call(
        body, name=name, grid=(4, geo.nbig), in_specs=specs, out_specs=[geo.main(0), any_spec, any_spec, bspec],
        out_shape=[shape, shape, shape, jax.ShapeDtypeStruct(bias.shape, F32)],
        scratch_shapes=[ext, ext, whole, whole, pltpu.SemaphoreType.DMA((2,))],
        compiler_params=_params(("arbitrary", "arbitrary"), VMEM_BIG))(
            h, h, h, h, h, h, h, bias, mask, dy, y, lse)


def _dil_combine(o_na, outs, lses, *, name):
    s, c = outs[0].shape
    t = _tile(s, (512, 256, 128, 64, 8))

    def body(na, o0, o1, o2, l0, l1, l2, y_ref, y16_ref, lt_ref):
        ls = [l0[...], l1[...], l2[...]]
        m = jnp.maximum(jnp.maximum(ls[0], ls[1]), ls[2])
        es = [jnp.exp(l - m) for l in ls]
        den = es[0] + es[1] + es[2]
        y = (es[0] / den) * o0[...] + (es[1] / den) * o1[...] + (es[2] / den) * o2[...]
        lt_ref[...] = m + jnp.log(den)
        y_ref[:, 0:c] = na[...]
        y_ref[:, c:2 * c] = y
        y16_ref[:, 0:c] = na[...].astype(BF16)
        y16_ref[:, c:2 * c] = y.astype(BF16)

    row = pl.BlockSpec((t, c), lambda i: (i, 0))
    wide = pl.BlockSpec((t, 2 * c), lambda i: (i, 0))
    return pl.pallas_call(body, name=name, grid=(s // t,), in_specs=[row] * 7, out_specs=[wide, wide, row],
                          out_shape=[jax.ShapeDtypeStruct((s, 2 * c), F32), jax.ShapeDtypeStruct((s, 2 * c), BF16),
                                     jax.ShapeDtypeStruct((s, c), F32)],
                          compiler_params=_params(("parallel",)))(o_na, *outs, *lses)


def _attn_dh(na, dil, *, name):
    s, c = na[0].shape
    t = _tile(s, (256, 128, 64, 8))

    def body(*refs):
        ins, o_ref = refs[:-1], refs[-1]
        for a in range(3):
            o_ref[:, a * c:(a + 1) * c] = ins[a][...].astype(BF16)
            o_ref[:, (3 + a) * c:(4 + a) * c] = (ins[3 + a][...] + ins[6 + a][...] + ins[9 + a][...]).astype(BF16)

    row = pl.BlockSpec((t, c), lambda i: (i, 0))
    flat = list(na) + [g[a] for g in dil for a in range(3)]
    return pl.pallas_call(body, name=name, grid=(s // t,), in_specs=[row] * 12,
                          out_specs=pl.BlockSpec((t, 6 * c), lambda i: (i, 0)),
                          out_shape=jax.ShapeDtypeStruct((s, 6 * c), BF16),
                          compiler_params=_params(("parallel",)))(*flat)


def _t5_bucket(rel):
    nb = N_BUCKETS // 2
    max_exact = nb // 2
    ret = np.where(rel > 0, nb, 0)
    n = np.abs(rel)
    large = max_exact + (np.log(np.maximum(n, 1).astype(np.float32) / np.float32(max_exact))
                         / np.float32(math.log(T5_MAX_DIST / max_exact)) * np.float32(nb - max_exact)).astype(np.int32)
    large = np.minimum(large, nb - 1)
    return (ret + np.where(n < max_exact, n, large)).astype(np.int32)


def _band_bucket_index(dil):
    tq, kw = DIL_TQ, DIL_KW
    rel = np.arange(kw)[None, :] - (kw - tq) // 2 - np.arange(tq)[:, None]
    return _t5_bucket(rel * dil)


def _band_bias(t5, dil, *, name):
    tq, kw = DIL_TQ, DIL_KW
    buckets = [int(b) for b in _t5_bucket(np.arange(-DIL_HALF, DIL_HALF + 1) * dil)]

    def body(t_ref, o_ref):
        hh = pl.program_id(0)
        rel = (lax.broadcasted_iota(jnp.int32, (tq, kw), 1) - (kw - tq) // 2
               - lax.broadcasted_iota(jnp.int32, (tq, kw), 0))
        acc = jnp.zeros((tq, kw), F32)
        for r, b in zip(range(-DIL_HALF, DIL_HALF + 1), buckets):
            acc = jnp.where(rel == r, t_ref[b * 8 + hh], acc)
        o_ref[0] = acc

    return pl.pallas_call(body, name=name, grid=(8,),
                          in_specs=[pl.BlockSpec(memory_space=pltpu.SMEM)],
                          out_specs=pl.BlockSpec((1, tq, kw), lambda h: (h, 0, 0)),
                          out_shape=jax.ShapeDtypeStruct((8, tq, kw), F32),
                          compiler_params=_params(("parallel",)))(t5.reshape(-1))


def _na_bias(rpb, *, name):
    nr, nc = 2 * NA_KH - 1, 2 * NA_KW - 1
    tq = NA_QROWS * GRID_W
    w = GRID_W

    def body(r_ref, o_ref):
        base = pl.program_id(0) * (nr * nc)
        lane = lax.broadcasted_iota(jnp.int32, (w, LANES), 1)
        upper = lane >= w
        diff = (lane & (w - 1)) - lax.broadcasted_iota(jnp.int32, (w, LANES), 0) + NA_KW - 1
        tiles = {}
        for i in range(NA_QROWS):
            for m in range(3 * NA_QROWS // 2):
                lo = 2 * m - i + NA_KH - 1 - NA_QROWS
                if lo not in tiles:
                    acc = jnp.zeros((w, LANES), F32)
                    for dc in range(nc):
                        v_lo = r_ref[base + lo * nc + dc] if 0 <= lo < nr else 0.0
                        v_hi = r_ref[base + (lo + 1) * nc + dc] if 0 <= lo + 1 < nr else 0.0
                        acc = jnp.where(diff == dc, jnp.where(upper, v_hi, v_lo), acc)
                    tiles[lo] = acc
                o_ref[0, i * w:(i + 1) * w, m * LANES:(m + 1) * LANES] = tiles[lo]

    return pl.pallas_call(body, name=name, grid=(8,),
                          in_specs=[pl.BlockSpec(memory_space=pltpu.SMEM)],
                          out_specs=pl.BlockSpec((1, tq, 3 * tq), lambda h: (h, 0, 0)),
                          out_shape=jax.ShapeDtypeStruct((8, tq, 3 * tq), F32),
                          compiler_params=_params(("parallel",)))(rpb.reshape(-1))


def _t5_grad(dbs, idxs, *, name):
    def body(d0, d1, d2, i0, i1, i2, o_ref):
        lane = lax.broadcasted_iota(jnp.int32, (1, LANES), 1)
        lines = [jnp.zeros((1, LANES), F32) for _ in range(8)]
        for dref, iref in ((d0, i0), (d1, i1), (d2, i2)):
            idx = iref[...]
            for hh in range(8):
                xh = dref[hh]
                for b in range(N_BUCKETS):
                    val = jnp.sum(jnp.sum(jnp.where(idx == b, xh, 0.0), axis=1, keepdims=True), axis=0, keepdims=True)
                    lines[hh] = lines[hh] + jnp.where(lane == b, val, 0.0)
        for hh in range(8):
            o_ref[hh:hh + 1, :] = lines[hh]

    out = pl.pallas_call(body, name=name, out_shape=jax.ShapeDtypeStruct((8, LANES), F32))(*dbs, *idxs)
    return out[:, :N_BUCKETS].T


def _rpb_grad(db, *, name):
    nr, nc = 2 * NA_KH - 1, 2 * NA_KW - 1
    tq = NA_QROWS * GRID_W
    w = GRID_W

    def body(d_ref, o_ref):
        x = d_ref[0]
        rows = []
        for dr in range(nr):
            acc = jnp.zeros((w, w), F32)
            for i in range(NA_QROWS):
                j = i + dr - (NA_KH - 1 - NA_QROWS)
                if 0 <= j < 3 * NA_QROWS:
                    acc = acc + x[i * w:(i + 1) * w, j * w:(j + 1) * w]
            rows.append(acc)
        diff = (lax.broadcasted_iota(jnp.int32, (w, w), 1) - lax.broadcasted_iota(jnp.int32, (w, w), 0)
                + NA_KW - 1)
        lane = lax.broadcasted_iota(jnp.int32, (1, LANES), 1)
        for dr in range(nr):
            line = jnp.zeros((1, LANES), F32)
            for dc in range(nc):
                val = jnp.sum(jnp.sum(jnp.where(diff == dc, rows[dr], 0.0), axis=1, keepdims=True),
                              axis=0, keepdims=True)
                line = jnp.where(lane == dc, val, line)
            o_ref[0, dr:dr + 1, :] = line

    out = pl.pallas_call(body, name=name, grid=(8,),
                         in_specs=[pl.BlockSpec((1, tq, 3 * tq), lambda h: (h, 0, 0))],
                         out_specs=pl.BlockSpec((1, nr, LANES), lambda h: (h, 0, 0)),
                         out_shape=jax.ShapeDtypeStruct((8, nr, LANES), F32),
                         compiler_params=_params(("parallel",)))(db)
    return out[:, :, :nc]


def _exchange(src, *, gather, name):
    shape = src.shape if not gather else (N_DEV,) + src.shape

    def body(src_ref, out_ref, send_sems, recv_sems, local_sem):
        _exchange_start(src_ref, out_ref, send_sems, recv_sems, local_sem, gather)
        _exchange_wait(src_ref, out_ref, send_sems, recv_sems, local_sem, gather)

    any_spec = pl.BlockSpec(memory_space=pl.ANY)
    return pl.pallas_call(
        body, name=name, in_specs=[any_spec], out_specs=any_spec, out_shape=jax.ShapeDtypeStruct(shape, src.dtype),
        scratch_shapes=_exchange_sems())(src)


class _Carried:
    def __init__(self, carry, grid):
        self.src, self.gather = carry if carry is not None else (None, False)
        self.on = self.src is not None
        self.grid = grid
        self.operands = [self.src] if self.on else []
        self.specs = [pl.BlockSpec(memory_space=pl.ANY)] if self.on else []
        self.scratch = _exchange_sems() if self.on else []
        self.out_shapes = []
        if self.on:
            shape = ((N_DEV,) + self.src.shape) if self.gather else self.src.shape
            self.out_shapes = [jax.ShapeDtypeStruct(shape, self.src.dtype)]

    def split(self, refs, n_in, n_out):
        refs = list(refs)
        if not self.on:
            return refs[:n_in], refs[n_in:n_in + n_out], refs[n_in + n_out:]
        self.refs = (refs[n_in], refs[n_in + 1 + n_out], *refs[-3:], self.gather)
        return refs[:n_in], refs[n_in + 1:n_in + 1 + n_out], refs[n_in + 2 + n_out:-3]

    def _at(self, last):
        hit = None
        for ax, size in enumerate(self.grid):
            here = pl.program_id(ax) == (size - 1 if last else 0)
            hit = here if hit is None else hit & here
        return hit

    def start(self):
        if self.on:
            pl.when(self._at(False))(lambda: _exchange_start(*self.refs))

    def wait(self):
        if self.on:
            pl.when(self._at(True))(lambda: _exchange_wait(*self.refs))


def _exchange_sems():
    return [pltpu.SemaphoreType.DMA((N_DEV - 1,)), pltpu.SemaphoreType.DMA((N_DEV - 1,)), pltpu.SemaphoreType.DMA]


def _exchange_copies(src_ref, out_ref, send_sems, recv_sems, local_sem, gather):
    x, y, c = lax.axis_index("x"), lax.axis_index("y"), lax.axis_index("c")
    me = 4 * x + 2 * y + c

    def outgoing(p):
        return src_ref if gather else src_ref.at[p]

    own = pltpu.make_async_copy(outgoing(me), out_ref.at[me], local_sem)
    sends, recvs = [], []
    for k in range(1, N_DEV):
        px = 1 - x if k & 4 else x
        py = 1 - y if k & 2 else y
        pc = 1 - c if k & 1 else c
        p = 4 * px + 2 * py + pc
        for dst, group in ((me, sends), (p, recvs)):
            group.append(pltpu.make_async_remote_copy(
                src_ref=outgoing(p), dst_ref=out_ref.at[dst], send_sem=send_sems.at[k - 1],
                recv_sem=recv_sems.at[k - 1], device_id=(px, py, pc), device_id_type=pl.DeviceIdType.MESH))
    return own, sends, recvs


def _exchange_start(*refs_and_mode):
    own, sends, _ = _exchange_copies(*refs_and_mode)
    own.start()
    for cp in sends:
        cp.start()


def _exchange_wait(*refs_and_mode):
    own, sends, recvs = _exchange_copies(*refs_and_mode)
    for cp in recvs:
        cp.wait_recv()
    for cp in sends:
        cp.wait_send()
    own.wait()


def _adamw(parts, w, m, v, *, name):
    layers = len(parts)
    rows, cols = w.shape
    per_layer = rows // layers
    t = _tile(per_layer, (FLAT_ROW_TILE, 128, 64, 32, 16, 8))
    nt = per_layer // t

    def body(*refs):
        p_refs = refs[:layers]
        w_ref, m_ref, v_ref, g_ref, d_ref, nm_ref, nv_ref = refs[layers:]
        layer = pl.program_id(0)
        g = None
        for l, p_ref in enumerate(p_refs):
            total = p_ref[0].astype(F32)
            for k in range(1, N_DEV):
                total = total + p_ref[k].astype(F32)
            g = total if g is None else jnp.where(layer == l, total, g)
        nm = ADAM_B1 * m_ref[...] + (1.0 - ADAM_B1) * g
        nv = ADAM_B2 * v_ref[...] + (1.0 - ADAM_B2) * (g * g)
        m_hat = nm / (1.0 - ADAM_B1 ** ADAM_STEP)
        v_hat = nv / (1.0 - ADAM_B2 ** ADAM_STEP)
        g_ref[...] = g
        d_ref[...] = -ADAM_LR * (m_hat / (jnp.sqrt(v_hat) + ADAM_EPS) + ADAM_WD * w_ref[...])
        nm_ref[...] = nm
        nv_ref[...] = nv

    def part_spec(l):
        return pl.BlockSpec((N_DEV, t, cols), lambda layer, i: (0, jnp.where(layer == l, i, 0), 0))

    row = pl.BlockSpec((t, cols), lambda layer, i: (layer * nt + i, 0))
    shape = jax.ShapeDtypeStruct((rows, cols), F32)
    return pl.pallas_call(body, name=name, grid=(layers, nt),
                          in_specs=[part_spec(l) for l in range(layers)] + [row, row, row],
                          out_specs=[row] * 4, out_shape=[shape] * 4,
                          compiler_params=_params(("parallel", "parallel"), VMEM_BIG))(*parts, w, m, v)


def _flatten(arrays, dtype, row_mult):
    flat = jnp.concatenate([a.reshape(-1).astype(dtype) for a in arrays])
    chunk = FLAT_COLS * row_mult
    padded = -(-flat.shape[0] // chunk) * chunk
    return jnp.pad(flat, (0, padded - flat.shape[0])).reshape(padded // FLAT_COLS, FLAT_COLS)


def _unflatten(flat, shapes):
    flat = flat.reshape(-1)
    out, pos = [], 0
    for shp in shapes:
        size = int(np.prod(shp))
        out.append(flat[pos:pos + size].reshape(shp))
        pos += size
    return out


def _gather_full(names, local, dtype, row_mult, label):
    got = _exchange(_flatten([local[n] for n in names], dtype, row_mult), gather=True, name=label)
    got = got.reshape(N_DEV, -1)
    full, pos = {}, 0
    for n in names:
        shp = local[n].shape
        size = int(np.prod(shp))
        piece = got[:, pos:pos + size].reshape(N_DEV, size // shp[-1], shp[-1])
        full[n] = jnp.transpose(piece, (1, 0, 2)).reshape(shp[:-1] + (N_DEV * shp[-1],))
        pos += size
    return full


def _scatter_rows(names, grads, row_mult):
    pieces = []
    for n in names:
        c = grads[n].shape[-1] // N_DEV
        pieces.append(jnp.transpose(grads[n].reshape(-1, N_DEV, c), (1, 0, 2)).reshape(N_DEV, -1))
    flat = jnp.concatenate(pieces, axis=1)
    chunk = FLAT_COLS * row_mult
    padded = -(-flat.shape[1] // chunk) * chunk
    return jnp.pad(flat, ((0, 0), (0, padded - flat.shape[1]))).reshape(N_DEV, padded // FLAT_COLS, FLAT_COLS)


def _from_shards(stacked, axis):
    _, a, b = stacked.shape
    if axis == 1:
        return jnp.transpose(stacked, (1, 0, 2)).reshape(a, N_DEV * b)
    return stacked.reshape(N_DEV * a, b)


def _to_shards(full, axis):
    ra, rb = full.shape
    if axis == 1:
        return jnp.transpose(full.reshape(ra, N_DEV, rb // N_DEV), (1, 0, 2))
    return full.reshape(N_DEV, ra // N_DEV, rb)


class _ShardedMatmulWeights:
    def __init__(self, local):
        self.local, self.full, self.parts = local, {}, {}

    def gather_src(self, n, l):
        return self.local[n][l].astype(BF16)

    def set_gathered(self, n, l, got):
        self.full[n, l] = _from_shards(got, SHARD_AXIS[n] - 1)

    def get(self, n, l):
        return self.full[n, l]

    def scatter_src(self, n, l, dw):
        return _to_shards(dw, SHARD_AXIS[n] - 1).astype(BF16)

    def set_scattered(self, n, l, parts):
        self.parts[n, l] = parts


def _role(role, i):
    mixer = 'attn_w_' if i % 2 == 0 else 'conv_w_'
    return {'in': (mixer + 'in', i // 2), 'out': (mixer + 'out', i // 2),
            'up': ('ffn_w_up', i), 'down': ('ffn_w_down', i)}[role]


def _local_step(x, tgt, w, big):
    s = x.shape[0]

    def gather_of(role, i):
        if role is None or i >= DEPTH:
            return None, None
        key = _role(role, i)
        return key, (big.gather_src(*key), True)

    def project(a, role, i, ln=None, gather=None):
        nxt, carry = gather_of(gather, i + 1)
        weight = big.get(*_role(role, i))
        if ln is None:
            out, got = _mm(a, weight, carry=carry, name=role + "_fwd")
        else:
            *out, got = _mm_ln(a, weight, *ln, carry=carry, name=role + "_fwd")
        if got is not None:
            big.set_gathered(*nxt, got)
        return out

    def project_back(a, d_out, role, i, ln=None, scatter=True):
        key = _role(role, i)
        dw, _ = _mm(a, d_out, ta=True, name=role + "_dw")
        carry = (big.scatter_src(*key, dw), False)
        if not scatter:
            return _mm(d_out, big.get(*key), tb=True, name=role + "_dx")[0], (key, carry)
        if ln is None:
            d_in, parts = _mm(d_out, big.get(*key), tb=True, carry=carry, name=role + "_dx")
        else:
            *d_in, parts = _mm_ln_bwd(d_out, big.get(*key), *ln, carry=carry, name=role + "_dx")
        if parts is not None:
            big.set_scattered(*key, parts)
        return d_in

    na_tq = NA_QROWS * GRID_W
    band_idx = [_band_bucket_index(d) for _, d in DIL_PATTERNS]
    band_bias = [_band_bias(w['t5_bias'], d, name=f"band_bias_{d}") for _, d in DIL_PATTERNS]
    band_mask = [_mask_tiles("band", DIL_TQ, DIL_KW, s // d, name=f"band_mask_{d}") for _, d in DIL_PATTERNS]
    na_mask = _mask_tiles("na", na_tq, 3 * na_tq, s, name="na_mask")
    na_cols, dil_cols = (0, 4, 8), (12, 16, 20)
    grads = {n: [None] * w[n].shape[0] for n in SMALL_SHARDED + REPLICATED if n != 't5_bias'}
    saved = []
    x16 = x.astype(BF16)

    for i in range(DEPTH):
        j = i // 2
        st = {'x': x, 'x16': x16}
        if i % 2 == 0:
            h = project(x16, 'in', i)
            na_bias = _na_bias(w['na_rpb'][j], name="na_bias")
            late = ('up', 'down', 'out') if i == 0 else ()
            rides = [gather_of(role, 0) for role in late] + [(None, None)] * 4
            outs, lses = [], []
            calls = [(na_bias, na_mask, 1, na_tq, NA_NSUB, na_cols, "na_fwd")] + [
                (bias, mask, d, DIL_TQ, DIL_NSUB[d], dil_cols, f"dil_fwd_{d}")
                for (_, d), bias, mask in zip(DIL_PATTERNS, band_bias, band_mask)]
            for (bias, mask, d, tq, nsub, cols, label), (key, carry) in zip(calls, rides):
                o, l, got = _attn_fwd(h, bias, mask, d=d, tq=tq, nsub=nsub, cols=cols, carry=carry, name=label)
                if got is not None:
                    big.set_gathered(*key, got)
                outs.append(o)
                lses.append(l)
            o_na, l_na = outs.pop(0), lses.pop(0)
            mid, mid16, l_dil = _dil_combine(o_na, outs, lses, name="dil_combine")
            st.update(h=h, mid=mid, mid16=mid16, l_na=l_na, l_dil=l_dil, na_bias=na_bias)
        else:
            h = project(x16, 'in', i)
            mid16, u2 = _conv_mid_fwd(h, w['conf_dw_w'][j], w['conf_dw_b'][j], w['conf_ln_g'][j], w['conf_ln_b'][j],
                                      w['sconv_w'][j], name="conv_mid_fwd")
            st.update(h=h, mid16=mid16, u2=u2)
        z_mix, xa, xa16 = project(mid16, 'out', i, ln=(x, w['mix_ln_g'][i], w['mix_ln_b'][i]), gather='out')
        hu = project(xa16, 'up', i, gather='in')
        nxt, carry = gather_of('up', i + 1)
        act16, got = _ffn_mid_fwd(hu, w['ffn_dw_w'][i], carry=carry, name="ffn_mid_fwd")
        if got is not None:
            big.set_gathered(*nxt, got)
        z_ffn, xb, xb16 = project(act16, 'down', i, ln=(xa, w['ffn_ln_g'][i], w['ffn_ln_b'][i]), gather='down')
        st.update(z_mix=z_mix, xa16=xa16, hu=hu, act16=act16, z_ffn=z_ffn)
        saved.append(st)
        x, x16 = xb, xb16

    loss, d_loss = _loss_head(x, tgt, name="loss_head")
    g_t5 = None
    dz, dz16, dg, db = _ln_bwd(saved[-1]['z_ffn'], w['ffn_ln_g'][-1], d_loss, None, name="last_ln_bwd")
    for i in reversed(range(DEPTH)):
        j = i // 2
        st = saved[i]
        grads['ffn_ln_g'][i], grads['ffn_ln_b'][i] = dg, db
        dact, (key, carry) = project_back(st['act16'], dz16, 'down', i, scatter=False)
        dhu, grads['ffn_dw_w'][i], parts = _ffn_mid_bwd(st['hu'], w['ffn_dw_w'][i], dact, carry=carry,
                                                       name="ffn_mid_bwd")
        if parts is not None:
            big.set_scattered(*key, parts)
        dz, dz1, dg, db = project_back(st['xa16'], dhu, 'up', i, ln=(st['z_mix'], w['mix_ln_g'][i], dz))
        grads['mix_ln_g'][i], grads['mix_ln_b'][i] = dg, db
        dmid = project_back(st['mid16'], dz1, 'out', i)
        if i % 2 == 0:
            h = st['h']
            dq, dk, dv, dbias = _attn_bwd(h, st['na_bias'], na_mask, dmid, st['mid'], st['l_na'], d=1, tq=na_tq,
                                          nsub=NA_NSUB, cols=na_cols, ycol=0, name="na_bwd")
            grads['na_rpb'][j] = _rpb_grad(dbias, name="rpb_grad")
            dil, dbs = [], []
            for (_, d), bias, mask in zip(DIL_PATTERNS, band_bias, band_mask):
                g = _attn_bwd(h, bias, mask, dmid, st['mid'], st['l_dil'], d=d, tq=DIL_TQ, nsub=DIL_NSUB[d],
                              cols=dil_cols, ycol=4, name=f"dil_bwd_{d}")
                dil.append(g[:3])
                dbs.append(g[3])
            t5 = _t5_grad(dbs, band_idx, name="t5_grad")
            g_t5 = t5 if g_t5 is None else g_t5 + t5
            dh = _attn_dh((dq, dk, dv), dil, name="attn_dh")
        else:
            dh, dw31, db31, dlg, dlb, dw3 = _conv_mid_bwd(st['h'], st['u2'], dmid, w['conf_dw_w'][j],
                                                          w['conf_ln_g'][j], w['conf_ln_b'][j], w['sconv_w'][j],
                                                          name="conv_mid_bwd")
            grads['conf_dw_w'][j], grads['conf_dw_b'][j] = dw31, db31
            grads['conf_ln_g'][j], grads['conf_ln_b'][j], grads['sconv_w'][j] = dlg, dlb, dw3
        if i > 0:
            below = saved[i - 1]
            dz, dz16, dg, db = project_back(st['x16'], dh, 'in', i, ln=(below['z_ffn'], w['ffn_ln_g'][i - 1], dz))
        else:
            dx = _axpy(dz, project_back(st['x16'], dh, 'in', i), name="grad_x")
    full = {n: jnp.stack(g) for n, g in grads.items()}
    full['t5_bias'] = g_t5
    return loss, dx, full


def kernel(x, t5_bias, attn_w_in, attn_w_out, na_rpb, conv_w_in, conf_dw_w, conf_dw_b, conf_ln_g, conf_ln_b, sconv_w, conv_w_out, ffn_w_up, ffn_dw_w, ffn_w_down, mix_ln_g, mix_ln_b, ffn_ln_g, ffn_ln_b, loss_target, m_t5_bias, m_attn_w_in, m_attn_w_out, m_na_rpb, m_conv_w_in, m_conf_dw_w, m_conf_dw_b, m_conf_ln_g, m_conf_ln_b, m_sconv_w, m_conv_w_out, m_ffn_w_up, m_ffn_dw_w, m_ffn_w_down, m_mix_ln_g, m_mix_ln_b, m_ffn_ln_g, m_ffn_ln_b, v_t5_bias, v_attn_w_in, v_attn_w_out, v_na_rpb, v_conv_w_in, v_conf_dw_w, v_conf_dw_b, v_conf_ln_g, v_conf_ln_b, v_sconv_w, v_conv_w_out, v_ffn_w_up, v_ffn_dw_w, v_ffn_w_down, v_mix_ln_g, v_mix_ln_b, v_ffn_ln_g, v_ffn_ln_b):
    args = dict(locals())
    local = {n: args[n] for n in WEIGHTS}
    mom1 = {n: args['m_' + n] for n in WEIGHTS}
    mom2 = {n: args['v_' + n] for n in WEIGHTS}

    kinds = ('grad', 'delta', 'new_m', 'new_v')
    small = {n: local[n] for n in REPLICATED}
    small.update(_gather_full(SMALL_SHARDED, local, F32, 8, "gather_small_weights"))
    big = _ShardedMatmulWeights({n: local[n] for n in MATMUL_WEIGHTS})
    n, l = _role('in', 0)
    big.set_gathered(n, l, _exchange(big.gather_src(n, l), gather=True, name="gather_first_in"))

    loss, dx, grads = _local_step(x[0], loss_target[0], small, big)
    loss = lax.psum(loss, MESH_AXES)

    out = {}
    for n in MATMUL_WEIGHTS:
        layers, a, b = local[n].shape
        res = _adamw([big.parts[n, l] for l in range(layers)],
                     *[t.reshape(layers * a, b) for t in (local[n], mom1[n], mom2[n])], name="adamw_" + n)
        for kind, r in zip(kinds, res):
            out[kind + '_' + n] = r.reshape(layers, a, b)
    for names, sharded, label in ((SMALL_SHARDED, True, "small"), (REPLICATED, False, "replicated")):
        shapes = [local[n].shape for n in names]
        if sharded:
            parts = _exchange(_scatter_rows(names, grads, 8), gather=False, name="scatter_small_grads")
        else:
            parts = _exchange(_flatten([grads[n] for n in names], F32, 8), gather=True, name="gather_replicated_grads")
        res = _adamw([parts], _flatten([local[n] for n in names], F32, 8), _flatten([mom1[n] for n in names], F32, 8),
                     _flatten([mom2[n] for n in names], F32, 8), name="adamw_" + label)
        for kind, flat in zip(kinds, res):
            for n, a in zip(names, _unflatten(flat, shapes)):
                out[kind + '_' + n] = a

    return (loss, dx[None], *[out[k + '_' + n] for k in ('grad', 'delta', 'new_m', 'new_v') for n in WEIGHTS])
```
